```python
import math
import jax
import jax.numpy as jnp
from jax import lax
import numpy as np

D_MODEL = 1024
BATCH = 32
SEQ = 256
DEPTH = 4
DEC_BATCH = 2
DEC_SEQ = 2048
PAST_LEN = 256

GRID_W = 64
N_MIXERS = 4
EXPAND = 2
E = EXPAND * D_MODEL
EPS = 1e-6
SSD_HEAD_DIM = 64
SSD_HEADS = E // SSD_HEAD_DIM
SSD_GROUPS = 8
SSD_STATE = 128
SSD_CHUNK = 128
CONV_W = 5
SSD_GN = SSD_GROUPS * SSD_STATE
SSD_CONV_CH = E + 2 * SSD_GN
SSD_IN = E + SSD_CONV_CH + 2 * SSD_HEADS
MLP_CHUNK = 128
MLP_GROUPS = 8
S5_GROUP = 16
S5_GROUPS = E // S5_GROUP
S5_STATE = 64
ATT_HEAD_DIM = 64
ATT_HEADS = E // ATT_HEAD_DIM
WIN_ROWS = 8
WIN_COLS = 16
ATT_BLOCK = 128
N_SSD = (DEPTH + 3) // 4
N_MLP = (DEPTH + 2) // 4
N_S5 = (DEPTH + 1) // 4
N_NAT = DEPTH // 4

kernel_name = 'hybrid_ssd_gmlp_s5_natten_diffusion_step'


def rms_norm(x, g):
    xf = x.astype(jnp.float32)
    y = xf * lax.rsqrt(jnp.mean(xf * xf, axis=-1, keepdims=True) + EPS)
    return (y * g.astype(jnp.float32)).astype(x.dtype)


def layer_norm(x, g, b):
    xf = x.astype(jnp.float32)
    xc = xf - jnp.mean(xf, axis=-1, keepdims=True)
    y = xc * lax.rsqrt(jnp.mean(xc * xc, axis=-1, keepdims=True) + EPS)
    return (y * g.astype(jnp.float32) + b.astype(jnp.float32)).astype(x.dtype)


def adaln(x, g, mod):
    shift, scale, gate = jnp.split(mod, 3, axis=-1)
    return rms_norm(x, g) * (1.0 + scale) + shift, gate


def dw_conv(x, w, bias):
    k = w.shape[0]
    l = x.shape[1]
    pad = k // 2
    xp = jnp.pad(x, ((0, 0), (pad, k - 1 - pad), (0, 0)))
    acc = xp[:, 0:l] * w[0]
    for j in range(1, k):
        acc = acc + xp[:, j:j + l] * w[j]
    return acc + bias


def segsum(x):
    t = x.shape[-1]
    xr = jnp.broadcast_to(x[..., :, None], x.shape + (t,))
    xr = jnp.where(jnp.tril(jnp.ones((t, t), bool), -1), xr, 0.0)
    ss = jnp.cumsum(xr, axis=-2)
    return jnp.where(jnp.tril(jnp.ones((t, t), bool)), ss, -jnp.inf)


def ssd_scan(x, dt, a, bmat, cmat, h0):
    b, l, h, p = x.shape
    g, n = bmat.shape[2], bmat.shape[3]
    r = h // g
    nc = l // SSD_CHUNK
    xd = (x * dt[..., None]).reshape(b, nc, SSD_CHUNK, g, r, p)
    da = (dt * a).astype(jnp.float32).reshape(b, nc, SSD_CHUNK, g, r)
    da = jnp.transpose(da, (0, 3, 4, 1, 2))
    bm = bmat.reshape(b, nc, SSD_CHUNK, g, n)
    cm = cmat.reshape(b, nc, SSD_CHUNK, g, n)
    a_cum = jnp.cumsum(da, axis=-1)
    decay_in = jnp.exp(segsum(da))
    y_diag = jnp.einsum('bclgn,bcsgn,bgrcls,bcsgrp->bclgrp', cm, bm, decay_in, xd)
    decay_st = jnp.exp(a_cum[..., -1:] - a_cum)
    states = jnp.einsum('bclgn,bgrcl,bclgrp->bcgrpn', bm, decay_st, xd)
    h0r = h0.reshape(b, g, r, p, n).astype(states.dtype)
    states = jnp.concatenate([h0r[:, None], states], axis=1)
    chunk_tot = jnp.pad(a_cum[..., -1], ((0, 0), (0, 0), (0, 0), (1, 0)))
    decay_ch = jnp.exp(segsum(chunk_tot))
    new_states = jnp.einsum('bgrzc,bcgrpn->bzgrpn', decay_ch, states)
    states, final = new_states[:, :-1], new_states[:, -1]
    y_off = jnp.einsum('bclgn,bcgrpn,bgrcl->bclgrp', cm, states, jnp.exp(a_cum))
    y = (y_diag + y_off).reshape(b, l, h, p)
    return y, final.reshape(b, h, p, n)


def ssd_mixer(u, conv_w, conv_b, dt_bias, a_log, d_skip, norm_g, h0):
    b, l, _ = u.shape
    z = u[..., :E]
    xbc = jax.nn.silu(dw_conv(u[..., E:E + SSD_CONV_CH], conv_w, conv_b))
    dt_raw = u[..., E + SSD_CONV_CH:].reshape(b, l, 2, SSD_HEADS)
    xs = xbc[..., :E].reshape(b, l, SSD_HEADS, SSD_HEAD_DIM)
    bm = xbc[..., E:E + SSD_GN].reshape(b, l, SSD_GROUPS, SSD_STATE)
    cm = xbc[..., E + SSD_GN:].reshape(b, l, SSD_GROUPS, SSD_STATE)
    dt = jax.nn.softplus((dt_raw + dt_bias).astype(jnp.float32))
    a = -jnp.exp(a_log.astype(jnp.float32))
    flip = lambda t: jnp.flip(t, axis=1)
    y_f, h_f = ssd_scan(xs, dt[:, :, 0], a[0], bm, cm, h0[:, 0])
    y_b, h_b = ssd_scan(flip(xs), flip(dt[:, :, 1]), a[1], flip(bm), flip(cm), h0[:, 1])
    y = y_f + flip(y_b) + d_skip[:, None] * xs
    y = rms_norm(y.reshape(b, l, E) * jax.nn.silu(z), norm_g)
    return y.astype(u.dtype), jnp.stack([h_f, h_b], axis=1)


def gmlp_mixer(u, ln_g, ln_b, w_s, b_s):
    b, l, _ = u.shape
    uu = jax.nn.gelu(u[..., :E])
    vv = layer_norm(jax.nn.gelu(u[..., E:2 * E]), ln_g, ln_b)
    z = u[..., 2 * E:]
    nc = l // MLP_CHUNK
    vv = vv.reshape(b, nc, MLP_CHUNK, MLP_GROUPS, E // MLP_GROUPS)
    s = jnp.einsum('gij,bcjgd->bcigd', w_s, vv) + b_s.T[:, :, None]
    return uu * s.reshape(b, l, E) * jax.nn.silu(z)


def complex_combine(e1, e2):
    a1r, a1i, b1r, b1i = e1
    a2r, a2i, b2r, b2i = e2
    return (a2r * a1r - a2i * a1i, a2r * a1i + a2i * a1r,
            a2r * b1r - a2i * b1i + b2r, a2r * b1i + a2i * b1r + b2i)


def s5_mixer(u, lam_re, lam_im, log_step, b_re, b_im, c_re, c_im, d_skip, w_glu, b_glu, h0):
    b, l, _ = u.shape
    uu = u[..., :E]
    z = u[..., E:]
    ug = uu.reshape(b, l, S5_GROUPS, S5_GROUP).astype(jnp.float32)
    ys = []
    finals = []
    for d in range(2):
        lr = lam_re[d].astype(jnp.float32)
        li = lam_im[d].astype(jnp.float32)
        step = jnp.exp(log_step[d].astype(jnp.float32))[:, None]
        mag = jnp.exp(lr * step)
        ang = li * step
        ab_re = mag * jnp.cos(ang)
        ab_im = mag * jnp.sin(ang)
        den = lr * lr + li * li
        nr = ab_re - 1.0
        cr = (nr * lr + ab_im * li) / den
        ci = (ab_im * lr - nr * li) / den
        br = b_re[d].astype(jnp.float32)
        bi = b_im[d].astype(jnp.float32)
        bb_re = cr[..., None] * br - ci[..., None] * bi
        bb_im = cr[..., None] * bi + ci[..., None] * br
        seq = ug if d == 0 else jnp.flip(ug, axis=1)
        drv_re = jnp.einsum('gnj,blgj->blgn', bb_re, seq)
        drv_im = jnp.einsum('gnj,blgj->blgn', bb_im, seq)
        h_re = h0[:, d, 0].astype(jnp.float32)
        h_im = h0[:, d, 1].astype(jnp.float32)
        drv_re = drv_re.at[:, 0].add(ab_re * h_re - ab_im * h_im)
        drv_im = drv_im.at[:, 0].add(ab_re * h_im + ab_im * h_re)
        a_re = jnp.broadcast_to(ab_re, (1, l, S5_GROUPS, S5_STATE))
        a_im = jnp.broadcast_to(ab_im, (1, l, S5_GROUPS, S5_STATE))
        _, _, s_re, s_im = lax.associative_scan(complex_combine, (a_re, a_im, drv_re, drv_im), axis=1)
        y_d = (jnp.einsum('gjn,blgn->blgj', c_re[d].astype(jnp.float32), s_re)
               - jnp.einsum('gjn,blgn->blgj', c_im[d].astype(jnp.float32), s_im))
        if d == 1:
            y_d = jnp.flip(y_d, axis=1)
        ys.append(y_d)
        finals.append(jnp.stack([s_re[:, -1], s_im[:, -1]], axis=1))
    y = ((ys[0] + ys[1]).reshape(b, l, E) + d_skip * uu).astype(u.dtype)
    y = jax.nn.gelu(y)
    y = y * jax.nn.sigmoid(y @ w_glu + b_glu)
    return y * jax.nn.silu(z), jnp.stack(finals, axis=1)


def split_heads(t):
    return t.reshape(t.shape[0], t.shape[1], ATT_HEADS, ATT_HEAD_DIM)


def context_attention(q, k, v):
    b, s = q.shape[0], q.shape[1]
    nb = s // ATT_BLOCK
    scale = ATT_HEAD_DIM ** -0.5
    qb = jnp.moveaxis(q.reshape(b, nb, ATT_BLOCK, ATT_HEADS, ATT_HEAD_DIM), 1, 0)

    def one_block(qi):
        logits = jnp.einsum('bqhd,bhkd->bhqk', qi, k).astype(jnp.float32) * scale
        p = jax.nn.softmax(logits, axis=-1).astype(v.dtype)
        return jnp.einsum('bhqk,bhkd->bqhd', p, v)

    out = lax.map(one_block, qb)
    return jnp.moveaxis(out, 0, 1).reshape(b, s, E)


def neighbourhood_attention(q, k, v, ck, cv, rpb):
    b, l = q.shape[0], q.shape[1]
    rows = l // GRID_W
    wr = min(WIN_ROWS, rows)
    nw = wr * GRID_W
    scale = ATT_HEAD_DIM ** -0.5
    qg = q.reshape(b, rows, GRID_W, ATT_HEADS, ATT_HEAD_DIM)
    kg = k.reshape(b, rows, GRID_W, ATT_HEADS, ATT_HEAD_DIM)
    vg = v.reshape(b, rows, GRID_W, ATT_HEADS, ATT_HEAD_DIM)
    r_idx = jnp.arange(rows)
    r_start = jnp.clip(r_idx - wr // 2, 0, rows - wr)
    key_rows = r_start[:, None] + jnp.arange(wr)[None, :]
    kb = kg[:, key_rows].reshape(b, rows, nw, ATT_HEADS, ATT_HEAD_DIM)
    vb = vg[:, key_rows].reshape(b, rows, nw, ATT_HEADS, ATT_HEAD_DIM)
    c_idx = jnp.arange(GRID_W)
    c_start = jnp.clip(c_idx - WIN_COLS // 2, 0, GRID_W - WIN_COLS)
    col_ok = (c_idx[None, :] >= c_start[:, None]) & (c_idx[None, :] < c_start[:, None] + WIN_COLS)
    mask = jnp.broadcast_to(col_ok[:, None, :], (GRID_W, wr, GRID_W)).reshape(GRID_W, nw)
    idx_r = key_rows - r_idx[:, None] + (WIN_ROWS - 1)
    idx_c = jnp.clip(c_idx[None, :] - c_idx[:, None] + (WIN_COLS - 1), 0, 2 * WIN_COLS - 2)
    bias = rpb[:, idx_r[:, None, :, None], idx_c[None, :, None, :]].reshape(ATT_HEADS, rows, GRID_W, nw)
    s_win = jnp.einsum('brqhd,brkhd->bhrqk', qg, kb).astype(jnp.float32) * scale + bias.astype(jnp.float32)
    s_win = jnp.where(mask, s_win, -jnp.inf)
    s_ctx = jnp.einsum('brqhd,bhpd->bhrqp', qg, ck).astype(jnp.float32) * scale
    p = jax.nn.softmax(jnp.concatenate([s_win, s_ctx], axis=-1), axis=-1).astype(v.dtype)
    out = (jnp.einsum('bhrqk,brkhd->brqhd', p[..., :nw], vb)
           + jnp.einsum('bhrqp,bhpd->brqhd', p[..., nw:], cv))
    return out.reshape(b, l, E)


def setup_inputs(seed: int = 0) -> dict:
    key = jax.random.key(seed)
    keys = iter(jax.random.split(key, 64))

    def nrm(shape, scale):
        return jax.random.normal(next(keys), shape, jnp.float32) * scale

    def unif(shape, lo, hi):
        return jax.random.uniform(next(keys), shape, jnp.float32, minval=lo, maxval=hi)

    dt0 = jnp.exp(unif((N_SSD, 2, SSD_HEADS), math.log(1e-3), math.log(1e-1)))
    inp = {}
    inp['x_prompt'] = nrm((BATCH, SEQ, D_MODEL), 1.0)
    inp['x_sample'] = nrm((DEC_BATCH, DEC_SEQ, D_MODEL), 1.0)
    inp['state_ssd'] = nrm((DEC_BATCH, N_SSD, 2, SSD_HEADS, SSD_HEAD_DIM, SSD_STATE), 0.1)
    inp['state_s5'] = nrm((DEC_BATCH, N_S5, 2, 2, S5_GROUPS, S5_STATE), 0.1)
    inp['cache_k'] = nrm((DEC_BATCH, N_NAT, ATT_HEADS, PAST_LEN, ATT_HEAD_DIM), 1.0)
    inp['cache_v'] = nrm((DEC_BATCH, N_NAT, ATT_HEADS, PAST_LEN, ATT_HEAD_DIM), 1.0)
    inp['c'] = nrm((DEC_BATCH, D_MODEL), 1.0)
    inp['c_ctx'] = nrm((D_MODEL,), 1.0)
    inp['norm_g'] = 1.0 + nrm((DEPTH, D_MODEL), 0.02)
    inp['w_mod'] = nrm((DEPTH, D_MODEL, 3 * D_MODEL), 0.5 * D_MODEL ** -0.5)
    inp['b_mod'] = nrm((DEPTH, 3 * D_MODEL), 0.02)
    inp['w_out'] = nrm((DEPTH, E, D_MODEL), E ** -0.5)
    inp['final_g'] = 1.0 + nrm((D_MODEL,), 0.02)
    inp['ssd_w_in'] = nrm((N_SSD, D_MODEL, SSD_IN), D_MODEL ** -0.5)
    inp['ssd_conv_w'] = nrm((N_SSD, CONV_W, SSD_CONV_CH), CONV_W ** -0.5)
    inp['ssd_conv_b'] = nrm((N_SSD, SSD_CONV_CH), 0.02)
    inp['ssd_dt_bias'] = dt0 + jnp.log(-jnp.expm1(-dt0))
    inp['ssd_a_log'] = jnp.log(unif((N_SSD, 2, SSD_HEADS), 1.0, 16.0))
    inp['ssd_d'] = 1.0 + nrm((N_SSD, SSD_HEADS), 0.02)
    inp['ssd_norm_g'] = 1.0 + nrm((N_SSD, E), 0.02)
    inp['mlp_w_in'] = nrm((N_MLP, D_MODEL, 3 * E), D_MODEL ** -0.5)
    inp['mlp_ln_g'] = 1.0 + nrm((N_MLP, E), 0.02)
    inp['mlp_ln_b'] = nrm((N_MLP, E), 0.02)
    inp['mlp_w_s'] = nrm((N_MLP, MLP_GROUPS, MLP_CHUNK, MLP_CHUNK), MLP_CHUNK ** -0.5)
    inp['mlp_b_s'] = 1.0 + nrm((N_MLP, MLP_GROUPS, MLP_CHUNK), 0.02)
    inp['s5_w_in'] = nrm((N_S5, D_MODEL, 2 * E), D_MODEL ** -0.5)
    inp['s5_lam_re'] = -0.5 + nrm((N_S5, 2, S5_GROUPS, S5_STATE), 0.01)
    inp['s5_lam_im'] = jnp.pi * jnp.arange(S5_STATE, dtype=jnp.float32) + nrm((N_S5, 2, S5_GROUPS, S5_STATE), 0.01)
    inp['s5_log_step'] = unif((N_S5, 2, S5_GROUPS), math.log(1e-3), math.log(1e-1))
    inp['s5_b_re'] = nrm((N_S5, 2, S5_GROUPS, S5_STATE, S5_GROUP), (2 * S5_GROUP) ** -0.5)
    inp['s5_b_im'] = nrm((N_S5, 2, S5_GROUPS, S5_STATE, S5_GROUP), (2 * S5_GROUP) ** -0.5)
    inp['s5_c_re'] = nrm((N_S5, 2, S5_GROUPS, S5_GROUP, S5_STATE), S5_STATE ** -0.5)
    inp['s5_c_im'] = nrm((N_S5, 2, S5_GROUPS, S5_GROUP, S5_STATE), S5_STATE ** -0.5)
    inp['s5_d'] = nrm((N_S5, E), 1.0)
    inp['s5_w_glu'] = nrm((N_S5, E, E), E ** -0.5)
    inp['s5_b_glu'] = nrm((N_S5, E), 0.02)
    inp['nat_w_in'] = nrm((N_NAT, D_MODEL, 4 * E), D_MODEL ** -0.5)
    inp['nat_rpb'] = nrm((N_NAT, ATT_HEADS, 2 * WIN_ROWS - 1, 2 * WIN_COLS - 1), 0.1)
    return inp


def reference(x_prompt, x_sample, state_ssd, state_s5, cache_k, cache_v, c, c_ctx,
              norm_g, w_mod, b_mod, w_out, final_g,
              ssd_w_in, ssd_conv_w, ssd_conv_b, ssd_dt_bias, ssd_a_log, ssd_d, ssd_norm_g,
              mlp_w_in, mlp_ln_g, mlp_ln_b, mlp_w_s, mlp_b_s,
              s5_w_in, s5_lam_re, s5_lam_im, s5_log_step, s5_b_re, s5_b_im, s5_c_re, s5_c_im,
              s5_d, s5_w_glu, s5_b_glu,
              nat_w_in, nat_rpb):
    xc = x_prompt
    xl = x_sample
    bc, sc_len = xc.shape[0], xc.shape[1]
    cond_ctx = jax.nn.silu(c_ctx)
    cond_lat = jax.nn.silu(c)[:, None, :]
    new_ssd, new_s5, new_k, new_v = [], [], [], []
    for i in range(DEPTH):
        kind = i % N_MIXERS
        j = i // N_MIXERS
        hc, gate_c = adaln(xc, norm_g[i], cond_ctx @ w_mod[i] + b_mod[i])
        hl, gate_l = adaln(xl, norm_g[i], cond_lat @ w_mod[i] + b_mod[i])
        if kind == 0:
            h0 = jnp.zeros((bc, 2, SSD_HEADS, SSD_HEAD_DIM, SSD_STATE), jnp.float32)
            yc, st = ssd_mixer(hc @ ssd_w_in[j], ssd_conv_w[j], ssd_conv_b[j], ssd_dt_bias[j],
                               ssd_a_log[j], ssd_d[j], ssd_norm_g[j], h0)
            yl, _ = ssd_mixer(hl @ ssd_w_in[j], ssd_conv_w[j], ssd_conv_b[j], ssd_dt_bias[j],
                              ssd_a_log[j], ssd_d[j], ssd_norm_g[j], state_ssd[:, j])
            new_ssd.append(st.astype(x_prompt.dtype))
        elif kind == 1:
            yc = gmlp_mixer(hc @ mlp_w_in[j], mlp_ln_g[j], mlp_ln_b[j], mlp_w_s[j], mlp_b_s[j])
            yl = gmlp_mixer(hl @ mlp_w_in[j], mlp_ln_g[j], mlp_ln_b[j], mlp_w_s[j], mlp_b_s[j])
        elif kind == 2:
            h0 = jnp.zeros((bc, 2, 2, S5_GROUPS, S5_STATE), jnp.float32)
            yc, st = s5_mixer(hc @ s5_w_in[j], s5_lam_re[j], s5_lam_im[j], s5_log_step[j], s5_b_re[j],
                              s5_b_im[j], s5_c_re[j], s5_c_im[j], s5_d[j], s5_w_glu[j], s5_b_glu[j], h0)
            yl, _ = s5_mixer(hl @ s5_w_in[j], s5_lam_re[j], s5_lam_im[j], s5_log_step[j], s5_b_re[j],
                             s5_b_im[j], s5_c_re[j], s5_c_im[j], s5_d[j], s5_w_glu[j], s5_b_glu[j],
                             state_s5[:, j])
            new_s5.append(st.astype(x_prompt.dtype))
        else:
            uc = hc @ nat_w_in[j]
            k_ctx = jnp.transpose(split_heads(uc[..., E:2 * E]), (0, 2, 1, 3))
            v_ctx = jnp.transpose(split_heads(uc[..., 2 * E:3 * E]), (0, 2, 1, 3))
            yc = context_attention(split_heads(uc[..., :E]), k_ctx, v_ctx) * jax.nn.silu(uc[..., 3 * E:])
            ul = hl @ nat_w_in[j]
            yl = neighbourhood_attention(split_heads(ul[..., :E]), split_heads(ul[..., E:2 * E]),
                                         split_heads(ul[..., 2 * E:3 * E]), cache_k[:, j], cache_v[:, j],
                                         nat_rpb[j]) * jax.nn.silu(ul[..., 3 * E:])
            new_k.append(k_ctx)
            new_v.append(v_ctx)
        xc = xc + gate_c * (yc @ w_out[i])
        xl = xl + gate_l * (yl @ w_out[i])
    y_prompt = rms_norm(xc, final_g)
    y_sample = rms_norm(xl, final_g)
    return (y_prompt, y_sample, jnp.stack(new_ssd, axis=1), jnp.stack(new_s5, axis=1),
            jnp.stack(new_k, axis=1), jnp.stack(new_v, axis=1))
```

```python
import functools

import jax
import jax.numpy as jnp
from jax import lax
from jax.experimental import pallas as pl
from jax.experimental.pallas import tpu as pltpu

F32 = jnp.float32
BF16 = jnp.bfloat16
EPS = 1e-6
HIGHEST = lax.Precision.HIGHEST

LANES = 128
CHUNK = 128
SSD_HEAD_DIM = 64
SSD_GROUP_HEADS = 4
SSD_GROUP_CH = SSD_HEAD_DIM * SSD_GROUP_HEADS
SSD_STATE = 128
CONV_W = 5
CONV_HALO = 8
S5_T = 16
S5_GROUP = 16
S5_STATE = 64
S5_PAIR_CH = 2 * S5_T * S5_GROUP
HEAD_DIM = 64
GRID_W = 64
WIN_ROWS = 8
WIN_COLS = 16
MASKED = -1e30
VMEM_LIMIT = 52 * 1024 * 1024


def _silu(x):
    return x * jax.nn.sigmoid(x)


def _gelu(x):
    return 0.5 * x * (1.0 + jnp.tanh(0.7978845608028654 * (x + 0.044715 * (x * x * x))))


def _softplus(x):
    return jnp.maximum(x, 0.0) + jnp.log1p(jnp.exp(-jnp.abs(x)))


def _dot(a, b):
    return jnp.dot(a, b, preferred_element_type=F32)


def _dot_nt(a, b):
    return lax.dot_general(a, b, (((1,), (1,)), ((), ())), preferred_element_type=F32)


def _params(*sem):
    return pltpu.CompilerParams(dimension_semantics=sem, vmem_limit_bytes=VMEM_LIMIT)


def _mod_row(i, *, tm, t_ctx, l_lat):
    start = i * tm
    return jnp.where(start < t_ctx, 0, 1 + (start - t_ctx) // l_lat)


def _mod_body(c_ref, w_ref, b_ref, o_ref):
    cond = _silu(c_ref[...]).astype(BF16)
    o_ref[...] = _dot(cond, w_ref[...].astype(BF16)) + b_ref[...]


def _modulation(cond8, w_mod, b_mod, *, tn=1024):
    depth, d, n = w_mod.shape
    return pl.pallas_call(
        _mod_body,
        grid=(depth, n // tn),
        in_specs=[pl.BlockSpec((8, d), lambda l, j: (0, 0)),
                  pl.BlockSpec((None, d, tn), lambda l, j: (l, 0, j)),
                  pl.BlockSpec((None, 1, tn), lambda l, j: (l, 0, j))],
        out_specs=pl.BlockSpec((None, 8, tn), lambda l, j: (l, 0, j)),
        out_shape=jax.ShapeDtypeStruct((depth, 8, n), F32),
        compiler_params=_params("parallel", "parallel"),
        name="modulation",
    )(cond8, w_mod, b_mod.reshape(depth, 1, n))


def _inproj_body(x_ref, shift_ref, scale_ref, g_ref, w_ref, o_ref, h_ref):
    @pl.when(pl.program_id(1) == 0)
    def _():
        x = x_ref[...]
        y = x * lax.rsqrt(jnp.mean(x * x, axis=-1, keepdims=True) + EPS) * g_ref[...]
        h_ref[...] = (y * (1.0 + scale_ref[...]) + shift_ref[...]).astype(BF16)

    o_ref[...] = _dot(h_ref[...], w_ref[...])


def _inproj(x, mods, layer, g, w, *, tm, tn, t_ctx, l_lat, name):
    t, d = x.shape
    n = w.shape[1]
    row = functools.partial(_mod_row, tm=tm, t_ctx=t_ctx, l_lat=l_lat)

    def mod_spec(part):
        return pl.BlockSpec((None, None, None, 1, d), lambda i, j: (layer, row(i), part, 0, 0))

    return pl.pallas_call(
        _inproj_body,
        grid=(t // tm, n // tn),
        in_specs=[pl.BlockSpec((tm, d), lambda i, j: (i, 0)), mod_spec(0), mod_spec(1),
                  pl.BlockSpec((1, d), lambda i, j: (0, 0)),
                  pl.BlockSpec((d, tn), lambda i, j: (0, j))],
        out_specs=pl.BlockSpec((tm, tn), lambda i, j: (i, j)),
        out_shape=jax.ShapeDtypeStruct((t, n), F32),
        scratch_shapes=[pltpu.VMEM((tm, d), BF16)],
        compiler_params=_params("parallel", "arbitrary"),
        name=name,
    )(x, mods, mods, g, w)


def _outproj_body(*refs, gated_norm, final_norm):
    y_ref, z_ref, x_ref, gate_ref, w_ref = refs[:5]
    rest = list(refs[5:])
    t = y_ref[...] * _silu(z_ref[...])
    if gated_norm:
        ng_ref = rest.pop(0)
        t = t * lax.rsqrt(jnp.mean(t * t, axis=-1, keepdims=True) + EPS) * ng_ref[...]
    if final_norm:
        fg_ref = rest.pop(0)
    o_ref = rest.pop(0)
    xn = x_ref[...] + gate_ref[...] * _dot(t.astype(BF16), w_ref[...])
    if final_norm:
        xn = xn * lax.rsqrt(jnp.mean(xn * xn, axis=-1, keepdims=True) + EPS) * fg_ref[...]
    o_ref[...] = xn


def _outproj(y, u, z_block, x, mods, layer, w, *, norm_g=None, final_g=None, tm, t_ctx, l_lat, name):
    t, e = y.shape
    d = x.shape[1]
    row = functools.partial(_mod_row, tm=tm, t_ctx=t_ctx, l_lat=l_lat)
    in_specs = [pl.BlockSpec((tm, e), lambda i: (i, 0)),
                pl.BlockSpec((tm, e), lambda i: (i, z_block)),
                pl.BlockSpec((tm, d), lambda i: (i, 0)),
                pl.BlockSpec((None, None, None, 1, d), lambda i: (layer, row(i), 2, 0, 0)),
                pl.BlockSpec((e, d), lambda i: (0, 0))]
    args = [y, u, x, mods, w]
    if norm_g is not None:
        in_specs.append(pl.BlockSpec((1, e), lambda i: (0, 0)))
        args.append(norm_g)
    if final_g is not None:
        in_specs.append(pl.BlockSpec((1, d), lambda i: (0, 0)))
        args.append(final_g)
    return pl.pallas_call(
        functools.partial(_outproj_body, gated_norm=norm_g is not None, final_norm=final_g is not None),
        grid=(t // tm,),
        in_specs=in_specs,
        out_specs=pl.BlockSpec((tm, d), lambda i: (i, 0)),
        out_shape=jax.ShapeDtypeStruct((t, d), F32),
        compiler_params=_params("parallel"),
        name=name,
    )(*args)


def _ssd_body(*refs, seq_len, has_h0, want_final):
    refs = list(refs)
    x_ref, b_ref, c_ref, dt_ref, wx_ref, wb_ref, wc_ref, bx_ref, bb_ref, bc_ref, dtb_ref, alog_ref, dsk_ref = refs[:13]
    rest = refs[13:]
    h0_ref = rest.pop(0) if has_h0 else None
    if want_final:
        y_ref, hf_ref = rest[0], rest[1]
        rest = rest[2:]
    else:
        y_ref, hf_ref = rest[0], None
        rest = rest[1:]
    pad_s, xbc_s, dt_s, da_s, sf_s, sb_s = rest
    nc = seq_len // CHUNK
    width = SSD_GROUP_CH + 2 * SSD_STATE

    zeros = jnp.zeros((CONV_HALO, width), F32)
    pad_s[0:CONV_HALO, :] = zeros
    pad_s[CONV_HALO + seq_len:2 * CONV_HALO + seq_len, :] = zeros

    def copy_chunk(ci, carry):
        base = pl.multiple_of(ci * CHUNK, CHUNK)
        dst = pl.ds(base + CONV_HALO, CHUNK)
        pad_s[dst, 0:SSD_GROUP_CH] = x_ref[pl.ds(base, CHUNK), :]
        pad_s[dst, SSD_GROUP_CH:SSD_GROUP_CH + SSD_STATE] = b_ref[pl.ds(base, CHUNK), :]
        pad_s[dst, SSD_GROUP_CH + SSD_STATE:width] = c_ref[pl.ds(base, CHUNK), :]
        return carry

    lax.fori_loop(0, nc, copy_chunk, 0)

    conv_w = jnp.concatenate([wx_ref[...], wb_ref[...], wc_ref[...]], axis=1)
    conv_b = jnp.concatenate([bx_ref[...], bb_ref[...], bc_ref[...]], axis=1)
    a_row = -jnp.exp(alog_ref[...])
    first = CONV_HALO - CONV_W // 2

    def conv_chunk(ci, carry):
        base = pl.multiple_of(ci * CHUNK, CHUNK)
        win = pad_s[pl.ds(base, CHUNK + 2 * CONV_HALO), :]
        acc = conv_b + win[first:first + CHUNK] * conv_w[0:1]
        for j in range(1, CONV_W):
            acc = acc + win[first + j:first + j + CHUNK] * conv_w[j:j + 1]
        xbc = _silu(acc)
        rows = pl.ds(base, CHUNK)
        xbc_s[rows, :] = xbc
        y_ref[rows, :] = xbc[:, 0:SSD_GROUP_CH] * dsk_ref[...]
        dt = _softplus(dt_ref[rows, :] + dtb_ref[...])
        dt_s[rows, :] = dt
        da_s[rows, :] = dt * a_row
        return carry

    lax.fori_loop(0, nc, conv_chunk, 0)

    if has_h0:
        sf_s[...] = h0_ref[0].reshape(SSD_GROUP_CH, SSD_STATE)
        sb_s[...] = h0_ref[1].reshape(SSD_GROUP_CH, SSD_STATE)
    else:
        sf_s[...] = jnp.zeros((SSD_GROUP_CH, SSD_STATE), F32)
        sb_s[...] = jnp.zeros((SSD_GROUP_CH, SSD_STATE), F32)

    row_i = lax.broadcasted_iota(jnp.int32, (CHUNK, CHUNK), 0)
    col_i = lax.broadcasted_iota(jnp.int32, (CHUNK, CHUNK), 1)
    lower = row_i >= col_i
    upper = row_i <= col_i
    lower_f = lower.astype(F32)
    head_of_lane = lax.broadcasted_iota(jnp.int32, (CHUNK, SSD_GROUP_CH), 1) // SSD_HEAD_DIM
    head_of_row = lax.broadcasted_iota(jnp.int32, (SSD_GROUP_CH, SSD_STATE), 0) // SSD_HEAD_DIM

    def per_head_lanes(cols):
        out = cols[SSD_GROUP_HEADS - 1]
        for r in range(SSD_GROUP_HEADS - 2, -1, -1):
            out = jnp.where(head_of_lane == r, cols[r], out)
        return out

    def one_direction(ci, backward, st_ref):
        rows = pl.ds(pl.multiple_of(ci * CHUNK, CHUNK), CHUNK)
        xbc = xbc_s[rows, :]
        x = xbc[:, 0:SSD_GROUP_CH]
        bm = xbc[:, SSD_GROUP_CH:SSD_GROUP_CH + SSD_STATE].astype(BF16)
        cm = xbc[:, SSD_GROUP_CH + SSD_STATE:width].astype(BF16)
        dt = dt_s[rows, :]
        da = da_s[rows, :]
        cum = jnp.dot(lower_f, da, precision=HIGHEST, preferred_element_type=F32)
        e = cum - da if backward else cum
        e_t = e.T
        s_prev = st_ref[...]
        g = _dot_nt(cm, bm)
        cs = _dot_nt(cm, s_prev.astype(BF16))
        dt_cols, off_cols, st_cols, tots, decays = [], [], [], [], []
        for r in range(SSD_GROUP_HEADS):
            col = r + SSD_GROUP_HEADS * int(backward)
            ec = e[:, col:col + 1]
            er = e_t[col:col + 1, :]
            tot = cum[CHUNK - 1:CHUNK, col:col + 1]
            if backward:
                decays.append(jnp.exp(jnp.where(upper, er - ec, -jnp.inf)))
                off_cols.append(jnp.exp(tot - ec))
                st_cols.append(jnp.exp(ec))
            else:
                decays.append(jnp.exp(jnp.where(lower, ec - er, -jnp.inf)))
                off_cols.append(jnp.exp(ec))
                st_cols.append(jnp.exp(tot - ec))
            dt_cols.append(dt[:, col:col + 1])
            tots.append(jnp.exp(tot))
        xd = x * per_head_lanes(dt_cols)
        xd_b = xd.astype(BF16)
        y = cs * per_head_lanes(off_cols)
        for r in range(SSD_GROUP_HEADS):
            yd = _dot((g * decays[r]).astype(BF16), xd_b)
            y = y + jnp.where(head_of_lane == r, yd, 0.0)
        y_ref[rows, :] += y
        w_t = (xd * per_head_lanes(st_cols)).T.astype(BF16)
        keep = tots[SSD_GROUP_HEADS - 1]
        for r in range(SSD_GROUP_HEADS - 2, -1, -1):
            keep = jnp.where(head_of_row == r, tots[r], keep)
        st_ref[...] = keep * s_prev + _dot(w_t, bm)

    def scan_chunk(i, carry):
        one_direction(i, False, sf_s)
        one_direction(nc - 1 - i, True, sb_s)
        return carry

    lax.fori_loop(0, nc, scan_chunk, 0)

    if want_final:
        hf_ref[0] = sf_s[...].reshape(SSD_GROUP_HEADS, SSD_HEAD_DIM, SSD_STATE)
        hf_ref[1] = sb_s[...].reshape(SSD_GROUP_HEADS, SSD_HEAD_DIM, SSD_STATE)


def _ssd_call(u, dt, conv_w, conv_b, dtb, alog, dsk, y_prev, h0, *, n_seq, seq_len, row_block0, want_final, name):
    t, n_u = u.shape
    e = dsk.shape[1]
    groups = e // SSD_GROUP_CH
    xb0 = e // SSD_GROUP_CH
    bb0 = 2 * e // SSD_STATE
    cb0 = bb0 + groups
    has_h0 = h0 is not None
    seq = lambda b: row_block0 + b
    in_specs = [pl.BlockSpec((seq_len, SSD_GROUP_CH), lambda b, g: (seq(b), xb0 + g)),
                pl.BlockSpec((seq_len, SSD_STATE), lambda b, g: (seq(b), bb0 + g)),
                pl.BlockSpec((seq_len, SSD_STATE), lambda b, g: (seq(b), cb0 + g)),
                pl.BlockSpec((seq_len, LANES), lambda b, g: (seq(b), g)),
                pl.BlockSpec((CONV_W, SSD_GROUP_CH), lambda b, g: (0, g)),
                pl.BlockSpec((CONV_W, SSD_STATE), lambda b, g: (0, bb0 - xb0 * 2 + g)),
                pl.BlockSpec((CONV_W, SSD_STATE), lambda b, g: (0, cb0 - xb0 * 2 + g)),
                pl.BlockSpec((1, SSD_GROUP_CH), lambda b, g: (0, g)),
                pl.BlockSpec((1, SSD_STATE), lambda b, g: (0, bb0 - xb0 * 2 + g)),
                pl.BlockSpec((1, SSD_STATE), lambda b, g: (0, cb0 - xb0 * 2 + g)),
                pl.BlockSpec((1, LANES), lambda b, g: (0, g)),
                pl.BlockSpec((1, LANES), lambda b, g: (0, g)),
                pl.BlockSpec((1, SSD_GROUP_CH), lambda b, g: (0, g))]
    args = [u, u, u, dt, conv_w, conv_w, conv_w, conv_b, conv_b, conv_b, dtb, alog, dsk]
    state_block = (None, 2, SSD_GROUP_HEADS, SSD_HEAD_DIM, SSD_STATE)
    if has_h0:
        in_specs.append(pl.BlockSpec(state_block, lambda b, g: (b, 0, g, 0, 0)))
        args.append(h0)
    aliases = {}
    if y_prev is not None:
        in_specs.append(pl.BlockSpec(memory_space=pl.ANY))
        args.append(y_prev)
        aliases = {len(args) - 1: 0}
    out_specs = [pl.BlockSpec((seq_len, SSD_GROUP_CH), lambda b, g: (seq(b), g))]
    out_shape = [jax.ShapeDtypeStruct((t, e), F32)]
    if want_final:
        out_specs.append(pl.BlockSpec(state_block, lambda b, g: (b, 0, g, 0, 0)))
        out_shape.append(jax.ShapeDtypeStruct((n_seq, 2, e // SSD_HEAD_DIM, SSD_HEAD_DIM, SSD_STATE), F32))
    width = SSD_GROUP_CH + 2 * SSD_STATE

    def body(*refs):
        refs = list(refs)
        if y_prev is not None:
            n_in = len(args)
            del refs[n_in - 1]
        _ssd_body(*refs, seq_len=seq_len, has_h0=has_h0, want_final=want_final)

    return pl.pallas_call(
        body,
        grid=(n_seq, groups),
        in_specs=in_specs,
        out_specs=out_specs,
        out_shape=out_shape,
        input_output_aliases=aliases,
        scratch_shapes=[pltpu.VMEM((seq_len + 2 * CONV_HALO, width), F32),
                        pltpu.VMEM((seq_len, width), F32),
                        pltpu.VMEM((seq_len, LANES), F32),
                        pltpu.VMEM((seq_len, LANES), F32),
                        pltpu.VMEM((SSD_GROUP_CH, SSD_STATE), F32),
                        pltpu.VMEM((SSD_GROUP_CH, SSD_STATE), F32)],
        compiler_params=_params("parallel", "parallel"),
        name=name,
    )(*args)


def _group_lanes(v, groups):
    per_group = v.reshape(2, groups, SSD_GROUP_HEADS).transpose(1, 0, 2).reshape(groups, 2 * SSD_GROUP_HEADS)
    return jnp.pad(per_group, ((0, 0), (0, LANES - 2 * SSD_GROUP_HEADS))).reshape(1, groups * LANES)


def _gmlp_body(u_ref, v_ref, lng_ref, lnb_ref, ws_ref, bs_ref, o_ref, *, tm):
    groups = ws_ref.shape[0]
    gch = u_ref.shape[1] // groups
    for c in range(tm // CHUNK):
        rows = slice(c * CHUNK, (c + 1) * CHUNK)
        v = _gelu(v_ref[rows, :])
        vc = v - jnp.mean(v, axis=-1, keepdims=True)
        vn = vc * lax.rsqrt(jnp.mean(vc * vc, axis=-1, keepdims=True) + EPS) * lng_ref[...] + lnb_ref[...]
        vb = vn.astype(BF16)
        for g in range(groups):
            cols = slice(g * gch, (g + 1) * gch)
            s = _dot(ws_ref[g].astype(BF16), vb[:, cols]) + bs_ref[:, cols]
            o_ref[rows, cols] = _gelu(u_ref[rows, cols]) * s


def _gmlp(u, ln_g, ln_b, w_s, b_lanes, *, tm=256):
    t = u.shape[0]
    e = ln_g.shape[1]
    return pl.pallas_call(
        functools.partial(_gmlp_body, tm=tm),
        grid=(t // tm,),
        in_specs=[pl.BlockSpec((tm, e), lambda i: (i, 0)),
                  pl.BlockSpec((tm, e), lambda i: (i, 1)),
                  pl.BlockSpec((1, e), lambda i: (0, 0)),
                  pl.BlockSpec((1, e), lambda i: (0, 0)),
                  pl.BlockSpec(w_s.shape, lambda i: (0, 0, 0)),
                  pl.BlockSpec((CHUNK, e), lambda i: (0, 0))],
        out_specs=pl.BlockSpec((tm, e), lambda i: (i, 0)),
        out_shape=jax.ShapeDtypeStruct((t, e), F32),
        compiler_params=_params("parallel"),
        name="gmlp",
    )(u, u, ln_g, ln_b, w_s, b_lanes)


def _s5_body(lr_ref, li_ref, ls_ref, brt_ref, bit_ref, cre_ref, cim_ref, h0_ref, uc_ref, ul_ref,
             yc_ref, yl_ref, fin_ref, win_s, t_s, ef_s, eb_s, z_s, d_s, spf_s, spb_s, *, n_ctx, j_ctx, n_lat, j_lat):
    st2 = 2 * S5_STATE
    tg = S5_T * S5_GROUP
    low = lax.broadcasted_iota(jnp.int32, (1, st2), 1) < S5_STATE
    tau = lax.broadcasted_iota(jnp.int32, (S5_T + 8, 1), 0).astype(F32)
    e_refs = (ef_s, eb_s)
    a16 = []
    t_s[...] = jnp.zeros(t_s.shape, BF16)
    for d in range(2):
        lr, li = lr_ref[d], li_ref[d]
        step = jnp.exp(ls_ref[d])
        mag = jnp.exp(tau * (lr * step))
        p_re = mag * jnp.cos(tau * (li * step))
        p_im = mag * jnp.sin(tau * (li * step))
        ab_re, ab_im = p_re[1:2], p_im[1:2]
        den = lr * lr + li * li
        nr = ab_re - 1.0
        cr = (nr * lr + ab_im * li) / den
        ci = (ab_im * lr - nr * li) / den
        brt, bit = brt_ref[d], bit_ref[d]
        bb_re = cr * brt - ci * bit
        bb_im = cr * bit + ci * brt
        cre, cim = cre_ref[d], cim_ref[d]
        q_re, q_im, e_re, e_im = [], [], [], []
        for k in range(S5_T):
            tq = S5_T - 1 - k if d == 0 else k
            te = k + 1 if d == 0 else S5_T - k
            pr, pi = p_re[tq:tq + 1], p_im[tq:tq + 1]
            q_re.append(pr * bb_re - pi * bb_im)
            q_im.append(pr * bb_im + pi * bb_re)
            pr, pi = p_re[te:te + 1], p_im[te:te + 1]
            e_re.append(cre * pr - cim * pi)
            e_im.append(-(cre * pi + cim * pr))
        q_re, q_im = jnp.concatenate(q_re, axis=0), jnp.concatenate(q_im, axis=0)
        e_re, e_im = jnp.concatenate(e_re, axis=0), jnp.concatenate(e_im, axis=0)
        c_tile_re = jnp.concatenate([cre] * S5_T, axis=0).astype(BF16)
        c_tile_im = jnp.concatenate([-cim] * S5_T, axis=0).astype(BF16)
        for gl in range(2):
            m = low if gl == 0 else jnp.logical_not(low)
            rows = slice(gl * tg, (gl + 1) * tg)
            wq_re = jnp.where(m, q_re, 0.0).astype(BF16)
            wq_im = jnp.where(m, q_im, 0.0).astype(BF16)
            win_s[rows, 2 * d * st2:(2 * d + 1) * st2] = wq_re
            win_s[rows, (2 * d + 1) * st2:(2 * d + 2) * st2] = wq_im
            e_refs[d][rows, 0:st2] = jnp.where(m, e_re, 0.0).astype(BF16)
            e_refs[d][rows, st2:2 * st2] = jnp.where(m, e_im, 0.0).astype(BF16)
            z = _dot_nt(wq_re, c_tile_re) + _dot_nt(wq_im, c_tile_im)
            if d == 0:
                z_s[gl, 0:tg, :] = z
            else:
                z_s[gl, tg - S5_GROUP:tg, :] += z[0:S5_GROUP]
                z_s[gl, tg:2 * tg - S5_GROUP, :] = z[S5_GROUP:tg]
        a16.append((p_re[S5_T:S5_T + 1], p_im[S5_T:S5_T + 1]))

    col_blk = lax.broadcasted_iota(jnp.int32, (tg, tg), 1) // S5_GROUP
    for gl in range(2):
        acc = jnp.zeros((tg, tg), F32)
        for k in range(S5_T):
            r0 = S5_GROUP * (S5_T - 1 - k)
            acc = jnp.where(col_blk == k, z_s[gl, r0:r0 + tg, :], acc)
        t_s[gl * tg:(gl + 1) * tg, gl * tg:(gl + 1) * tg] = acc.astype(BF16)

    def run(u_ref, y_ref, nb, nj, init, fin_out):
        r = nb * nj
        u = u_ref[...]
        d_s[0:r, :] = _dot(u, win_s[...])
        for d in range(2):
            a_re, a_im = a16[d]
            sp = spf_s if d == 0 else spb_s
            s_re, s_im = init[d]
            order = range(nj) if d == 0 else range(nj - 1, -1, -1)
            for j in order:
                rows = slice(j * nb, (j + 1) * nb)
                sp[rows, 0:st2] = s_re
                sp[rows, st2:2 * st2] = s_im
                d_re = d_s[rows, 2 * d * st2:(2 * d + 1) * st2]
                d_im = d_s[rows, (2 * d + 1) * st2:(2 * d + 2) * st2]
                s_re, s_im = a_re * s_re - a_im * s_im + d_re, a_re * s_im + a_im * s_re + d_im
            if fin_out is not None:
                fin_out[d, 0] = s_re
                fin_out[d, 1] = s_im
        y_ref[...] = (_dot(u, t_s[...]) + _dot_nt(spf_s[0:r, :].astype(BF16), ef_s[...])
                      + _dot_nt(spb_s[0:r, :].astype(BF16), eb_s[...]))

    zero = jnp.zeros((n_ctx, st2), F32)
    run(uc_ref, yc_ref, n_ctx, j_ctx, ((zero, zero), (zero, zero)), fin_ref)
    run(ul_ref, yl_ref, n_lat, j_lat, ((h0_ref[0, 0], h0_ref[0, 1]), (h0_ref[1, 0], h0_ref[1, 1])), None)


def _s5_core(prm, h0, u_ctx, u_lat, *, n_ctx, j_ctx, n_lat, j_lat):
    pairs = prm[0].shape[0]
    st2 = 2 * S5_STATE
    rc, rl = n_ctx * j_ctx, n_lat * j_lat
    rmax = max(rc, rl)
    vec = pl.BlockSpec((None, 2, 1, st2), lambda p: (p, 0, 0, 0))
    mat = pl.BlockSpec((None, 2, S5_GROUP, st2), lambda p: (p, 0, 0, 0))
    return pl.pallas_call(
        functools.partial(_s5_body, n_ctx=n_ctx, j_ctx=j_ctx, n_lat=n_lat, j_lat=j_lat),
        grid=(pairs,),
        in_specs=[vec, vec, vec, mat, mat, mat, mat,
                  pl.BlockSpec((None, 2, 2, n_lat, st2), lambda p: (p, 0, 0, 0, 0)),
                  pl.BlockSpec((rc, S5_PAIR_CH), lambda p: (0, p)),
                  pl.BlockSpec((rl, S5_PAIR_CH), lambda p: (0, p))],
        out_specs=[pl.BlockSpec((rc, S5_PAIR_CH), lambda p: (0, p)),
                   pl.BlockSpec((rl, S5_PAIR_CH), lambda p: (0, p)),
                   pl.BlockSpec((None, 2, 2, n_ctx, st2), lambda p: (p, 0, 0, 0, 0))],
        out_shape=[jax.ShapeDtypeStruct((rc, pairs * S5_PAIR_CH), F32),
                   jax.ShapeDtypeStruct((rl, pairs * S5_PAIR_CH), F32),
                   jax.ShapeDtypeStruct((pairs, 2, 2, n_ctx, st2), F32)],
        scratch_shapes=[pltpu.VMEM((S5_PAIR_CH, 4 * st2), BF16),
                        pltpu.VMEM((S5_PAIR_CH, S5_PAIR_CH), BF16),
                        pltpu.VMEM((S5_PAIR_CH, 2 * st2), BF16),
                        pltpu.VMEM((S5_PAIR_CH, 2 * st2), BF16),
                        pltpu.VMEM((2, 2 * S5_T * S5_GROUP, S5_T * S5_GROUP), F32),
                        pltpu.VMEM((rmax, 4 * st2), F32),
                        pltpu.VMEM((rmax, 2 * st2), F32),
                        pltpu.VMEM((rmax, 2 * st2), F32)],
        compiler_params=_params("parallel"),
        name="s5_core",
    )(*prm, h0, u_ctx, u_lat)


def _s5_regroup(uu, nb, seq_len):
    nj = seq_len // S5_T
    groups = uu.shape[1] // S5_GROUP
    a = uu.reshape(nb, nj, S5_T, groups, S5_GROUP).transpose(1, 0, 3, 2, 4)
    return a.reshape(nj * nb, groups * S5_T * S5_GROUP).astype(BF16)


def _s5_ungroup(yt, nb, seq_len):
    nj = seq_len // S5_T
    groups = yt.shape[1] // (S5_T * S5_GROUP)
    a = yt.reshape(nj, nb, groups, S5_T, S5_GROUP).transpose(1, 0, 3, 2, 4)
    return a.reshape(nb * seq_len, groups * S5_GROUP)


def _s5_glu_body(ys_ref, u_ref, dsk_ref, w_ref, b_ref, o_ref):
    y = _gelu(ys_ref[...] + dsk_ref[...] * u_ref[...])
    o_ref[...] = y * jax.nn.sigmoid(_dot(y.astype(BF16), w_ref[...]) + b_ref[...])


def _s5_glu(ys, u, dsk, w, b, *, tm=512):
    t, e = ys.shape
    return pl.pallas_call(
        _s5_glu_body,
        grid=(t // tm,),
        in_specs=[pl.BlockSpec((tm, e), lambda i: (i, 0)),
                  pl.BlockSpec((tm, e), lambda i: (i, 0)),
                  pl.BlockSpec((1, e), lambda i: (0, 0)),
                  pl.BlockSpec((e, e), lambda i: (0, 0)),
                  pl.BlockSpec((1, e), lambda i: (0, 0))],
        out_specs=pl.BlockSpec((tm, e), lambda i: (i, 0)),
        out_shape=jax.ShapeDtypeStruct((t, e), F32),
        compiler_params=_params("parallel"),
        name="s5_glu",
    )(ys, u, dsk, w, b)


def _ctx_attn_body(q_ref, k_ref, v_ref, o_ref, ko_ref, vo_ref):
    scale = HEAD_DIM ** -0.5
    outs = []
    for h in range(LANES // HEAD_DIM):
        cols = slice(h * HEAD_DIM, (h + 1) * HEAD_DIM)
        q, k, v = q_ref[:, cols], k_ref[:, cols], v_ref[:, cols]
        ko_ref[h] = k
        vo_ref[h] = v
        s = _dot_nt(q.astype(BF16), k.astype(BF16)) * scale
        p = jnp.exp(s - jnp.max(s, axis=-1, keepdims=True))
        p = p / jnp.sum(p, axis=-1, keepdims=True)
        outs.append(_dot(p.astype(BF16), v.astype(BF16)))
    o_ref[...] = jnp.concatenate(outs, axis=1)


def _ctx_attention(u, y_prev, *, n_seq, seq_len, e):
    t = u.shape[0]
    hp = e // LANES
    heads = e // HEAD_DIM
    kv_block = (None, LANES // HEAD_DIM, seq_len, HEAD_DIM)
    in_specs = [pl.BlockSpec((seq_len, LANES), lambda b, h: (b, h)),
                pl.BlockSpec((seq_len, LANES), lambda b, h: (b, hp + h)),
                pl.BlockSpec((seq_len, LANES), lambda b, h: (b, 2 * hp + h))]
    return pl.pallas_call(
        _ctx_attn_body,
        grid=(n_seq, hp),
        in_specs=in_specs,
        out_specs=[pl.BlockSpec((seq_len, LANES), lambda b, h: (b, h)),
                   pl.BlockSpec(kv_block, lambda b, h: (b, h, 0, 0)),
                   pl.BlockSpec(kv_block, lambda b, h: (b, h, 0, 0))],
        out_shape=[jax.ShapeDtypeStruct((t, e), F32),
                   jax.ShapeDtypeStruct((n_seq, heads, seq_len, HEAD_DIM), F32),
                   jax.ShapeDtypeStruct((n_seq, heads, seq_len, HEAD_DIM), F32)],
        compiler_params=_params("parallel", "parallel"),
        name="ctx_attention",
    )(u, u, u)


def _nat_bias_body(rpb_ref, o_ref, *, n_rel_rows, n_rel_cols):
    h = pl.program_id(0)
    q = lax.broadcasted_iota(jnp.int32, (GRID_W, GRID_W), 0)
    kc = lax.broadcasted_iota(jnp.int32, (GRID_W, GRID_W), 1)
    rel = kc - q + (WIN_COLS - 1)
    c_start = jnp.clip(q - WIN_COLS // 2, 0, GRID_W - WIN_COLS)
    ok = (kc >= c_start) & (kc < c_start + WIN_COLS)
    for i in range(n_rel_rows):
        acc = jnp.zeros((GRID_W, GRID_W), F32)
        for dcol in range(n_rel_cols):
            acc = jnp.where(rel == dcol, rpb_ref[(h * n_rel_rows + i) * n_rel_cols + dcol], acc)
        tile = jnp.where(ok, acc, MASKED)
        if i < n_rel_rows - 1:
            o_ref[i, :, 0:GRID_W] = tile
        if i > 0:
            o_ref[i - 1, :, GRID_W:2 * GRID_W] = tile


def _nat_bias(rpb):
    heads, n_rel_rows, n_rel_cols = rpb.shape
    return pl.pallas_call(
        functools.partial(_nat_bias_body, n_rel_rows=n_rel_rows, n_rel_cols=n_rel_cols),
        grid=(heads,),
        in_specs=[pl.BlockSpec(memory_space=pltpu.SMEM)],
        out_specs=pl.BlockSpec((None, n_rel_rows - 1, GRID_W, 2 * GRID_W), lambda h: (h, 0, 0, 0)),
        out_shape=jax.ShapeDtypeStruct((heads, n_rel_rows - 1, GRID_W, 2 * GRID_W), F32),
        compiler_params=_params("parallel"),
        name="nat_bias",
    )(rpb.reshape(-1))


def _nat_body(q_ref, k_ref, v_ref, ck_ref, cv_ref, bias_ref, o_ref, *, rows):
    scale = HEAD_DIM ** -0.5
    wr = min(WIN_ROWS, rows)
    nw = wr * GRID_W

    def one_row(r, carry):
        r_start = jnp.clip(r - wr // 2, 0, rows - wr)
        i0 = (WIN_ROWS - 1) - (r - r_start)
        q_rows = pl.ds(pl.multiple_of(r * GRID_W, GRID_W), GRID_W)
        k_rows = pl.ds(pl.multiple_of(r_start * GRID_W, GRID_W), nw)
        q2, k2, v2 = q_ref[q_rows, :], k_ref[k_rows, :], v_ref[k_rows, :]
        outs = []
        for h in range(LANES // HEAD_DIM):
            cols = slice(h * HEAD_DIM, (h + 1) * HEAD_DIM)
            q = q2[:, cols].astype(BF16)
            bias = jnp.concatenate([bias_ref[h, i0 + 2 * jj] for jj in range(wr // 2)], axis=1)
            s_win = _dot_nt(q, k2[:, cols].astype(BF16)) * scale + bias
            s_ctx = _dot_nt(q, ck_ref[h].astype(BF16)) * scale
            m = jnp.maximum(jnp.max(s_win, axis=-1, keepdims=True), jnp.max(s_ctx, axis=-1, keepdims=True))
            p_win = jnp.exp(s_win - m)
            p_ctx = jnp.exp(s_ctx - m)
            den = jnp.sum(p_win, axis=-1, keepdims=True) + jnp.sum(p_ctx, axis=-1, keepdims=True)
            inv = 1.0 / den
            outs.append(_dot((p_win * inv).astype(BF16), v2[:, cols].astype(BF16))
                        + _dot((p_ctx * inv).astype(BF16), cv_ref[h].astype(BF16)))
        o_ref[q_rows, :] = jnp.concatenate(outs, axis=1)
        return carry

    lax.fori_loop(0, rows, one_row, 0)


def _nat_attention(u, cache_k, cache_v, bias, y_prev, *, n_seq, seq_len, row_block0, e):
    t = u.shape[0]
    hp = e // LANES
    hpb = LANES // HEAD_DIM
    past = cache_k.shape[2]
    seq = lambda b: row_block0 + b
    cache_block = (None, hpb, past, HEAD_DIM)
    return pl.pallas_call(
        lambda q, k, v, ck, cv, bs, yp, o: _nat_body(q, k, v, ck, cv, bs, o, rows=seq_len // GRID_W),
        grid=(n_seq, hp),
        in_specs=[pl.BlockSpec((seq_len, LANES), lambda b, h: (seq(b), h)),
                  pl.BlockSpec((seq_len, LANES), lambda b, h: (seq(b), hp + h)),
                  pl.BlockSpec((seq_len, LANES), lambda b, h: (seq(b), 2 * hp + h)),
                  pl.BlockSpec(cache_block, lambda b, h: (b, h, 0, 0)),
                  pl.BlockSpec(cache_block, lambda b, h: (b, h, 0, 0)),
                  pl.BlockSpec((hpb,) + bias.shape[1:], lambda b, h: (h, 0, 0, 0)),
                  pl.BlockSpec(memory_space=pl.ANY)],
        out_specs=pl.BlockSpec((seq_len, LANES), lambda b, h: (seq(b), h)),
        out_shape=jax.ShapeDtypeStruct((t, e), F32),
        input_output_aliases={6: 0},
        compiler_params=_params("parallel", "parallel"),
        name="nat_attention",
    )(u, u, u, cache_k, cache_v, bias, y_prev)


def kernel(x_prompt, x_sample, state_ssd, state_s5, cache_k, cache_v, c, c_ctx, norm_g, w_mod, b_mod, w_out, final_g, ssd_w_in, ssd_conv_w, ssd_conv_b, ssd_dt_bias, ssd_a_log, ssd_d, ssd_norm_g, mlp_w_in, mlp_ln_g, mlp_ln_b, mlp_w_s, mlp_b_s, s5_w_in, s5_lam_re, s5_lam_im, s5_log_step, s5_b_re, s5_b_im, s5_c_re, s5_c_im, s5_d, s5_w_glu, s5_b_glu, nat_w_in, nat_rpb):
    n_ctx, l_ctx, d = x_prompt.shape
    n_lat, l_lat, _ = x_sample.shape
    t_ctx = n_ctx * l_ctx
    depth = norm_g.shape[0]
    e = w_out.shape[1]
    assert depth == 4 and l_lat % l_ctx == 0 and t_ctx % l_lat == 0
    tm = 512
    tiles = dict(tm=tm, t_ctx=t_ctx, l_lat=l_lat)

    x = jnp.concatenate([x_prompt.reshape(t_ctx, d), x_sample.reshape(n_lat * l_lat, d)], axis=0)
    cond8 = jnp.concatenate([c_ctx[None], c, jnp.zeros((8 - 1 - n_lat, d), F32)], axis=0)
    mods = _modulation(cond8, w_mod, b_mod).reshape(depth, 8, 3, 1, d)
    w_out_b = w_out.astype(BF16)

    groups = e // SSD_GROUP_CH
    n_main = 3 * e
    w_dt = ssd_w_in[0][:, n_main:].reshape(d, 2, groups, SSD_GROUP_HEADS).transpose(0, 2, 1, 3)
    w_dt = jnp.pad(w_dt.reshape(d, groups, 2 * SSD_GROUP_HEADS), ((0, 0), (0, 0), (0, LANES - 2 * SSD_GROUP_HEADS)))
    u = _inproj(x, mods, 0, norm_g[0:1], ssd_w_in[0][:, :n_main].astype(BF16), tn=512, name="ssd_in", **tiles)
    dt = _inproj(x, mods, 0, norm_g[0:1], w_dt.reshape(d, groups * LANES).astype(BF16), tn=groups * LANES,
                 name="ssd_dt_in", **tiles)
    ssd_args = (ssd_conv_w[0], ssd_conv_b[0:1], _group_lanes(ssd_dt_bias[0], groups),
                _group_lanes(ssd_a_log[0], groups), jnp.repeat(ssd_d[0], SSD_HEAD_DIM)[None])
    y, new_ssd = _ssd_call(u, dt, *ssd_args, None, None, n_seq=n_ctx, seq_len=l_ctx, row_block0=0,
                           want_final=True, name="ssd_ctx")
    y, = _ssd_call(u, dt, *ssd_args, y, state_ssd[:, 0], n_seq=n_lat, seq_len=l_lat, row_block0=t_ctx // l_lat,
                   want_final=False, name="ssd_lat")
    x = _outproj(y, u, 0, x, mods, 0, w_out_b[0], norm_g=ssd_norm_g[0:1], name="ssd_out", **tiles)

    u = _inproj(x, mods, 1, norm_g[1:2], mlp_w_in[0].astype(BF16), tn=512, name="mlp_in", **tiles)
    b_lanes = jnp.repeat(mlp_b_s[0].T, e // mlp_b_s.shape[1], axis=1)
    y = _gmlp(u, mlp_ln_g[0:1], mlp_ln_b[0:1], mlp_w_s[0], b_lanes)
    x = _outproj(y, u, 2, x, mods, 1, w_out_b[1], name="mlp_out", **tiles)

    u = _inproj(x, mods, 2, norm_g[2:3], s5_w_in[0].astype(BF16), tn=512, name="s5_in", **tiles)
    s5_groups = e // S5_GROUP
    pairs = s5_groups // 2
    st2 = 2 * S5_STATE

    def pair_vec(v):
        return v.reshape(2, pairs, 1, st2).transpose(1, 0, 2, 3)

    def pair_mat_b(v):
        return v.reshape(2, pairs, 2, S5_STATE, S5_GROUP).transpose(1, 0, 4, 2, 3).reshape(pairs, 2, S5_GROUP, st2)

    def pair_mat_c(v):
        return v.reshape(2, pairs, 2, S5_GROUP, S5_STATE).transpose(1, 0, 3, 2, 4).reshape(pairs, 2, S5_GROUP, st2)

    log_step = jnp.repeat(s5_log_step[0][:, :, None], S5_STATE, axis=2)
    prm = (pair_vec(s5_lam_re[0]), pair_vec(s5_lam_im[0]), pair_vec(log_step), pair_mat_b(s5_b_re[0]),
           pair_mat_b(s5_b_im[0]), pair_mat_c(s5_c_re[0]), pair_mat_c(s5_c_im[0]))
    h0 = state_s5[:, 0].reshape(n_lat, 2, 2, pairs, st2).transpose(3, 1, 2, 0, 4)
    uu = u[:, :e]
    y_ctx, y_lat, fin = _s5_core(prm, h0, _s5_regroup(uu[:t_ctx], n_ctx, l_ctx), _s5_regroup(uu[t_ctx:], n_lat, l_lat),
                                 n_ctx=n_ctx, j_ctx=l_ctx // S5_T, n_lat=n_lat, j_lat=l_lat // S5_T)
    ys = jnp.concatenate([_s5_ungroup(y_ctx, n_ctx, l_ctx), _s5_ungroup(y_lat, n_lat, l_lat)], axis=0)
    new_s5 = fin.transpose(3, 1, 2, 0, 4).reshape(n_ctx, 1, 2, 2, s5_groups, S5_STATE)
    y = _s5_glu(ys, u, s5_d[0:1], s5_w_glu[0].astype(BF16), s5_b_glu[0:1])
    x = _outproj(y, u, 1, x, mods, 2, w_out_b[2], name="s5_out", **tiles)

    u = _inproj(x, mods, 3, norm_g[3:4], nat_w_in[0].astype(BF16), tn=512, name="nat_in", **tiles)
    y, new_k, new_v = _ctx_attention(u, None, n_seq=n_ctx, seq_len=l_ctx, e=e)
    y = _nat_attention(u, cache_k[:, 0], cache_v[:, 0], _nat_bias(nat_rpb[0]), y, n_seq=n_lat, seq_len=l_lat,
                       row_block0=t_ctx // l_lat, e=e)
    out = _outproj(y, u, 3, x, mods, 3, w_out_b[3], final_g=final_g[None], name="nat_out", **tiles)

    return (out[:t_ctx].reshape(n_ctx, l_ctx, d), out[t_ctx:].reshape(n_lat, l_lat, d),
            new_ssd[:, None], new_s5, new_k[:, None], new_v[:, None])
```

```python
import functools

import jax
import jax.numpy as jnp
from jax import lax
from jax.experimental import pallas as pl
from jax.experimental.pallas import tpu as pltpu

F32 = jnp.float32
BF16 = jnp.bfloat16
EPS = 1e-6
HIGHEST = lax.Precision.HIGHEST

LANES = 128
CHUNK = 128
SSD_HEAD_DIM = 64
SSD_GROUP_HEADS = 4
SSD_GROUP_CH = SSD_HEAD_DIM * SSD_GROUP_HEADS
SSD_STATE = 128
CONV_W = 5
CONV_HALO = 8
S5_T = 8
S5_GROUP = 16
S5_STATE = 64
S5_BLOCK_ST = (LANES // S5_GROUP) * S5_STATE
HEAD_DIM = 64
GRID_W = 64
WIN_ROWS = 8
WIN_COLS = 16
MASKED = -1e30
VMEM_LIMIT = 52 * 1024 * 1024


def _silu(x):
    return x * jax.nn.sigmoid(x)


def _gelu(x):
    return 0.5 * x * (1.0 + jnp.tanh(0.7978845608028654 * (x + 0.044715 * (x * x * x))))


def _softplus(x):
    return jnp.maximum(x, 0.0) + jnp.log1p(jnp.exp(-jnp.abs(x)))


def _dot(a, b):
    return jnp.dot(a, b, preferred_element_type=F32)


def _dot_nt(a, b):
    return lax.dot_general(a, b, (((1,), (1,)), ((), ())), preferred_element_type=F32)


def _params(*sem):
    return pltpu.CompilerParams(dimension_semantics=sem, vmem_limit_bytes=VMEM_LIMIT)


def _mod_row(i, *, tm, t_ctx, l_lat):
    start = i * tm
    return jnp.where(start < t_ctx, 0, 1 + (start - t_ctx) // l_lat)


def _mod_body(c_ref, w_ref, b_ref, o_ref):
    cond = _silu(c_ref[...]).astype(BF16)
    o_ref[...] = _dot(cond, w_ref[...].astype(BF16)) + b_ref[...]


def _modulation(cond8, w_mod, b_mod, *, tn=1024):
    depth, d, n = w_mod.shape
    return pl.pallas_call(
        _mod_body,
        grid=(depth, n // tn),
        in_specs=[pl.BlockSpec((8, d), lambda l, j: (0, 0)),
                  pl.BlockSpec((None, d, tn), lambda l, j: (l, 0, j)),
                  pl.BlockSpec((None, 1, tn), lambda l, j: (l, 0, j))],
        out_specs=pl.BlockSpec((None, 8, tn), lambda l, j: (l, 0, j)),
        out_shape=jax.ShapeDtypeStruct((depth, 8, n), F32),
        compiler_params=_params("parallel", "parallel"),
        name="modulation",
    )(cond8, w_mod, b_mod.reshape(depth, 1, n))


def _inproj_body(x_ref, shift_ref, scale_ref, g_ref, w_ref, o_ref, h_ref):
    @pl.when(pl.program_id(1) == 0)
    def _():
        x = x_ref[...]
        y = x * lax.rsqrt(jnp.mean(x * x, axis=-1, keepdims=True) + EPS) * g_ref[...]
        h_ref[...] = (y * (1.0 + scale_ref[...]) + shift_ref[...]).astype(BF16)

    o_ref[...] = _dot(h_ref[...], w_ref[...])


def _inproj(x, mods, layer, g, w, *, tm, tn, t_ctx, l_lat, name):
    t, d = x.shape
    n = w.shape[1]
    row = functools.partial(_mod_row, tm=tm, t_ctx=t_ctx, l_lat=l_lat)

    def mod_spec(part):
        return pl.BlockSpec((None, None, None, 1, d), lambda i, j: (layer, row(i), part, 0, 0))

    return pl.pallas_call(
        _inproj_body,
        grid=(t // tm, n // tn),
        in_specs=[pl.BlockSpec((tm, d), lambda i, j: (i, 0)), mod_spec(0), mod_spec(1),
                  pl.BlockSpec((1, d), lambda i, j: (0, 0)),
                  pl.BlockSpec((d, tn), lambda i, j: (0, j))],
        out_specs=pl.BlockSpec((tm, tn), lambda i, j: (i, j)),
        out_shape=jax.ShapeDtypeStruct((t, n), F32),
        scratch_shapes=[pltpu.VMEM((tm, d), BF16)],
        compiler_params=_params("parallel", "arbitrary"),
        name=name,
    )(x, mods, mods, g, w)


def _outproj_body(*refs, gated_norm, final_norm):
    y_ref, z_ref, x_ref, gate_ref, w_ref = refs[:5]
    rest = list(refs[5:])
    t = y_ref[...] * _silu(z_ref[...])
    if gated_norm:
        ng_ref = rest.pop(0)
        t = t * lax.rsqrt(jnp.mean(t * t, axis=-1, keepdims=True) + EPS) * ng_ref[...]
    if final_norm:
        fg_ref = rest.pop(0)
    o_ref = rest.pop(0)
    xn = x_ref[...] + gate_ref[...] * _dot(t.astype(BF16), w_ref[...])
    if final_norm:
        xn = xn * lax.rsqrt(jnp.mean(xn * xn, axis=-1, keepdims=True) + EPS) * fg_ref[...]
    o_ref[...] = xn


def _outproj(y, u, z_block, x, mods, layer, w, *, norm_g=None, final_g=None, tm, t_ctx, l_lat, name):
    t, e = y.shape
    d = x.shape[1]
    row = functools.partial(_mod_row, tm=tm, t_ctx=t_ctx, l_lat=l_lat)
    in_specs = [pl.BlockSpec((tm, e), lambda i: (i, 0)),
                pl.BlockSpec((tm, e), lambda i: (i, z_block)),
                pl.BlockSpec((tm, d), lambda i: (i, 0)),
                pl.BlockSpec((None, None, None, 1, d), lambda i: (layer, row(i), 2, 0, 0)),
                pl.BlockSpec((e, d), lambda i: (0, 0))]
    args = [y, u, x, mods, w]
    if norm_g is not None:
        in_specs.append(pl.BlockSpec((1, e), lambda i: (0, 0)))
        args.append(norm_g)
    if final_g is not None:
        in_specs.append(pl.BlockSpec((1, d), lambda i: (0, 0)))
        args.append(final_g)
    return pl.pallas_call(
        functools.partial(_outproj_body, gated_norm=norm_g is not None, final_norm=final_g is not None),
        grid=(t // tm,),
        in_specs=in_specs,
        out_specs=pl.BlockSpec((tm, d), lambda i: (i, 0)),
        out_shape=jax.ShapeDtypeStruct((t, d), F32),
        compiler_params=_params("parallel"),
        name=name,
    )(*args)


def _ssd_body(*refs, seq_len, has_h0, want_final):
    refs = list(refs)
    x_ref, b_ref, c_ref, dt_ref, wx_ref, wb_ref, wc_ref, bx_ref, bb_ref, bc_ref, dtb_ref, alog_ref, dsk_ref = refs[:13]
    rest = refs[13:]
    h0_ref = rest.pop(0) if has_h0 else None
    if want_final:
        y_ref, hf_ref = rest[0], rest[1]
        rest = rest[2:]
    else:
        y_ref, hf_ref = rest[0], None
        rest = rest[1:]
    pad_s, xbc_s, dt_s, da_s, sf_s, sb_s = rest
    nc = seq_len // CHUNK
    width = SSD_GROUP_CH + 2 * SSD_STATE

    zeros = jnp.zeros((CONV_HALO, width), F32)
    pad_s[0:CONV_HALO, :] = zeros
    pad_s[CONV_HALO + seq_len:2 * CONV_HALO + seq_len, :] = zeros

    def copy_chunk(ci, carry):
        base = pl.multiple_of(ci * CHUNK, CHUNK)
        dst = pl.ds(base + CONV_HALO, CHUNK)
        pad_s[dst, 0:SSD_GROUP_CH] = x_ref[pl.ds(base, CHUNK), :]
        pad_s[dst, SSD_GROUP_CH:SSD_GROUP_CH + SSD_STATE] = b_ref[pl.ds(base, CHUNK), :]
        pad_s[dst, SSD_GROUP_CH + SSD_STATE:width] = c_ref[pl.ds(base, CHUNK), :]
        return carry

    lax.fori_loop(0, nc, copy_chunk, 0)

    conv_w = jnp.concatenate([wx_ref[...], wb_ref[...], wc_ref[...]], axis=1)
    conv_b = jnp.concatenate([bx_ref[...], bb_ref[...], bc_ref[...]], axis=1)
    a_row = -jnp.exp(alog_ref[...])
    first = CONV_HALO - CONV_W // 2

    def conv_chunk(ci, carry):
        base = pl.multiple_of(ci * CHUNK, CHUNK)
        win = pad_s[pl.ds(base, CHUNK + 2 * CONV_HALO), :]
        acc = conv_b + win[first:first + CHUNK] * conv_w[0:1]
        for j in range(1, CONV_W):
            acc = acc + win[first + j:first + j + CHUNK] * conv_w[j:j + 1]
        xbc = _silu(acc)
        rows = pl.ds(base, CHUNK)
        xbc_s[rows, :] = xbc
        y_ref[rows, :] = xbc[:, 0:SSD_GROUP_CH] * dsk_ref[...]
        dt = _softplus(dt_ref[rows, :] + dtb_ref[...])
        dt_s[rows, :] = dt
        da_s[rows, :] = dt * a_row
        return carry

    lax.fori_loop(0, nc, conv_chunk, 0)

    if has_h0:
        sf_s[...] = h0_ref[0].reshape(SSD_GROUP_CH, SSD_STATE)
        sb_s[...] = h0_ref[1].reshape(SSD_GROUP_CH, SSD_STATE)
    else:
        sf_s[...] = jnp.zeros((SSD_GROUP_CH, SSD_STATE), F32)
        sb_s[...] = jnp.zeros((SSD_GROUP_CH, SSD_STATE), F32)

    row_i = lax.broadcasted_iota(jnp.int32, (CHUNK, CHUNK), 0)
    col_i = lax.broadcasted_iota(jnp.int32, (CHUNK, CHUNK), 1)
    lower = row_i >= col_i
    upper = row_i <= col_i
    lower_f = lower.astype(F32)
    head_of_lane = lax.broadcasted_iota(jnp.int32, (CHUNK, SSD_GROUP_CH), 1) // SSD_HEAD_DIM
    head_of_row = lax.broadcasted_iota(jnp.int32, (SSD_GROUP_CH, SSD_STATE), 0) // SSD_HEAD_DIM

    def per_head_lanes(cols):
        out = cols[SSD_GROUP_HEADS - 1]
        for r in range(SSD_GROUP_HEADS - 2, -1, -1):
            out = jnp.where(head_of_lane == r, cols[r], out)
        return out

    def one_direction(ci, backward, st_ref):
        rows = pl.ds(pl.multiple_of(ci * CHUNK, CHUNK), CHUNK)
        xbc = xbc_s[rows, :]
        x = xbc[:, 0:SSD_GROUP_CH]
        bm = xbc[:, SSD_GROUP_CH:SSD_GROUP_CH + SSD_STATE].astype(BF16)
        cm = xbc[:, SSD_GROUP_CH + SSD_STATE:width].astype(BF16)
        dt = dt_s[rows, :]
        da = da_s[rows, :]
        cum = jnp.dot(lower_f, da, precision=HIGHEST, preferred_element_type=F32)
        e = cum - da if backward else cum
        e_t = e.T
        s_prev = st_ref[...]
        g = _dot_nt(cm, bm)
        cs = _dot_nt(cm, s_prev.astype(BF16))
        dt_cols, off_cols, st_cols, tots, decays = [], [], [], [], []
        for r in range(SSD_GROUP_HEADS):
            col = r + SSD_GROUP_HEADS * int(backward)
            ec = e[:, col:col + 1]
            er = e_t[col:col + 1, :]
            tot = cum[CHUNK - 1:CHUNK, col:col + 1]
            if backward:
                decays.append(jnp.exp(jnp.where(upper, er - ec, -jnp.inf)))
                off_cols.append(jnp.exp(tot - ec))
                st_cols.append(jnp.exp(ec))
            else:
                decays.append(jnp.exp(jnp.where(lower, ec - er, -jnp.inf)))
                off_cols.append(jnp.exp(ec))
                st_cols.append(jnp.exp(tot - ec))
            dt_cols.append(dt[:, col:col + 1])
            tots.append(jnp.exp(tot))
        xd = x * per_head_lanes(dt_cols)
        xd_b = xd.astype(BF16)
        y = cs * per_head_lanes(off_cols)
        for r in range(SSD_GROUP_HEADS):
            yd = _dot((g * decays[r]).astype(BF16), xd_b)
            y = y + jnp.where(head_of_lane == r, yd, 0.0)
        y_ref[rows, :] += y
        w_t = (xd * per_head_lanes(st_cols)).T.astype(BF16)
        keep = tots[SSD_GROUP_HEADS - 1]
        for r in range(SSD_GROUP_HEADS - 2, -1, -1):
            keep = jnp.where(head_of_row == r, tots[r], keep)
        st_ref[...] = keep * s_prev + _dot(w_t, bm)

    def scan_chunk(i, carry):
        one_direction(i, False, sf_s)
        one_direction(nc - 1 - i, True, sb_s)
        return carry

    lax.fori_loop(0, nc, scan_chunk, 0)

    if want_final:
        hf_ref[0] = sf_s[...].reshape(SSD_GROUP_HEADS, SSD_HEAD_DIM, SSD_STATE)
        hf_ref[1] = sb_s[...].reshape(SSD_GROUP_HEADS, SSD_HEAD_DIM, SSD_STATE)


def _ssd_call(u, dt, conv_w, conv_b, dtb, alog, dsk, y_prev, h0, *, n_seq, seq_len, row_block0, want_final, name):
    t, n_u = u.shape
    e = dsk.shape[1]
    groups = e // SSD_GROUP_CH
    xb0 = e // SSD_GROUP_CH
    bb0 = 2 * e // SSD_STATE
    cb0 = bb0 + groups
    has_h0 = h0 is not None
    seq = lambda b: row_block0 + b
    in_specs = [pl.BlockSpec((seq_len, SSD_GROUP_CH), lambda b, g: (seq(b), xb0 + g)),
                pl.BlockSpec((seq_len, SSD_STATE), lambda b, g: (seq(b), bb0 + g)),
                pl.BlockSpec((seq_len, SSD_STATE), lambda b, g: (seq(b), cb0 + g)),
                pl.BlockSpec((seq_len, LANES), lambda b, g: (seq(b), g)),
                pl.BlockSpec((CONV_W, SSD_GROUP_CH), lambda b, g: (0, g)),
                pl.BlockSpec((CONV_W, SSD_STATE), lambda b, g: (0, bb0 - xb0 * 2 + g)),
                pl.BlockSpec((CONV_W, SSD_STATE), lambda b, g: (0, cb0 - xb0 * 2 + g)),
                pl.BlockSpec((1, SSD_GROUP_CH), lambda b, g: (0, g)),
                pl.BlockSpec((1, SSD_STATE), lambda b, g: (0, bb0 - xb0 * 2 + g)),
                pl.BlockSpec((1, SSD_STATE), lambda b, g: (0, cb0 - xb0 * 2 + g)),
                pl.BlockSpec((1, LANES), lambda b, g: (0, g)),
                pl.BlockSpec((1, LANES), lambda b, g: (0, g)),
                pl.BlockSpec((1, SSD_GROUP_CH), lambda b, g: (0, g))]
    args = [u, u, u, dt, conv_w, conv_w, conv_w, conv_b, conv_b, conv_b, dtb, alog, dsk]
    state_block = (None, 2, SSD_GROUP_HEADS, SSD_HEAD_DIM, SSD_STATE)
    if has_h0:
        in_specs.append(pl.BlockSpec(state_block, lambda b, g: (b, 0, g, 0, 0)))
        args.append(h0)
    aliases = {}
    if y_prev is not None:
        in_specs.append(pl.BlockSpec(memory_space=pl.ANY))
        args.append(y_prev)
        aliases = {len(args) - 1: 0}
    out_specs = [pl.BlockSpec((seq_len, SSD_GROUP_CH), lambda b, g: (seq(b), g))]
    out_shape = [jax.ShapeDtypeStruct((t, e), F32)]
    if want_final:
        out_specs.append(pl.BlockSpec(state_block, lambda b, g: (b, 0, g, 0, 0)))
        out_shape.append(jax.ShapeDtypeStruct((n_seq, 2, e // SSD_HEAD_DIM, SSD_HEAD_DIM, SSD_STATE), F32))
    width = SSD_GROUP_CH + 2 * SSD_STATE

    def body(*refs):
        refs = list(refs)
        if y_prev is not None:
            n_in = len(args)
            del refs[n_in - 1]
        _ssd_body(*refs, seq_len=seq_len, has_h0=has_h0, want_final=want_final)

    return pl.pallas_call(
        body,
        grid=(n_seq, groups),
        in_specs=in_specs,
        out_specs=out_specs,
        out_shape=out_shape,
        input_output_aliases=aliases,
        scratch_shapes=[pltpu.VMEM((seq_len + 2 * CONV_HALO, width), F32),
                        pltpu.VMEM((seq_len, width), F32),
                        pltpu.VMEM((seq_len, LANES), F32),
                        pltpu.VMEM((seq_len, LANES), F32),
                        pltpu.VMEM((SSD_GROUP_CH, SSD_STATE), F32),
                        pltpu.VMEM((SSD_GROUP_CH, SSD_STATE), F32)],
        compiler_params=_params("parallel", "parallel"),
        name=name,
    )(*args)


def _group_lanes(v, groups):
    per_group = v.reshape(2, groups, SSD_GROUP_HEADS).transpose(1, 0, 2).reshape(groups, 2 * SSD_GROUP_HEADS)
    return jnp.pad(per_group, ((0, 0), (0, LANES - 2 * SSD_GROUP_HEADS))).reshape(1, groups * LANES)


def _gmlp_body(u_ref, v_ref, lng_ref, lnb_ref, ws_ref, bs_ref, o_ref, *, tm):
    groups = ws_ref.shape[0]
    gch = u_ref.shape[1] // groups
    for c in range(tm // CHUNK):
        rows = slice(c * CHUNK, (c + 1) * CHUNK)
        v = _gelu(v_ref[rows, :])
        vc = v - jnp.mean(v, axis=-1, keepdims=True)
        vn = vc * lax.rsqrt(jnp.mean(vc * vc, axis=-1, keepdims=True) + EPS) * lng_ref[...] + lnb_ref[...]
        vb = vn.astype(BF16)
        for g in range(groups):
            cols = slice(g * gch, (g + 1) * gch)
            s = _dot(ws_ref[g].astype(BF16), vb[:, cols]) + bs_ref[:, cols]
            o_ref[rows, cols] = _gelu(u_ref[rows, cols]) * s


def _gmlp(u, ln_g, ln_b, w_s, b_lanes, *, tm=256):
    t = u.shape[0]
    e = ln_g.shape[1]
    return pl.pallas_call(
        functools.partial(_gmlp_body, tm=tm),
        grid=(t // tm,),
        in_specs=[pl.BlockSpec((tm, e), lambda i: (i, 0)),
                  pl.BlockSpec((tm, e), lambda i: (i, 1)),
                  pl.BlockSpec((1, e), lambda i: (0, 0)),
                  pl.BlockSpec((1, e), lambda i: (0, 0)),
                  pl.BlockSpec(w_s.shape, lambda i: (0, 0, 0)),
                  pl.BlockSpec((CHUNK, e), lambda i: (0, 0))],
        out_specs=pl.BlockSpec((tm, e), lambda i: (i, 0)),
        out_shape=jax.ShapeDtypeStruct((t, e), F32),
        compiler_params=_params("parallel"),
        name="gmlp",
    )(u, u, ln_g, ln_b, w_s, b_lanes)


def _s5_body(lr_ref, li_ref, ls_ref, brt_ref, bit_ref, cre_ref, cim_ref, h0_ref, u_ref, y_ref, fin_ref,
             win_s, t_s, ef_s, eb_s, z_s, a_s, zin_s, d_s, spf_s, spb_s, yo_s,
             *, seqs, l_ctx, n_lat, l_lat, ctx_parts):
    bst = S5_BLOCK_ST
    ng = LANES // S5_GROUP
    kw = S5_T * LANES
    part = pl.program_id(1)

    @pl.when(part == 0)
    def _build():
        own = (lax.broadcasted_iota(jnp.int32, (ng, S5_GROUP, bst), 2) // S5_STATE
               == lax.broadcasted_iota(jnp.int32, (ng, S5_GROUP, bst), 0))

        def spread(v):
            return jnp.where(own, v[None], 0.0).reshape(LANES, bst).astype(BF16)

        tau = lax.broadcasted_iota(jnp.int32, (S5_T + 8, 1), 0).astype(F32)
        e_refs = (ef_s, eb_s)
        for d in range(2):
            lr, li = lr_ref[d], li_ref[d]
            step = jnp.exp(ls_ref[d])
            mag = jnp.exp(tau * (lr * step))
            p_re = mag * jnp.cos(tau * (li * step))
            p_im = mag * jnp.sin(tau * (li * step))
            ab_re, ab_im = p_re[1:2], p_im[1:2]
            den = lr * lr + li * li
            nr = ab_re - 1.0
            cr = (nr * lr + ab_im * li) / den
            ci = (ab_im * lr - nr * li) / den
            brt, bit = brt_ref[d], bit_ref[d]
            bb_re = cr * brt - ci * bit
            bb_im = cr * bit + ci * brt
            cre, cim = cre_ref[d], cim_ref[d]
            for k in range(S5_T):
                rows = slice(k * LANES, (k + 1) * LANES)
                tq = S5_T - 1 - k if d == 0 else k
                te = k + 1 if d == 0 else S5_T - k
                pr, pi = p_re[tq:tq + 1], p_im[tq:tq + 1]
                win_s[rows, 2 * d * bst:(2 * d + 1) * bst] = spread(pr * bb_re - pi * bb_im)
                win_s[rows, (2 * d + 1) * bst:(2 * d + 2) * bst] = spread(pr * bb_im + pi * bb_re)
                pr, pi = p_re[te:te + 1], p_im[te:te + 1]
                e_refs[d][rows, 0:bst] = spread(cre * pr - cim * pi)
                e_refs[d][rows, bst:2 * bst] = spread(-(cre * pi + cim * pr))
            c_own = jnp.concatenate([spread(cre), spread(-cim)], axis=1)
            z = _dot_nt(win_s[:, 2 * d * bst:(2 * d + 2) * bst], c_own)
            if d == 0:
                z_s[0:kw, :] = z
            else:
                z_s[kw - LANES:kw, :] += z[0:LANES]
                z_s[kw:2 * kw - LANES, :] = z[LANES:kw]
            a_s[2 * d:2 * d + 1, :] = p_re[S5_T:S5_T + 1]
            a_s[2 * d + 1:2 * d + 2, :] = p_im[S5_T:S5_T + 1]
        for k in range(S5_T):
            r0 = (S5_T - 1 - k) * LANES
            t_s[:, k * LANES:(k + 1) * LANES] = z_s[r0:r0 + kw, :].astype(BF16)

    def outputs():
        yo_s[...] = (_dot(zin_s[...], t_s[...]) + _dot_nt(spf_s[...].astype(BF16), ef_s[...])
                     + _dot_nt(spb_s[...].astype(BF16), eb_s[...]))

    @pl.when(part < ctx_parts)
    def _context():
        nj = l_ctx // S5_T
        for j in range(nj):
            for k in range(S5_T):
                zin_s[j * seqs:(j + 1) * seqs, k * LANES:(k + 1) * LANES] = (
                    u_ref[pl.ds(j * S5_T + k, seqs, stride=l_ctx), :].astype(BF16))
        d_s[...] = _dot(zin_s[...], win_s[...])
        coef = [jnp.broadcast_to(a_s[i:i + 1, :], (seqs, bst)) for i in range(4)]
        zero = jnp.zeros((seqs, bst), F32)

        def step(j, carry):
            fr, fi, br, bi = carry
            rf = pl.ds(pl.multiple_of(j * seqs, seqs), seqs)
            rb = pl.ds(pl.multiple_of((nj - 1 - j) * seqs, seqs), seqs)
            spf_s[rf, 0:bst] = fr
            spf_s[rf, bst:2 * bst] = fi
            spb_s[rb, 0:bst] = br
            spb_s[rb, bst:2 * bst] = bi
            return (coef[0] * fr - coef[1] * fi + d_s[rf, 0:bst],
                    coef[0] * fi + coef[1] * fr + d_s[rf, bst:2 * bst],
                    coef[2] * br - coef[3] * bi + d_s[rb, 2 * bst:3 * bst],
                    coef[2] * bi + coef[3] * br + d_s[rb, 3 * bst:4 * bst])

        fr, fi, br, bi = lax.fori_loop(0, nj, step, (zero, zero, zero, zero))
        fin_ref[0, 0] = fr
        fin_ref[0, 1] = fi
        fin_ref[1, 0] = br
        fin_ref[1, 1] = bi
        outputs()
        for j in range(nj):
            for k in range(S5_T):
                y_ref[pl.ds(j * S5_T + k, seqs, stride=l_ctx), :] = (
                    yo_s[j * seqs:(j + 1) * seqs, k * LANES:(k + 1) * LANES])

    @pl.when(part == ctx_parts)
    def _latent():
        nj = l_lat // S5_T
        for b in range(n_lat):
            for k in range(S5_T):
                zin_s[b * nj:(b + 1) * nj, k * LANES:(k + 1) * LANES] = (
                    u_ref[pl.ds(b * l_lat + k, nj, stride=S5_T), :].astype(BF16))
        d_s[...] = _dot(zin_s[...], win_s[...])
        coef_re = jnp.concatenate([jnp.broadcast_to(a_s[0:1, :], (n_lat, bst)),
                                   jnp.broadcast_to(a_s[2:3, :], (n_lat, bst))], axis=0)
        coef_im = jnp.concatenate([jnp.broadcast_to(a_s[1:2, :], (n_lat, bst)),
                                   jnp.broadcast_to(a_s[3:4, :], (n_lat, bst))], axis=0)
        s_re0 = jnp.concatenate([h0_ref[0, 0], h0_ref[1, 0]], axis=0)
        s_im0 = jnp.concatenate([h0_ref[0, 1], h0_ref[1, 1]], axis=0)

        def step(j, carry):
            s_re, s_im = carry
            d_re, d_im = [], []
            for b in range(n_lat):
                row = pl.ds(b * nj + j, 1)
                spf_s[row, 0:bst] = s_re[b:b + 1]
                spf_s[row, bst:2 * bst] = s_im[b:b + 1]
                d_re.append(d_s[row, 0:bst])
                d_im.append(d_s[row, bst:2 * bst])
            for b in range(n_lat):
                row = pl.ds(b * nj + nj - 1 - j, 1)
                spb_s[row, 0:bst] = s_re[n_lat + b:n_lat + b + 1]
                spb_s[row, bst:2 * bst] = s_im[n_lat + b:n_lat + b + 1]
                d_re.append(d_s[row, 2 * bst:3 * bst])
                d_im.append(d_s[row, 3 * bst:4 * bst])
            d_re = jnp.concatenate(d_re, axis=0)
            d_im = jnp.concatenate(d_im, axis=0)
            return coef_re * s_re - coef_im * s_im + d_re, coef_re * s_im + coef_im * s_re + d_im

        lax.fori_loop(0, nj, step, (s_re0, s_im0))
        outputs()
        for b in range(n_lat):
            for k in range(S5_T):
                y_ref[pl.ds(b * l_lat + k, nj, stride=S5_T), :] = yo_s[b * nj:(b + 1) * nj, k * LANES:(k + 1) * LANES]


def _s5_core(prm, h0, u, *, n_ctx, l_ctx, n_lat, l_lat, e):
    blocks = prm[0].shape[0]
    part_tokens = n_lat * l_lat
    t = u.shape[0]
    n_parts = t // part_tokens
    seqs = part_tokens // l_ctx
    r = part_tokens // S5_T
    vec = pl.BlockSpec((None, 2, 1, S5_BLOCK_ST), lambda g, p: (g, 0, 0, 0))
    mat = pl.BlockSpec((None, 2, S5_GROUP, S5_BLOCK_ST), lambda g, p: (g, 0, 0, 0))
    k_in = S5_T * LANES
    return pl.pallas_call(
        functools.partial(_s5_body, seqs=seqs, l_ctx=l_ctx, n_lat=n_lat, l_lat=l_lat, ctx_parts=n_parts - 1),
        grid=(blocks, n_parts),
        in_specs=[vec, vec, vec, mat, mat, mat, mat,
                  pl.BlockSpec((None, 2, 2, n_lat, S5_BLOCK_ST), lambda g, p: (g, 0, 0, 0, 0)),
                  pl.BlockSpec((part_tokens, LANES), lambda g, p: (p, g))],
        out_specs=[pl.BlockSpec((part_tokens, LANES), lambda g, p: (p, g)),
                   pl.BlockSpec((None, 2, 2, seqs, S5_BLOCK_ST),
                                lambda g, p: (g, 0, 0, jnp.minimum(p, n_parts - 2), 0))],
        out_shape=[jax.ShapeDtypeStruct((t, e), F32),
                   jax.ShapeDtypeStruct((blocks, 2, 2, n_ctx, S5_BLOCK_ST), F32)],
        scratch_shapes=[pltpu.VMEM((k_in, 4 * S5_BLOCK_ST), BF16),
                        pltpu.VMEM((k_in, k_in), BF16),
                        pltpu.VMEM((k_in, 2 * S5_BLOCK_ST), BF16),
                        pltpu.VMEM((k_in, 2 * S5_BLOCK_ST), BF16),
                        pltpu.VMEM(((2 * S5_T - 1) * LANES, LANES), F32),
                        pltpu.VMEM((8, S5_BLOCK_ST), F32),
                        pltpu.VMEM((r, k_in), BF16),
                        pltpu.VMEM((r, 4 * S5_BLOCK_ST), F32),
                        pltpu.VMEM((r, 2 * S5_BLOCK_ST), F32),
                        pltpu.VMEM((r, 2 * S5_BLOCK_ST), F32),
                        pltpu.VMEM((r, k_in), F32)],
        compiler_params=_params("parallel", "arbitrary"),
        name="s5_core",
    )(*prm, h0, u)


def _s5_glu_body(ys_ref, u_ref, dsk_ref, w_ref, b_ref, o_ref):
    y = _gelu(ys_ref[...] + dsk_ref[...] * u_ref[...])
    o_ref[...] = y * jax.nn.sigmoid(_dot(y.astype(BF16), w_ref[...]) + b_ref[...])


def _s5_glu(ys, u, dsk, w, b, *, tm=512):
    t, e = ys.shape
    return pl.pallas_call(
        _s5_glu_body,
        grid=(t // tm,),
        in_specs=[pl.BlockSpec((tm, e), lambda i: (i, 0)),
                  pl.BlockSpec((tm, e), lambda i: (i, 0)),
                  pl.BlockSpec((1, e), lambda i: (0, 0)),
                  pl.BlockSpec((e, e), lambda i: (0, 0)),
                  pl.BlockSpec((1, e), lambda i: (0, 0))],
        out_specs=pl.BlockSpec((tm, e), lambda i: (i, 0)),
        out_shape=jax.ShapeDtypeStruct((t, e), F32),
        compiler_params=_params("parallel"),
        name="s5_glu",
    )(ys, u, dsk, w, b)


def _ctx_attn_body(q_ref, k_ref, v_ref, o_ref, ko_ref, vo_ref):
    scale = HEAD_DIM ** -0.5
    outs = []
    for h in range(LANES // HEAD_DIM):
        cols = slice(h * HEAD_DIM, (h + 1) * HEAD_DIM)
        q, k, v = q_ref[:, cols], k_ref[:, cols], v_ref[:, cols]
        ko_ref[h] = k
        vo_ref[h] = v
        s = _dot_nt(q.astype(BF16), k.astype(BF16)) * scale
        p = jnp.exp(s - jnp.max(s, axis=-1, keepdims=True))
        p = p / jnp.sum(p, axis=-1, keepdims=True)
        outs.append(_dot(p.astype(BF16), v.astype(BF16)))
    o_ref[...] = jnp.concatenate(outs, axis=1)


def _ctx_attention(u, y_prev, *, n_seq, seq_len, e):
    t = u.shape[0]
    hp = e // LANES
    heads = e // HEAD_DIM
    kv_block = (None, LANES // HEAD_DIM, seq_len, HEAD_DIM)
    in_specs = [pl.BlockSpec((seq_len, LANES), lambda b, h: (b, h)),
                pl.BlockSpec((seq_len, LANES), lambda b, h: (b, hp + h)),
                pl.BlockSpec((seq_len, LANES), lambda b, h: (b, 2 * hp + h))]
    return pl.pallas_call(
        _ctx_attn_body,
        grid=(n_seq, hp),
        in_specs=in_specs,
        out_specs=[pl.BlockSpec((seq_len, LANES), lambda b, h: (b, h)),
                   pl.BlockSpec(kv_block, lambda b, h: (b, h, 0, 0)),
                   pl.BlockSpec(kv_block, lambda b, h: (b, h, 0, 0))],
        out_shape=[jax.ShapeDtypeStruct((t, e), F32),
                   jax.ShapeDtypeStruct((n_seq, heads, seq_len, HEAD_DIM), F32),
                   jax.ShapeDtypeStruct((n_seq, heads, seq_len, HEAD_DIM), F32)],
        compiler_params=_params("parallel", "parallel"),
        name="ctx_attention",
    )(u, u, u)


def _nat_bias_body(rpb_ref, o_ref, *, n_rel_rows, n_rel_cols):
    h = pl.program_id(0)
    q = lax.broadcasted_iota(jnp.int32, (GRID_W, GRID_W), 0)
    kc = lax.broadcasted_iota(jnp.int32, (GRID_W, GRID_W), 1)
    rel = kc - q + (WIN_COLS - 1)
    c_start = jnp.clip(q - WIN_COLS // 2, 0, GRID_W - WIN_COLS)
    ok = (kc >= c_start) & (kc < c_start + WIN_COLS)
    for i in range(n_rel_rows):
        acc = jnp.zeros((GRID_W, GRID_W), F32)
        for dcol in range(n_rel_cols):
            acc = jnp.where(rel == dcol, rpb_ref[(h * n_rel_rows + i) * n_rel_cols + dcol], acc)
        tile = jnp.where(ok, acc, MASKED)
        if i < n_rel_rows - 1:
            o_ref[i, :, 0:GRID_W] = tile
        if i > 0:
            o_ref[i - 1, :, GRID_W:2 * GRID_W] = tile


def _nat_bias(rpb):
    heads, n_rel_rows, n_rel_cols = rpb.shape
    return pl.pallas_call(
        functools.partial(_nat_bias_body, n_rel_rows=n_rel_rows, n_rel_cols=n_rel_cols),
        grid=(heads,),
        in_specs=[pl.BlockSpec(memory_space=pltpu.SMEM)],
        out_specs=pl.BlockSpec((None, n_rel_rows - 1, GRID_W, 2 * GRID_W), lambda h: (h, 0, 0, 0)),
        out_shape=jax.ShapeDtypeStruct((heads, n_rel_rows - 1, GRID_W, 2 * GRID_W), F32),
        compiler_params=_params("parallel"),
        name="nat_bias",
    )(rpb.reshape(-1))


def _nat_body(q_ref, k_ref, v_ref, ck_ref, cv_ref, bias_ref, o_ref, *, rows):
    scale = HEAD_DIM ** -0.5
    wr = min(WIN_ROWS, rows)
    nw = wr * GRID_W

    def one_row(r, carry):
        r_start = jnp.clip(r - wr // 2, 0, rows - wr)
        i0 = (WIN_ROWS - 1) - (r - r_start)
        q_rows = pl.ds(pl.multiple_of(r * GRID_W, GRID_W), GRID_W)
        k_rows = pl.ds(pl.multiple_of(r_start * GRID_W, GRID_W), nw)
        q2, k2, v2 = q_ref[q_rows, :], k_ref[k_rows, :], v_ref[k_rows, :]
        outs = []
        for h in range(LANES // HEAD_DIM):
            cols = slice(h * HEAD_DIM, (h + 1) * HEAD_DIM)
            q = q2[:, cols].astype(BF16)
            bias = jnp.concatenate([bias_ref[h, i0 + 2 * jj] for jj in range(wr // 2)], axis=1)
            s_win = _dot_nt(q, k2[:, cols].astype(BF16)) * scale + bias
            s_ctx = _dot_nt(q, ck_ref[h].astype(BF16)) * scale
            m = jnp.maximum(jnp.max(s_win, axis=-1, keepdims=True), jnp.max(s_ctx, axis=-1, keepdims=True))
            p_win = jnp.exp(s_win - m)
            p_ctx = jnp.exp(s_ctx - m)
            den = jnp.sum(p_win, axis=-1, keepdims=True) + jnp.sum(p_ctx, axis=-1, keepdims=True)
            inv = 1.0 / den
            outs.append(_dot((p_win * inv).astype(BF16), v2[:, cols].astype(BF16))
                        + _dot((p_ctx * inv).astype(BF16), cv_ref[h].astype(BF16)))
        o_ref[q_rows, :] = jnp.concatenate(outs, axis=1)
        return carry

    lax.fori_loop(0, rows, one_row, 0)


def _nat_attention(u, cache_k, cache_v, bias, y_prev, *, n_seq, seq_len, row_block0, e):
    t = u.shape[0]
    hp = e // LANES
    hpb = LANES // HEAD_DIM
    past = cache_k.shape[2]
    seq = lambda b: row_block0 + b
    cache_block = (None, hpb, past, HEAD_DIM)
    return pl.pallas_call(
        lambda q, k, v, ck, cv, bs, yp, o: _nat_body(q, k, v, ck, cv, bs, o, rows=seq_len // GRID_W),
        grid=(n_seq, hp),
        in_specs=[pl.BlockSpec((seq_len, LANES), lambda b, h: (seq(b), h)),
                  pl.BlockSpec((seq_len, LANES), lambda b, h: (seq(b), hp + h)),
                  pl.BlockSpec((seq_len, LANES), lambda b, h: (seq(b), 2 * hp + h)),
                  pl.BlockSpec(cache_block, lambda b, h: (b, h, 0, 0)),
                  pl.BlockSpec(cache_block, lambda b, h: (b, h, 0, 0)),
                  pl.BlockSpec((hpb,) + bias.shape[1:], lambda b, h: (h, 0, 0, 0)),
                  pl.BlockSpec(memory_space=pl.ANY)],
        out_specs=pl.BlockSpec((seq_len, LANES), lambda b, h: (seq(b), h)),
        out_shape=jax.ShapeDtypeStruct((t, e), F32),
        input_output_aliases={6: 0},
        compiler_params=_params("parallel", "parallel"),
        name="nat_attention",
    )(u, u, u, cache_k, cache_v, bias, y_prev)


def kernel(x_prompt, x_sample, state_ssd, state_s5, cache_k, cache_v, c, c_ctx, norm_g, w_mod, b_mod, w_out, final_g, ssd_w_in, ssd_conv_w, ssd_conv_b, ssd_dt_bias, ssd_a_log, ssd_d, ssd_norm_g, mlp_w_in, mlp_ln_g, mlp_ln_b, mlp_w_s, mlp_b_s, s5_w_in, s5_lam_re, s5_lam_im, s5_log_step, s5_b_re, s5_b_im, s5_c_re, s5_c_im, s5_d, s5_w_glu, s5_b_glu, nat_w_in, nat_rpb):
    n_ctx, l_ctx, d = x_prompt.shape
    n_lat, l_lat, _ = x_sample.shape
    t_ctx = n_ctx * l_ctx
    depth = norm_g.shape[0]
    e = w_out.shape[1]
    assert depth == 4 and l_lat % l_ctx == 0 and t_ctx % (n_lat * l_lat) == 0
    tiles = dict(tm=512, t_ctx=t_ctx, l_lat=l_lat)
    tiles_in = dict(tm=1024, tn=1024, t_ctx=t_ctx, l_lat=l_lat)

    x = jnp.concatenate([x_prompt.reshape(t_ctx, d), x_sample.reshape(n_lat * l_lat, d)], axis=0)
    cond8 = jnp.concatenate([c_ctx[None], c, jnp.zeros((8 - 1 - n_lat, d), F32)], axis=0)
    mods = _modulation(cond8, w_mod, b_mod).reshape(depth, 8, 3, 1, d)
    w_out_b = w_out.astype(BF16)

    groups = e // SSD_GROUP_CH
    n_main = 3 * e
    w_dt = ssd_w_in[0][:, n_main:].reshape(d, 2, groups, SSD_GROUP_HEADS).transpose(0, 2, 1, 3)
    w_dt = jnp.pad(w_dt.reshape(d, groups, 2 * SSD_GROUP_HEADS), ((0, 0), (0, 0), (0, LANES - 2 * SSD_GROUP_HEADS)))
    u = _inproj(x, mods, 0, norm_g[0:1], ssd_w_in[0][:, :n_main].astype(BF16), name="ssd_in", **tiles_in)
    dt = _inproj(x, mods, 0, norm_g[0:1], w_dt.reshape(d, groups * LANES).astype(BF16), name="ssd_dt_in", **tiles_in)
    ssd_args = (ssd_conv_w[0], ssd_conv_b[0:1], _group_lanes(ssd_dt_bias[0], groups),
                _group_lanes(ssd_a_log[0], groups), jnp.repeat(ssd_d[0], SSD_HEAD_DIM)[None])
    y, new_ssd = _ssd_call(u, dt, *ssd_args, None, None, n_seq=n_ctx, seq_len=l_ctx, row_block0=0,
                           want_final=True, name="ssd_ctx")
    y, = _ssd_call(u, dt, *ssd_args, y, state_ssd[:, 0], n_seq=n_lat, seq_len=l_lat, row_block0=t_ctx // l_lat,
                   want_final=False, name="ssd_lat")
    x = _outproj(y, u, 0, x, mods, 0, w_out_b[0], norm_g=ssd_norm_g[0:1], name="ssd_out", **tiles)

    u = _inproj(x, mods, 1, norm_g[1:2], mlp_w_in[0].astype(BF16), name="mlp_in", **tiles_in)
    b_lanes = jnp.repeat(mlp_b_s[0].T, e // mlp_b_s.shape[1], axis=1)
    y = _gmlp(u, mlp_ln_g[0:1], mlp_ln_b[0:1], mlp_w_s[0], b_lanes)
    x = _outproj(y, u, 2, x, mods, 1, w_out_b[1], name="mlp_out", **tiles)

    u = _inproj(x, mods, 2, norm_g[2:3], s5_w_in[0].astype(BF16), name="s5_in", **tiles_in)
    s5_groups = e // S5_GROUP
    bg = LANES // S5_GROUP
    blocks = s5_groups // bg

    def block_vec(v):
        return v.reshape(2, blocks, 1, S5_BLOCK_ST).transpose(1, 0, 2, 3)

    def block_mat_b(v):
        v = v.reshape(2, blocks, bg, S5_STATE, S5_GROUP).transpose(1, 0, 4, 2, 3)
        return v.reshape(blocks, 2, S5_GROUP, S5_BLOCK_ST)

    def block_mat_c(v):
        v = v.reshape(2, blocks, bg, S5_GROUP, S5_STATE).transpose(1, 0, 3, 2, 4)
        return v.reshape(blocks, 2, S5_GROUP, S5_BLOCK_ST)

    log_step = jnp.repeat(s5_log_step[0][:, :, None], S5_STATE, axis=2)
    prm = (block_vec(s5_lam_re[0]), block_vec(s5_lam_im[0]), block_vec(log_step), block_mat_b(s5_b_re[0]),
           block_mat_b(s5_b_im[0]), block_mat_c(s5_c_re[0]), block_mat_c(s5_c_im[0]))
    h0 = state_s5[:, 0].reshape(n_lat, 2, 2, blocks, S5_BLOCK_ST).transpose(3, 1, 2, 0, 4)
    ys, fin = _s5_core(prm, h0, u, n_ctx=n_ctx, l_ctx=l_ctx, n_lat=n_lat, l_lat=l_lat, e=e)
    new_s5 = fin.transpose(3, 1, 2, 0, 4).reshape(n_ctx, 1, 2, 2, s5_groups, S5_STATE)
    y = _s5_glu(ys, u, s5_d[0:1], s5_w_glu[0].astype(BF16), s5_b_glu[0:1])
    x = _outproj(y, u, 1, x, mods, 2, w_out_b[2], name="s5_out", **tiles)

    u = _inproj(x, mods, 3, norm_g[3:4], nat_w_in[0].astype(BF16), name="nat_in", **tiles_in)
    y, new_k, new_v = _ctx_attention(u, None, n_seq=n_ctx, seq_len=l_ctx, e=e)
    y = _nat_attention(u, cache_k[:, 0], cache_v[:, 0], _nat_bias(nat_rpb[0]), y, n_seq=n_lat, seq_len=l_lat,
                       row_block0=t_ctx // l_lat, e=e)
    out = _outproj(y, u, 3, x, mods, 3, w_out_b[3], final_g=final_g[None], name="nat_out", **tiles)

    return (out[:t_ctx].reshape(n_ctx, l_ctx, d), out[t_ctx:].reshape(n_lat, l_lat, d),
            new_ssd[:, None], new_s5, new_k[:, None], new_v[:, None])
```

```python
import functools

import jax
import jax.numpy as jnp
from jax import lax
from jax.experimental import pallas as pl
from jax.experimental.pallas import tpu as pltpu

F32 = jnp.float32
BF16 = jnp.bfloat16
EPS = 1e-6
HIGHEST = lax.Precision.HIGHEST

LANES = 128
CHUNK = 128
SSD_HEAD_DIM = 64
SSD_GROUP_HEADS = 4
SSD_GROUP_CH = SSD_HEAD_DIM * SSD_GROUP_HEADS
SSD_STATE = 128
CONV_W = 5
CONV_HALO = 8
S5_T = 8
S5_GROUP = 16
S5_STATE = 64
S5_BLOCK_ST = (LANES // S5_GROUP) * S5_STATE
HEAD_DIM = 64
GRID_W = 64
WIN_ROWS = 8
WIN_COLS = 16
NAT_ROWS_PER_STEP = 4
MASKED = -1e30
VMEM_LIMIT = 52 * 1024 * 1024


def _silu(x):
    return x * jax.nn.sigmoid(x)


def _gelu(x):
    return 0.5 * x * (1.0 + jnp.tanh(0.7978845608028654 * (x + 0.044715 * (x * x * x))))


def _softplus(x):
    return jnp.maximum(x, 0.0) + jnp.log1p(jnp.exp(-jnp.abs(x)))


def _dot(a, b):
    return jnp.dot(a, b, preferred_element_type=F32)


def _dot_nt(a, b):
    return lax.dot_general(a, b, (((1,), (1,)), ((), ())), preferred_element_type=F32)


def _params(*sem):
    return pltpu.CompilerParams(dimension_semantics=sem, vmem_limit_bytes=VMEM_LIMIT)


def _mod_row(i, *, tm, t_ctx, l_lat):
    start = i * tm
    return jnp.where(start < t_ctx, 0, 1 + (start - t_ctx) // l_lat)


def _mod_body(c_ref, w_ref, b_ref, o_ref):
    cond = _silu(c_ref[...]).astype(BF16)
    o_ref[...] = _dot(cond, w_ref[...].astype(BF16)) + b_ref[...]


def _modulation(cond8, w_mod, b_mod, *, tn=1024):
    depth, d, n = w_mod.shape
    return pl.pallas_call(
        _mod_body,
        grid=(depth, n // tn),
        in_specs=[pl.BlockSpec((8, d), lambda l, j: (0, 0)),
                  pl.BlockSpec((None, d, tn), lambda l, j: (l, 0, j)),
                  pl.BlockSpec((None, 1, tn), lambda l, j: (l, 0, j))],
        out_specs=pl.BlockSpec((None, 8, tn), lambda l, j: (l, 0, j)),
        out_shape=jax.ShapeDtypeStruct((depth, 8, n), F32),
        compiler_params=_params("parallel", "parallel"),
        name="modulation",
    )(cond8, w_mod, b_mod.reshape(depth, 1, n))


def _inproj_body(x_ref, shift_ref, scale_ref, g_ref, w_ref, o_ref, h_ref):
    @pl.when(pl.program_id(1) == 0)
    def _():
        x = x_ref[...]
        y = x * lax.rsqrt(jnp.mean(x * x, axis=-1, keepdims=True) + EPS) * g_ref[...]
        h_ref[...] = (y * (1.0 + scale_ref[...]) + shift_ref[...]).astype(BF16)

    o_ref[...] = _dot(h_ref[...], w_ref[...])


def _inproj(x, mods, layer, g, w, *, tm, tn, t_ctx, l_lat, name):
    t, d = x.shape
    n = w.shape[1]
    row = functools.partial(_mod_row, tm=tm, t_ctx=t_ctx, l_lat=l_lat)

    def mod_spec(part):
        return pl.BlockSpec((None, None, None, 1, d), lambda i, j: (layer, row(i), part, 0, 0))

    return pl.pallas_call(
        _inproj_body,
        grid=(t // tm, n // tn),
        in_specs=[pl.BlockSpec((tm, d), lambda i, j: (i, 0)), mod_spec(0), mod_spec(1),
                  pl.BlockSpec((1, d), lambda i, j: (0, 0)),
                  pl.BlockSpec((d, tn), lambda i, j: (0, j))],
        out_specs=pl.BlockSpec((tm, tn), lambda i, j: (i, j)),
        out_shape=jax.ShapeDtypeStruct((t, n), F32),
        scratch_shapes=[pltpu.VMEM((tm, d), BF16)],
        compiler_params=_params("parallel", "arbitrary"),
        name=name,
    )(x, mods, mods, g, w)


def _outproj_body(*refs, gated_norm, final_norm):
    y_ref, z_ref, x_ref, gate_ref, w_ref = refs[:5]
    rest = list(refs[5:])
    t = y_ref[...] * _silu(z_ref[...])
    if gated_norm:
        ng_ref = rest.pop(0)
        t = t * lax.rsqrt(jnp.mean(t * t, axis=-1, keepdims=True) + EPS) * ng_ref[...]
    if final_norm:
        fg_ref = rest.pop(0)
    o_ref = rest.pop(0)
    xn = x_ref[...] + gate_ref[...] * _dot(t.astype(BF16), w_ref[...])
    if final_norm:
        xn = xn * lax.rsqrt(jnp.mean(xn * xn, axis=-1, keepdims=True) + EPS) * fg_ref[...]
    o_ref[...] = xn


def _outproj(y, u, z_block, x, mods, layer, w, *, norm_g=None, final_g=None, tm, t_ctx, l_lat, name):
    t, e = y.shape
    d = x.shape[1]
    row = functools.partial(_mod_row, tm=tm, t_ctx=t_ctx, l_lat=l_lat)
    in_specs = [pl.BlockSpec((tm, e), lambda i: (i, 0)),
                pl.BlockSpec((tm, e), lambda i: (i, z_block)),
                pl.BlockSpec((tm, d), lambda i: (i, 0)),
                pl.BlockSpec((None, None, None, 1, d), lambda i: (layer, row(i), 2, 0, 0)),
                pl.BlockSpec((e, d), lambda i: (0, 0))]
    args = [y, u, x, mods, w]
    if norm_g is not None:
        in_specs.append(pl.BlockSpec((1, e), lambda i: (0, 0)))
        args.append(norm_g)
    if final_g is not None:
        in_specs.append(pl.BlockSpec((1, d), lambda i: (0, 0)))
        args.append(final_g)
    return pl.pallas_call(
        functools.partial(_outproj_body, gated_norm=norm_g is not None, final_norm=final_g is not None),
        grid=(t // tm,),
        in_specs=in_specs,
        out_specs=pl.BlockSpec((tm, d), lambda i: (i, 0)),
        out_shape=jax.ShapeDtypeStruct((t, d), F32),
        compiler_params=_params("parallel"),
        name=name,
    )(*args)


def _ssd_body(*refs, seq_len, has_h0, want_final):
    refs = list(refs)
    x_ref, b_ref, c_ref, dt_ref, wx_ref, wb_ref, wc_ref, bx_ref, bb_ref, bc_ref, dtb_ref, alog_ref, dsk_ref = refs[:13]
    rest = refs[13:]
    h0_ref = rest.pop(0) if has_h0 else None
    if want_final:
        y_ref, hf_ref = rest[0], rest[1]
        rest = rest[2:]
    else:
        y_ref, hf_ref = rest[0], None
        rest = rest[1:]
    pad_s, xbc_s, dt_s, da_s, sf_s, sb_s = rest
    nc = seq_len // CHUNK
    width = SSD_GROUP_CH + 2 * SSD_STATE

    zeros = jnp.zeros((CONV_HALO, width), F32)
    pad_s[0:CONV_HALO, :] = zeros
    pad_s[CONV_HALO + seq_len:2 * CONV_HALO + seq_len, :] = zeros

    def copy_chunk(ci, carry):
        base = pl.multiple_of(ci * CHUNK, CHUNK)
        dst = pl.ds(base + CONV_HALO, CHUNK)
        pad_s[dst, 0:SSD_GROUP_CH] = x_ref[pl.ds(base, CHUNK), :]
        pad_s[dst, SSD_GROUP_CH:SSD_GROUP_CH + SSD_STATE] = b_ref[pl.ds(base, CHUNK), :]
        pad_s[dst, SSD_GROUP_CH + SSD_STATE:width] = c_ref[pl.ds(base, CHUNK), :]
        return carry

    lax.fori_loop(0, nc, copy_chunk, 0)

    conv_w = jnp.concatenate([wx_ref[...], wb_ref[...], wc_ref[...]], axis=1)
    conv_b = jnp.concatenate([bx_ref[...], bb_ref[...], bc_ref[...]], axis=1)
    a_row = -jnp.exp(alog_ref[...])
    first = CONV_HALO - CONV_W // 2

    def conv_chunk(ci, carry):
        base = pl.multiple_of(ci * CHUNK, CHUNK)
        win = pad_s[pl.ds(base, CHUNK + 2 * CONV_HALO), :]
        acc = conv_b + win[first:first + CHUNK] * conv_w[0:1]
        for j in range(1, CONV_W):
            acc = acc + win[first + j:first + j + CHUNK] * conv_w[j:j + 1]
        xbc = _silu(acc)
        rows = pl.ds(base, CHUNK)
        xbc_s[rows, :] = xbc
        y_ref[rows, :] = xbc[:, 0:SSD_GROUP_CH] * dsk_ref[...]
        dt = _softplus(dt_ref[rows, :] + dtb_ref[...])
        dt_s[rows, :] = dt
        da_s[rows, :] = dt * a_row
        return carry

    lax.fori_loop(0, nc, conv_chunk, 0)

    if has_h0:
        sf_s[...] = h0_ref[0].reshape(SSD_GROUP_CH, SSD_STATE)
        sb_s[...] = h0_ref[1].reshape(SSD_GROUP_CH, SSD_STATE)
    else:
        sf_s[...] = jnp.zeros((SSD_GROUP_CH, SSD_STATE), F32)
        sb_s[...] = jnp.zeros((SSD_GROUP_CH, SSD_STATE), F32)

    row_i = lax.broadcasted_iota(jnp.int32, (CHUNK, CHUNK), 0)
    col_i = lax.broadcasted_iota(jnp.int32, (CHUNK, CHUNK), 1)
    lower = row_i >= col_i
    upper = row_i <= col_i
    lower_f = lower.astype(F32)
    head_of_lane = lax.broadcasted_iota(jnp.int32, (CHUNK, SSD_GROUP_CH), 1) // SSD_HEAD_DIM
    head_of_row = lax.broadcasted_iota(jnp.int32, (SSD_GROUP_CH, SSD_STATE), 0) // SSD_HEAD_DIM

    def per_head_lanes(cols):
        out = cols[SSD_GROUP_HEADS - 1]
        for r in range(SSD_GROUP_HEADS - 2, -1, -1):
            out = jnp.where(head_of_lane == r, cols[r], out)
        return out

    def one_direction(ci, backward, st_ref):
        rows = pl.ds(pl.multiple_of(ci * CHUNK, CHUNK), CHUNK)
        xbc = xbc_s[rows, :]
        x = xbc[:, 0:SSD_GROUP_CH]
        bm = xbc[:, SSD_GROUP_CH:SSD_GROUP_CH + SSD_STATE].astype(BF16)
        cm = xbc[:, SSD_GROUP_CH + SSD_STATE:width].astype(BF16)
        dt = dt_s[rows, :]
        da = da_s[rows, :]
        cum = jnp.dot(lower_f, da, precision=HIGHEST, preferred_element_type=F32)
        e = cum - da if backward else cum
        e_t = e.T
        s_prev = st_ref[...]
        g = _dot_nt(cm, bm)
        cs = _dot_nt(cm, s_prev.astype(BF16))
        dt_cols, off_cols, st_cols, tots, decays = [], [], [], [], []
        for r in range(SSD_GROUP_HEADS):
            col = r + SSD_GROUP_HEADS * int(backward)
            ec = e[:, col:col + 1]
            er = e_t[col:col + 1, :]
            tot = cum[CHUNK - 1:CHUNK, col:col + 1]
            if backward:
                decays.append(jnp.exp(jnp.where(upper, er - ec, -jnp.inf)))
                off_cols.append(jnp.exp(tot - ec))
                st_cols.append(jnp.exp(ec))
            else:
                decays.append(jnp.exp(jnp.where(lower, ec - er, -jnp.inf)))
                off_cols.append(jnp.exp(ec))
                st_cols.append(jnp.exp(tot - ec))
            dt_cols.append(dt[:, col:col + 1])
            tots.append(jnp.exp(tot))
        xd = x * per_head_lanes(dt_cols)
        xd_b = xd.astype(BF16)
        y = cs * per_head_lanes(off_cols)
        for r in range(SSD_GROUP_HEADS):
            yd = _dot((g * decays[r]).astype(BF16), xd_b)
            y = y + jnp.where(head_of_lane == r, yd, 0.0)
        y_ref[rows, :] += y
        w_t = (xd * per_head_lanes(st_cols)).T.astype(BF16)
        keep = tots[SSD_GROUP_HEADS - 1]
        for r in range(SSD_GROUP_HEADS - 2, -1, -1):
            keep = jnp.where(head_of_row == r, tots[r], keep)
        st_ref[...] = keep * s_prev + _dot(w_t, bm)

    def scan_chunk(i, carry):
        one_direction(i, False, sf_s)
        one_direction(nc - 1 - i, True, sb_s)
        return carry

    lax.fori_loop(0, nc, scan_chunk, 0)

    if want_final:
        hf_ref[0] = sf_s[...].reshape(SSD_GROUP_HEADS, SSD_HEAD_DIM, SSD_STATE)
        hf_ref[1] = sb_s[...].reshape(SSD_GROUP_HEADS, SSD_HEAD_DIM, SSD_STATE)


def _ssd_call(u, dt, conv_w, conv_b, dtb, alog, dsk, y_prev, h0, *, n_seq, seq_len, row_block0, want_final, name):
    t, n_u = u.shape
    e = dsk.shape[1]
    groups = e // SSD_GROUP_CH
    xb0 = e // SSD_GROUP_CH
    bb0 = 2 * e // SSD_STATE
    cb0 = bb0 + groups
    has_h0 = h0 is not None
    seq = lambda b: row_block0 + b
    in_specs = [pl.BlockSpec((seq_len, SSD_GROUP_CH), lambda b, g: (seq(b), xb0 + g)),
                pl.BlockSpec((seq_len, SSD_STATE), lambda b, g: (seq(b), bb0 + g)),
                pl.BlockSpec((seq_len, SSD_STATE), lambda b, g: (seq(b), cb0 + g)),
                pl.BlockSpec((seq_len, LANES), lambda b, g: (seq(b), g)),
                pl.BlockSpec((CONV_W, SSD_GROUP_CH), lambda b, g: (0, g)),
                pl.BlockSpec((CONV_W, SSD_STATE), lambda b, g: (0, bb0 - xb0 * 2 + g)),
                pl.BlockSpec((CONV_W, SSD_STATE), lambda b, g: (0, cb0 - xb0 * 2 + g)),
                pl.BlockSpec((1, SSD_GROUP_CH), lambda b, g: (0, g)),
                pl.BlockSpec((1, SSD_STATE), lambda b, g: (0, bb0 - xb0 * 2 + g)),
                pl.BlockSpec((1, SSD_STATE), lambda b, g: (0, cb0 - xb0 * 2 + g)),
                pl.BlockSpec((1, LANES), lambda b, g: (0, g)),
                pl.BlockSpec((1, LANES), lambda b, g: (0, g)),
                pl.BlockSpec((1, SSD_GROUP_CH), lambda b, g: (0, g))]
    args = [u, u, u, dt, conv_w, conv_w, conv_w, conv_b, conv_b, conv_b, dtb, alog, dsk]
    state_block = (None, 2, SSD_GROUP_HEADS, SSD_HEAD_DIM, SSD_STATE)
    if has_h0:
        in_specs.append(pl.BlockSpec(state_block, lambda b, g: (b, 0, g, 0, 0)))
        args.append(h0)
    aliases = {}
    if y_prev is not None:
        in_specs.append(pl.BlockSpec(memory_space=pl.ANY))
        args.append(y_prev)
        aliases = {len(args) - 1: 0}
    out_specs = [pl.BlockSpec((seq_len, SSD_GROUP_CH), lambda b, g: (seq(b), g))]
    out_shape = [jax.ShapeDtypeStruct((t, e), F32)]
    if want_final:
        out_specs.append(pl.BlockSpec(state_block, lambda b, g: (b, 0, g, 0, 0)))
        out_shape.append(jax.ShapeDtypeStruct((n_seq, 2, e // SSD_HEAD_DIM, SSD_HEAD_DIM, SSD_STATE), F32))
    width = SSD_GROUP_CH + 2 * SSD_STATE

    def body(*refs):
        refs = list(refs)
        if y_prev is not None:
            n_in = len(args)
            del refs[n_in - 1]
        _ssd_body(*refs, seq_len=seq_len, has_h0=has_h0, want_final=want_final)

    return pl.pallas_call(
        body,
        grid=(n_seq, groups),
        in_specs=in_specs,
        out_specs=out_specs,
        out_shape=out_shape,
        input_output_aliases=aliases,
        scratch_shapes=[pltpu.VMEM((seq_len + 2 * CONV_HALO, width), F32),
                        pltpu.VMEM((seq_len, width), F32),
                        pltpu.VMEM((seq_len, LANES), F32),
                        pltpu.VMEM((seq_len, LANES), F32),
                        pltpu.VMEM((SSD_GROUP_CH, SSD_STATE), F32),
                        pltpu.VMEM((SSD_GROUP_CH, SSD_STATE), F32)],
        compiler_params=_params("parallel", "parallel"),
        name=name,
    )(*args)


def _group_lanes(v, groups):
    per_group = v.reshape(2, groups, SSD_GROUP_HEADS).transpose(1, 0, 2).reshape(groups, 2 * SSD_GROUP_HEADS)
    return jnp.pad(per_group, ((0, 0), (0, LANES - 2 * SSD_GROUP_HEADS))).reshape(1, groups * LANES)


def _gmlp_body(u_ref, v_ref, lng_ref, lnb_ref, ws_ref, bs_ref, o_ref, *, tm):
    groups = ws_ref.shape[0]
    gch = u_ref.shape[1] // groups
    for c in range(tm // CHUNK):
        rows = slice(c * CHUNK, (c + 1) * CHUNK)
        v = _gelu(v_ref[rows, :])
        vc = v - jnp.mean(v, axis=-1, keepdims=True)
        vn = vc * lax.rsqrt(jnp.mean(vc * vc, axis=-1, keepdims=True) + EPS) * lng_ref[...] + lnb_ref[...]
        vb = vn.astype(BF16)
        for g in range(groups):
            cols = slice(g * gch, (g + 1) * gch)
            s = _dot(ws_ref[g].astype(BF16), vb[:, cols]) + bs_ref[:, cols]
            o_ref[rows, cols] = _gelu(u_ref[rows, cols]) * s


def _gmlp(u, ln_g, ln_b, w_s, b_lanes, *, tm=256):
    t = u.shape[0]
    e = ln_g.shape[1]
    return pl.pallas_call(
        functools.partial(_gmlp_body, tm=tm),
        grid=(t // tm,),
        in_specs=[pl.BlockSpec((tm, e), lambda i: (i, 0)),
                  pl.BlockSpec((tm, e), lambda i: (i, 1)),
                  pl.BlockSpec((1, e), lambda i: (0, 0)),
                  pl.BlockSpec((1, e), lambda i: (0, 0)),
                  pl.BlockSpec(w_s.shape, lambda i: (0, 0, 0)),
                  pl.BlockSpec((CHUNK, e), lambda i: (0, 0))],
        out_specs=pl.BlockSpec((tm, e), lambda i: (i, 0)),
        out_shape=jax.ShapeDtypeStruct((t, e), F32),
        compiler_params=_params("parallel"),
        name="gmlp",
    )(u, u, ln_g, ln_b, w_s, b_lanes)


def _s5_body(lr_ref, li_ref, ls_ref, brt_ref, bit_ref, cre_ref, cim_ref, h0_ref, u_ref, y_ref, fin_ref,
             win_s, t_s, ef_s, eb_s, z_s, a_s, zin_s, d_s, spf_s, spb_s, yo_s,
             *, seqs, l_ctx, n_lat, l_lat, ctx_parts):
    bst = S5_BLOCK_ST
    ng = LANES // S5_GROUP
    kw = S5_T * LANES
    part = pl.program_id(1)

    @pl.when(part == 0)
    def _build():
        own = (lax.broadcasted_iota(jnp.int32, (ng, S5_GROUP, bst), 2) // S5_STATE
               == lax.broadcasted_iota(jnp.int32, (ng, S5_GROUP, bst), 0))

        def spread(v):
            return jnp.where(own, v[None], 0.0).reshape(LANES, bst).astype(BF16)

        tau = lax.broadcasted_iota(jnp.int32, (S5_T + 8, 1), 0).astype(F32)
        e_refs = (ef_s, eb_s)
        for d in range(2):
            lr, li = lr_ref[d], li_ref[d]
            step = jnp.exp(ls_ref[d])
            mag = jnp.exp(tau * (lr * step))
            p_re = mag * jnp.cos(tau * (li * step))
            p_im = mag * jnp.sin(tau * (li * step))
            ab_re, ab_im = p_re[1:2], p_im[1:2]
            den = lr * lr + li * li
            nr = ab_re - 1.0
            cr = (nr * lr + ab_im * li) / den
            ci = (ab_im * lr - nr * li) / den
            brt, bit = brt_ref[d], bit_ref[d]
            bb_re = cr * brt - ci * bit
            bb_im = cr * bit + ci * brt
            cre, cim = cre_ref[d], cim_ref[d]
            for k in range(S5_T):
                rows = slice(k * LANES, (k + 1) * LANES)
                tq = S5_T - 1 - k if d == 0 else k
                te = k + 1 if d == 0 else S5_T - k
                pr, pi = p_re[tq:tq + 1], p_im[tq:tq + 1]
                win_s[rows, 2 * d * bst:(2 * d + 1) * bst] = spread(pr * bb_re - pi * bb_im)
                win_s[rows, (2 * d + 1) * bst:(2 * d + 2) * bst] = spread(pr * bb_im + pi * bb_re)
                pr, pi = p_re[te:te + 1], p_im[te:te + 1]
                e_refs[d][rows, 0:bst] = spread(cre * pr - cim * pi)
                e_refs[d][rows, bst:2 * bst] = spread(-(cre * pi + cim * pr))
            c_own = jnp.concatenate([spread(cre), spread(-cim)], axis=1)
            z = _dot_nt(win_s[:, 2 * d * bst:(2 * d + 2) * bst], c_own)
            if d == 0:
                z_s[0:kw, :] = z
            else:
                z_s[kw - LANES:kw, :] += z[0:LANES]
                z_s[kw:2 * kw - LANES, :] = z[LANES:kw]
            a_s[2 * d:2 * d + 1, :] = p_re[S5_T:S5_T + 1]
            a_s[2 * d + 1:2 * d + 2, :] = p_im[S5_T:S5_T + 1]
        for k in range(S5_T):
            r0 = (S5_T - 1 - k) * LANES
            t_s[:, k * LANES:(k + 1) * LANES] = z_s[r0:r0 + kw, :].astype(BF16)

    def outputs():
        yo_s[...] = (_dot(zin_s[...], t_s[...]) + _dot_nt(spf_s[...].astype(BF16), ef_s[...])
                     + _dot_nt(spb_s[...].astype(BF16), eb_s[...]))

    @pl.when(part < ctx_parts)
    def _context():
        nj = l_ctx // S5_T
        for j in range(nj):
            for k in range(S5_T):
                zin_s[j * seqs:(j + 1) * seqs, k * LANES:(k + 1) * LANES] = (
                    u_ref[pl.ds(j * S5_T + k, seqs, stride=l_ctx), :].astype(BF16))
        d_s[...] = _dot(zin_s[...], win_s[...])
        coef = [jnp.broadcast_to(a_s[i:i + 1, :], (seqs, bst)) for i in range(4)]
        zero = jnp.zeros((seqs, bst), F32)

        def step(j, carry):
            fr, fi, br, bi = carry
            rf = pl.ds(pl.multiple_of(j * seqs, seqs), seqs)
            rb = pl.ds(pl.multiple_of((nj - 1 - j) * seqs, seqs), seqs)
            spf_s[rf, 0:bst] = fr
            spf_s[rf, bst:2 * bst] = fi
            spb_s[rb, 0:bst] = br
            spb_s[rb, bst:2 * bst] = bi
            return (coef[0] * fr - coef[1] * fi + d_s[rf, 0:bst],
                    coef[0] * fi + coef[1] * fr + d_s[rf, bst:2 * bst],
                    coef[2] * br - coef[3] * bi + d_s[rb, 2 * bst:3 * bst],
                    coef[2] * bi + coef[3] * br + d_s[rb, 3 * bst:4 * bst])

        fr, fi, br, bi = lax.fori_loop(0, nj, step, (zero, zero, zero, zero))
        fin_ref[0, 0] = fr
        fin_ref[0, 1] = fi
        fin_ref[1, 0] = br
        fin_ref[1, 1] = bi
        outputs()
        for j in range(nj):
            for k in range(S5_T):
                y_ref[pl.ds(j * S5_T + k, seqs, stride=l_ctx), :] = (
                    yo_s[j * seqs:(j + 1) * seqs, k * LANES:(k + 1) * LANES])

    @pl.when(part == ctx_parts)
    def _latent():
        nj = l_lat // S5_T
        for b in range(n_lat):
            for k in range(S5_T):
                zin_s[b * nj:(b + 1) * nj, k * LANES:(k + 1) * LANES] = (
                    u_ref[pl.ds(b * l_lat + k, nj, stride=S5_T), :].astype(BF16))
        d_s[...] = _dot(zin_s[...], win_s[...])
        coef_re = jnp.concatenate([jnp.broadcast_to(a_s[0:1, :], (n_lat, bst)),
                                   jnp.broadcast_to(a_s[2:3, :], (n_lat, bst))], axis=0)
        coef_im = jnp.concatenate([jnp.broadcast_to(a_s[1:2, :], (n_lat, bst)),
                                   jnp.broadcast_to(a_s[3:4, :], (n_lat, bst))], axis=0)
        s_re0 = jnp.concatenate([h0_ref[0, 0], h0_ref[1, 0]], axis=0)
        s_im0 = jnp.concatenate([h0_ref[0, 1], h0_ref[1, 1]], axis=0)

        def step(j, carry):
            s_re, s_im = carry
            d_re, d_im = [], []
            for b in range(n_lat):
                row = pl.ds(b * nj + j, 1)
                spf_s[row, 0:bst] = s_re[b:b + 1]
                spf_s[row, bst:2 * bst] = s_im[b:b + 1]
                d_re.append(d_s[row, 0:bst])
                d_im.append(d_s[row, bst:2 * bst])
            for b in range(n_lat):
                row = pl.ds(b * nj + nj - 1 - j, 1)
                spb_s[row, 0:bst] = s_re[n_lat + b:n_lat + b + 1]
                spb_s[row, bst:2 * bst] = s_im[n_lat + b:n_lat + b + 1]
                d_re.append(d_s[row, 2 * bst:3 * bst])
                d_im.append(d_s[row, 3 * bst:4 * bst])
            d_re = jnp.concatenate(d_re, axis=0)
            d_im = jnp.concatenate(d_im, axis=0)
            return coef_re * s_re - coef_im * s_im + d_re, coef_re * s_im + coef_im * s_re + d_im

        lax.fori_loop(0, nj, step, (s_re0, s_im0))
        outputs()
        for b in range(n_lat):
            for k in range(S5_T):
                y_ref[pl.ds(b * l_lat + k, nj, stride=S5_T), :] = yo_s[b * nj:(b + 1) * nj, k * LANES:(k + 1) * LANES]


def _s5_core(prm, h0, u, *, n_ctx, l_ctx, n_lat, l_lat, e):
    blocks = prm[0].shape[0]
    part_tokens = n_lat * l_lat
    t = u.shape[0]
    n_parts = t // part_tokens
    seqs = part_tokens // l_ctx
    r = part_tokens // S5_T
    vec = pl.BlockSpec((None, 2, 1, S5_BLOCK_ST), lambda g, p: (g, 0, 0, 0))
    mat = pl.BlockSpec((None, 2, S5_GROUP, S5_BLOCK_ST), lambda g, p: (g, 0, 0, 0))
    k_in = S5_T * LANES
    return pl.pallas_call(
        functools.partial(_s5_body, seqs=seqs, l_ctx=l_ctx, n_lat=n_lat, l_lat=l_lat, ctx_parts=n_parts - 1),
        grid=(blocks, n_parts),
        in_specs=[vec, vec, vec, mat, mat, mat, mat,
                  pl.BlockSpec((None, 2, 2, n_lat, S5_BLOCK_ST), lambda g, p: (g, 0, 0, 0, 0)),
                  pl.BlockSpec((part_tokens, LANES), lambda g, p: (p, g))],
        out_specs=[pl.BlockSpec((part_tokens, LANES), lambda g, p: (p, g)),
                   pl.BlockSpec((None, 2, 2, seqs, S5_BLOCK_ST),
                                lambda g, p: (g, 0, 0, jnp.minimum(p, n_parts - 2), 0))],
        out_shape=[jax.ShapeDtypeStruct((t, e), F32),
                   jax.ShapeDtypeStruct((blocks, 2, 2, n_ctx, S5_BLOCK_ST), F32)],
        scratch_shapes=[pltpu.VMEM((k_in, 4 * S5_BLOCK_ST), BF16),
                        pltpu.VMEM((k_in, k_in), BF16),
                        pltpu.VMEM((k_in, 2 * S5_BLOCK_ST), BF16),
                        pltpu.VMEM((k_in, 2 * S5_BLOCK_ST), BF16),
                        pltpu.VMEM(((2 * S5_T - 1) * LANES, LANES), F32),
                        pltpu.VMEM((8, S5_BLOCK_ST), F32),
                        pltpu.VMEM((r, k_in), BF16),
                        pltpu.VMEM((r, 4 * S5_BLOCK_ST), F32),
                        pltpu.VMEM((r, 2 * S5_BLOCK_ST), F32),
                        pltpu.VMEM((r, 2 * S5_BLOCK_ST), F32),
                        pltpu.VMEM((r, k_in), F32)],
        compiler_params=_params("parallel", "arbitrary"),
        name="s5_core",
    )(*prm, h0, u)


def _s5_glu_body(ys_ref, u_ref, dsk_ref, w_ref, b_ref, o_ref):
    y = _gelu(ys_ref[...] + dsk_ref[...] * u_ref[...])
    o_ref[...] = y * jax.nn.sigmoid(_dot(y.astype(BF16), w_ref[...]) + b_ref[...])


def _s5_glu(ys, u, dsk, w, b, *, tm=512):
    t, e = ys.shape
    return pl.pallas_call(
        _s5_glu_body,
        grid=(t // tm,),
        in_specs=[pl.BlockSpec((tm, e), lambda i: (i, 0)),
                  pl.BlockSpec((tm, e), lambda i: (i, 0)),
                  pl.BlockSpec((1, e), lambda i: (0, 0)),
                  pl.BlockSpec((e, e), lambda i: (0, 0)),
                  pl.BlockSpec((1, e), lambda i: (0, 0))],
        out_specs=pl.BlockSpec((tm, e), lambda i: (i, 0)),
        out_shape=jax.ShapeDtypeStruct((t, e), F32),
        compiler_params=_params("parallel"),
        name="s5_glu",
    )(ys, u, dsk, w, b)


def _ctx_attn_body(q_ref, k_ref, v_ref, o_ref, ko_ref, vo_ref):
    scale = HEAD_DIM ** -0.5
    seq_len = q_ref.shape[0]
    first = lax.broadcasted_iota(jnp.int32, (seq_len, LANES), 1) < HEAD_DIM
    masks = (first, jnp.logical_not(first))
    n_pairs = q_ref.shape[1] // LANES
    logits, values = [], []
    for pair in range(n_pairs):
        cols = slice(pair * LANES, (pair + 1) * LANES)
        q2, k2, v2 = q_ref[:, cols] * scale, k_ref[:, cols], v_ref[:, cols]
        kb = k2.astype(BF16)
        values.append(v2.astype(BF16))
        for h in range(LANES // HEAD_DIM):
            ko_ref[2 * pair + h] = k2[:, h * HEAD_DIM:(h + 1) * HEAD_DIM]
            vo_ref[2 * pair + h] = v2[:, h * HEAD_DIM:(h + 1) * HEAD_DIM]
            logits.append(_dot_nt(jnp.where(masks[h], q2, 0.0).astype(BF16), kb))
    probs = []
    for s in logits:
        p = jnp.exp(s - jnp.max(s, axis=-1, keepdims=True))
        probs.append((p * (1.0 / jnp.sum(p, axis=-1, keepdims=True))).astype(BF16))
    for pair in range(n_pairs):
        o0, o1 = _dot(probs[2 * pair], values[pair]), _dot(probs[2 * pair + 1], values[pair])
        o_ref[:, pair * LANES:(pair + 1) * LANES] = jnp.where(first, o0, o1)


def _ctx_attention(u, y_prev, *, n_seq, seq_len, e, width=512):
    t = u.shape[0]
    hp = e // width
    heads = e // HEAD_DIM
    kv_block = (None, width // HEAD_DIM, seq_len, HEAD_DIM)
    in_specs = [pl.BlockSpec((seq_len, width), lambda b, h: (b, h)),
                pl.BlockSpec((seq_len, width), lambda b, h: (b, hp + h)),
                pl.BlockSpec((seq_len, width), lambda b, h: (b, 2 * hp + h))]
    return pl.pallas_call(
        _ctx_attn_body,
        grid=(n_seq, hp),
        in_specs=in_specs,
        out_specs=[pl.BlockSpec((seq_len, width), lambda b, h: (b, h)),
                   pl.BlockSpec(kv_block, lambda b, h: (b, h, 0, 0)),
                   pl.BlockSpec(kv_block, lambda b, h: (b, h, 0, 0))],
        out_shape=[jax.ShapeDtypeStruct((t, e), F32),
                   jax.ShapeDtypeStruct((n_seq, heads, seq_len, HEAD_DIM), F32),
                   jax.ShapeDtypeStruct((n_seq, heads, seq_len, HEAD_DIM), F32)],
        compiler_params=_params("parallel", "parallel"),
        name="ctx_attention",
    )(u, u, u)


def _nat_bias_body(rpb_ref, o_ref, *, n_rel_rows, n_rel_cols):
    h = pl.program_id(0)
    q = lax.broadcasted_iota(jnp.int32, (GRID_W, GRID_W), 0)
    kc = lax.broadcasted_iota(jnp.int32, (GRID_W, GRID_W), 1)
    rel = kc - q + (WIN_COLS - 1)
    c_start = jnp.clip(q - WIN_COLS // 2, 0, GRID_W - WIN_COLS)
    ok = (kc >= c_start) & (kc < c_start + WIN_COLS)
    for i in range(n_rel_rows):
        acc = jnp.zeros((GRID_W, GRID_W), F32)
        for dcol in range(n_rel_cols):
            acc = jnp.where(rel == dcol, rpb_ref[(h * n_rel_rows + i) * n_rel_cols + dcol], acc)
        tile = jnp.where(ok, acc, MASKED)
        if i < n_rel_rows - 1:
            o_ref[i, :, 0:GRID_W] = tile
        if i > 0:
            o_ref[i - 1, :, GRID_W:2 * GRID_W] = tile


def _nat_bias(rpb):
    heads, n_rel_rows, n_rel_cols = rpb.shape
    return pl.pallas_call(
        functools.partial(_nat_bias_body, n_rel_rows=n_rel_rows, n_rel_cols=n_rel_cols),
        grid=(heads,),
        in_specs=[pl.BlockSpec(memory_space=pltpu.SMEM)],
        out_specs=pl.BlockSpec((None, n_rel_rows - 1, GRID_W, 2 * GRID_W), lambda h: (h, 0, 0, 0)),
        out_shape=jax.ShapeDtypeStruct((heads, n_rel_rows - 1, GRID_W, 2 * GRID_W), F32),
        compiler_params=_params("parallel"),
        name="nat_bias",
    )(rpb.reshape(-1))


def _nat_body(q_ref, k_ref, v_ref, ck_ref, cv_ref, bias_ref, o_ref, kb_s, vb_s, ckb_s, cvb_s, *, rows):
    scale = HEAD_DIM ** -0.5
    wr = min(WIN_ROWS, rows)
    nw = wr * GRID_W
    kb_s[...] = k_ref[...].astype(BF16)
    vb_s[...] = v_ref[...].astype(BF16)
    ckb_s[...] = jnp.concatenate([ck_ref[0], ck_ref[1]], axis=1).astype(BF16)
    cvb_s[...] = jnp.concatenate([cv_ref[0], cv_ref[1]], axis=1).astype(BF16)
    first = lax.broadcasted_iota(jnp.int32, (GRID_W, LANES), 1) < HEAD_DIM

    heads = tuple(range(LANES // HEAD_DIM))
    masks = (first, jnp.logical_not(first))

    def row_group(g, carry):
        rws = [g * NAT_ROWS_PER_STEP + i for i in range(NAT_ROWS_PER_STEP)]
        starts = [jnp.clip(r - wr // 2, 0, rows - wr) for r in rws]
        q_rows = [pl.ds(pl.multiple_of(r * GRID_W, GRID_W), GRID_W) for r in rws]
        k_rows = [pl.ds(pl.multiple_of(rs * GRID_W, GRID_W), nw) for rs in starts]
        logits = []
        for r, rs, qr, kr in zip(rws, starts, q_rows, k_rows):
            q2 = q_ref[qr, :] * scale
            kb = kb_s[kr, :]
            for h in heads:
                q = jnp.where(masks[h], q2, 0.0).astype(BF16)
                logits.append((_dot_nt(q, kb), _dot_nt(q, ckb_s[...])))
        probs = []
        for idx, (s_win, s_ctx) in enumerate(logits):
            r, rs, h = rws[idx // 2], starts[idx // 2], heads[idx % 2]
            i0 = (WIN_ROWS - 1) - (r - rs)
            s_win = s_win + jnp.concatenate([bias_ref[h, i0 + 2 * jj] for jj in range(wr // 2)], axis=1)
            m = jnp.maximum(jnp.max(s_win, axis=-1, keepdims=True), jnp.max(s_ctx, axis=-1, keepdims=True))
            p_win = jnp.exp(s_win - m)
            p_ctx = jnp.exp(s_ctx - m)
            inv = 1.0 / (jnp.sum(p_win, axis=-1, keepdims=True) + jnp.sum(p_ctx, axis=-1, keepdims=True))
            probs.append(((p_win * inv).astype(BF16), (p_ctx * inv).astype(BF16)))
        for i, (qr, kr) in enumerate(zip(q_rows, k_rows)):
            vb = vb_s[kr, :]
            outs = [_dot(probs[2 * i + h][0], vb) + _dot(probs[2 * i + h][1], cvb_s[...]) for h in heads]
            o_ref[qr, :] = jnp.where(first, outs[0], outs[1])
        return carry

    lax.fori_loop(0, rows // NAT_ROWS_PER_STEP, row_group, 0)


def _nat_attention(u, cache_k, cache_v, bias, y_prev, *, n_seq, seq_len, row_block0, e):
    t = u.shape[0]
    hp = e // LANES
    hpb = LANES // HEAD_DIM
    past = cache_k.shape[2]
    seq = lambda b: row_block0 + b
    cache_block = (None, hpb, past, HEAD_DIM)
    return pl.pallas_call(
        lambda q, k, v, ck, cv, bs, yp, o, *scratch: _nat_body(q, k, v, ck, cv, bs, o, *scratch,
                                                               rows=seq_len // GRID_W),
        grid=(n_seq, hp),
        in_specs=[pl.BlockSpec((seq_len, LANES), lambda b, h: (seq(b), h)),
                  pl.BlockSpec((seq_len, LANES), lambda b, h: (seq(b), hp + h)),
                  pl.BlockSpec((seq_len, LANES), lambda b, h: (seq(b), 2 * hp + h)),
                  pl.BlockSpec(cache_block, lambda b, h: (b, h, 0, 0)),
                  pl.BlockSpec(cache_block, lambda b, h: (b, h, 0, 0)),
                  pl.BlockSpec((hpb,) + bias.shape[1:], lambda b, h: (h, 0, 0, 0)),
                  pl.BlockSpec(memory_space=pl.ANY)],
        out_specs=pl.BlockSpec((seq_len, LANES), lambda b, h: (seq(b), h)),
        out_shape=jax.ShapeDtypeStruct((t, e), F32),
        input_output_aliases={6: 0},
        scratch_shapes=[pltpu.VMEM((seq_len, LANES), BF16), pltpu.VMEM((seq_len, LANES), BF16),
                        pltpu.VMEM((past, LANES), BF16), pltpu.VMEM((past, LANES), BF16)],
        compiler_params=_params("parallel", "parallel"),
        name="nat_attention",
    )(u, u, u, cache_k, cache_v, bias, y_prev)


def kernel(x_prompt, x_sample, state_ssd, state_s5, cache_k, cache_v, c, c_ctx, norm_g, w_mod, b_mod, w_out, final_g, ssd_w_in, ssd_conv_w, ssd_conv_b, ssd_dt_bias, ssd_a_log, ssd_d, ssd_norm_g, mlp_w_in, mlp_ln_g, mlp_ln_b, mlp_w_s, mlp_b_s, s5_w_in, s5_lam_re, s5_lam_im, s5_log_step, s5_b_re, s5_b_im, s5_c_re, s5_c_im, s5_d, s5_w_glu, s5_b_glu, nat_w_in, nat_rpb):
    n_ctx, l_ctx, d = x_prompt.shape
    n_lat, l_lat, _ = x_sample.shape
    t_ctx = n_ctx * l_ctx
    depth = norm_g.shape[0]
    e = w_out.shape[1]
    assert depth == 4 and l_lat % l_ctx == 0 and t_ctx % (n_lat * l_lat) == 0
    tiles = dict(tm=512, t_ctx=t_ctx, l_lat=l_lat)
    tiles_in = dict(tm=1024, tn=1024, t_ctx=t_ctx, l_lat=l_lat)

    x = jnp.concatenate([x_prompt.reshape(t_ctx, d), x_sample.reshape(n_lat * l_lat, d)], axis=0)
    cond8 = jnp.concatenate([c_ctx[None], c, jnp.zeros((8 - 1 - n_lat, d), F32)], axis=0)
    mods = _modulation(cond8, w_mod, b_mod).reshape(depth, 8, 3, 1, d)
    w_out_b = w_out.astype(BF16)

    groups = e // SSD_GROUP_CH
    n_main = 3 * e
    w_dt = ssd_w_in[0][:, n_main:].reshape(d, 2, groups, SSD_GROUP_HEADS).transpose(0, 2, 1, 3)
    w_dt = jnp.pad(w_dt.reshape(d, groups, 2 * SSD_GROUP_HEADS), ((0, 0), (0, 0), (0, LANES - 2 * SSD_GROUP_HEADS)))
    u = _inproj(x, mods, 0, norm_g[0:1], ssd_w_in[0][:, :n_main].astype(BF16), name="ssd_in", **tiles_in)
    dt = _inproj(x, mods, 0, norm_g[0:1], w_dt.reshape(d, groups * LANES).astype(BF16), name="ssd_dt_in", **tiles_in)
    ssd_args = (ssd_conv_w[0], ssd_conv_b[0:1], _group_lanes(ssd_dt_bias[0], groups),
                _group_lanes(ssd_a_log[0], groups), jnp.repeat(ssd_d[0], SSD_HEAD_DIM)[None])
    y, new_ssd = _ssd_call(u, dt, *ssd_args, None, None, n_seq=n_ctx, seq_len=l_ctx, row_block0=0,
                           want_final=True, name="ssd_ctx")
    y, = _ssd_call(u, dt, *ssd_args, y, state_ssd[:, 0], n_seq=n_lat, seq_len=l_lat, row_block0=t_ctx // l_lat,
                   want_final=False, name="ssd_lat")
    x = _outproj(y, u, 0, x, mods, 0, w_out_b[0], norm_g=ssd_norm_g[0:1], name="ssd_out", **tiles)

    u = _inproj(x, mods, 1, norm_g[1:2], mlp_w_in[0].astype(BF16), name="mlp_in", **tiles_in)
    b_lanes = jnp.repeat(mlp_b_s[0].T, e // mlp_b_s.shape[1], axis=1)
    y = _gmlp(u, mlp_ln_g[0:1], mlp_ln_b[0:1], mlp_w_s[0], b_lanes)
    x = _outproj(y, u, 2, x, mods, 1, w_out_b[1], name="mlp_out", **tiles)

    u = _inproj(x, mods, 2, norm_g[2:3], s5_w_in[0].astype(BF16), name="s5_in", **tiles_in)
    s5_groups = e // S5_GROUP
    bg = LANES // S5_GROUP
    blocks = s5_groups // bg

    def block_vec(v):
        return v.reshape(2, blocks, 1, S5_BLOCK_ST).transpose(1, 0, 2, 3)

    def block_mat_b(v):
        v = v.reshape(2, blocks, bg, S5_STATE, S5_GROUP).transpose(1, 0, 4, 2, 3)
        return v.reshape(blocks, 2, S5_GROUP, S5_BLOCK_ST)

    def block_mat_c(v):
        v = v.reshape(2, blocks, bg, S5_GROUP, S5_STATE).transpose(1, 0, 3, 2, 4)
        return v.reshape(blocks, 2, S5_GROUP, S5_BLOCK_ST)

    log_step = jnp.repeat(s5_log_step[0][:, :, None], S5_STATE, axis=2)
    prm = (block_vec(s5_lam_re[0]), block_vec(s5_lam_im[0]), block_vec(log_step), block_mat_b(s5_b_re[0]),
           block_mat_b(s5_b_im[0]), block_mat_c(s5_c_re[0]), block_mat_c(s5_c_im[0]))
    h0 = state_s5[:, 0].reshape(n_lat, 2, 2, blocks, S5_BLOCK_ST).transpose(3, 1, 2, 0, 4)
    ys, fin = _s5_core(prm, h0, u, n_ctx=n_ctx, l_ctx=l_ctx, n_lat=n_lat, l_lat=l_lat, e=e)
    new_s5 = fin.transpose(3, 1, 2, 0, 4).reshape(n_ctx, 1, 2, 2, s5_groups, S5_STATE)
    y = _s5_glu(ys, u, s5_d[0:1], s5_w_glu[0].astype(BF16), s5_b_glu[0:1])
    x = _outproj(y, u, 1, x, mods, 2, w_out_b[2], name="s5_out", **tiles)

    u = _inproj(x, mods, 3, norm_g[3:4], nat_w_in[0].astype(BF16), name="nat_in", **tiles_in)
    y, new_k, new_v = _ctx_attention(u, None, n_seq=n_ctx, seq_len=l_ctx, e=e)
    y = _nat_attention(u, cache_k[:, 0], cache_v[:, 0], _nat_bias(nat_rpb[0]), y, n_seq=n_lat, seq_len=l_lat,
                       row_block0=t_ctx // l_lat, e=e)
    out = _outproj(y, u, 3, x, mods, 3, w_out_b[3], final_g=final_g[None], name="nat_out", **tiles)

    return (out[:t_ctx].reshape(n_ctx, l_ctx, d), out[t_ctx:].reshape(n_lat, l_lat, d),
            new_ssd[:, None], new_s5, new_k[:, None], new_v[:, None])
```

```python
import functools

import jax
import jax.numpy as jnp
from jax import lax
from jax.experimental import pallas as pl
from jax.experimental.pallas import tpu as pltpu

F32 = jnp.float32
BF16 = jnp.bfloat16
EPS = 1e-6
HIGHEST = lax.Precision.HIGHEST

LANES = 128
CHUNK = 128
SSD_HEAD_DIM = 64
SSD_GROUP_HEADS = 4
SSD_GROUP_CH = SSD_HEAD_DIM * SSD_GROUP_HEADS
SSD_STATE = 128
CONV_W = 5
CONV_HALO = 8
S5_T = 8
S5_GROUP = 16
S5_STATE = 64
S5_BLOCK_ST = (LANES // S5_GROUP) * S5_STATE
HEAD_DIM = 64
GRID_W = 64
WIN_ROWS = 8
WIN_COLS = 16
NAT_ROWS_PER_STEP = 4
MASKED = -1e30
VMEM_LIMIT = 52 * 1024 * 1024


def _silu(x):
    return x * jax.nn.sigmoid(x)


def _gelu(x):
    return 0.5 * x * (1.0 + jnp.tanh(0.7978845608028654 * (x + 0.044715 * (x * x * x))))


def _softplus(x):
    return jnp.maximum(x, 0.0) + jnp.log1p(jnp.exp(-jnp.abs(x)))


def _dot(a, b):
    return jnp.dot(a, b, preferred_element_type=F32)


def _dot_nt(a, b):
    return lax.dot_general(a, b, (((1,), (1,)), ((), ())), preferred_element_type=F32)


def _params(*sem):
    return pltpu.CompilerParams(dimension_semantics=sem, vmem_limit_bytes=VMEM_LIMIT)


def _mod_row(i, *, tm, t_ctx, l_lat):
    start = i * tm
    return jnp.where(start < t_ctx, 0, 1 + (start - t_ctx) // l_lat)


def _mod_body(c_ref, w_ref, b_ref, o_ref):
    cond = _silu(c_ref[...]).astype(BF16)
    o_ref[...] = _dot(cond, w_ref[...].astype(BF16)) + b_ref[...]


def _modulation(cond8, w_mod, b_mod, *, tn=1024):
    depth, d, n = w_mod.shape
    return pl.pallas_call(
        _mod_body,
        grid=(depth, n // tn),
        in_specs=[pl.BlockSpec((8, d), lambda l, j: (0, 0)),
                  pl.BlockSpec((None, d, tn), lambda l, j: (l, 0, j)),
                  pl.BlockSpec((None, 1, tn), lambda l, j: (l, 0, j))],
        out_specs=pl.BlockSpec((None, 8, tn), lambda l, j: (l, 0, j)),
        out_shape=jax.ShapeDtypeStruct((depth, 8, n), F32),
        compiler_params=_params("parallel", "parallel"),
        name="modulation",
    )(cond8, w_mod, b_mod.reshape(depth, 1, n))


def _inproj_body(x_ref, shift_ref, scale_ref, g_ref, w_ref, o_ref, h_ref):
    @pl.when(pl.program_id(1) == 0)
    def _():
        x = x_ref[...]
        y = x * lax.rsqrt(jnp.mean(x * x, axis=-1, keepdims=True) + EPS) * g_ref[...]
        h_ref[...] = (y * (1.0 + scale_ref[...]) + shift_ref[...]).astype(BF16)

    o_ref[...] = _dot(h_ref[...], w_ref[...])


def _inproj(x, mods, layer, g, w, *, tm, tn, t_ctx, l_lat, name):
    t, d = x.shape
    n = w.shape[1]
    row = functools.partial(_mod_row, tm=tm, t_ctx=t_ctx, l_lat=l_lat)

    def mod_spec(part):
        return pl.BlockSpec((None, None, None, 1, d), lambda i, j: (layer, row(i), part, 0, 0))

    return pl.pallas_call(
        _inproj_body,
        grid=(t // tm, n // tn),
        in_specs=[pl.BlockSpec((tm, d), lambda i, j: (i, 0)), mod_spec(0), mod_spec(1),
                  pl.BlockSpec((1, d), lambda i, j: (0, 0)),
                  pl.BlockSpec((d, tn), lambda i, j: (0, j))],
        out_specs=pl.BlockSpec((tm, tn), lambda i, j: (i, j)),
        out_shape=jax.ShapeDtypeStruct((t, n), F32),
        scratch_shapes=[pltpu.VMEM((tm, d), BF16)],
        compiler_params=_params("parallel", "arbitrary"),
        name=name,
    )(x, mods, mods, g, w)


def _outproj_body(*refs, gated_norm, final_norm):
    y_ref, z_ref, x_ref, gate_ref, w_ref = refs[:5]
    rest = list(refs[5:])
    t = y_ref[...] * _silu(z_ref[...])
    if gated_norm:
        ng_ref = rest.pop(0)
        t = t * lax.rsqrt(jnp.mean(t * t, axis=-1, keepdims=True) + EPS) * ng_ref[...]
    if final_norm:
        fg_ref = rest.pop(0)
    o_ref = rest.pop(0)
    xn = x_ref[...] + gate_ref[...] * _dot(t.astype(BF16), w_ref[...])
    if final_norm:
        xn = xn * lax.rsqrt(jnp.mean(xn * xn, axis=-1, keepdims=True) + EPS) * fg_ref[...]
    o_ref[...] = xn


def _outproj(y, u, z_block, x, mods, layer, w, *, norm_g=None, final_g=None, tm, t_ctx, l_lat, name):
    t, e = y.shape
    d = x.shape[1]
    row = functools.partial(_mod_row, tm=tm, t_ctx=t_ctx, l_lat=l_lat)
    in_specs = [pl.BlockSpec((tm, e), lambda i: (i, 0)),
                pl.BlockSpec((tm, e), lambda i: (i, z_block)),
                pl.BlockSpec((tm, d), lambda i: (i, 0)),
                pl.BlockSpec((None, None, None, 1, d), lambda i: (layer, row(i), 2, 0, 0)),
                pl.BlockSpec((e, d), lambda i: (0, 0))]
    args = [y, u, x, mods, w]
    if norm_g is not None:
        in_specs.append(pl.BlockSpec((1, e), lambda i: (0, 0)))
        args.append(norm_g)
    if final_g is not None:
        in_specs.append(pl.BlockSpec((1, d), lambda i: (0, 0)))
        args.append(final_g)
    return pl.pallas_call(
        functools.partial(_outproj_body, gated_norm=norm_g is not None, final_norm=final_g is not None),
        grid=(t // tm,),
        in_specs=in_specs,
        out_specs=pl.BlockSpec((tm, d), lambda i: (i, 0)),
        out_shape=jax.ShapeDtypeStruct((t, d), F32),
        compiler_params=_params("parallel"),
        name=name,
    )(*args)


def _ssd_body(*refs, seq_len, has_h0, want_final):
    refs = list(refs)
    x_ref, b_ref, c_ref, dt_ref, wx_ref, wb_ref, wc_ref, bx_ref, bb_ref, bc_ref, dtb_ref, alog_ref, dsk_ref = refs[:13]
    rest = refs[13:]
    h0_ref = rest.pop(0) if has_h0 else None
    if want_final:
        y_ref, hf_ref = rest[0], rest[1]
        rest = rest[2:]
    else:
        y_ref, hf_ref = rest[0], None
        rest = rest[1:]
    pad_s, xb_s, xt_s, bm_s, bt_s, cm_s, e_s, et_s, ct_s, dtt_s, sf_s, sb_s = rest
    nc = seq_len // CHUNK
    width = SSD_GROUP_CH + 2 * SSD_STATE
    nh = SSD_GROUP_HEADS

    zeros = jnp.zeros((CONV_HALO, width), F32)
    pad_s[0:CONV_HALO, :] = zeros
    pad_s[CONV_HALO + seq_len:2 * CONV_HALO + seq_len, :] = zeros

    def copy_chunk(ci, carry):
        base = pl.multiple_of(ci * CHUNK, CHUNK)
        dst = pl.ds(base + CONV_HALO, CHUNK)
        pad_s[dst, 0:SSD_GROUP_CH] = x_ref[pl.ds(base, CHUNK), :]
        pad_s[dst, SSD_GROUP_CH:SSD_GROUP_CH + SSD_STATE] = b_ref[pl.ds(base, CHUNK), :]
        pad_s[dst, SSD_GROUP_CH + SSD_STATE:width] = c_ref[pl.ds(base, CHUNK), :]
        return carry

    lax.fori_loop(0, nc, copy_chunk, 0)

    conv_w = jnp.concatenate([wx_ref[...], wb_ref[...], wc_ref[...]], axis=1)
    conv_b = jnp.concatenate([bx_ref[...], bb_ref[...], bc_ref[...]], axis=1)
    a_row = -jnp.exp(alog_ref[...])
    first = CONV_HALO - CONV_W // 2
    row_i = lax.broadcasted_iota(jnp.int32, (CHUNK, CHUNK), 0)
    col_i = lax.broadcasted_iota(jnp.int32, (CHUNK, CHUNK), 1)
    lower = row_i >= col_i
    upper = row_i <= col_i
    lower_f = lower.astype(F32)
    fwd_lane = lax.broadcasted_iota(jnp.int32, (1, LANES), 1) < nh

    def prep_chunk(ci, carry):
        base = pl.multiple_of(ci * CHUNK, CHUNK)
        win = pad_s[pl.ds(base, CHUNK + 2 * CONV_HALO), :]
        acc = conv_b + win[first:first + CHUNK] * conv_w[0:1]
        for j in range(1, CONV_W):
            acc = acc + win[first + j:first + j + CHUNK] * conv_w[j:j + 1]
        xbc = _silu(acc)
        rows = pl.ds(base, CHUNK)
        x = xbc[:, 0:SSD_GROUP_CH]
        bm = xbc[:, SSD_GROUP_CH:SSD_GROUP_CH + SSD_STATE]
        y_ref[rows, :] = x * dsk_ref[...]
        xb_s[rows, :] = x.astype(BF16)
        xt_s[ci] = x.T
        bm_s[rows, :] = bm.astype(BF16)
        bt_s[ci] = bm.T.astype(BF16)
        cm_s[rows, :] = xbc[:, SSD_GROUP_CH + SSD_STATE:width].astype(BF16)
        dt = _softplus(dt_ref[rows, :] + dtb_ref[...])
        da = dt * a_row
        cum = jnp.dot(lower_f, da, precision=HIGHEST, preferred_element_type=F32)
        e = jnp.where(fwd_lane, cum, cum - da)
        e_s[rows, :] = e
        et_s[ci] = e.T[0:2 * nh]
        ct_s[ci] = cum.T[0:2 * nh]
        dtt_s[ci] = dt.T[0:2 * nh]
        return carry

    lax.fori_loop(0, nc, prep_chunk, 0)

    if has_h0:
        sf_s[...] = h0_ref[0].reshape(SSD_GROUP_CH, SSD_STATE)
        sb_s[...] = h0_ref[1].reshape(SSD_GROUP_CH, SSD_STATE)
    else:
        sf_s[...] = jnp.zeros((SSD_GROUP_CH, SSD_STATE), F32)
        sb_s[...] = jnp.zeros((SSD_GROUP_CH, SSD_STATE), F32)

    head_of_lane = lax.broadcasted_iota(jnp.int32, (CHUNK, SSD_GROUP_CH), 1) // SSD_HEAD_DIM
    st_refs = (sf_s, sb_s)

    def scan_chunk(i, carry):
        chunks = (i, nc - 1 - i)
        rows = [pl.ds(pl.multiple_of(ci * CHUNK, CHUNK), CHUNK) for ci in chunks]
        s_prev = [st_refs[d][...] for d in range(2)]
        g, cs = [], []
        for d in range(2):
            cm = cm_s[rows[d], :]
            g.append(_dot(cm, bt_s[chunks[d]]))
            cs.append(_dot_nt(cm, s_prev[d].astype(BF16)))
        w_rows, keeps, ys = [], [], []
        for d in range(2):
            e = e_s[rows[d], :]
            e_t, c_t, dt_t = et_s[chunks[d]], ct_s[chunks[d]], dtt_s[chunks[d]]
            xb = xb_s[rows[d], :]
            y = jnp.zeros((CHUNK, SSD_GROUP_CH), F32)
            w_d, keep_d = [], []
            for r in range(nh):
                col = r + nh * d
                ec = jnp.broadcast_to(e[:, col:col + 1], (CHUNK, CHUNK))
                er = e_t[col:col + 1, :]
                dt_r = dt_t[col:col + 1, :]
                tot = c_t[col:col + 1, CHUNK - 1:CHUNK]
                if d == 0:
                    decay = jnp.exp(jnp.where(lower, ec - er, -jnp.inf))
                    off = jnp.exp(ec)
                    w_d.append(jnp.broadcast_to(dt_r * jnp.exp(tot - er), (SSD_HEAD_DIM, CHUNK)))
                else:
                    decay = jnp.exp(jnp.where(upper, er - ec, -jnp.inf))
                    off = jnp.exp(tot - ec)
                    w_d.append(jnp.broadcast_to(dt_r * jnp.exp(er), (SSD_HEAD_DIM, CHUNK)))
                keep_d.append(jnp.broadcast_to(jnp.exp(tot), (SSD_HEAD_DIM, SSD_STATE)))
                yd = _dot((g[d] * decay * dt_r).astype(BF16), xb)
                off2 = jnp.concatenate([off] * (SSD_GROUP_CH // CHUNK), axis=1)
                y = jnp.where(head_of_lane == r, yd + cs[d] * off2, y)
            ys.append(y)
            w_rows.append(jnp.concatenate(w_d, axis=0))
            keeps.append(jnp.concatenate(keep_d, axis=0))
        for d in range(2):
            w_t = (xt_s[chunks[d]] * w_rows[d]).astype(BF16)
            st_refs[d][...] = keeps[d] * s_prev[d] + _dot(w_t, bm_s[rows[d], :])
            y_ref[rows[d], :] += ys[d]
        return carry

    lax.fori_loop(0, nc, scan_chunk, 0)

    if want_final:
        hf_ref[0] = sf_s[...].reshape(SSD_GROUP_HEADS, SSD_HEAD_DIM, SSD_STATE)
        hf_ref[1] = sb_s[...].reshape(SSD_GROUP_HEADS, SSD_HEAD_DIM, SSD_STATE)


def _ssd_call(u, dt, conv_w, conv_b, dtb, alog, dsk, y_prev, h0, *, n_seq, seq_len, row_block0, want_final, name):
    t, n_u = u.shape
    e = dsk.shape[1]
    groups = e // SSD_GROUP_CH
    xb0 = e // SSD_GROUP_CH
    bb0 = 2 * e // SSD_STATE
    cb0 = bb0 + groups
    has_h0 = h0 is not None
    seq = lambda b: row_block0 + b
    in_specs = [pl.BlockSpec((seq_len, SSD_GROUP_CH), lambda b, g: (seq(b), xb0 + g)),
                pl.BlockSpec((seq_len, SSD_STATE), lambda b, g: (seq(b), bb0 + g)),
                pl.BlockSpec((seq_len, SSD_STATE), lambda b, g: (seq(b), cb0 + g)),
                pl.BlockSpec((seq_len, LANES), lambda b, g: (seq(b), g)),
                pl.BlockSpec((CONV_W, SSD_GROUP_CH), lambda b, g: (0, g)),
                pl.BlockSpec((CONV_W, SSD_STATE), lambda b, g: (0, bb0 - xb0 * 2 + g)),
                pl.BlockSpec((CONV_W, SSD_STATE), lambda b, g: (0, cb0 - xb0 * 2 + g)),
                pl.BlockSpec((1, SSD_GROUP_CH), lambda b, g: (0, g)),
                pl.BlockSpec((1, SSD_STATE), lambda b, g: (0, bb0 - xb0 * 2 + g)),
                pl.BlockSpec((1, SSD_STATE), lambda b, g: (0, cb0 - xb0 * 2 + g)),
                pl.BlockSpec((1, LANES), lambda b, g: (0, g)),
                pl.BlockSpec((1, LANES), lambda b, g: (0, g)),
                pl.BlockSpec((1, SSD_GROUP_CH), lambda b, g: (0, g))]
    args = [u, u, u, dt, conv_w, conv_w, conv_w, conv_b, conv_b, conv_b, dtb, alog, dsk]
    state_block = (None, 2, SSD_GROUP_HEADS, SSD_HEAD_DIM, SSD_STATE)
    if has_h0:
        in_specs.append(pl.BlockSpec(state_block, lambda b, g: (b, 0, g, 0, 0)))
        args.append(h0)
    aliases = {}
    if y_prev is not None:
        in_specs.append(pl.BlockSpec(memory_space=pl.ANY))
        args.append(y_prev)
        aliases = {len(args) - 1: 0}
    out_specs = [pl.BlockSpec((seq_len, SSD_GROUP_CH), lambda b, g: (seq(b), g))]
    out_shape = [jax.ShapeDtypeStruct((t, e), F32)]
    if want_final:
        out_specs.append(pl.BlockSpec(state_block, lambda b, g: (b, 0, g, 0, 0)))
        out_shape.append(jax.ShapeDtypeStruct((n_seq, 2, e // SSD_HEAD_DIM, SSD_HEAD_DIM, SSD_STATE), F32))
    width = SSD_GROUP_CH + 2 * SSD_STATE
    nc = seq_len // CHUNK

    def body(*refs):
        refs = list(refs)
        if y_prev is not None:
            n_in = len(args)
            del refs[n_in - 1]
        _ssd_body(*refs, seq_len=seq_len, has_h0=has_h0, want_final=want_final)

    return pl.pallas_call(
        body,
        grid=(n_seq, groups),
        in_specs=in_specs,
        out_specs=out_specs,
        out_shape=out_shape,
        input_output_aliases=aliases,
        scratch_shapes=[pltpu.VMEM((seq_len + 2 * CONV_HALO, width), F32),
                        pltpu.VMEM((seq_len, SSD_GROUP_CH), BF16),
                        pltpu.VMEM((nc, SSD_GROUP_CH, CHUNK), F32),
                        pltpu.VMEM((seq_len, SSD_STATE), BF16),
                        pltpu.VMEM((nc, SSD_STATE, CHUNK), BF16),
                        pltpu.VMEM((seq_len, SSD_STATE), BF16),
                        pltpu.VMEM((seq_len, LANES), F32),
                        pltpu.VMEM((nc, 2 * SSD_GROUP_HEADS, CHUNK), F32),
                        pltpu.VMEM((nc, 2 * SSD_GROUP_HEADS, CHUNK), F32),
                        pltpu.VMEM((nc, 2 * SSD_GROUP_HEADS, CHUNK), F32),
                        pltpu.VMEM((SSD_GROUP_CH, SSD_STATE), F32),
                        pltpu.VMEM((SSD_GROUP_CH, SSD_STATE), F32)],
        compiler_params=_params("parallel", "parallel"),
        name=name,
    )(*args)


def _group_lanes(v, groups):
    per_group = v.reshape(2, groups, SSD_GROUP_HEADS).transpose(1, 0, 2).reshape(groups, 2 * SSD_GROUP_HEADS)
    return jnp.pad(per_group, ((0, 0), (0, LANES - 2 * SSD_GROUP_HEADS))).reshape(1, groups * LANES)


def _gmlp_body(u_ref, v_ref, lng_ref, lnb_ref, ws_ref, bs_ref, o_ref, *, tm):
    groups = ws_ref.shape[0]
    gch = u_ref.shape[1] // groups
    for c in range(tm // CHUNK):
        rows = slice(c * CHUNK, (c + 1) * CHUNK)
        v = _gelu(v_ref[rows, :])
        vc = v - jnp.mean(v, axis=-1, keepdims=True)
        vn = vc * lax.rsqrt(jnp.mean(vc * vc, axis=-1, keepdims=True) + EPS) * lng_ref[...] + lnb_ref[...]
        vb = vn.astype(BF16)
        for g in range(groups):
            cols = slice(g * gch, (g + 1) * gch)
            s = _dot(ws_ref[g].astype(BF16), vb[:, cols]) + bs_ref[:, cols]
            o_ref[rows, cols] = _gelu(u_ref[rows, cols]) * s


def _gmlp(u, ln_g, ln_b, w_s, b_lanes, *, tm=256):
    t = u.shape[0]
    e = ln_g.shape[1]
    return pl.pallas_call(
        functools.partial(_gmlp_body, tm=tm),
        grid=(t // tm,),
        in_specs=[pl.BlockSpec((tm, e), lambda i: (i, 0)),
                  pl.BlockSpec((tm, e), lambda i: (i, 1)),
                  pl.BlockSpec((1, e), lambda i: (0, 0)),
                  pl.BlockSpec((1, e), lambda i: (0, 0)),
                  pl.BlockSpec(w_s.shape, lambda i: (0, 0, 0)),
                  pl.BlockSpec((CHUNK, e), lambda i: (0, 0))],
        out_specs=pl.BlockSpec((tm, e), lambda i: (i, 0)),
        out_shape=jax.ShapeDtypeStruct((t, e), F32),
        compiler_params=_params("parallel"),
        name="gmlp",
    )(u, u, ln_g, ln_b, w_s, b_lanes)


def _s5_body(lr_ref, li_ref, ls_ref, brt_ref, bit_ref, cre_ref, cim_ref, h0_ref, u_ref, y_ref, fin_ref,
             win_s, t_s, ef_s, eb_s, z_s, a_s, zin_s, d_s, spf_s, spb_s, yo_s,
             *, seqs, l_ctx, n_lat, l_lat, ctx_parts):
    bst = S5_BLOCK_ST
    ng = LANES // S5_GROUP
    kw = S5_T * LANES
    part = pl.program_id(1)

    @pl.when(part == 0)
    def _build():
        own = (lax.broadcasted_iota(jnp.int32, (ng, S5_GROUP, bst), 2) // S5_STATE
               == lax.broadcasted_iota(jnp.int32, (ng, S5_GROUP, bst), 0))

        def spread(v):
            return jnp.where(own, v[None], 0.0).reshape(LANES, bst).astype(BF16)

        tau = lax.broadcasted_iota(jnp.int32, (S5_T + 8, 1), 0).astype(F32)
        e_refs = (ef_s, eb_s)
        for d in range(2):
            lr, li = lr_ref[d], li_ref[d]
            step = jnp.exp(ls_ref[d])
            mag = jnp.exp(tau * (lr * step))
            p_re = mag * jnp.cos(tau * (li * step))
            p_im = mag * jnp.sin(tau * (li * step))
            ab_re, ab_im = p_re[1:2], p_im[1:2]
            den = lr * lr + li * li
            nr = ab_re - 1.0
            cr = (nr * lr + ab_im * li) / den
            ci = (ab_im * lr - nr * li) / den
            brt, bit = brt_ref[d], bit_ref[d]
            bb_re = cr * brt - ci * bit
            bb_im = cr * bit + ci * brt
            cre, cim = cre_ref[d], cim_ref[d]
            for k in range(S5_T):
                rows = slice(k * LANES, (k + 1) * LANES)
                tq = S5_T - 1 - k if d == 0 else k
                te = k + 1 if d == 0 else S5_T - k
                pr, pi = p_re[tq:tq + 1], p_im[tq:tq + 1]
                win_s[rows, 2 * d * bst:(2 * d + 1) * bst] = spread(pr * bb_re - pi * bb_im)
                win_s[rows, (2 * d + 1) * bst:(2 * d + 2) * bst] = spread(pr * bb_im + pi * bb_re)
                pr, pi = p_re[te:te + 1], p_im[te:te + 1]
                e_refs[d][rows, 0:bst] = spread(cre * pr - cim * pi)
                e_refs[d][rows, bst:2 * bst] = spread(-(cre * pi + cim * pr))
            c_own = jnp.concatenate([spread(cre), spread(-cim)], axis=1)
            z = _dot_nt(win_s[:, 2 * d * bst:(2 * d + 2) * bst], c_own)
            if d == 0:
                z_s[0:kw, :] = z
            else:
                z_s[kw - LANES:kw, :] += z[0:LANES]
                z_s[kw:2 * kw - LANES, :] = z[LANES:kw]
            a_s[2 * d:2 * d + 1, :] = p_re[S5_T:S5_T + 1]
            a_s[2 * d + 1:2 * d + 2, :] = p_im[S5_T:S5_T + 1]
        for k in range(S5_T):
            r0 = (S5_T - 1 - k) * LANES
            t_s[:, k * LANES:(k + 1) * LANES] = z_s[r0:r0 + kw, :].astype(BF16)

    def outputs():
        yo_s[...] = (_dot(zin_s[...], t_s[...]) + _dot_nt(spf_s[...].astype(BF16), ef_s[...])
                     + _dot_nt(spb_s[...].astype(BF16), eb_s[...]))

    @pl.when(part < ctx_parts)
    def _context():
        nj = l_ctx // S5_T
        for j in range(nj):
            for k in range(S5_T):
                zin_s[j * seqs:(j + 1) * seqs, k * LANES:(k + 1) * LANES] = (
                    u_ref[pl.ds(j * S5_T + k, seqs, stride=l_ctx), :].astype(BF16))
        d_s[...] = _dot(zin_s[...], win_s[...])
        coef = [jnp.broadcast_to(a_s[i:i + 1, :], (seqs, bst)) for i in range(4)]
        zero = jnp.zeros((seqs, bst), F32)

        def step(j, carry):
            fr, fi, br, bi = carry
            rf = pl.ds(pl.multiple_of(j * seqs, seqs), seqs)
            rb = pl.ds(pl.multiple_of((nj - 1 - j) * seqs, seqs), seqs)
            spf_s[rf, 0:bst] = fr
            spf_s[rf, bst:2 * bst] = fi
            spb_s[rb, 0:bst] = br
            spb_s[rb, bst:2 * bst] = bi
            return (coef[0] * fr - coef[1] * fi + d_s[rf, 0:bst],
                    coef[0] * fi + coef[1] * fr + d_s[rf, bst:2 * bst],
                    coef[2] * br - coef[3] * bi + d_s[rb, 2 * bst:3 * bst],
                    coef[2] * bi + coef[3] * br + d_s[rb, 3 * bst:4 * bst])

        fr, fi, br, bi = lax.fori_loop(0, nj, step, (zero, zero, zero, zero))
        fin_ref[0, 0] = fr
        fin_ref[0, 1] = fi
        fin_ref[1, 0] = br
        fin_ref[1, 1] = bi
        outputs()
        for j in range(nj):
            for k in range(S5_T):
                y_ref[pl.ds(j * S5_T + k, seqs, stride=l_ctx), :] = (
                    yo_s[j * seqs:(j + 1) * seqs, k * LANES:(k + 1) * LANES])

    @pl.when(part == ctx_parts)
    def _latent():
        nj = l_lat // S5_T
        for b in range(n_lat):
            for k in range(S5_T):
                zin_s[b * nj:(b + 1) * nj, k * LANES:(k + 1) * LANES] = (
                    u_ref[pl.ds(b * l_lat + k, nj, stride=S5_T), :].astype(BF16))
        d_s[...] = _dot(zin_s[...], win_s[...])
        coef_re = jnp.concatenate([jnp.broadcast_to(a_s[0:1, :], (n_lat, bst)),
                                   jnp.broadcast_to(a_s[2:3, :], (n_lat, bst))], axis=0)
        coef_im = jnp.concatenate([jnp.broadcast_to(a_s[1:2, :], (n_lat, bst)),
                                   jnp.broadcast_to(a_s[3:4, :], (n_lat, bst))], axis=0)
        s_re0 = jnp.concatenate([h0_ref[0, 0], h0_ref[1, 0]], axis=0)
        s_im0 = jnp.concatenate([h0_ref[0, 1], h0_ref[1, 1]], axis=0)

        def step(j, carry):
            s_re, s_im = carry
            d_re, d_im = [], []
            for b in range(n_lat):
                row = pl.ds(b * nj + j, 1)
                spf_s[row, 0:bst] = s_re[b:b + 1]
                spf_s[row, bst:2 * bst] = s_im[b:b + 1]
                d_re.append(d_s[row, 0:bst])
                d_im.append(d_s[row, bst:2 * bst])
            for b in range(n_lat):
                row = pl.ds(b * nj + nj - 1 - j, 1)
                spb_s[row, 0:bst] = s_re[n_lat + b:n_lat + b + 1]
                spb_s[row, bst:2 * bst] = s_im[n_lat + b:n_lat + b + 1]
                d_re.append(d_s[row, 2 * bst:3 * bst])
                d_im.append(d_s[row, 3 * bst:4 * bst])
            d_re = jnp.concatenate(d_re, axis=0)
            d_im = jnp.concatenate(d_im, axis=0)
            return coef_re * s_re - coef_im * s_im + d_re, coef_re * s_im + coef_im * s_re + d_im

        lax.fori_loop(0, nj, step, (s_re0, s_im0))
        outputs()
        for b in range(n_lat):
            for k in range(S5_T):
                y_ref[pl.ds(b * l_lat + k, nj, stride=S5_T), :] = yo_s[b * nj:(b + 1) * nj, k * LANES:(k + 1) * LANES]


def _s5_core(prm, h0, u, *, n_ctx, l_ctx, n_lat, l_lat, e):
    blocks = prm[0].shape[0]
    part_tokens = n_lat * l_lat
    t = u.shape[0]
    n_parts = t // part_tokens
    seqs = part_tokens // l_ctx
    r = part_tokens // S5_T
    vec = pl.BlockSpec((None, 2, 1, S5_BLOCK_ST), lambda g, p: (g, 0, 0, 0))
    mat = pl.BlockSpec((None, 2, S5_GROUP, S5_BLOCK_ST), lambda g, p: (g, 0, 0, 0))
    k_in = S5_T * LANES
    return pl.pallas_call(
        functools.partial(_s5_body, seqs=seqs, l_ctx=l_ctx, n_lat=n_lat, l_lat=l_lat, ctx_parts=n_parts - 1),
        grid=(blocks, n_parts),
        in_specs=[vec, vec, vec, mat, mat, mat, mat,
                  pl.BlockSpec((None, 2, 2, n_lat, S5_BLOCK_ST), lambda g, p: (g, 0, 0, 0, 0)),
                  pl.BlockSpec((part_tokens, LANES), lambda g, p: (p, g))],
        out_specs=[pl.BlockSpec((part_tokens, LANES), lambda g, p: (p, g)),
                   pl.BlockSpec((None, 2, 2, seqs, S5_BLOCK_ST),
                                lambda g, p: (g, 0, 0, jnp.minimum(p, n_parts - 2), 0))],
        out_shape=[jax.ShapeDtypeStruct((t, e), F32),
                   jax.ShapeDtypeStruct((blocks, 2, 2, n_ctx, S5_BLOCK_ST), F32)],
        scratch_shapes=[pltpu.VMEM((k_in, 4 * S5_BLOCK_ST), BF16),
                        pltpu.VMEM((k_in, k_in), BF16),
                        pltpu.VMEM((k_in, 2 * S5_BLOCK_ST), BF16),
                        pltpu.VMEM((k_in, 2 * S5_BLOCK_ST), BF16),
                        pltpu.VMEM(((2 * S5_T - 1) * LANES, LANES), F32),
                        pltpu.VMEM((8, S5_BLOCK_ST), F32),
                        pltpu.VMEM((r, k_in), BF16),
                        pltpu.VMEM((r, 4 * S5_BLOCK_ST), F32),
                        pltpu.VMEM((r, 2 * S5_BLOCK_ST), F32),
                        pltpu.VMEM((r, 2 * S5_BLOCK_ST), F32),
                        pltpu.VMEM((r, k_in), F32)],
        compiler_params=_params("parallel", "arbitrary"),
        name="s5_core",
    )(*prm, h0, u)


def _s5_glu_body(ys_ref, u_ref, dsk_ref, w_ref, b_ref, o_ref):
    y = _gelu(ys_ref[...] + dsk_ref[...] * u_ref[...])
    o_ref[...] = y * jax.nn.sigmoid(_dot(y.astype(BF16), w_ref[...]) + b_ref[...])


def _s5_glu(ys, u, dsk, w, b, *, tm=512):
    t, e = ys.shape
    return pl.pallas_call(
        _s5_glu_body,
        grid=(t // tm,),
        in_specs=[pl.BlockSpec((tm, e), lambda i: (i, 0)),
                  pl.BlockSpec((tm, e), lambda i: (i, 0)),
                  pl.BlockSpec((1, e), lambda i: (0, 0)),
                  pl.BlockSpec((e, e), lambda i: (0, 0)),
                  pl.BlockSpec((1, e), lambda i: (0, 0))],
        out_specs=pl.BlockSpec((tm, e), lambda i: (i, 0)),
        out_shape=jax.ShapeDtypeStruct((t, e), F32),
        compiler_params=_params("parallel"),
        name="s5_glu",
    )(ys, u, dsk, w, b)


def _ctx_attn_body(q_ref, k_ref, v_ref, o_ref, ko_ref, vo_ref):
    scale = HEAD_DIM ** -0.5
    seq_len = q_ref.shape[0]
    first = lax.broadcasted_iota(jnp.int32, (seq_len, LANES), 1) < HEAD_DIM
    masks = (first, jnp.logical_not(first))
    n_pairs = q_ref.shape[1] // LANES
    logits, values = [], []
    for pair in range(n_pairs):
        cols = slice(pair * LANES, (pair + 1) * LANES)
        q2, k2, v2 = q_ref[:, cols] * scale, k_ref[:, cols], v_ref[:, cols]
        kb = k2.astype(BF16)
        values.append(v2.astype(BF16))
        for h in range(LANES // HEAD_DIM):
            ko_ref[2 * pair + h] = k2[:, h * HEAD_DIM:(h + 1) * HEAD_DIM]
            vo_ref[2 * pair + h] = v2[:, h * HEAD_DIM:(h + 1) * HEAD_DIM]
            logits.append(_dot_nt(jnp.where(masks[h], q2, 0.0).astype(BF16), kb))
    probs = []
    for s in logits:
        p = jnp.exp(s - jnp.max(s, axis=-1, keepdims=True))
        probs.append((p * (1.0 / jnp.sum(p, axis=-1, keepdims=True))).astype(BF16))
    for pair in range(n_pairs):
        o0, o1 = _dot(probs[2 * pair], values[pair]), _dot(probs[2 * pair + 1], values[pair])
        o_ref[:, pair * LANES:(pair + 1) * LANES] = jnp.where(first, o0, o1)


def _ctx_attention(u, y_prev, *, n_seq, seq_len, e, width=512):
    t = u.shape[0]
    hp = e // width
    heads = e // HEAD_DIM
    kv_block = (None, width // HEAD_DIM, seq_len, HEAD_DIM)
    in_specs = [pl.BlockSpec((seq_len, width), lambda b, h: (b, h)),
                pl.BlockSpec((seq_len, width), lambda b, h: (b, hp + h)),
                pl.BlockSpec((seq_len, width), lambda b, h: (b, 2 * hp + h))]
    return pl.pallas_call(
        _ctx_attn_body,
        grid=(n_seq, hp),
        in_specs=in_specs,
        out_specs=[pl.BlockSpec((seq_len, width), lambda b, h: (b, h)),
                   pl.BlockSpec(kv_block, lambda b, h: (b, h, 0, 0)),
                   pl.BlockSpec(kv_block, lambda b, h: (b, h, 0, 0))],
        out_shape=[jax.ShapeDtypeStruct((t, e), F32),
                   jax.ShapeDtypeStruct((n_seq, heads, seq_len, HEAD_DIM), F32),
                   jax.ShapeDtypeStruct((n_seq, heads, seq_len, HEAD_DIM), F32)],
        compiler_params=_params("parallel", "parallel"),
        name="ctx_attention",
    )(u, u, u)


def _nat_bias_body(rpb_ref, o_ref, *, n_rel_rows, n_rel_cols):
    h = pl.program_id(0)
    q = lax.broadcasted_iota(jnp.int32, (GRID_W, GRID_W), 0)
    kc = lax.broadcasted_iota(jnp.int32, (GRID_W, GRID_W), 1)
    rel = kc - q + (WIN_COLS - 1)
    c_start = jnp.clip(q - WIN_COLS // 2, 0, GRID_W - WIN_COLS)
    ok = (kc >= c_start) & (kc < c_start + WIN_COLS)
    for i in range(n_rel_rows):
        acc = jnp.zeros((GRID_W, GRID_W), F32)
        for dcol in range(n_rel_cols):
            acc = jnp.where(rel == dcol, rpb_ref[(h * n_rel_rows + i) * n_rel_cols + dcol], acc)
        tile = jnp.where(ok, acc, MASKED)
        if i < n_rel_rows - 1:
            o_ref[i, :, 0:GRID_W] = tile
        if i > 0:
            o_ref[i - 1, :, GRID_W:2 * GRID_W] = tile


def _nat_bias(rpb):
    heads, n_rel_rows, n_rel_cols = rpb.shape
    return pl.pallas_call(
        functools.partial(_nat_bias_body, n_rel_rows=n_rel_rows, n_rel_cols=n_rel_cols),
        grid=(heads,),
        in_specs=[pl.BlockSpec(memory_space=pltpu.SMEM)],
        out_specs=pl.BlockSpec((None, n_rel_rows - 1, GRID_W, 2 * GRID_W), lambda h: (h, 0, 0, 0)),
        out_shape=jax.ShapeDtypeStruct((heads, n_rel_rows - 1, GRID_W, 2 * GRID_W), F32),
        compiler_params=_params("parallel"),
        name="nat_bias",
    )(rpb.reshape(-1))


def _nat_body(q_ref, k_ref, v_ref, ck_ref, cv_ref, bias_ref, o_ref, kb_s, vb_s, ckb_s, cvb_s, *, rows):
    scale = HEAD_DIM ** -0.5
    wr = min(WIN_ROWS, rows)
    nw = wr * GRID_W
    kb_s[...] = k_ref[...].astype(BF16)
    vb_s[...] = v_ref[...].astype(BF16)
    ckb_s[...] = jnp.concatenate([ck_ref[0], ck_ref[1]], axis=1).astype(BF16)
    cvb_s[...] = jnp.concatenate([cv_ref[0], cv_ref[1]], axis=1).astype(BF16)
    first = lax.broadcasted_iota(jnp.int32, (GRID_W, LANES), 1) < HEAD_DIM

    heads = tuple(range(LANES // HEAD_DIM))
    masks = (first, jnp.logical_not(first))

    def row_group(g, carry):
        rws = [g * NAT_ROWS_PER_STEP + i for i in range(NAT_ROWS_PER_STEP)]
        starts = [jnp.clip(r - wr // 2, 0, rows - wr) for r in rws]
        q_rows = [pl.ds(pl.multiple_of(r * GRID_W, GRID_W), GRID_W) for r in rws]
        k_rows = [pl.ds(pl.multiple_of(rs * GRID_W, GRID_W), nw) for rs in starts]
        logits = []
        for r, rs, qr, kr in zip(rws, starts, q_rows, k_rows):
            q2 = q_ref[qr, :] * scale
            kb = kb_s[kr, :]
            for h in heads:
                q = jnp.where(masks[h], q2, 0.0).astype(BF16)
                logits.append((_dot_nt(q, kb), _dot_nt(q, ckb_s[...])))
        probs = []
        for idx, (s_win, s_ctx) in enumerate(logits):
            r, rs, h = rws[idx // 2], starts[idx // 2], heads[idx % 2]
            i0 = (WIN_ROWS - 1) - (r - rs)
            s_win = s_win + jnp.concatenate([bias_ref[h, i0 + 2 * jj] for jj in range(wr // 2)], axis=1)
            m = jnp.maximum(jnp.max(s_win, axis=-1, keepdims=True), jnp.max(s_ctx, axis=-1, keepdims=True))
            p_win = jnp.exp(s_win - m)
            p_ctx = jnp.exp(s_ctx - m)
            inv = 1.0 / (jnp.sum(p_win, axis=-1, keepdims=True) + jnp.sum(p_ctx, axis=-1, keepdims=True))
            probs.append(((p_win * inv).astype(BF16), (p_ctx * inv).astype(BF16)))
        for i, (qr, kr) in enumerate(zip(q_rows, k_rows)):
            vb = vb_s[kr, :]
            outs = [_dot(probs[2 * i + h][0], vb) + _dot(probs[2 * i + h][1], cvb_s[...]) for h in heads]
            o_ref[qr, :] = jnp.where(first, outs[0], outs[1])
        return carry

    lax.fori_loop(0, rows // NAT_ROWS_PER_STEP, row_group, 0)


def _nat_attention(u, cache_k, cache_v, bias, y_prev, *, n_seq, seq_len, row_block0, e):
    t = u.shape[0]
    hp = e // LANES
    hpb = LANES // HEAD_DIM
    past = cache_k.shape[2]
    seq = lambda b: row_block0 + b
    cache_block = (None, hpb, past, HEAD_DIM)
    return pl.pallas_call(
        lambda q, k, v, ck, cv, bs, yp, o, *scratch: _nat_body(q, k, v, ck, cv, bs, o, *scratch,
                                                               rows=seq_len // GRID_W),
        grid=(n_seq, hp),
        in_specs=[pl.BlockSpec((seq_len, LANES), lambda b, h: (seq(b), h)),
                  pl.BlockSpec((seq_len, LANES), lambda b, h: (seq(b), hp + h)),
                  pl.BlockSpec((seq_len, LANES), lambda b, h: (seq(b), 2 * hp + h)),
                  pl.BlockSpec(cache_block, lambda b, h: (b, h, 0, 0)),
                  pl.BlockSpec(cache_block, lambda b, h: (b, h, 0, 0)),
                  pl.BlockSpec((hpb,) + bias.shape[1:], lambda b, h: (h, 0, 0, 0)),
                  pl.BlockSpec(memory_space=pl.ANY)],
        out_specs=pl.BlockSpec((seq_len, LANES), lambda b, h: (seq(b), h)),
        out_shape=jax.ShapeDtypeStruct((t, e), F32),
        input_output_aliases={6: 0},
        scratch_shapes=[pltpu.VMEM((seq_len, LANES), BF16), pltpu.VMEM((seq_len, LANES), BF16),
                        pltpu.VMEM((past, LANES), BF16), pltpu.VMEM((past, LANES), BF16)],
        compiler_params=_params("parallel", "parallel"),
        name="nat_attention",
    )(u, u, u, cache_k, cache_v, bias, y_prev)


def kernel(x_prompt, x_sample, state_ssd, state_s5, cache_k, cache_v, c, c_ctx, norm_g, w_mod, b_mod, w_out, final_g, ssd_w_in, ssd_conv_w, ssd_conv_b, ssd_dt_bias, ssd_a_log, ssd_d, ssd_norm_g, mlp_w_in, mlp_ln_g, mlp_ln_b, mlp_w_s, mlp_b_s, s5_w_in, s5_lam_re, s5_lam_im, s5_log_step, s5_b_re, s5_b_im, s5_c_re, s5_c_im, s5_d, s5_w_glu, s5_b_glu, nat_w_in, nat_rpb):
    n_ctx, l_ctx, d = x_prompt.shape
    n_lat, l_lat, _ = x_sample.shape
    t_ctx = n_ctx * l_ctx
    depth = norm_g.shape[0]
    e = w_out.shape[1]
    assert depth == 4 and l_lat % l_ctx == 0 and t_ctx % (n_lat * l_lat) == 0
    tiles = dict(tm=512, t_ctx=t_ctx, l_lat=l_lat)
    tiles_in = dict(tm=1024, tn=1024, t_ctx=t_ctx, l_lat=l_lat)

    x = jnp.concatenate([x_prompt.reshape(t_ctx, d), x_sample.reshape(n_lat * l_lat, d)], axis=0)
    cond8 = jnp.concatenate([c_ctx[None], c, jnp.zeros((8 - 1 - n_lat, d), F32)], axis=0)
    mods = _modulation(cond8, w_mod, b_mod).reshape(depth, 8, 3, 1, d)
    w_out_b = w_out.astype(BF16)

    groups = e // SSD_GROUP_CH
    n_main = 3 * e
    w_dt = ssd_w_in[0][:, n_main:].reshape(d, 2, groups, SSD_GROUP_HEADS).transpose(0, 2, 1, 3)
    w_dt = jnp.pad(w_dt.reshape(d, groups, 2 * SSD_GROUP_HEADS), ((0, 0), (0, 0), (0, LANES - 2 * SSD_GROUP_HEADS)))
    u = _inproj(x, mods, 0, norm_g[0:1], ssd_w_in[0][:, :n_main].astype(BF16), name="ssd_in", **tiles_in)
    dt = _inproj(x, mods, 0, norm_g[0:1], w_dt.reshape(d, groups * LANES).astype(BF16), name="ssd_dt_in", **tiles_in)
    ssd_args = (ssd_conv_w[0], ssd_conv_b[0:1], _group_lanes(ssd_dt_bias[0], groups),
                _group_lanes(ssd_a_log[0], groups), jnp.repeat(ssd_d[0], SSD_HEAD_DIM)[None])
    y, new_ssd = _ssd_call(u, dt, *ssd_args, None, None, n_seq=n_ctx, seq_len=l_ctx, row_block0=0,
                           want_final=True, name="ssd_ctx")
    y, = _ssd_call(u, dt, *ssd_args, y, state_ssd[:, 0], n_seq=n_lat, seq_len=l_lat, row_block0=t_ctx // l_lat,
                   want_final=False, name="ssd_lat")
    x = _outproj(y, u, 0, x, mods, 0, w_out_b[0], norm_g=ssd_norm_g[0:1], name="ssd_out", **tiles)

    u = _inproj(x, mods, 1, norm_g[1:2], mlp_w_in[0].astype(BF16), name="mlp_in", **tiles_in)
    b_lanes = jnp.repeat(mlp_b_s[0].T, e // mlp_b_s.shape[1], axis=1)
    y = _gmlp(u, mlp_ln_g[0:1], mlp_ln_b[0:1], mlp_w_s[0], b_lanes)
    x = _outproj(y, u, 2, x, mods, 1, w_out_b[1], name="mlp_out", **tiles)

    u = _inproj(x, mods, 2, norm_g[2:3], s5_w_in[0].astype(BF16), name="s5_in", **tiles_in)
    s5_groups = e // S5_GROUP
    bg = LANES // S5_GROUP
    blocks = s5_groups // bg

    def block_vec(v):
        return v.reshape(2, blocks, 1, S5_BLOCK_ST).transpose(1, 0, 2, 3)

    def block_mat_b(v):
        v = v.reshape(2, blocks, bg, S5_STATE, S5_GROUP).transpose(1, 0, 4, 2, 3)
        return v.reshape(blocks, 2, S5_GROUP, S5_BLOCK_ST)

    def block_mat_c(v):
        v = v.reshape(2, blocks, bg, S5_GROUP, S5_STATE).transpose(1, 0, 3, 2, 4)
        return v.reshape(blocks, 2, S5_GROUP, S5_BLOCK_ST)

    log_step = jnp.repeat(s5_log_step[0][:, :, None], S5_STATE, axis=2)
    prm = (block_vec(s5_lam_re[0]), block_vec(s5_lam_im[0]), block_vec(log_step), block_mat_b(s5_b_re[0]),
           block_mat_b(s5_b_im[0]), block_mat_c(s5_c_re[0]), block_mat_c(s5_c_im[0]))
    h0 = state_s5[:, 0].reshape(n_lat, 2, 2, blocks, S5_BLOCK_ST).transpose(3, 1, 2, 0, 4)
    ys, fin = _s5_core(prm, h0, u, n_ctx=n_ctx, l_ctx=l_ctx, n_lat=n_lat, l_lat=l_lat, e=e)
    new_s5 = fin.transpose(3, 1, 2, 0, 4).reshape(n_ctx, 1, 2, 2, s5_groups, S5_STATE)
    y = _s5_glu(ys, u, s5_d[0:1], s5_w_glu[0].astype(BF16), s5_b_glu[0:1])
    x = _outproj(y, u, 1, x, mods, 2, w_out_b[2], name="s5_out", **tiles)

    u = _inproj(x, mods, 3, norm_g[3:4], nat_w_in[0].astype(BF16), name="nat_in", **tiles_in)
    y, new_k, new_v = _ctx_attention(u, None, n_seq=n_ctx, seq_len=l_ctx, e=e)
    y = _nat_attention(u, cache_k[:, 0], cache_v[:, 0], _nat_bias(nat_rpb[0]), y, n_seq=n_lat, seq_len=l_lat,
                       row_block0=t_ctx // l_lat, e=e)
    out = _outproj(y, u, 3, x, mods, 3, w_out_b[3], final_g=final_g[None], name="nat_out", **tiles)

    return (out[:t_ctx].reshape(n_ctx, l_ctx, d), out[t_ctx:].reshape(n_lat, l_lat, d),
            new_ssd[:, None], new_s5, new_k[:, None], new_v[:, None])
```

```python
import functools

import jax
import jax.numpy as jnp
from jax import lax
from jax.experimental import pallas as pl
from jax.experimental.pallas import tpu as pltpu

F32 = jnp.float32
BF16 = jnp.bfloat16
ACT = jnp.bfloat16
EPS = 1e-6
HIGHEST = lax.Precision.HIGHEST

LANES = 128
CHUNK = 128
SSD_HEAD_DIM = 64
SSD_GROUP_HEADS = 4
SSD_GROUP_CH = SSD_HEAD_DIM * SSD_GROUP_HEADS
SSD_STATE = 128
CONV_W = 5
CONV_HALO = 8
S5_T = 8
S5_GROUP = 16
S5_STATE = 64
S5_BLOCK_ST = (LANES // S5_GROUP) * S5_STATE
HEAD_DIM = 64
GRID_W = 64
WIN_ROWS = 8
WIN_COLS = 16
NAT_ROWS_PER_STEP = 4
MASKED = -1e30
VMEM_LIMIT = 52 * 1024 * 1024


def _silu(x):
    return x * jax.nn.sigmoid(x)


def _gelu(x):
    return 0.5 * x * (1.0 + jnp.tanh(0.7978845608028654 * (x + 0.044715 * (x * x * x))))


def _softplus(x):
    return jnp.maximum(x, 0.0) + jnp.log1p(jnp.exp(-jnp.abs(x)))


def _dot(a, b):
    return jnp.dot(a, b, preferred_element_type=F32)


def _dot_nt(a, b):
    return lax.dot_general(a, b, (((1,), (1,)), ((), ())), preferred_element_type=F32)


def _params(*sem):
    return pltpu.CompilerParams(dimension_semantics=sem, vmem_limit_bytes=VMEM_LIMIT)


def _mod_row(i, *, tm, t_ctx, l_lat):
    start = i * tm
    return jnp.where(start < t_ctx, 0, 1 + (start - t_ctx) // l_lat)


def _mod_body(c_ref, w_ref, b_ref, o_ref):
    cond = _silu(c_ref[...]).astype(BF16)
    o_ref[...] = _dot(cond, w_ref[...].astype(BF16)) + b_ref[...]


def _modulation(cond8, w_mod, b_mod, *, tn=1024):
    depth, d, n = w_mod.shape
    return pl.pallas_call(
        _mod_body,
        grid=(depth, n // tn),
        in_specs=[pl.BlockSpec((8, d), lambda l, j: (0, 0)),
                  pl.BlockSpec((None, d, tn), lambda l, j: (l, 0, j)),
                  pl.BlockSpec((None, 1, tn), lambda l, j: (l, 0, j))],
        out_specs=pl.BlockSpec((None, 8, tn), lambda l, j: (l, 0, j)),
        out_shape=jax.ShapeDtypeStruct((depth, 8, n), F32),
        compiler_params=_params("parallel", "parallel"),
        name="modulation",
    )(cond8, w_mod, b_mod.reshape(depth, 1, n))


def _inproj_body(x_ref, shift_ref, scale_ref, g_ref, w_ref, o_ref, h_ref):
    @pl.when(pl.program_id(1) == 0)
    def _():
        x = x_ref[...]
        y = x * lax.rsqrt(jnp.mean(x * x, axis=-1, keepdims=True) + EPS) * g_ref[...]
        h_ref[...] = (y * (1.0 + scale_ref[...]) + shift_ref[...]).astype(BF16)

    o_ref[...] = _dot(h_ref[...], w_ref[...]).astype(o_ref.dtype)


def _inproj(x, mods, layer, g, w, *, tm, tn, t_ctx, l_lat, name, out_dtype=ACT):
    t, d = x.shape
    n = w.shape[1]
    row = functools.partial(_mod_row, tm=tm, t_ctx=t_ctx, l_lat=l_lat)

    def mod_spec(part):
        return pl.BlockSpec((None, None, None, 1, d), lambda i, j: (layer, row(i), part, 0, 0))

    return pl.pallas_call(
        _inproj_body,
        grid=(t // tm, n // tn),
        in_specs=[pl.BlockSpec((tm, d), lambda i, j: (i, 0)), mod_spec(0), mod_spec(1),
                  pl.BlockSpec((1, d), lambda i, j: (0, 0)),
                  pl.BlockSpec((d, tn), lambda i, j: (0, j))],
        out_specs=pl.BlockSpec((tm, tn), lambda i, j: (i, j)),
        out_shape=jax.ShapeDtypeStruct((t, n), out_dtype),
        scratch_shapes=[pltpu.VMEM((tm, d), BF16)],
        compiler_params=_params("parallel", "arbitrary"),
        name=name,
    )(x, mods, mods, g, w)


def _outproj_body(*refs, gated_norm, final_norm):
    y_ref, z_ref, x_ref, gate_ref, w_ref = refs[:5]
    rest = list(refs[5:])
    t = y_ref[...].astype(F32) * _silu(z_ref[...].astype(F32))
    if gated_norm:
        ng_ref = rest.pop(0)
        t = t * lax.rsqrt(jnp.mean(t * t, axis=-1, keepdims=True) + EPS) * ng_ref[...]
    if final_norm:
        fg_ref = rest.pop(0)
    o_ref = rest.pop(0)
    xn = x_ref[...] + gate_ref[...] * _dot(t.astype(BF16), w_ref[...])
    if final_norm:
        xn = xn * lax.rsqrt(jnp.mean(xn * xn, axis=-1, keepdims=True) + EPS) * fg_ref[...]
    o_ref[...] = xn


def _outproj(y, u, z_block, x, mods, layer, w, *, norm_g=None, final_g=None, tm, t_ctx, l_lat, name):
    t, e = y.shape
    d = x.shape[1]
    row = functools.partial(_mod_row, tm=tm, t_ctx=t_ctx, l_lat=l_lat)
    in_specs = [pl.BlockSpec((tm, e), lambda i: (i, 0)),
                pl.BlockSpec((tm, e), lambda i: (i, z_block)),
                pl.BlockSpec((tm, d), lambda i: (i, 0)),
                pl.BlockSpec((None, None, None, 1, d), lambda i: (layer, row(i), 2, 0, 0)),
                pl.BlockSpec((e, d), lambda i: (0, 0))]
    args = [y, u, x, mods, w]
    if norm_g is not None:
        in_specs.append(pl.BlockSpec((1, e), lambda i: (0, 0)))
        args.append(norm_g)
    if final_g is not None:
        in_specs.append(pl.BlockSpec((1, d), lambda i: (0, 0)))
        args.append(final_g)
    return pl.pallas_call(
        functools.partial(_outproj_body, gated_norm=norm_g is not None, final_norm=final_g is not None),
        grid=(t // tm,),
        in_specs=in_specs,
        out_specs=pl.BlockSpec((tm, d), lambda i: (i, 0)),
        out_shape=jax.ShapeDtypeStruct((t, d), F32),
        compiler_params=_params("parallel"),
        name=name,
    )(*args)


def _ssd_body(*refs, seq_len, has_h0, want_final):
    refs = list(refs)
    x_ref, b_ref, c_ref, dt_ref, wx_ref, wb_ref, wc_ref, bx_ref, bb_ref, bc_ref, dtb_ref, alog_ref, dsk_ref = refs[:13]
    rest = refs[13:]
    h0_ref = rest.pop(0) if has_h0 else None
    if want_final:
        y_ref, hf_ref = rest[0], rest[1]
        rest = rest[2:]
    else:
        y_ref, hf_ref = rest[0], None
        rest = rest[1:]
    pad_s, xb_s, xt_s, bm_s, bt_s, cm_s, e_s, et_s, ct_s, dtt_s, y_s, sf_s, sb_s = rest
    nc = seq_len // CHUNK
    width = SSD_GROUP_CH + 2 * SSD_STATE
    nh = SSD_GROUP_HEADS

    zeros = jnp.zeros((CONV_HALO, width), F32)
    pad_s[0:CONV_HALO, :] = zeros
    pad_s[CONV_HALO + seq_len:2 * CONV_HALO + seq_len, :] = zeros

    def copy_chunk(ci, carry):
        base = pl.multiple_of(ci * CHUNK, CHUNK)
        dst = pl.ds(base + CONV_HALO, CHUNK)
        pad_s[dst, 0:SSD_GROUP_CH] = x_ref[pl.ds(base, CHUNK), :].astype(F32)
        pad_s[dst, SSD_GROUP_CH:SSD_GROUP_CH + SSD_STATE] = b_ref[pl.ds(base, CHUNK), :].astype(F32)
        pad_s[dst, SSD_GROUP_CH + SSD_STATE:width] = c_ref[pl.ds(base, CHUNK), :].astype(F32)
        return carry

    lax.fori_loop(0, nc, copy_chunk, 0)

    conv_w = jnp.concatenate([wx_ref[...], wb_ref[...], wc_ref[...]], axis=1)
    conv_b = jnp.concatenate([bx_ref[...], bb_ref[...], bc_ref[...]], axis=1)
    a_row = -jnp.exp(alog_ref[...])
    first = CONV_HALO - CONV_W // 2
    row_i = lax.broadcasted_iota(jnp.int32, (CHUNK, CHUNK), 0)
    col_i = lax.broadcasted_iota(jnp.int32, (CHUNK, CHUNK), 1)
    lower = row_i >= col_i
    upper = row_i <= col_i
    lower_f = lower.astype(F32)
    fwd_lane = lax.broadcasted_iota(jnp.int32, (1, LANES), 1) < nh

    def prep_chunk(ci, carry):
        base = pl.multiple_of(ci * CHUNK, CHUNK)
        win = pad_s[pl.ds(base, CHUNK + 2 * CONV_HALO), :]
        acc = conv_b + win[first:first + CHUNK] * conv_w[0:1]
        for j in range(1, CONV_W):
            acc = acc + win[first + j:first + j + CHUNK] * conv_w[j:j + 1]
        xbc = _silu(acc)
        rows = pl.ds(base, CHUNK)
        x = xbc[:, 0:SSD_GROUP_CH]
        bm = xbc[:, SSD_GROUP_CH:SSD_GROUP_CH + SSD_STATE]
        y_s[rows, :] = x * dsk_ref[...]
        xb_s[rows, :] = x.astype(BF16)
        xt_s[ci] = x.T
        bm_s[rows, :] = bm.astype(BF16)
        bt_s[ci] = bm.T.astype(BF16)
        cm_s[rows, :] = xbc[:, SSD_GROUP_CH + SSD_STATE:width].astype(BF16)
        dt = _softplus(dt_ref[rows, :] + dtb_ref[...])
        da = dt * a_row
        cum = jnp.dot(lower_f, da, precision=HIGHEST, preferred_element_type=F32)
        e = jnp.where(fwd_lane, cum, cum - da)
        e_s[rows, :] = e
        et_s[ci] = e.T[0:2 * nh]
        ct_s[ci] = cum.T[0:2 * nh]
        dtt_s[ci] = dt.T[0:2 * nh]
        return carry

    lax.fori_loop(0, nc, prep_chunk, 0)

    if has_h0:
        sf_s[...] = h0_ref[0].reshape(SSD_GROUP_CH, SSD_STATE)
        sb_s[...] = h0_ref[1].reshape(SSD_GROUP_CH, SSD_STATE)
    else:
        sf_s[...] = jnp.zeros((SSD_GROUP_CH, SSD_STATE), F32)
        sb_s[...] = jnp.zeros((SSD_GROUP_CH, SSD_STATE), F32)

    head_of_lane = lax.broadcasted_iota(jnp.int32, (CHUNK, SSD_GROUP_CH), 1) // SSD_HEAD_DIM
    st_refs = (sf_s, sb_s)

    def scan_chunk(i, carry):
        chunks = (i, nc - 1 - i)
        rows = [pl.ds(pl.multiple_of(ci * CHUNK, CHUNK), CHUNK) for ci in chunks]
        s_prev = [st_refs[d][...] for d in range(2)]
        g, cs = [], []
        for d in range(2):
            cm = cm_s[rows[d], :]
            g.append(_dot(cm, bt_s[chunks[d]]))
            cs.append(_dot_nt(cm, s_prev[d].astype(BF16)))
        w_rows, keeps, ys = [], [], []
        for d in range(2):
            e = e_s[rows[d], :]
            e_t, c_t, dt_t = et_s[chunks[d]], ct_s[chunks[d]], dtt_s[chunks[d]]
            xb = xb_s[rows[d], :]
            y = jnp.zeros((CHUNK, SSD_GROUP_CH), F32)
            w_d, keep_d = [], []
            for r in range(nh):
                col = r + nh * d
                ec = jnp.broadcast_to(e[:, col:col + 1], (CHUNK, CHUNK))
                er = e_t[col:col + 1, :]
                dt_r = dt_t[col:col + 1, :]
                tot = c_t[col:col + 1, CHUNK - 1:CHUNK]
                if d == 0:
                    decay = jnp.exp(jnp.where(lower, ec - er, -jnp.inf))
                    off = jnp.exp(ec)
                    w_d.append(jnp.broadcast_to(dt_r * jnp.exp(tot - er), (SSD_HEAD_DIM, CHUNK)))
                else:
                    decay = jnp.exp(jnp.where(upper, er - ec, -jnp.inf))
                    off = jnp.exp(tot - ec)
                    w_d.append(jnp.broadcast_to(dt_r * jnp.exp(er), (SSD_HEAD_DIM, CHUNK)))
                keep_d.append(jnp.broadcast_to(jnp.exp(tot), (SSD_HEAD_DIM, SSD_STATE)))
                yd = _dot((g[d] * decay * dt_r).astype(BF16), xb)
                off2 = jnp.concatenate([off] * (SSD_GROUP_CH // CHUNK), axis=1)
                y = jnp.where(head_of_lane == r, yd + cs[d] * off2, y)
            ys.append(y)
            w_rows.append(jnp.concatenate(w_d, axis=0))
            keeps.append(jnp.concatenate(keep_d, axis=0))
        for d in range(2):
            w_t = (xt_s[chunks[d]] * w_rows[d]).astype(BF16)
            st_refs[d][...] = keeps[d] * s_prev[d] + _dot(w_t, bm_s[rows[d], :])
            y_s[rows[d], :] += ys[d]
        return carry

    lax.fori_loop(0, nc, scan_chunk, 0)

    def emit_chunk(ci, carry):
        rows = pl.ds(pl.multiple_of(ci * CHUNK, CHUNK), CHUNK)
        y_ref[rows, :] = y_s[rows, :].astype(y_ref.dtype)
        return carry

    lax.fori_loop(0, nc, emit_chunk, 0)

    if want_final:
        hf_ref[0] = sf_s[...].reshape(SSD_GROUP_HEADS, SSD_HEAD_DIM, SSD_STATE)
        hf_ref[1] = sb_s[...].reshape(SSD_GROUP_HEADS, SSD_HEAD_DIM, SSD_STATE)


def _ssd_call(u, dt, conv_w, conv_b, dtb, alog, dsk, y_prev, h0, *, n_seq, seq_len, row_block0, want_final, name):
    t, n_u = u.shape
    e = dsk.shape[1]
    groups = e // SSD_GROUP_CH
    xb0 = e // SSD_GROUP_CH
    bb0 = 2 * e // SSD_STATE
    cb0 = bb0 + groups
    has_h0 = h0 is not None
    seq = lambda b: row_block0 + b
    in_specs = [pl.BlockSpec((seq_len, SSD_GROUP_CH), lambda b, g: (seq(b), xb0 + g)),
                pl.BlockSpec((seq_len, SSD_STATE), lambda b, g: (seq(b), bb0 + g)),
                pl.BlockSpec((seq_len, SSD_STATE), lambda b, g: (seq(b), cb0 + g)),
                pl.BlockSpec((seq_len, LANES), lambda b, g: (seq(b), g)),
                pl.BlockSpec((CONV_W, SSD_GROUP_CH), lambda b, g: (0, g)),
                pl.BlockSpec((CONV_W, SSD_STATE), lambda b, g: (0, bb0 - xb0 * 2 + g)),
                pl.BlockSpec((CONV_W, SSD_STATE), lambda b, g: (0, cb0 - xb0 * 2 + g)),
                pl.BlockSpec((1, SSD_GROUP_CH), lambda b, g: (0, g)),
                pl.BlockSpec((1, SSD_STATE), lambda b, g: (0, bb0 - xb0 * 2 + g)),
                pl.BlockSpec((1, SSD_STATE), lambda b, g: (0, cb0 - xb0 * 2 + g)),
                pl.BlockSpec((1, LANES), lambda b, g: (0, g)),
                pl.BlockSpec((1, LANES), lambda b, g: (0, g)),
                pl.BlockSpec((1, SSD_GROUP_CH), lambda b, g: (0, g))]
    args = [u, u, u, dt, conv_w, conv_w, conv_w, conv_b, conv_b, conv_b, dtb, alog, dsk]
    state_block = (None, 2, SSD_GROUP_HEADS, SSD_HEAD_DIM, SSD_STATE)
    if has_h0:
        in_specs.append(pl.BlockSpec(state_block, lambda b, g: (b, 0, g, 0, 0)))
        args.append(h0)
    aliases = {}
    if y_prev is not None:
        in_specs.append(pl.BlockSpec(memory_space=pl.ANY))
        args.append(y_prev)
        aliases = {len(args) - 1: 0}
    out_specs = [pl.BlockSpec((seq_len, SSD_GROUP_CH), lambda b, g: (seq(b), g))]
    out_shape = [jax.ShapeDtypeStruct((t, e), ACT)]
    if want_final:
        out_specs.append(pl.BlockSpec(state_block, lambda b, g: (b, 0, g, 0, 0)))
        out_shape.append(jax.ShapeDtypeStruct((n_seq, 2, e // SSD_HEAD_DIM, SSD_HEAD_DIM, SSD_STATE), F32))
    width = SSD_GROUP_CH + 2 * SSD_STATE
    nc = seq_len // CHUNK

    def body(*refs):
        refs = list(refs)
        if y_prev is not None:
            n_in = len(args)
            del refs[n_in - 1]
        _ssd_body(*refs, seq_len=seq_len, has_h0=has_h0, want_final=want_final)

    return pl.pallas_call(
        body,
        grid=(n_seq, groups),
        in_specs=in_specs,
        out_specs=out_specs,
        out_shape=out_shape,
        input_output_aliases=aliases,
        scratch_shapes=[pltpu.VMEM((seq_len + 2 * CONV_HALO, width), F32),
                        pltpu.VMEM((seq_len, SSD_GROUP_CH), BF16),
                        pltpu.VMEM((nc, SSD_GROUP_CH, CHUNK), F32),
                        pltpu.VMEM((seq_len, SSD_STATE), BF16),
                        pltpu.VMEM((nc, SSD_STATE, CHUNK), BF16),
                        pltpu.VMEM((seq_len, SSD_STATE), BF16),
                        pltpu.VMEM((seq_len, LANES), F32),
                        pltpu.VMEM((nc, 2 * SSD_GROUP_HEADS, CHUNK), F32),
                        pltpu.VMEM((nc, 2 * SSD_GROUP_HEADS, CHUNK), F32),
                        pltpu.VMEM((nc, 2 * SSD_GROUP_HEADS, CHUNK), F32),
                        pltpu.VMEM((seq_len, SSD_GROUP_CH), F32),
                        pltpu.VMEM((SSD_GROUP_CH, SSD_STATE), F32),
                        pltpu.VMEM((SSD_GROUP_CH, SSD_STATE), F32)],
        compiler_params=_params("parallel", "parallel"),
        name=name,
    )(*args)


def _group_lanes(v, groups):
    per_group = v.reshape(2, groups, SSD_GROUP_HEADS).transpose(1, 0, 2).reshape(groups, 2 * SSD_GROUP_HEADS)
    return jnp.pad(per_group, ((0, 0), (0, LANES - 2 * SSD_GROUP_HEADS))).reshape(1, groups * LANES)


def _gmlp_body(u_ref, v_ref, lng_ref, lnb_ref, ws_ref, bs_ref, o_ref, *, tm):
    groups = ws_ref.shape[0]
    gch = u_ref.shape[1] // groups
    for c in range(tm // CHUNK):
        rows = slice(c * CHUNK, (c + 1) * CHUNK)
        v = _gelu(v_ref[rows, :].astype(F32))
        vc = v - jnp.mean(v, axis=-1, keepdims=True)
        vn = vc * lax.rsqrt(jnp.mean(vc * vc, axis=-1, keepdims=True) + EPS) * lng_ref[...] + lnb_ref[...]
        vb = vn.astype(BF16)
        for g in range(groups):
            cols = slice(g * gch, (g + 1) * gch)
            s = _dot(ws_ref[g].astype(BF16), vb[:, cols]) + bs_ref[:, cols]
            o_ref[rows, cols] = (_gelu(u_ref[rows, cols].astype(F32)) * s).astype(o_ref.dtype)


def _gmlp(u, ln_g, ln_b, w_s, b_lanes, *, tm=256):
    t = u.shape[0]
    e = ln_g.shape[1]
    return pl.pallas_call(
        functools.partial(_gmlp_body, tm=tm),
        grid=(t // tm,),
        in_specs=[pl.BlockSpec((tm, e), lambda i: (i, 0)),
                  pl.BlockSpec((tm, e), lambda i: (i, 1)),
                  pl.BlockSpec((1, e), lambda i: (0, 0)),
                  pl.BlockSpec((1, e), lambda i: (0, 0)),
                  pl.BlockSpec(w_s.shape, lambda i: (0, 0, 0)),
                  pl.BlockSpec((CHUNK, e), lambda i: (0, 0))],
        out_specs=pl.BlockSpec((tm, e), lambda i: (i, 0)),
        out_shape=jax.ShapeDtypeStruct((t, e), ACT),
        compiler_params=_params("parallel"),
        name="gmlp",
    )(u, u, ln_g, ln_b, w_s, b_lanes)


def _s5_body(lr_ref, li_ref, ls_ref, brt_ref, bit_ref, cre_ref, cim_ref, h0_ref, u_ref, y_ref, fin_ref,
             win_s, t_s, ef_s, eb_s, z_s, a_s, zin_s, d_s, spf_s, spb_s, yo_s, uf_s,
             *, seqs, l_ctx, n_lat, l_lat, ctx_parts):
    bst = S5_BLOCK_ST
    ng = LANES // S5_GROUP
    kw = S5_T * LANES
    part = pl.program_id(1)

    @pl.when(part == 0)
    def _build():
        own = (lax.broadcasted_iota(jnp.int32, (ng, S5_GROUP, bst), 2) // S5_STATE
               == lax.broadcasted_iota(jnp.int32, (ng, S5_GROUP, bst), 0))

        def spread(v):
            return jnp.where(own, v[None], 0.0).reshape(LANES, bst).astype(BF16)

        tau = lax.broadcasted_iota(jnp.int32, (S5_T + 8, 1), 0).astype(F32)
        e_refs = (ef_s, eb_s)
        for d in range(2):
            lr, li = lr_ref[d], li_ref[d]
            step = jnp.exp(ls_ref[d])
            mag = jnp.exp(tau * (lr * step))
            p_re = mag * jnp.cos(tau * (li * step))
            p_im = mag * jnp.sin(tau * (li * step))
            ab_re, ab_im = p_re[1:2], p_im[1:2]
            den = lr * lr + li * li
            nr = ab_re - 1.0
            cr = (nr * lr + ab_im * li) / den
            ci = (ab_im * lr - nr * li) / den
            brt, bit = brt_ref[d], bit_ref[d]
            bb_re = cr * brt - ci * bit
            bb_im = cr * bit + ci * brt
            cre, cim = cre_ref[d], cim_ref[d]
            for k in range(S5_T):
                rows = slice(k * LANES, (k + 1) * LANES)
                tq = S5_T - 1 - k if d == 0 else k
                te = k + 1 if d == 0 else S5_T - k
                pr, pi = p_re[tq:tq + 1], p_im[tq:tq + 1]
                win_s[rows, 2 * d * bst:(2 * d + 1) * bst] = spread(pr * bb_re - pi * bb_im)
                win_s[rows, (2 * d + 1) * bst:(2 * d + 2) * bst] = spread(pr * bb_im + pi * bb_re)
                pr, pi = p_re[te:te + 1], p_im[te:te + 1]
                e_refs[d][rows, 0:bst] = spread(cre * pr - cim * pi)
                e_refs[d][rows, bst:2 * bst] = spread(-(cre * pi + cim * pr))
            c_own = jnp.concatenate([spread(cre), spread(-cim)], axis=1)
            z = _dot_nt(win_s[:, 2 * d * bst:(2 * d + 2) * bst], c_own)
            if d == 0:
                z_s[0:kw, :] = z
            else:
                z_s[kw - LANES:kw, :] += z[0:LANES]
                z_s[kw:2 * kw - LANES, :] = z[LANES:kw]
            a_s[2 * d:2 * d + 1, :] = p_re[S5_T:S5_T + 1]
            a_s[2 * d + 1:2 * d + 2, :] = p_im[S5_T:S5_T + 1]
        for k in range(S5_T):
            r0 = (S5_T - 1 - k) * LANES
            t_s[:, k * LANES:(k + 1) * LANES] = z_s[r0:r0 + kw, :].astype(BF16)

    def outputs():
        yo_s[...] = (_dot(zin_s[...], t_s[...]) + _dot_nt(spf_s[...].astype(BF16), ef_s[...])
                     + _dot_nt(spb_s[...].astype(BF16), eb_s[...]))

    @pl.when(part < ctx_parts)
    def _context():
        nj = l_ctx // S5_T
        uf_s[...] = u_ref[...].astype(F32)
        for j in range(nj):
            for k in range(S5_T):
                zin_s[j * seqs:(j + 1) * seqs, k * LANES:(k + 1) * LANES] = (
                    uf_s[pl.ds(j * S5_T + k, seqs, stride=l_ctx), :].astype(BF16))
        d_s[...] = _dot(zin_s[...], win_s[...])
        coef = [jnp.broadcast_to(a_s[i:i + 1, :], (seqs, bst)) for i in range(4)]
        zero = jnp.zeros((seqs, bst), F32)

        def step(j, carry):
            fr, fi, br, bi = carry
            rf = pl.ds(pl.multiple_of(j * seqs, seqs), seqs)
            rb = pl.ds(pl.multiple_of((nj - 1 - j) * seqs, seqs), seqs)
            spf_s[rf, 0:bst] = fr
            spf_s[rf, bst:2 * bst] = fi
            spb_s[rb, 0:bst] = br
            spb_s[rb, bst:2 * bst] = bi
            return (coef[0] * fr - coef[1] * fi + d_s[rf, 0:bst],
                    coef[0] * fi + coef[1] * fr + d_s[rf, bst:2 * bst],
                    coef[2] * br - coef[3] * bi + d_s[rb, 2 * bst:3 * bst],
                    coef[2] * bi + coef[3] * br + d_s[rb, 3 * bst:4 * bst])

        fr, fi, br, bi = lax.fori_loop(0, nj, step, (zero, zero, zero, zero))
        fin_ref[0, 0] = fr
        fin_ref[0, 1] = fi
        fin_ref[1, 0] = br
        fin_ref[1, 1] = bi
        outputs()
        for j in range(nj):
            for k in range(S5_T):
                y_ref[pl.ds(j * S5_T + k, seqs, stride=l_ctx), :] = (
                    yo_s[j * seqs:(j + 1) * seqs, k * LANES:(k + 1) * LANES])

    @pl.when(part == ctx_parts)
    def _latent():
        nj = l_lat // S5_T
        uf_s[...] = u_ref[...].astype(F32)
        for b in range(n_lat):
            for k in range(S5_T):
                zin_s[b * nj:(b + 1) * nj, k * LANES:(k + 1) * LANES] = (
                    uf_s[pl.ds(b * l_lat + k, nj, stride=S5_T), :].astype(BF16))
        d_s[...] = _dot(zin_s[...], win_s[...])
        coef_re = jnp.concatenate([jnp.broadcast_to(a_s[0:1, :], (n_lat, bst)),
                                   jnp.broadcast_to(a_s[2:3, :], (n_lat, bst))], axis=0)
        coef_im = jnp.concatenate([jnp.broadcast_to(a_s[1:2, :], (n_lat, bst)),
                                   jnp.broadcast_to(a_s[3:4, :], (n_lat, bst))], axis=0)
        s_re0 = jnp.concatenate([h0_ref[0, 0], h0_ref[1, 0]], axis=0)
        s_im0 = jnp.concatenate([h0_ref[0, 1], h0_ref[1, 1]], axis=0)

        def step(j, carry):
            s_re, s_im = carry
            d_re, d_im = [], []
            for b in range(n_lat):
                row = pl.ds(b * nj + j, 1)
                spf_s[row, 0:bst] = s_re[b:b + 1]
                spf_s[row, bst:2 * bst] = s_im[b:b + 1]
                d_re.append(d_s[row, 0:bst])
                d_im.append(d_s[row, bst:2 * bst])
            for b in range(n_lat):
                row = pl.ds(b * nj + nj - 1 - j, 1)
                spb_s[row, 0:bst] = s_re[n_lat + b:n_lat + b + 1]
                spb_s[row, bst:2 * bst] = s_im[n_lat + b:n_lat + b + 1]
                d_re.append(d_s[row, 2 * bst:3 * bst])
                d_im.append(d_s[row, 3 * bst:4 * bst])
            d_re = jnp.concatenate(d_re, axis=0)
            d_im = jnp.concatenate(d_im, axis=0)
            return coef_re * s_re - coef_im * s_im + d_re, coef_re * s_im + coef_im * s_re + d_im

        lax.fori_loop(0, nj, step, (s_re0, s_im0))
        outputs()
        for b in range(n_lat):
            for k in range(S5_T):
                y_ref[pl.ds(b * l_lat + k, nj, stride=S5_T), :] = yo_s[b * nj:(b + 1) * nj, k * LANES:(k + 1) * LANES]


def _s5_core(prm, h0, u, *, n_ctx, l_ctx, n_lat, l_lat, e):
    blocks = prm[0].shape[0]
    part_tokens = n_lat * l_lat
    t = u.shape[0]
    n_parts = t // part_tokens
    seqs = part_tokens // l_ctx
    r = part_tokens // S5_T
    vec = pl.BlockSpec((None, 2, 1, S5_BLOCK_ST), lambda g, p: (g, 0, 0, 0))
    mat = pl.BlockSpec((None, 2, S5_GROUP, S5_BLOCK_ST), lambda g, p: (g, 0, 0, 0))
    k_in = S5_T * LANES
    return pl.pallas_call(
        functools.partial(_s5_body, seqs=seqs, l_ctx=l_ctx, n_lat=n_lat, l_lat=l_lat, ctx_parts=n_parts - 1),
        grid=(blocks, n_parts),
        in_specs=[vec, vec, vec, mat, mat, mat, mat,
                  pl.BlockSpec((None, 2, 2, n_lat, S5_BLOCK_ST), lambda g, p: (g, 0, 0, 0, 0)),
                  pl.BlockSpec((part_tokens, LANES), lambda g, p: (p, g))],
        out_specs=[pl.BlockSpec((part_tokens, LANES), lambda g, p: (p, g)),
                   pl.BlockSpec((None, 2, 2, seqs, S5_BLOCK_ST),
                                lambda g, p: (g, 0, 0, jnp.minimum(p, n_parts - 2), 0))],
        out_shape=[jax.ShapeDtypeStruct((t, e), F32),
                   jax.ShapeDtypeStruct((blocks, 2, 2, n_ctx, S5_BLOCK_ST), F32)],
        scratch_shapes=[pltpu.VMEM((k_in, 4 * S5_BLOCK_ST), BF16),
                        pltpu.VMEM((k_in, k_in), BF16),
                        pltpu.VMEM((k_in, 2 * S5_BLOCK_ST), BF16),
                        pltpu.VMEM((k_in, 2 * S5_BLOCK_ST), BF16),
                        pltpu.VMEM(((2 * S5_T - 1) * LANES, LANES), F32),
                        pltpu.VMEM((8, S5_BLOCK_ST), F32),
                        pltpu.VMEM((r, k_in), BF16),
                        pltpu.VMEM((r, 4 * S5_BLOCK_ST), F32),
                        pltpu.VMEM((r, 2 * S5_BLOCK_ST), F32),
                        pltpu.VMEM((r, 2 * S5_BLOCK_ST), F32),
                        pltpu.VMEM((r, k_in), F32),
                        pltpu.VMEM((part_tokens, LANES), F32)],
        compiler_params=_params("parallel", "arbitrary"),
        name="s5_core",
    )(*prm, h0, u)


def _s5_glu_body(ys_ref, u_ref, dsk_ref, w_ref, b_ref, o_ref):
    y = _gelu(ys_ref[...] + dsk_ref[...] * u_ref[...].astype(F32))
    o_ref[...] = (y * jax.nn.sigmoid(_dot(y.astype(BF16), w_ref[...]) + b_ref[...])).astype(o_ref.dtype)


def _s5_glu(ys, u, dsk, w, b, *, tm=512):
    t, e = ys.shape
    return pl.pallas_call(
        _s5_glu_body,
        grid=(t // tm,),
        in_specs=[pl.BlockSpec((tm, e), lambda i: (i, 0)),
                  pl.BlockSpec((tm, e), lambda i: (i, 0)),
                  pl.BlockSpec((1, e), lambda i: (0, 0)),
                  pl.BlockSpec((e, e), lambda i: (0, 0)),
                  pl.BlockSpec((1, e), lambda i: (0, 0))],
        out_specs=pl.BlockSpec((tm, e), lambda i: (i, 0)),
        out_shape=jax.ShapeDtypeStruct((t, e), ACT),
        compiler_params=_params("parallel"),
        name="s5_glu",
    )(ys, u, dsk, w, b)


def _ctx_attn_body(q_ref, k_ref, v_ref, o_ref, ko_ref, vo_ref):
    scale = HEAD_DIM ** -0.5
    seq_len = q_ref.shape[0]
    first = lax.broadcasted_iota(jnp.int32, (seq_len, LANES), 1) < HEAD_DIM
    masks = (first, jnp.logical_not(first))
    n_pairs = q_ref.shape[1] // LANES
    logits, values = [], []
    for pair in range(n_pairs):
        cols = slice(pair * LANES, (pair + 1) * LANES)
        q2, k2, v2 = q_ref[:, cols].astype(F32) * scale, k_ref[:, cols], v_ref[:, cols]
        kb = k2.astype(BF16)
        values.append(v2.astype(BF16))
        for h in range(LANES // HEAD_DIM):
            ko_ref[2 * pair + h] = k2[:, h * HEAD_DIM:(h + 1) * HEAD_DIM].astype(F32)
            vo_ref[2 * pair + h] = v2[:, h * HEAD_DIM:(h + 1) * HEAD_DIM].astype(F32)
            logits.append(_dot_nt(jnp.where(masks[h], q2, 0.0).astype(BF16), kb))
    probs = []
    for s in logits:
        p = jnp.exp(s - jnp.max(s, axis=-1, keepdims=True))
        probs.append((p * (1.0 / jnp.sum(p, axis=-1, keepdims=True))).astype(BF16))
    for pair in range(n_pairs):
        o0, o1 = _dot(probs[2 * pair], values[pair]), _dot(probs[2 * pair + 1], values[pair])
        o_ref[:, pair * LANES:(pair + 1) * LANES] = jnp.where(first, o0, o1).astype(o_ref.dtype)


def _ctx_attention(u, y_prev, *, n_seq, seq_len, e, width=512):
    t = u.shape[0]
    hp = e // width
    heads = e // HEAD_DIM
    kv_block = (None, width // HEAD_DIM, seq_len, HEAD_DIM)
    in_specs = [pl.BlockSpec((seq_len, width), lambda b, h: (b, h)),
                pl.BlockSpec((seq_len, width), lambda b, h: (b, hp + h)),
                pl.BlockSpec((seq_len, width), lambda b, h: (b, 2 * hp + h))]
    return pl.pallas_call(
        _ctx_attn_body,
        grid=(n_seq, hp),
        in_specs=in_specs,
        out_specs=[pl.BlockSpec((seq_len, width), lambda b, h: (b, h)),
                   pl.BlockSpec(kv_block, lambda b, h: (b, h, 0, 0)),
                   pl.BlockSpec(kv_block, lambda b, h: (b, h, 0, 0))],
        out_shape=[jax.ShapeDtypeStruct((t, e), ACT),
                   jax.ShapeDtypeStruct((n_seq, heads, seq_len, HEAD_DIM), F32),
                   jax.ShapeDtypeStruct((n_seq, heads, seq_len, HEAD_DIM), F32)],
        compiler_params=_params("parallel", "parallel"),
        name="ctx_attention",
    )(u, u, u)


def _nat_bias_body(rpb_ref, o_ref, *, n_rel_rows, n_rel_cols):
    h = pl.program_id(0)
    q = lax.broadcasted_iota(jnp.int32, (GRID_W, GRID_W), 0)
    kc = lax.broadcasted_iota(jnp.int32, (GRID_W, GRID_W), 1)
    rel = kc - q + (WIN_COLS - 1)
    c_start = jnp.clip(q - WIN_COLS // 2, 0, GRID_W - WIN_COLS)
    ok = (kc >= c_start) & (kc < c_start + WIN_COLS)
    for i in range(n_rel_rows):
        acc = jnp.zeros((GRID_W, GRID_W), F32)
        for dcol in range(n_rel_cols):
            acc = jnp.where(rel == dcol, rpb_ref[(h * n_rel_rows + i) * n_rel_cols + dcol], acc)
        tile = jnp.where(ok, acc, MASKED)
        if i < n_rel_rows - 1:
            o_ref[i, :, 0:GRID_W] = tile
        if i > 0:
            o_ref[i - 1, :, GRID_W:2 * GRID_W] = tile


def _nat_bias(rpb):
    heads, n_rel_rows, n_rel_cols = rpb.shape
    return pl.pallas_call(
        functools.partial(_nat_bias_body, n_rel_rows=n_rel_rows, n_rel_cols=n_rel_cols),
        grid=(heads,),
        in_specs=[pl.BlockSpec(memory_space=pltpu.SMEM)],
        out_specs=pl.BlockSpec((None, n_rel_rows - 1, GRID_W, 2 * GRID_W), lambda h: (h, 0, 0, 0)),
        out_shape=jax.ShapeDtypeStruct((heads, n_rel_rows - 1, GRID_W, 2 * GRID_W), F32),
        compiler_params=_params("parallel"),
        name="nat_bias",
    )(rpb.reshape(-1))


def _nat_body(q_ref, k_ref, v_ref, ck_ref, cv_ref, bias_ref, o_ref, ckb_s, cvb_s, *, rows):
    scale = HEAD_DIM ** -0.5
    wr = min(WIN_ROWS, rows)
    nw = wr * GRID_W
    ckb_s[...] = jnp.concatenate([ck_ref[0], ck_ref[1]], axis=1).astype(BF16)
    cvb_s[...] = jnp.concatenate([cv_ref[0], cv_ref[1]], axis=1).astype(BF16)
    first = lax.broadcasted_iota(jnp.int32, (GRID_W, LANES), 1) < HEAD_DIM

    heads = tuple(range(LANES // HEAD_DIM))
    masks = (first, jnp.logical_not(first))

    def row_group(g, carry):
        rws = [g * NAT_ROWS_PER_STEP + i for i in range(NAT_ROWS_PER_STEP)]
        starts = [jnp.clip(r - wr // 2, 0, rows - wr) for r in rws]
        q_rows = [pl.ds(pl.multiple_of(r * GRID_W, GRID_W), GRID_W) for r in rws]
        k_rows = [pl.ds(pl.multiple_of(rs * GRID_W, GRID_W), nw) for rs in starts]
        logits = []
        for r, rs, qr, kr in zip(rws, starts, q_rows, k_rows):
            q2 = q_ref[qr, :].astype(F32) * scale
            kb = k_ref[kr, :].astype(BF16)
            for h in heads:
                q = jnp.where(masks[h], q2, 0.0).astype(BF16)
                logits.append((_dot_nt(q, kb), _dot_nt(q, ckb_s[...])))
        probs = []
        for idx, (s_win, s_ctx) in enumerate(logits):
            r, rs, h = rws[idx // 2], starts[idx // 2], heads[idx % 2]
            i0 = (WIN_ROWS - 1) - (r - rs)
            s_win = s_win + jnp.concatenate([bias_ref[h, i0 + 2 * jj] for jj in range(wr // 2)], axis=1)
            m = jnp.maximum(jnp.max(s_win, axis=-1, keepdims=True), jnp.max(s_ctx, axis=-1, keepdims=True))
            p_win = jnp.exp(s_win - m)
            p_ctx = jnp.exp(s_ctx - m)
            inv = 1.0 / (jnp.sum(p_win, axis=-1, keepdims=True) + jnp.sum(p_ctx, axis=-1, keepdims=True))
            probs.append(((p_win * inv).astype(BF16), (p_ctx * inv).astype(BF16)))
        for i, (qr, kr) in enumerate(zip(q_rows, k_rows)):
            vb = v_ref[kr, :].astype(BF16)
            outs = [_dot(probs[2 * i + h][0], vb) + _dot(probs[2 * i + h][1], cvb_s[...]) for h in heads]
            o_ref[qr, :] = jnp.where(first, outs[0], outs[1]).astype(o_ref.dtype)
        return carry

    lax.fori_loop(0, rows // NAT_ROWS_PER_STEP, row_group, 0)


def _nat_attention(u, cache_k, cache_v, bias, y_prev, *, n_seq, seq_len, row_block0, e):
    t = u.shape[0]
    hp = e // LANES
    hpb = LANES // HEAD_DIM
    past = cache_k.shape[2]
    seq = lambda b: row_block0 + b
    cache_block = (None, hpb, past, HEAD_DIM)
    return pl.pallas_call(
        lambda q, k, v, ck, cv, bs, yp, o, *scratch: _nat_body(q, k, v, ck, cv, bs, o, *scratch,
                                                               rows=seq_len // GRID_W),
        grid=(n_seq, hp),
        in_specs=[pl.BlockSpec((seq_len, LANES), lambda b, h: (seq(b), h)),
                  pl.BlockSpec((seq_len, LANES), lambda b, h: (seq(b), hp + h)),
                  pl.BlockSpec((seq_len, LANES), lambda b, h: (seq(b), 2 * hp + h)),
                  pl.BlockSpec(cache_block, lambda b, h: (b, h, 0, 0)),
                  pl.BlockSpec(cache_block, lambda b, h: (b, h, 0, 0)),
                  pl.BlockSpec((hpb,) + bias.shape[1:], lambda b, h: (h, 0, 0, 0)),
                  pl.BlockSpec(memory_space=pl.ANY)],
        out_specs=pl.BlockSpec((seq_len, LANES), lambda b, h: (seq(b), h)),
        out_shape=jax.ShapeDtypeStruct((t, e), ACT),
        input_output_aliases={6: 0},
        scratch_shapes=[pltpu.VMEM((past, LANES), BF16), pltpu.VMEM((past, LANES), BF16)],
        compiler_params=_params("parallel", "parallel"),
        name="nat_attention",
    )(u, u, u, cache_k, cache_v, bias, y_prev)


def kernel(x_prompt, x_sample, state_ssd, state_s5, cache_k, cache_v, c, c_ctx, norm_g, w_mod, b_mod, w_out, final_g, ssd_w_in, ssd_conv_w, ssd_conv_b, ssd_dt_bias, ssd_a_log, ssd_d, ssd_norm_g, mlp_w_in, mlp_ln_g, mlp_ln_b, mlp_w_s, mlp_b_s, s5_w_in, s5_lam_re, s5_lam_im, s5_log_step, s5_b_re, s5_b_im, s5_c_re, s5_c_im, s5_d, s5_w_glu, s5_b_glu, nat_w_in, nat_rpb):
    n_ctx, l_ctx, d = x_prompt.shape
    n_lat, l_lat, _ = x_sample.shape
    t_ctx = n_ctx * l_ctx
    depth = norm_g.shape[0]
    e = w_out.shape[1]
    assert depth == 4 and l_lat % l_ctx == 0 and t_ctx % (n_lat * l_lat) == 0
    tiles = dict(tm=512, t_ctx=t_ctx, l_lat=l_lat)
    tiles_in = dict(tm=1024, tn=1024, t_ctx=t_ctx, l_lat=l_lat)

    x = jnp.concatenate([x_prompt.reshape(t_ctx, d), x_sample.reshape(n_lat * l_lat, d)], axis=0)
    cond8 = jnp.concatenate([c_ctx[None], c, jnp.zeros((8 - 1 - n_lat, d), F32)], axis=0)
    mods = _modulation(cond8, w_mod, b_mod).reshape(depth, 8, 3, 1, d)
    w_out_b = w_out.astype(BF16)

    groups = e // SSD_GROUP_CH
    n_main = 3 * e
    w_dt = ssd_w_in[0][:, n_main:].reshape(d, 2, groups, SSD_GROUP_HEADS).transpose(0, 2, 1, 3)
    w_dt = jnp.pad(w_dt.reshape(d, groups, 2 * SSD_GROUP_HEADS), ((0, 0), (0, 0), (0, LANES - 2 * SSD_GROUP_HEADS)))
    u = _inproj(x, mods, 0, norm_g[0:1], ssd_w_in[0][:, :n_main].astype(BF16), name="ssd_in", **tiles_in)
    dt = _inproj(x, mods, 0, norm_g[0:1], w_dt.reshape(d, groups * LANES).astype(BF16), name="ssd_dt_in", out_dtype=F32,
                 **tiles_in)
    ssd_args = (ssd_conv_w[0], ssd_conv_b[0:1], _group_lanes(ssd_dt_bias[0], groups),
                _group_lanes(ssd_a_log[0], groups), jnp.repeat(ssd_d[0], SSD_HEAD_DIM)[None])
    y, new_ssd = _ssd_call(u, dt, *ssd_args, None, None, n_seq=n_ctx, seq_len=l_ctx, row_block0=0,
                           want_final=True, name="ssd_ctx")
    y, = _ssd_call(u, dt, *ssd_args, y, state_ssd[:, 0], n_seq=n_lat, seq_len=l_lat, row_block0=t_ctx // l_lat,
                   want_final=False, name="ssd_lat")
    x = _outproj(y, u, 0, x, mods, 0, w_out_b[0], norm_g=ssd_norm_g[0:1], name="ssd_out", **tiles)

    u = _inproj(x, mods, 1, norm_g[1:2], mlp_w_in[0].astype(BF16), name="mlp_in", **tiles_in)
    b_lanes = jnp.repeat(mlp_b_s[0].T, e // mlp_b_s.shape[1], axis=1)
    y = _gmlp(u, mlp_ln_g[0:1], mlp_ln_b[0:1], mlp_w_s[0], b_lanes)
    x = _outproj(y, u, 2, x, mods, 1, w_out_b[1], name="mlp_out", **tiles)

    u = _inproj(x, mods, 2, norm_g[2:3], s5_w_in[0].astype(BF16), name="s5_in", **tiles_in)
    s5_groups = e // S5_GROUP
    bg = LANES // S5_GROUP
    blocks = s5_groups // bg

    def block_vec(v):
        return v.reshape(2, blocks, 1, S5_BLOCK_ST).transpose(1, 0, 2, 3)

    def block_mat_b(v):
        v = v.reshape(2, blocks, bg, S5_STATE, S5_GROUP).transpose(1, 0, 4, 2, 3)
        return v.reshape(blocks, 2, S5_GROUP, S5_BLOCK_ST)

    def block_mat_c(v):
        v = v.reshape(2, blocks, bg, S5_GROUP, S5_STATE).transpose(1, 0, 3, 2, 4)
        return v.reshape(blocks, 2, S5_GROUP, S5_BLOCK_ST)

    log_step = jnp.repeat(s5_log_step[0][:, :, None], S5_STATE, axis=2)
    prm = (block_vec(s5_lam_re[0]), block_vec(s5_lam_im[0]), block_vec(log_step), block_mat_b(s5_b_re[0]),
           block_mat_b(s5_b_im[0]), block_mat_c(s5_c_re[0]), block_mat_c(s5_c_im[0]))
    h0 = state_s5[:, 0].reshape(n_lat, 2, 2, blocks, S5_BLOCK_ST).transpose(3, 1, 2, 0, 4)
    ys, fin = _s5_core(prm, h0, u, n_ctx=n_ctx, l_ctx=l_ctx, n_lat=n_lat, l_lat=l_lat, e=e)
    new_s5 = fin.transpose(3, 1, 2, 0, 4).reshape(n_ctx, 1, 2, 2, s5_groups, S5_STATE)
    y = _s5_glu(ys, u, s5_d[0:1], s5_w_glu[0].astype(BF16), s5_b_glu[0:1])
    x = _outproj(y, u, 1, x, mods, 2, w_out_b[2], name="s5_out", **tiles)

    u = _inproj(x, mods, 3, norm_g[3:4], nat_w_in[0].astype(BF16), name="nat_in", **tiles_in)
    y, new_k, new_v = _ctx_attention(u, None, n_seq=n_ctx, seq_len=l_ctx, e=e)
    y = _nat_attention(u, cache_k[:, 0], cache_v[:, 0], _nat_bias(nat_rpb[0]), y, n_seq=n_lat, seq_len=l_lat,
                       row_block0=t_ctx // l_lat, e=e)
    out = _outproj(y, u, 3, x, mods, 3, w_out_b[3], final_g=final_g[None], name="nat_out", **tiles)

    return (out[:t_ctx].reshape(n_ctx, l_ctx, d), out[t_ctx:].reshape(n_lat, l_lat, d),
            new_ssd[:, None], new_s5, new_k[:, None], new_v[:, None])
```

```python
import functools

import jax
import jax.numpy as jnp
from jax import lax
from jax.experimental import pallas as pl
from jax.experimental.pallas import tpu as pltpu

F32 = jnp.float32
BF16 = jnp.bfloat16
ACT = jnp.bfloat16
EPS = 1e-6
HIGHEST = lax.Precision.HIGHEST

LANES = 128
CHUNK = 128
SSD_HEAD_DIM = 64
SSD_GROUP_HEADS = 4
SSD_GROUP_CH = SSD_HEAD_DIM * SSD_GROUP_HEADS
SSD_STATE = 128
CONV_W = 5
CONV_HALO = 16
SSD_PREP_CHUNKS = 2
CONV_WIN = 2 * CHUNK
S5_T = 8
S5_GROUP = 16
S5_STATE = 64
S5_BLOCK_ST = (LANES // S5_GROUP) * S5_STATE
HEAD_DIM = 64
GRID_W = 64
WIN_ROWS = 8
WIN_COLS = 16
NAT_ROWS_PER_STEP = 4
MASKED = -1e30
VMEM_LIMIT = 52 * 1024 * 1024


def _silu(x):
    return x * jax.nn.sigmoid(x)


def _gelu(x):
    return 0.5 * x * (1.0 + jnp.tanh(0.7978845608028654 * (x + 0.044715 * (x * x * x))))


def _softplus(x):
    return jnp.maximum(x, 0.0) + jnp.log1p(jnp.exp(-jnp.abs(x)))


def _dot(a, b):
    return jnp.dot(a, b, preferred_element_type=F32)


def _dot_nt(a, b):
    return lax.dot_general(a, b, (((1,), (1,)), ((), ())), preferred_element_type=F32)


def _cumsum_rows(lower_tri, x):
    hi = x.astype(BF16)
    r1 = x - hi.astype(F32)
    mid = r1.astype(BF16)
    lo = (r1 - mid.astype(F32)).astype(BF16)
    return _dot(lower_tri, hi) + _dot(lower_tri, mid) + _dot(lower_tri, lo)


def _params(*sem):
    return pltpu.CompilerParams(dimension_semantics=sem, vmem_limit_bytes=VMEM_LIMIT)


def _mod_row(i, *, tm, t_ctx, l_lat):
    start = i * tm
    return jnp.where(start < t_ctx, 0, 1 + (start - t_ctx) // l_lat)


def _mod_body(c_ref, w_ref, b_ref, o_ref):
    cond = _silu(c_ref[...]).astype(BF16)
    o_ref[...] = _dot(cond, w_ref[...].astype(BF16)) + b_ref[...]


def _modulation(cond8, w_mod, b_mod, *, tn=1024):
    depth, d, n = w_mod.shape
    return pl.pallas_call(
        _mod_body,
        grid=(depth, n // tn),
        in_specs=[pl.BlockSpec((8, d), lambda l, j: (0, 0)),
                  pl.BlockSpec((None, d, tn), lambda l, j: (l, 0, j)),
                  pl.BlockSpec((None, 1, tn), lambda l, j: (l, 0, j))],
        out_specs=pl.BlockSpec((None, 8, tn), lambda l, j: (l, 0, j)),
        out_shape=jax.ShapeDtypeStruct((depth, 8, n), F32),
        compiler_params=_params("parallel", "parallel"),
        name="modulation",
    )(cond8, w_mod, b_mod.reshape(depth, 1, n))


def _inproj_body(x_ref, shift_ref, scale_ref, g_ref, w_ref, o_ref, h_ref):
    @pl.when(pl.program_id(1) == 0)
    def _():
        x = x_ref[...]
        y = x * lax.rsqrt(jnp.mean(x * x, axis=-1, keepdims=True) + EPS) * g_ref[...]
        h_ref[...] = (y * (1.0 + scale_ref[...]) + shift_ref[...]).astype(BF16)

    o_ref[...] = _dot(h_ref[...], w_ref[...]).astype(o_ref.dtype)


def _inproj(x, mods, layer, g, w, *, tm, tn, t_ctx, l_lat, name, out_dtype=ACT):
    t, d = x.shape
    n = w.shape[1]
    row = functools.partial(_mod_row, tm=tm, t_ctx=t_ctx, l_lat=l_lat)

    def mod_spec(part):
        return pl.BlockSpec((None, None, None, 1, d), lambda i, j: (layer, row(i), part, 0, 0))

    return pl.pallas_call(
        _inproj_body,
        grid=(t // tm, n // tn),
        in_specs=[pl.BlockSpec((tm, d), lambda i, j: (i, 0)), mod_spec(0), mod_spec(1),
                  pl.BlockSpec((1, d), lambda i, j: (0, 0)),
                  pl.BlockSpec((d, tn), lambda i, j: (0, j))],
        out_specs=pl.BlockSpec((tm, tn), lambda i, j: (i, j)),
        out_shape=jax.ShapeDtypeStruct((t, n), out_dtype),
        scratch_shapes=[pltpu.VMEM((tm, d), BF16)],
        compiler_params=_params("parallel", "arbitrary"),
        name=name,
    )(x, mods, mods, g, w)


def _outproj_body(*refs, gated_norm, final_norm):
    y_ref, z_ref, x_ref, gate_ref, w_ref = refs[:5]
    rest = list(refs[5:])
    t = y_ref[...].astype(F32) * _silu(z_ref[...].astype(F32))
    if gated_norm:
        ng_ref = rest.pop(0)
        t = t * lax.rsqrt(jnp.mean(t * t, axis=-1, keepdims=True) + EPS) * ng_ref[...]
    if final_norm:
        fg_ref = rest.pop(0)
    o_ref = rest.pop(0)
    xn = x_ref[...] + gate_ref[...] * _dot(t.astype(BF16), w_ref[...])
    if final_norm:
        xn = xn * lax.rsqrt(jnp.mean(xn * xn, axis=-1, keepdims=True) + EPS) * fg_ref[...]
    o_ref[...] = xn


def _outproj(y, u, z_block, x, mods, layer, w, *, norm_g=None, final_g=None, tm, t_ctx, l_lat, name):
    t, e = y.shape
    d = x.shape[1]
    row = functools.partial(_mod_row, tm=tm, t_ctx=t_ctx, l_lat=l_lat)
    in_specs = [pl.BlockSpec((tm, e), lambda i: (i, 0)),
                pl.BlockSpec((tm, e), lambda i: (i, z_block)),
                pl.BlockSpec((tm, d), lambda i: (i, 0)),
                pl.BlockSpec((None, None, None, 1, d), lambda i: (layer, row(i), 2, 0, 0)),
                pl.BlockSpec((e, d), lambda i: (0, 0))]
    args = [y, u, x, mods, w]
    if norm_g is not None:
        in_specs.append(pl.BlockSpec((1, e), lambda i: (0, 0)))
        args.append(norm_g)
    if final_g is not None:
        in_specs.append(pl.BlockSpec((1, d), lambda i: (0, 0)))
        args.append(final_g)
    return pl.pallas_call(
        functools.partial(_outproj_body, gated_norm=norm_g is not None, final_norm=final_g is not None),
        grid=(t // tm,),
        in_specs=in_specs,
        out_specs=pl.BlockSpec((tm, d), lambda i: (i, 0)),
        out_shape=jax.ShapeDtypeStruct((t, d), F32),
        compiler_params=_params("parallel"),
        name=name,
    )(*args)


def _ssd_body(*refs, seq_len, has_h0, want_final):
    refs = list(refs)
    x_ref, b_ref, c_ref, dt_ref, wx_ref, wb_ref, wc_ref, bx_ref, bb_ref, bc_ref, dtb_ref, alog_ref, dsk_ref = refs[:13]
    rest = refs[13:]
    h0_ref = rest.pop(0) if has_h0 else None
    if want_final:
        y_ref, hf_ref = rest[0], rest[1]
        rest = rest[2:]
    else:
        y_ref, hf_ref = rest[0], None
        rest = rest[1:]
    pad_s, shift_s, xb_s, xt_s, bm_s, bt_s, cm_s, e_s, et_s, ct_s, dtt_s, y_s, sf_s, sb_s = rest
    nc = seq_len // CHUNK
    width = SSD_GROUP_CH + 2 * SSD_STATE
    nh = SSD_GROUP_HEADS

    pad_s[0:CONV_HALO, :] = jnp.zeros((CONV_HALO, width), ACT)
    tail = CONV_WIN - CHUNK - CONV_HALO
    pad_s[CONV_HALO + seq_len:CONV_HALO + seq_len + tail, :] = jnp.zeros((tail, width), ACT)
    win_row = lax.broadcasted_iota(jnp.int32, (CHUNK, CONV_WIN), 1)
    tok_row = lax.broadcasted_iota(jnp.int32, (CHUNK, CONV_WIN), 0)
    taps = [j for j in range(CONV_W) if j != CONV_W // 2]
    for n, j in enumerate(taps):
        shift_s[n] = (win_row == tok_row + (CONV_HALO + j - CONV_W // 2)).astype(BF16)

    def copy_chunk(ci, carry):
        base = pl.multiple_of(ci * CHUNK, CHUNK)
        dst = pl.ds(base + CONV_HALO, CHUNK)
        pad_s[dst, 0:SSD_GROUP_CH] = x_ref[pl.ds(base, CHUNK), :]
        pad_s[dst, SSD_GROUP_CH:SSD_GROUP_CH + SSD_STATE] = b_ref[pl.ds(base, CHUNK), :]
        pad_s[dst, SSD_GROUP_CH + SSD_STATE:width] = c_ref[pl.ds(base, CHUNK), :]
        return carry

    lax.fori_loop(0, nc, copy_chunk, 0)

    conv_w = jnp.concatenate([wx_ref[...], wb_ref[...], wc_ref[...]], axis=1)
    conv_b = jnp.concatenate([bx_ref[...], bb_ref[...], bc_ref[...]], axis=1)
    a_row = -jnp.exp(alog_ref[...])
    row_i = lax.broadcasted_iota(jnp.int32, (CHUNK, CHUNK), 0)
    col_i = lax.broadcasted_iota(jnp.int32, (CHUNK, CHUNK), 1)
    lower = row_i >= col_i
    upper = row_i <= col_i
    lower_b = lower.astype(BF16)
    fwd_lane = lax.broadcasted_iota(jnp.int32, (1, LANES), 1) < nh

    def prep_chunks(pi, carry):
        cis = [pi * SSD_PREP_CHUNKS + n for n in range(SSD_PREP_CHUNKS)]
        rows = [pl.ds(pl.multiple_of(ci * CHUNK, CHUNK), CHUNK) for ci in cis]
        mid = CONV_W // 2
        half = width // 2
        units = [(n, h) for n in range(SSD_PREP_CHUNKS) for h in range(2)]

        def window(unit):
            n, h = unit
            return pad_s[pl.ds(pl.multiple_of(cis[n] * CHUNK, CHUNK), CONV_WIN), h * half:(h + 1) * half]

        def shift_dots(unit):
            win = window(unit)
            return [_dot(shift_s[t], win) for t in range(len(taps))]

        def finish(unit, shifted):
            n, h = unit
            ci, r = cis[n], rows[n]
            cols = slice(h * half, (h + 1) * half)
            acc = conv_b[:, cols] + window(unit)[CONV_HALO:CONV_HALO + CHUNK].astype(F32) * conv_w[mid:mid + 1, cols]
            for t, j in enumerate(taps):
                acc = acc + shifted[t] * conv_w[j:j + 1, cols]
            v = _silu(acc)
            if h == 0:
                y_s[r, :] = v * dsk_ref[...]
                xb_s[r, :] = v.astype(BF16)
                xt_s[ci] = v.T
            else:
                bm = v[:, 0:SSD_STATE]
                bm_s[r, :] = bm.astype(BF16)
                bt_s[ci] = bm.T.astype(BF16)
                cm_s[r, :] = v[:, SSD_STATE:half].astype(BF16)

        dts = [_softplus(dt_ref[r, :] + dtb_ref[...]) for r in rows]
        das = [dt * a_row for dt in dts]
        pending = shift_dots(units[0])
        cums = []
        for i, unit in enumerate(units):
            following = shift_dots(units[i + 1]) if i + 1 < len(units) else None
            if unit[1] == 0:
                cums.append(_cumsum_rows(lower_b, das[unit[0]]))
            finish(unit, pending)
            pending = following
        for ci, r, dt, da, cum in zip(cis, rows, dts, das, cums):
            e = jnp.where(fwd_lane, cum, cum - da)
            e_s[r, :] = e
            et_s[ci] = e.T[0:2 * nh]
            ct_s[ci] = cum.T[0:2 * nh]
            dtt_s[ci] = dt.T[0:2 * nh]
        return carry

    lax.fori_loop(0, nc // SSD_PREP_CHUNKS, prep_chunks, 0)

    if has_h0:
        sf_s[...] = h0_ref[0].reshape(SSD_GROUP_CH, SSD_STATE)
        sb_s[...] = h0_ref[1].reshape(SSD_GROUP_CH, SSD_STATE)
    else:
        sf_s[...] = jnp.zeros((SSD_GROUP_CH, SSD_STATE), F32)
        sb_s[...] = jnp.zeros((SSD_GROUP_CH, SSD_STATE), F32)

    head_of_lane = lax.broadcasted_iota(jnp.int32, (CHUNK, SSD_GROUP_CH), 1) // SSD_HEAD_DIM
    st_refs = (sf_s, sb_s)

    def scan_chunk(i, carry):
        chunks = (i, nc - 1 - i)
        rows = [pl.ds(pl.multiple_of(ci * CHUNK, CHUNK), CHUNK) for ci in chunks]
        s_prev = [st_refs[d][...] for d in range(2)]
        g, cs = [], []
        for d in range(2):
            cm = cm_s[rows[d], :]
            g.append(_dot(cm, bt_s[chunks[d]]))
            cs.append(_dot_nt(cm, s_prev[d].astype(BF16)))
        w_rows, keeps, ys = [], [], []
        for d in range(2):
            e = e_s[rows[d], :]
            e_t, c_t, dt_t = et_s[chunks[d]], ct_s[chunks[d]], dtt_s[chunks[d]]
            xb = xb_s[rows[d], :]
            y = jnp.zeros((CHUNK, SSD_GROUP_CH), F32)
            w_d, keep_d = [], []
            for r in range(nh):
                col = r + nh * d
                ec = jnp.broadcast_to(e[:, col:col + 1], (CHUNK, CHUNK))
                er = e_t[col:col + 1, :]
                dt_r = dt_t[col:col + 1, :]
                tot = c_t[col:col + 1, CHUNK - 1:CHUNK]
                if d == 0:
                    decay = jnp.exp(jnp.where(lower, ec - er, -jnp.inf))
                    off = jnp.exp(ec)
                    w_d.append(jnp.broadcast_to(dt_r * jnp.exp(tot - er), (SSD_HEAD_DIM, CHUNK)))
                else:
                    decay = jnp.exp(jnp.where(upper, er - ec, -jnp.inf))
                    off = jnp.exp(tot - ec)
                    w_d.append(jnp.broadcast_to(dt_r * jnp.exp(er), (SSD_HEAD_DIM, CHUNK)))
                keep_d.append(jnp.broadcast_to(jnp.exp(tot), (SSD_HEAD_DIM, SSD_STATE)))
                yd = _dot((g[d] * decay * dt_r).astype(BF16), xb)
                off2 = jnp.concatenate([off] * (SSD_GROUP_CH // CHUNK), axis=1)
                y = jnp.where(head_of_lane == r, yd + cs[d] * off2, y)
            ys.append(y)
            w_rows.append(jnp.concatenate(w_d, axis=0))
            keeps.append(jnp.concatenate(keep_d, axis=0))
        for d in range(2):
            w_t = (xt_s[chunks[d]] * w_rows[d]).astype(BF16)
            st_refs[d][...] = keeps[d] * s_prev[d] + _dot(w_t, bm_s[rows[d], :])
            y_s[rows[d], :] += ys[d]
        return carry

    lax.fori_loop(0, nc, scan_chunk, 0)

    def emit_chunk(ci, carry):
        rows = pl.ds(pl.multiple_of(ci * CHUNK, CHUNK), CHUNK)
        y_ref[rows, :] = y_s[rows, :].astype(y_ref.dtype)
        return carry

    lax.fori_loop(0, nc, emit_chunk, 0)

    if want_final:
        hf_ref[0] = sf_s[...].reshape(SSD_GROUP_HEADS, SSD_HEAD_DIM, SSD_STATE)
        hf_ref[1] = sb_s[...].reshape(SSD_GROUP_HEADS, SSD_HEAD_DIM, SSD_STATE)


def _ssd_call(u, dt, conv_w, conv_b, dtb, alog, dsk, y_prev, h0, *, n_seq, seq_len, row_block0, want_final, name):
    t, n_u = u.shape
    e = dsk.shape[1]
    groups = e // SSD_GROUP_CH
    xb0 = e // SSD_GROUP_CH
    bb0 = 2 * e // SSD_STATE
    cb0 = bb0 + groups
    has_h0 = h0 is not None
    seq = lambda b: row_block0 + b
    in_specs = [pl.BlockSpec((seq_len, SSD_GROUP_CH), lambda b, g: (seq(b), xb0 + g)),
                pl.BlockSpec((seq_len, SSD_STATE), lambda b, g: (seq(b), bb0 + g)),
                pl.BlockSpec((seq_len, SSD_STATE), lambda b, g: (seq(b), cb0 + g)),
                pl.BlockSpec((seq_len, LANES), lambda b, g: (seq(b), g)),
                pl.BlockSpec((CONV_W, SSD_GROUP_CH), lambda b, g: (0, g)),
                pl.BlockSpec((CONV_W, SSD_STATE), lambda b, g: (0, bb0 - xb0 * 2 + g)),
                pl.BlockSpec((CONV_W, SSD_STATE), lambda b, g: (0, cb0 - xb0 * 2 + g)),
                pl.BlockSpec((1, SSD_GROUP_CH), lambda b, g: (0, g)),
                pl.BlockSpec((1, SSD_STATE), lambda b, g: (0, bb0 - xb0 * 2 + g)),
                pl.BlockSpec((1, SSD_STATE), lambda b, g: (0, cb0 - xb0 * 2 + g)),
                pl.BlockSpec((1, LANES), lambda b, g: (0, g)),
                pl.BlockSpec((1, LANES), lambda b, g: (0, g)),
                pl.BlockSpec((1, SSD_GROUP_CH), lambda b, g: (0, g))]
    args = [u, u, u, dt, conv_w, conv_w, conv_w, conv_b, conv_b, conv_b, dtb, alog, dsk]
    state_block = (None, 2, SSD_GROUP_HEADS, SSD_HEAD_DIM, SSD_STATE)
    if has_h0:
        in_specs.append(pl.BlockSpec(state_block, lambda b, g: (b, 0, g, 0, 0)))
        args.append(h0)
    aliases = {}
    if y_prev is not None:
        in_specs.append(pl.BlockSpec(memory_space=pl.ANY))
        args.append(y_prev)
        aliases = {len(args) - 1: 0}
    out_specs = [pl.BlockSpec((seq_len, SSD_GROUP_CH), lambda b, g: (seq(b), g))]
    out_shape = [jax.ShapeDtypeStruct((t, e), ACT)]
    if want_final:
        out_specs.append(pl.BlockSpec(state_block, lambda b, g: (b, 0, g, 0, 0)))
        out_shape.append(jax.ShapeDtypeStruct((n_seq, 2, e // SSD_HEAD_DIM, SSD_HEAD_DIM, SSD_STATE), F32))
    width = SSD_GROUP_CH + 2 * SSD_STATE
    nc = seq_len // CHUNK

    def body(*refs):
        refs = list(refs)
        if y_prev is not None:
            n_in = len(args)
            del refs[n_in - 1]
        _ssd_body(*refs, seq_len=seq_len, has_h0=has_h0, want_final=want_final)

    return pl.pallas_call(
        body,
        grid=(n_seq, groups),
        in_specs=in_specs,
        out_specs=out_specs,
        out_shape=out_shape,
        input_output_aliases=aliases,
        scratch_shapes=[pltpu.VMEM((seq_len + CONV_WIN - CHUNK, width), ACT),
                        pltpu.VMEM((CONV_W - 1, CHUNK, CONV_WIN), BF16),
                        pltpu.VMEM((seq_len, SSD_GROUP_CH), BF16),
                        pltpu.VMEM((nc, SSD_GROUP_CH, CHUNK), F32),
                        pltpu.VMEM((seq_len, SSD_STATE), BF16),
                        pltpu.VMEM((nc, SSD_STATE, CHUNK), BF16),
                        pltpu.VMEM((seq_len, SSD_STATE), BF16),
                        pltpu.VMEM((seq_len, LANES), F32),
                        pltpu.VMEM((nc, 2 * SSD_GROUP_HEADS, CHUNK), F32),
                        pltpu.VMEM((nc, 2 * SSD_GROUP_HEADS, CHUNK), F32),
                        pltpu.VMEM((nc, 2 * SSD_GROUP_HEADS, CHUNK), F32),
                        pltpu.VMEM((seq_len, SSD_GROUP_CH), F32),
                        pltpu.VMEM((SSD_GROUP_CH, SSD_STATE), F32),
                        pltpu.VMEM((SSD_GROUP_CH, SSD_STATE), F32)],
        compiler_params=_params("parallel", "parallel"),
        name=name,
    )(*args)


def _group_lanes(v, groups):
    per_group = v.reshape(2, groups, SSD_GROUP_HEADS).transpose(1, 0, 2).reshape(groups, 2 * SSD_GROUP_HEADS)
    return jnp.pad(per_group, ((0, 0), (0, LANES - 2 * SSD_GROUP_HEADS))).reshape(1, groups * LANES)


def _gmlp_body(u_ref, v_ref, lng_ref, lnb_ref, ws_ref, bs_ref, o_ref, *, tm):
    groups = ws_ref.shape[0]
    gch = u_ref.shape[1] // groups
    for c in range(tm // CHUNK):
        rows = slice(c * CHUNK, (c + 1) * CHUNK)
        v = _gelu(v_ref[rows, :].astype(F32))
        vc = v - jnp.mean(v, axis=-1, keepdims=True)
        vn = vc * lax.rsqrt(jnp.mean(vc * vc, axis=-1, keepdims=True) + EPS) * lng_ref[...] + lnb_ref[...]
        vb = vn.astype(BF16)
        for g in range(groups):
            cols = slice(g * gch, (g + 1) * gch)
            s = _dot(ws_ref[g].astype(BF16), vb[:, cols]) + bs_ref[:, cols]
            o_ref[rows, cols] = (_gelu(u_ref[rows, cols].astype(F32)) * s).astype(o_ref.dtype)


def _gmlp(u, ln_g, ln_b, w_s, b_lanes, *, tm=256):
    t = u.shape[0]
    e = ln_g.shape[1]
    return pl.pallas_call(
        functools.partial(_gmlp_body, tm=tm),
        grid=(t // tm,),
        in_specs=[pl.BlockSpec((tm, e), lambda i: (i, 0)),
                  pl.BlockSpec((tm, e), lambda i: (i, 1)),
                  pl.BlockSpec((1, e), lambda i: (0, 0)),
                  pl.BlockSpec((1, e), lambda i: (0, 0)),
                  pl.BlockSpec(w_s.shape, lambda i: (0, 0, 0)),
                  pl.BlockSpec((CHUNK, e), lambda i: (0, 0))],
        out_specs=pl.BlockSpec((tm, e), lambda i: (i, 0)),
        out_shape=jax.ShapeDtypeStruct((t, e), ACT),
        compiler_params=_params("parallel"),
        name="gmlp",
    )(u, u, ln_g, ln_b, w_s, b_lanes)


def _s5_body(lr_ref, li_ref, ls_ref, brt_ref, bit_ref, cre_ref, cim_ref, h0_ref, u_ref, y_ref, fin_ref,
             win_s, t_s, ef_s, eb_s, z_s, a_s, zin_s, d_s, spf_s, spb_s, yo_s, uf_s,
             *, seqs, l_ctx, n_lat, l_lat, ctx_parts):
    bst = S5_BLOCK_ST
    ng = LANES // S5_GROUP
    kw = S5_T * LANES
    part = pl.program_id(1)

    @pl.when(part == 0)
    def _build():
        own = (lax.broadcasted_iota(jnp.int32, (ng, S5_GROUP, bst), 2) // S5_STATE
               == lax.broadcasted_iota(jnp.int32, (ng, S5_GROUP, bst), 0))

        def spread(v):
            return jnp.where(own, v[None], 0.0).reshape(LANES, bst).astype(BF16)

        tau = lax.broadcasted_iota(jnp.int32, (S5_T + 8, 1), 0).astype(F32)
        e_refs = (ef_s, eb_s)
        for d in range(2):
            lr, li = lr_ref[d], li_ref[d]
            step = jnp.exp(ls_ref[d])
            mag = jnp.exp(tau * (lr * step))
            p_re = mag * jnp.cos(tau * (li * step))
            p_im = mag * jnp.sin(tau * (li * step))
            ab_re, ab_im = p_re[1:2], p_im[1:2]
            den = lr * lr + li * li
            nr = ab_re - 1.0
            cr = (nr * lr + ab_im * li) / den
            ci = (ab_im * lr - nr * li) / den
            brt, bit = brt_ref[d], bit_ref[d]
            bb_re = cr * brt - ci * bit
            bb_im = cr * bit + ci * brt
            cre, cim = cre_ref[d], cim_ref[d]
            for k in range(S5_T):
                rows = slice(k * LANES, (k + 1) * LANES)
                tq = S5_T - 1 - k if d == 0 else k
                te = k + 1 if d == 0 else S5_T - k
                pr, pi = p_re[tq:tq + 1], p_im[tq:tq + 1]
                win_s[rows, 2 * d * bst:(2 * d + 1) * bst] = spread(pr * bb_re - pi * bb_im)
                win_s[rows, (2 * d + 1) * bst:(2 * d + 2) * bst] = spread(pr * bb_im + pi * bb_re)
                pr, pi = p_re[te:te + 1], p_im[te:te + 1]
                e_refs[d][rows, 0:bst] = spread(cre * pr - cim * pi)
                e_refs[d][rows, bst:2 * bst] = spread(-(cre * pi + cim * pr))
            c_own = jnp.concatenate([spread(cre), spread(-cim)], axis=1)
            z = _dot_nt(win_s[:, 2 * d * bst:(2 * d + 2) * bst], c_own)
            if d == 0:
                z_s[0:kw, :] = z
            else:
                z_s[kw - LANES:kw, :] += z[0:LANES]
                z_s[kw:2 * kw - LANES, :] = z[LANES:kw]
            a_s[2 * d:2 * d + 1, :] = p_re[S5_T:S5_T + 1]
            a_s[2 * d + 1:2 * d + 2, :] = p_im[S5_T:S5_T + 1]
        for k in range(S5_T):
            r0 = (S5_T - 1 - k) * LANES
            t_s[:, k * LANES:(k + 1) * LANES] = z_s[r0:r0 + kw, :].astype(BF16)

    def outputs():
        yo_s[...] = (_dot(zin_s[...], t_s[...]) + _dot_nt(spf_s[...].astype(BF16), ef_s[...])
                     + _dot_nt(spb_s[...].astype(BF16), eb_s[...]))

    @pl.when(part < ctx_parts)
    def _context():
        nj = l_ctx // S5_T
        uf_s[...] = u_ref[...].astype(F32)
        for j in range(nj):
            for k in range(S5_T):
                zin_s[j * seqs:(j + 1) * seqs, k * LANES:(k + 1) * LANES] = (
                    uf_s[pl.ds(j * S5_T + k, seqs, stride=l_ctx), :].astype(BF16))
        d_s[...] = _dot(zin_s[...], win_s[...])
        coef = [jnp.broadcast_to(a_s[i:i + 1, :], (seqs, bst)) for i in range(4)]
        zero = jnp.zeros((seqs, bst), F32)

        def step(j, carry):
            fr, fi, br, bi = carry
            rf = pl.ds(pl.multiple_of(j * seqs, seqs), seqs)
            rb = pl.ds(pl.multiple_of((nj - 1 - j) * seqs, seqs), seqs)
            spf_s[rf, 0:bst] = fr
            spf_s[rf, bst:2 * bst] = fi
            spb_s[rb, 0:bst] = br
            spb_s[rb, bst:2 * bst] = bi
            return (coef[0] * fr - coef[1] * fi + d_s[rf, 0:bst],
                    coef[0] * fi + coef[1] * fr + d_s[rf, bst:2 * bst],
                    coef[2] * br - coef[3] * bi + d_s[rb, 2 * bst:3 * bst],
                    coef[2] * bi + coef[3] * br + d_s[rb, 3 * bst:4 * bst])

        fr, fi, br, bi = lax.fori_loop(0, nj, step, (zero, zero, zero, zero))
        fin_ref[0, 0] = fr
        fin_ref[0, 1] = fi
        fin_ref[1, 0] = br
        fin_ref[1, 1] = bi
        outputs()
        for j in range(nj):
            for k in range(S5_T):
                y_ref[pl.ds(j * S5_T + k, seqs, stride=l_ctx), :] = (
                    yo_s[j * seqs:(j + 1) * seqs, k * LANES:(k + 1) * LANES])

    @pl.when(part == ctx_parts)
    def _latent():
        nj = l_lat // S5_T
        uf_s[...] = u_ref[...].astype(F32)
        for b in range(n_lat):
            for k in range(S5_T):
                zin_s[b * nj:(b + 1) * nj, k * LANES:(k + 1) * LANES] = (
                    uf_s[pl.ds(b * l_lat + k, nj, stride=S5_T), :].astype(BF16))
        d_s[...] = _dot(zin_s[...], win_s[...])
        coef_re = jnp.concatenate([jnp.broadcast_to(a_s[0:1, :], (n_lat, bst)),
                                   jnp.broadcast_to(a_s[2:3, :], (n_lat, bst))], axis=0)
        coef_im = jnp.concatenate([jnp.broadcast_to(a_s[1:2, :], (n_lat, bst)),
                                   jnp.broadcast_to(a_s[3:4, :], (n_lat, bst))], axis=0)
        s_re0 = jnp.concatenate([h0_ref[0, 0], h0_ref[1, 0]], axis=0)
        s_im0 = jnp.concatenate([h0_ref[0, 1], h0_ref[1, 1]], axis=0)

        def step(j, carry):
            s_re, s_im = carry
            d_re, d_im = [], []
            for b in range(n_lat):
                row = pl.ds(b * nj + j, 1)
                spf_s[row, 0:bst] = s_re[b:b + 1]
                spf_s[row, bst:2 * bst] = s_im[b:b + 1]
                d_re.append(d_s[row, 0:bst])
                d_im.append(d_s[row, bst:2 * bst])
            for b in range(n_lat):
                row = pl.ds(b * nj + nj - 1 - j, 1)
                spb_s[row, 0:bst] = s_re[n_lat + b:n_lat + b + 1]
                spb_s[row, bst:2 * bst] = s_im[n_lat + b:n_lat + b + 1]
                d_re.append(d_s[row, 2 * bst:3 * bst])
                d_im.append(d_s[row, 3 * bst:4 * bst])
            d_re = jnp.concatenate(d_re, axis=0)
            d_im = jnp.concatenate(d_im, axis=0)
            return coef_re * s_re - coef_im * s_im + d_re, coef_re * s_im + coef_im * s_re + d_im

        lax.fori_loop(0, nj, step, (s_re0, s_im0))
        outputs()
        for b in range(n_lat):
            for k in range(S5_T):
                y_ref[pl.ds(b * l_lat + k, nj, stride=S5_T), :] = yo_s[b * nj:(b + 1) * nj, k * LANES:(k + 1) * LANES]


def _s5_core(prm, h0, u, *, n_ctx, l_ctx, n_lat, l_lat, e):
    blocks = prm[0].shape[0]
    part_tokens = n_lat * l_lat
    t = u.shape[0]
    n_parts = t // part_tokens
    seqs = part_tokens // l_ctx
    r = part_tokens // S5_T
    vec = pl.BlockSpec((None, 2, 1, S5_BLOCK_ST), lambda g, p: (g, 0, 0, 0))
    mat = pl.BlockSpec((None, 2, S5_GROUP, S5_BLOCK_ST), lambda g, p: (g, 0, 0, 0))
    k_in = S5_T * LANES
    return pl.pallas_call(
        functools.partial(_s5_body, seqs=seqs, l_ctx=l_ctx, n_lat=n_lat, l_lat=l_lat, ctx_parts=n_parts - 1),
        grid=(blocks, n_parts),
        in_specs=[vec, vec, vec, mat, mat, mat, mat,
                  pl.BlockSpec((None, 2, 2, n_lat, S5_BLOCK_ST), lambda g, p: (g, 0, 0, 0, 0)),
                  pl.BlockSpec((part_tokens, LANES), lambda g, p: (p, g))],
        out_specs=[pl.BlockSpec((part_tokens, LANES), lambda g, p: (p, g)),
                   pl.BlockSpec((None, 2, 2, seqs, S5_BLOCK_ST),
                                lambda g, p: (g, 0, 0, jnp.minimum(p, n_parts - 2), 0))],
        out_shape=[jax.ShapeDtypeStruct((t, e), F32),
                   jax.ShapeDtypeStruct((blocks, 2, 2, n_ctx, S5_BLOCK_ST), F32)],
        scratch_shapes=[pltpu.VMEM((k_in, 4 * S5_BLOCK_ST), BF16),
                        pltpu.VMEM((k_in, k_in), BF16),
                        pltpu.VMEM((k_in, 2 * S5_BLOCK_ST), BF16),
                        pltpu.VMEM((k_in, 2 * S5_BLOCK_ST), BF16),
                        pltpu.VMEM(((2 * S5_T - 1) * LANES, LANES), F32),
                        pltpu.VMEM((8, S5_BLOCK_ST), F32),
                        pltpu.VMEM((r, k_in), BF16),
                        pltpu.VMEM((r, 4 * S5_BLOCK_ST), F32),
                        pltpu.VMEM((r, 2 * S5_BLOCK_ST), F32),
                        pltpu.VMEM((r, 2 * S5_BLOCK_ST), F32),
                        pltpu.VMEM((r, k_in), F32),
                        pltpu.VMEM((part_tokens, LANES), F32)],
        compiler_params=_params("parallel", "arbitrary"),
        name="s5_core",
    )(*prm, h0, u)


def _s5_glu_body(ys_ref, u_ref, dsk_ref, w_ref, b_ref, o_ref):
    y = _gelu(ys_ref[...] + dsk_ref[...] * u_ref[...].astype(F32))
    o_ref[...] = (y * jax.nn.sigmoid(_dot(y.astype(BF16), w_ref[...]) + b_ref[...])).astype(o_ref.dtype)


def _s5_glu(ys, u, dsk, w, b, *, tm=512):
    t, e = ys.shape
    return pl.pallas_call(
        _s5_glu_body,
        grid=(t // tm,),
        in_specs=[pl.BlockSpec((tm, e), lambda i: (i, 0)),
                  pl.BlockSpec((tm, e), lambda i: (i, 0)),
                  pl.BlockSpec((1, e), lambda i: (0, 0)),
                  pl.BlockSpec((e, e), lambda i: (0, 0)),
                  pl.BlockSpec((1, e), lambda i: (0, 0))],
        out_specs=pl.BlockSpec((tm, e), lambda i: (i, 0)),
        out_shape=jax.ShapeDtypeStruct((t, e), ACT),
        compiler_params=_params("parallel"),
        name="s5_glu",
    )(ys, u, dsk, w, b)


def _ctx_attn_body(q_ref, k_ref, v_ref, o_ref, ko_ref, vo_ref):
    scale = HEAD_DIM ** -0.5
    seq_len = q_ref.shape[0]
    first = lax.broadcasted_iota(jnp.int32, (seq_len, LANES), 1) < HEAD_DIM
    masks = (first, jnp.logical_not(first))
    n_pairs = q_ref.shape[1] // LANES
    logits, values = [], []
    for pair in range(n_pairs):
        cols = slice(pair * LANES, (pair + 1) * LANES)
        q2, k2, v2 = q_ref[:, cols].astype(F32) * scale, k_ref[:, cols], v_ref[:, cols]
        kb = k2.astype(BF16)
        values.append(v2.astype(BF16))
        for h in range(LANES // HEAD_DIM):
            ko_ref[2 * pair + h] = k2[:, h * HEAD_DIM:(h + 1) * HEAD_DIM].astype(F32)
            vo_ref[2 * pair + h] = v2[:, h * HEAD_DIM:(h + 1) * HEAD_DIM].astype(F32)
            logits.append(_dot_nt(jnp.where(masks[h], q2, 0.0).astype(BF16), kb))
    probs = []
    for s in logits:
        p = jnp.exp(s - jnp.max(s, axis=-1, keepdims=True))
        probs.append((p * (1.0 / jnp.sum(p, axis=-1, keepdims=True))).astype(BF16))
    for pair in range(n_pairs):
        o0, o1 = _dot(probs[2 * pair], values[pair]), _dot(probs[2 * pair + 1], values[pair])
        o_ref[:, pair * LANES:(pair + 1) * LANES] = jnp.where(first, o0, o1).astype(o_ref.dtype)


def _ctx_attention(u, y_prev, *, n_seq, seq_len, e, width=512):
    t = u.shape[0]
    hp = e // width
    heads = e // HEAD_DIM
    kv_block = (None, width // HEAD_DIM, seq_len, HEAD_DIM)
    in_specs = [pl.BlockSpec((seq_len, width), lambda b, h: (b, h)),
                pl.BlockSpec((seq_len, width), lambda b, h: (b, hp + h)),
                pl.BlockSpec((seq_len, width), lambda b, h: (b, 2 * hp + h))]
    return pl.pallas_call(
        _ctx_attn_body,
        grid=(n_seq, hp),
        in_specs=in_specs,
        out_specs=[pl.BlockSpec((seq_len, width), lambda b, h: (b, h)),
                   pl.BlockSpec(kv_block, lambda b, h: (b, h, 0, 0)),
                   pl.BlockSpec(kv_block, lambda b, h: (b, h, 0, 0))],
        out_shape=[jax.ShapeDtypeStruct((t, e), ACT),
                   jax.ShapeDtypeStruct((n_seq, heads, seq_len, HEAD_DIM), F32),
                   jax.ShapeDtypeStruct((n_seq, heads, seq_len, HEAD_DIM), F32)],
        compiler_params=_params("parallel", "parallel"),
        name="ctx_attention",
    )(u, u, u)


def _nat_bias_body(rpb_ref, o_ref, *, n_rel_rows, n_rel_cols):
    h = pl.program_id(0)
    q = lax.broadcasted_iota(jnp.int32, (GRID_W, GRID_W), 0)
    kc = lax.broadcasted_iota(jnp.int32, (GRID_W, GRID_W), 1)
    rel = kc - q + (WIN_COLS - 1)
    c_start = jnp.clip(q - WIN_COLS // 2, 0, GRID_W - WIN_COLS)
    ok = (kc >= c_start) & (kc < c_start + WIN_COLS)
    for i in range(n_rel_rows):
        acc = jnp.zeros((GRID_W, GRID_W), F32)
        for dcol in range(n_rel_cols):
            acc = jnp.where(rel == dcol, rpb_ref[(h * n_rel_rows + i) * n_rel_cols + dcol], acc)
        tile = jnp.where(ok, acc, MASKED)
        if i < n_rel_rows - 1:
            o_ref[i, :, 0:GRID_W] = tile
        if i > 0:
            o_ref[i - 1, :, GRID_W:2 * GRID_W] = tile


def _nat_bias(rpb):
    heads, n_rel_rows, n_rel_cols = rpb.shape
    return pl.pallas_call(
        functools.partial(_nat_bias_body, n_rel_rows=n_rel_rows, n_rel_cols=n_rel_cols),
        grid=(heads,),
        in_specs=[pl.BlockSpec(memory_space=pltpu.SMEM)],
        out_specs=pl.BlockSpec((None, n_rel_rows - 1, GRID_W, 2 * GRID_W), lambda h: (h, 0, 0, 0)),
        out_shape=jax.ShapeDtypeStruct((heads, n_rel_rows - 1, GRID_W, 2 * GRID_W), F32),
        compiler_params=_params("parallel"),
        name="nat_bias",
    )(rpb.reshape(-1))


def _nat_body(q_ref, k_ref, v_ref, ck_ref, cv_ref, bias_ref, o_ref, ckb_s, cvb_s, *, rows):
    scale = HEAD_DIM ** -0.5
    wr = min(WIN_ROWS, rows)
    nw = wr * GRID_W
    ckb_s[...] = jnp.concatenate([ck_ref[0], ck_ref[1]], axis=1).astype(BF16)
    cvb_s[...] = jnp.concatenate([cv_ref[0], cv_ref[1]], axis=1).astype(BF16)
    first = lax.broadcasted_iota(jnp.int32, (GRID_W, LANES), 1) < HEAD_DIM

    heads = tuple(range(LANES // HEAD_DIM))
    masks = (first, jnp.logical_not(first))

    def row_group(g, carry):
        rws = [g * NAT_ROWS_PER_STEP + i for i in range(NAT_ROWS_PER_STEP)]
        starts = [jnp.clip(r - wr // 2, 0, rows - wr) for r in rws]
        q_rows = [pl.ds(pl.multiple_of(r * GRID_W, GRID_W), GRID_W) for r in rws]
        k_rows = [pl.ds(pl.multiple_of(rs * GRID_W, GRID_W), nw) for rs in starts]
        logits = []
        for r, rs, qr, kr in zip(rws, starts, q_rows, k_rows):
            q2 = q_ref[qr, :].astype(F32) * scale
            kb = k_ref[kr, :].astype(BF16)
            for h in heads:
                q = jnp.where(masks[h], q2, 0.0).astype(BF16)
                logits.append((_dot_nt(q, kb), _dot_nt(q, ckb_s[...])))
        probs = []
        for idx, (s_win, s_ctx) in enumerate(logits):
            r, rs, h = rws[idx // 2], starts[idx // 2], heads[idx % 2]
            i0 = (WIN_ROWS - 1) - (r - rs)
            s_win = s_win + jnp.concatenate([bias_ref[h, i0 + 2 * jj] for jj in range(wr // 2)], axis=1)
            m = jnp.maximum(jnp.max(s_win, axis=-1, keepdims=True), jnp.max(s_ctx, axis=-1, keepdims=True))
            p_win = jnp.exp(s_win - m)
            p_ctx = jnp.exp(s_ctx - m)
            inv = 1.0 / (jnp.sum(p_win, axis=-1, keepdims=True) + jnp.sum(p_ctx, axis=-1, keepdims=True))
            probs.append(((p_win * inv).astype(BF16), (p_ctx * inv).astype(BF16)))
        for i, (qr, kr) in enumerate(zip(q_rows, k_rows)):
            vb = v_ref[kr, :].astype(BF16)
            outs = [_dot(probs[2 * i + h][0], vb) + _dot(probs[2 * i + h][1], cvb_s[...]) for h in heads]
            o_ref[qr, :] = jnp.where(first, outs[0], outs[1]).astype(o_ref.dtype)
        return carry

    lax.fori_loop(0, rows // NAT_ROWS_PER_STEP, row_group, 0)


def _nat_attention(u, cache_k, cache_v, bias, y_prev, *, n_seq, seq_len, row_block0, e):
    t = u.shape[0]
    hp = e // LANES
    hpb = LANES // HEAD_DIM
    past = cache_k.shape[2]
    seq = lambda b: row_block0 + b
    cache_block = (None, hpb, past, HEAD_DIM)
    return pl.pallas_call(
        lambda q, k, v, ck, cv, bs, yp, o, *scratch: _nat_body(q, k, v, ck, cv, bs, o, *scratch,
                                                               rows=seq_len // GRID_W),
        grid=(n_seq, hp),
        in_specs=[pl.BlockSpec((seq_len, LANES), lambda b, h: (seq(b), h)),
                  pl.BlockSpec((seq_len, LANES), lambda b, h: (seq(b), hp + h)),
                  pl.BlockSpec((seq_len, LANES), lambda b, h: (seq(b), 2 * hp + h)),
                  pl.BlockSpec(cache_block, lambda b, h: (b, h, 0, 0)),
                  pl.BlockSpec(cache_block, lambda b, h: (b, h, 0, 0)),
                  pl.BlockSpec((hpb,) + bias.shape[1:], lambda b, h: (h, 0, 0, 0)),
                  pl.BlockSpec(memory_space=pl.ANY)],
        out_specs=pl.BlockSpec((seq_len, LANES), lambda b, h: (seq(b), h)),
        out_shape=jax.ShapeDtypeStruct((t, e), ACT),
        input_output_aliases={6: 0},
        scratch_shapes=[pltpu.VMEM((past, LANES), BF16), pltpu.VMEM((past, LANES), BF16)],
        compiler_params=_params("parallel", "parallel"),
        name="nat_attention",
    )(u, u, u, cache_k, cache_v, bias, y_prev)


def kernel(x_prompt, x_sample, state_ssd, state_s5, cache_k, cache_v, c, c_ctx, norm_g, w_mod, b_mod, w_out, final_g, ssd_w_in, ssd_conv_w, ssd_conv_b, ssd_dt_bias, ssd_a_log, ssd_d, ssd_norm_g, mlp_w_in, mlp_ln_g, mlp_ln_b, mlp_w_s, mlp_b_s, s5_w_in, s5_lam_re, s5_lam_im, s5_log_step, s5_b_re, s5_b_im, s5_c_re, s5_c_im, s5_d, s5_w_glu, s5_b_glu, nat_w_in, nat_rpb):
    n_ctx, l_ctx, d = x_prompt.shape
    n_lat, l_lat, _ = x_sample.shape
    t_ctx = n_ctx * l_ctx
    depth = norm_g.shape[0]
    e = w_out.shape[1]
    assert depth == 4 and l_lat % l_ctx == 0 and t_ctx % (n_lat * l_lat) == 0
    tiles = dict(tm=512, t_ctx=t_ctx, l_lat=l_lat)
    tiles_in = dict(tm=1024, tn=1024, t_ctx=t_ctx, l_lat=l_lat)

    x = jnp.concatenate([x_prompt.reshape(t_ctx, d), x_sample.reshape(n_lat * l_lat, d)], axis=0)
    cond8 = jnp.concatenate([c_ctx[None], c, jnp.zeros((8 - 1 - n_lat, d), F32)], axis=0)
    mods = _modulation(cond8, w_mod, b_mod).reshape(depth, 8, 3, 1, d)
    w_out_b = w_out.astype(BF16)

    groups = e // SSD_GROUP_CH
    n_main = 3 * e
    w_dt = ssd_w_in[0][:, n_main:].reshape(d, 2, groups, SSD_GROUP_HEADS).transpose(0, 2, 1, 3)
    w_dt = jnp.pad(w_dt.reshape(d, groups, 2 * SSD_GROUP_HEADS), ((0, 0), (0, 0), (0, LANES - 2 * SSD_GROUP_HEADS)))
    u = _inproj(x, mods, 0, norm_g[0:1], ssd_w_in[0][:, :n_main].astype(BF16), name="ssd_in", **tiles_in)
    dt = _inproj(x, mods, 0, norm_g[0:1], w_dt.reshape(d, groups * LANES).astype(BF16), name="ssd_dt_in", out_dtype=F32,
                 **tiles_in)
    ssd_args = (ssd_conv_w[0], ssd_conv_b[0:1], _group_lanes(ssd_dt_bias[0], groups),
                _group_lanes(ssd_a_log[0], groups), jnp.repeat(ssd_d[0], SSD_HEAD_DIM)[None])
    y, new_ssd = _ssd_call(u, dt, *ssd_args, None, None, n_seq=n_ctx, seq_len=l_ctx, row_block0=0,
                           want_final=True, name="ssd_ctx")
    y, = _ssd_call(u, dt, *ssd_args, y, state_ssd[:, 0], n_seq=n_lat, seq_len=l_lat, row_block0=t_ctx // l_lat,
                   want_final=False, name="ssd_lat")
    x = _outproj(y, u, 0, x, mods, 0, w_out_b[0], norm_g=ssd_norm_g[0:1], name="ssd_out", **tiles)

    u = _inproj(x, mods, 1, norm_g[1:2], mlp_w_in[0].astype(BF16), name="mlp_in", **tiles_in)
    b_lanes = jnp.repeat(mlp_b_s[0].T, e // mlp_b_s.shape[1], axis=1)
    y = _gmlp(u, mlp_ln_g[0:1], mlp_ln_b[0:1], mlp_w_s[0], b_lanes)
    x = _outproj(y, u, 2, x, mods, 1, w_out_b[1], name="mlp_out", **tiles)

    u = _inproj(x, mods, 2, norm_g[2:3], s5_w_in[0].astype(BF16), name="s5_in", **tiles_in)
    s5_groups = e // S5_GROUP
    bg = LANES // S5_GROUP
    blocks = s5_groups // bg

    def block_vec(v):
        return v.reshape(2, blocks, 1, S5_BLOCK_ST).transpose(1, 0, 2, 3)

    def block_mat_b(v):
        v = v.reshape(2, blocks, bg, S5_STATE, S5_GROUP).transpose(1, 0, 4, 2, 3)
        return v.reshape(blocks, 2, S5_GROUP, S5_BLOCK_ST)

    def block_mat_c(v):
        v = v.reshape(2, blocks, bg, S5_GROUP, S5_STATE).transpose(1, 0, 3, 2, 4)
        return v.reshape(blocks, 2, S5_GROUP, S5_BLOCK_ST)

    log_step = jnp.repeat(s5_log_step[0][:, :, None], S5_STATE, axis=2)
    prm = (block_vec(s5_lam_re[0]), block_vec(s5_lam_im[0]), block_vec(log_step), block_mat_b(s5_b_re[0]),
           block_mat_b(s5_b_im[0]), block_mat_c(s5_c_re[0]), block_mat_c(s5_c_im[0]))
    h0 = state_s5[:, 0].reshape(n_lat, 2, 2, blocks, S5_BLOCK_ST).transpose(3, 1, 2, 0, 4)
    ys, fin = _s5_core(prm, h0, u, n_ctx=n_ctx, l_ctx=l_ctx, n_lat=n_lat, l_lat=l_lat, e=e)
    new_s5 = fin.transpose(3, 1, 2, 0, 4).reshape(n_ctx, 1, 2, 2, s5_groups, S5_STATE)
    y = _s5_glu(ys, u, s5_d[0:1], s5_w_glu[0].astype(BF16), s5_b_glu[0:1])
    x = _outproj(y, u, 1, x, mods, 2, w_out_b[2], name="s5_out", **tiles)

    u = _inproj(x, mods, 3, norm_g[3:4], nat_w_in[0].astype(BF16), name="nat_in", **tiles_in)
    y, new_k, new_v = _ctx_attention(u, None, n_seq=n_ctx, seq_len=l_ctx, e=e)
    y = _nat_attention(u, cache_k[:, 0], cache_v[:, 0], _nat_bias(nat_rpb[0]), y, n_seq=n_lat, seq_len=l_lat,
                       row_block0=t_ctx // l_lat, e=e)
    out = _outproj(y, u, 3, x, mods, 3, w_out_b[3], final_g=final_g[None], name="nat_out", **tiles)

    return (out[:t_ctx].reshape(n_ctx, l_ctx, d), out[t_ctx:].reshape(n_lat, l_lat, d),
            new_ssd[:, None], new_s5, new_k[:, None], new_v[:, None])
```

```python
import functools

import jax
import jax.numpy as jnp
from jax import lax
from jax.experimental import pallas as pl
from jax.experimental.pallas import tpu as pltpu

F32 = jnp.float32
BF16 = jnp.bfloat16
ACT = jnp.bfloat16
EPS = 1e-6
HIGHEST = lax.Precision.HIGHEST

LANES = 128
CHUNK = 128
SSD_HEAD_DIM = 64
SSD_GROUP_HEADS = 4
SSD_GROUP_CH = SSD_HEAD_DIM * SSD_GROUP_HEADS
SSD_STATE = 128
CONV_W = 5
CONV_HALO = 16
SSD_PREP_CHUNKS = 2
CONV_WIN = 2 * CHUNK
S5_T = 8
S5_GROUP = 16
S5_STATE = 64
S5_BLOCK_ST = (LANES // S5_GROUP) * S5_STATE
HEAD_DIM = 64
GRID_W = 64
WIN_ROWS = 8
WIN_COLS = 16
NAT_ROWS_PER_STEP = 8
MASKED = -1e30
VMEM_LIMIT = 52 * 1024 * 1024


def _silu(x):
    return x * jax.nn.sigmoid(x)


def _gelu(x):
    return 0.5 * x * (1.0 + jnp.tanh(0.7978845608028654 * (x + 0.044715 * (x * x * x))))


def _softplus(x):
    return jnp.maximum(x, 0.0) + jnp.log1p(jnp.exp(-jnp.abs(x)))


def _dot(a, b):
    return jnp.dot(a, b, preferred_element_type=F32)


def _dot_nt(a, b):
    return lax.dot_general(a, b, (((1,), (1,)), ((), ())), preferred_element_type=F32)


def _cumsum_rows(lower_tri, x):
    hi = x.astype(BF16)
    r1 = x - hi.astype(F32)
    mid = r1.astype(BF16)
    lo = (r1 - mid.astype(F32)).astype(BF16)
    return _dot(lower_tri, hi) + _dot(lower_tri, mid) + _dot(lower_tri, lo)


def _params(*sem):
    return pltpu.CompilerParams(dimension_semantics=sem, vmem_limit_bytes=VMEM_LIMIT)


def _mod_row(i, *, tm, t_ctx, l_lat):
    start = i * tm
    return jnp.where(start < t_ctx, 0, 1 + (start - t_ctx) // l_lat)


def _mod_body(c_ref, w_ref, b_ref, o_ref):
    cond = _silu(c_ref[...]).astype(BF16)
    o_ref[...] = _dot(cond, w_ref[...].astype(BF16)) + b_ref[...]


def _modulation(cond8, w_mod, b_mod, *, tn=1024):
    depth, d, n = w_mod.shape
    return pl.pallas_call(
        _mod_body,
        grid=(depth, n // tn),
        in_specs=[pl.BlockSpec((8, d), lambda l, j: (0, 0)),
                  pl.BlockSpec((None, d, tn), lambda l, j: (l, 0, j)),
                  pl.BlockSpec((None, 1, tn), lambda l, j: (l, 0, j))],
        out_specs=pl.BlockSpec((None, 8, tn), lambda l, j: (l, 0, j)),
        out_shape=jax.ShapeDtypeStruct((depth, 8, n), F32),
        compiler_params=_params("parallel", "parallel"),
        name="modulation",
    )(cond8, w_mod, b_mod.reshape(depth, 1, n))


def _inproj_body(x_ref, shift_ref, scale_ref, g_ref, w_ref, o_ref, h_ref):
    @pl.when(pl.program_id(1) == 0)
    def _():
        x = x_ref[...]
        y = x * lax.rsqrt(jnp.mean(x * x, axis=-1, keepdims=True) + EPS) * g_ref[...]
        h_ref[...] = (y * (1.0 + scale_ref[...]) + shift_ref[...]).astype(BF16)

    o_ref[...] = _dot(h_ref[...], w_ref[...]).astype(o_ref.dtype)


def _inproj(x, mods, layer, g, w, *, tm, tn, t_ctx, l_lat, name, out_dtype=ACT):
    t, d = x.shape
    n = w.shape[1]
    row = functools.partial(_mod_row, tm=tm, t_ctx=t_ctx, l_lat=l_lat)

    def mod_spec(part):
        return pl.BlockSpec((None, None, None, 1, d), lambda i, j: (layer, row(i), part, 0, 0))

    return pl.pallas_call(
        _inproj_body,
        grid=(t // tm, n // tn),
        in_specs=[pl.BlockSpec((tm, d), lambda i, j: (i, 0)), mod_spec(0), mod_spec(1),
                  pl.BlockSpec((1, d), lambda i, j: (0, 0)),
                  pl.BlockSpec((d, tn), lambda i, j: (0, j))],
        out_specs=pl.BlockSpec((tm, tn), lambda i, j: (i, j)),
        out_shape=jax.ShapeDtypeStruct((t, n), out_dtype),
        scratch_shapes=[pltpu.VMEM((tm, d), BF16)],
        compiler_params=_params("parallel", "arbitrary"),
        name=name,
    )(x, mods, mods, g, w)


def _outproj_body(*refs, gated_norm, final_norm):
    y_ref, z_ref, x_ref, gate_ref, w_ref = refs[:5]
    rest = list(refs[5:])
    t = y_ref[...].astype(F32) * _silu(z_ref[...].astype(F32))
    if gated_norm:
        ng_ref = rest.pop(0)
        t = t * lax.rsqrt(jnp.mean(t * t, axis=-1, keepdims=True) + EPS) * ng_ref[...]
    if final_norm:
        fg_ref = rest.pop(0)
    o_ref = rest.pop(0)
    xn = x_ref[...] + gate_ref[...] * _dot(t.astype(BF16), w_ref[...])
    if final_norm:
        xn = xn * lax.rsqrt(jnp.mean(xn * xn, axis=-1, keepdims=True) + EPS) * fg_ref[...]
    o_ref[...] = xn


def _outproj(y, u, z_block, x, mods, layer, w, *, norm_g=None, final_g=None, tm, t_ctx, l_lat, name):
    t, e = y.shape
    d = x.shape[1]
    row = functools.partial(_mod_row, tm=tm, t_ctx=t_ctx, l_lat=l_lat)
    in_specs = [pl.BlockSpec((tm, e), lambda i: (i, 0)),
                pl.BlockSpec((tm, e), lambda i: (i, z_block)),
                pl.BlockSpec((tm, d), lambda i: (i, 0)),
                pl.BlockSpec((None, None, None, 1, d), lambda i: (layer, row(i), 2, 0, 0)),
                pl.BlockSpec((e, d), lambda i: (0, 0))]
    args = [y, u, x, mods, w]
    if norm_g is not None:
        in_specs.append(pl.BlockSpec((1, e), lambda i: (0, 0)))
        args.append(norm_g)
    if final_g is not None:
        in_specs.append(pl.BlockSpec((1, d), lambda i: (0, 0)))
        args.append(final_g)
    return pl.pallas_call(
        functools.partial(_outproj_body, gated_norm=norm_g is not None, final_norm=final_g is not None),
        grid=(t // tm,),
        in_specs=in_specs,
        out_specs=pl.BlockSpec((tm, d), lambda i: (i, 0)),
        out_shape=jax.ShapeDtypeStruct((t, d), F32),
        compiler_params=_params("parallel"),
        name=name,
    )(*args)


def _ssd_body(*refs, seq_len, seqs_per_step, has_h0, want_final):
    refs = list(refs)
    x_ref, b_ref, c_ref, dt_ref, wx_ref, wb_ref, wc_ref, bx_ref, bb_ref, bc_ref, dtb_ref, alog_ref, dsk_ref = refs[:13]
    rest = refs[13:]
    h0_ref = rest.pop(0) if has_h0 else None
    if want_final:
        y_ref, hf_ref = rest[0], rest[1]
        rest = rest[2:]
    else:
        y_ref, hf_ref = rest[0], None
        rest = rest[1:]
    pad_s, shift_s, xb_s, xt_s, bm_s, bt_s, cm_s, e_s, et_s, ct_s, dtt_s, y_s, sf_s, sb_s = rest
    nc = seq_len // CHUNK
    width = SSD_GROUP_CH + 2 * SSD_STATE
    nh = SSD_GROUP_HEADS

    pad_s[0:CONV_HALO, :] = jnp.zeros((CONV_HALO, width), ACT)
    tail = CONV_WIN - CHUNK - CONV_HALO
    pad_s[CONV_HALO + seq_len:CONV_HALO + seq_len + tail, :] = jnp.zeros((tail, width), ACT)
    win_row = lax.broadcasted_iota(jnp.int32, (CHUNK, CONV_WIN), 1)
    tok_row = lax.broadcasted_iota(jnp.int32, (CHUNK, CONV_WIN), 0)
    taps = [j for j in range(CONV_W) if j != CONV_W // 2]
    for n, j in enumerate(taps):
        shift_s[n] = (win_row == tok_row + (CONV_HALO + j - CONV_W // 2)).astype(BF16)

    def copy_chunk(off, ci, carry):
        base = pl.multiple_of(ci * CHUNK, CHUNK)
        src = pl.ds(off + base, CHUNK)
        dst = pl.ds(base + CONV_HALO, CHUNK)
        pad_s[dst, 0:SSD_GROUP_CH] = x_ref[src, :]
        pad_s[dst, SSD_GROUP_CH:SSD_GROUP_CH + SSD_STATE] = b_ref[src, :]
        pad_s[dst, SSD_GROUP_CH + SSD_STATE:width] = c_ref[src, :]
        return carry

    conv_w = jnp.concatenate([wx_ref[...], wb_ref[...], wc_ref[...]], axis=1)
    conv_b = jnp.concatenate([bx_ref[...], bb_ref[...], bc_ref[...]], axis=1)
    a_row = -jnp.exp(alog_ref[...])
    row_i = lax.broadcasted_iota(jnp.int32, (CHUNK, CHUNK), 0)
    col_i = lax.broadcasted_iota(jnp.int32, (CHUNK, CHUNK), 1)
    lower = row_i >= col_i
    upper = row_i <= col_i
    lower_b = lower.astype(BF16)
    fwd_lane = lax.broadcasted_iota(jnp.int32, (1, LANES), 1) < nh

    def prep_chunks(off, pi, carry):
        cis = [pi * SSD_PREP_CHUNKS + n for n in range(SSD_PREP_CHUNKS)]
        rows = [pl.ds(pl.multiple_of(ci * CHUNK, CHUNK), CHUNK) for ci in cis]
        mid = CONV_W // 2
        half = width // 2
        units = [(n, h) for n in range(SSD_PREP_CHUNKS) for h in range(2)]

        def window(unit):
            n, h = unit
            return pad_s[pl.ds(pl.multiple_of(cis[n] * CHUNK, CHUNK), CONV_WIN), h * half:(h + 1) * half]

        def shift_dots(unit):
            win = window(unit)
            return [_dot(shift_s[t], win) for t in range(len(taps))]

        def finish(unit, shifted):
            n, h = unit
            ci, r = cis[n], rows[n]
            cols = slice(h * half, (h + 1) * half)
            acc = conv_b[:, cols] + window(unit)[CONV_HALO:CONV_HALO + CHUNK].astype(F32) * conv_w[mid:mid + 1, cols]
            for t, j in enumerate(taps):
                acc = acc + shifted[t] * conv_w[j:j + 1, cols]
            v = _silu(acc)
            if h == 0:
                y_s[r, :] = v * dsk_ref[...]
                xb_s[r, :] = v.astype(BF16)
                xt_s[ci] = v.T
            else:
                bm = v[:, 0:SSD_STATE]
                bm_s[r, :] = bm.astype(BF16)
                bt_s[ci] = bm.T.astype(BF16)
                cm_s[r, :] = v[:, SSD_STATE:half].astype(BF16)

        dts = [_softplus(dt_ref[pl.ds(off + pl.multiple_of(ci * CHUNK, CHUNK), CHUNK), :] + dtb_ref[...]) for ci in cis]
        das = [dt * a_row for dt in dts]
        pending = shift_dots(units[0])
        cums = []
        for i, unit in enumerate(units):
            following = shift_dots(units[i + 1]) if i + 1 < len(units) else None
            if unit[1] == 0:
                cums.append(_cumsum_rows(lower_b, das[unit[0]]))
            finish(unit, pending)
            pending = following
        for ci, r, dt, da, cum in zip(cis, rows, dts, das, cums):
            e = jnp.where(fwd_lane, cum, cum - da)
            e_s[r, :] = e
            et_s[ci] = e.T[0:2 * nh]
            ct_s[ci] = cum.T[0:2 * nh]
            dtt_s[ci] = dt.T[0:2 * nh]
        return carry

    head_of_lane = lax.broadcasted_iota(jnp.int32, (CHUNK, SSD_GROUP_CH), 1) // SSD_HEAD_DIM
    st_refs = (sf_s, sb_s)

    def scan_chunk(i, carry):
        chunks = (i, nc - 1 - i)
        rows = [pl.ds(pl.multiple_of(ci * CHUNK, CHUNK), CHUNK) for ci in chunks]
        s_prev = [st_refs[d][...] for d in range(2)]
        g, cs = [], []
        for d in range(2):
            cm = cm_s[rows[d], :]
            g.append(_dot(cm, bt_s[chunks[d]]))
            cs.append(_dot_nt(cm, s_prev[d].astype(BF16)))
        w_rows, keeps, ys = [], [], []
        for d in range(2):
            e = e_s[rows[d], :]
            e_t, c_t, dt_t = et_s[chunks[d]], ct_s[chunks[d]], dtt_s[chunks[d]]
            xb = xb_s[rows[d], :]
            y = jnp.zeros((CHUNK, SSD_GROUP_CH), F32)
            w_d, keep_d = [], []
            for r in range(nh):
                col = r + nh * d
                ec = jnp.broadcast_to(e[:, col:col + 1], (CHUNK, CHUNK))
                er = e_t[col:col + 1, :]
                dt_r = dt_t[col:col + 1, :]
                tot = c_t[col:col + 1, CHUNK - 1:CHUNK]
                if d == 0:
                    decay = jnp.exp(jnp.where(lower, ec - er, -jnp.inf))
                    off = jnp.exp(ec)
                    w_d.append(jnp.broadcast_to(dt_r * jnp.exp(tot - er), (SSD_HEAD_DIM, CHUNK)))
                else:
                    decay = jnp.exp(jnp.where(upper, er - ec, -jnp.inf))
                    off = jnp.exp(tot - ec)
                    w_d.append(jnp.broadcast_to(dt_r * jnp.exp(er), (SSD_HEAD_DIM, CHUNK)))
                keep_d.append(jnp.broadcast_to(jnp.exp(tot), (SSD_HEAD_DIM, SSD_STATE)))
                yd = _dot((g[d] * decay * dt_r).astype(BF16), xb)
                off2 = jnp.concatenate([off] * (SSD_GROUP_CH // CHUNK), axis=1)
                y = jnp.where(head_of_lane == r, yd + cs[d] * off2, y)
            ys.append(y)
            w_rows.append(jnp.concatenate(w_d, axis=0))
            keeps.append(jnp.concatenate(keep_d, axis=0))
        for d in range(2):
            w_t = (xt_s[chunks[d]] * w_rows[d]).astype(BF16)
            st_refs[d][...] = keeps[d] * s_prev[d] + _dot(w_t, bm_s[rows[d], :])
            y_s[rows[d], :] += ys[d]
        return carry

    def emit_chunk(off, ci, carry):
        base = pl.multiple_of(ci * CHUNK, CHUNK)
        y_ref[pl.ds(off + base, CHUNK), :] = y_s[pl.ds(base, CHUNK), :].astype(y_ref.dtype)
        return carry

    def one_sequence(s, carry):
        off = pl.multiple_of(s * seq_len, CHUNK)
        lax.fori_loop(0, nc, functools.partial(copy_chunk, off), 0)
        lax.fori_loop(0, nc // SSD_PREP_CHUNKS, functools.partial(prep_chunks, off), 0)
        if has_h0:
            sf_s[...] = h0_ref[s, 0].reshape(SSD_GROUP_CH, SSD_STATE)
            sb_s[...] = h0_ref[s, 1].reshape(SSD_GROUP_CH, SSD_STATE)
        else:
            sf_s[...] = jnp.zeros((SSD_GROUP_CH, SSD_STATE), F32)
            sb_s[...] = jnp.zeros((SSD_GROUP_CH, SSD_STATE), F32)
        lax.fori_loop(0, nc, scan_chunk, 0)
        lax.fori_loop(0, nc, functools.partial(emit_chunk, off), 0)
        if want_final:
            hf_ref[s, 0] = sf_s[...].reshape(SSD_GROUP_HEADS, SSD_HEAD_DIM, SSD_STATE)
            hf_ref[s, 1] = sb_s[...].reshape(SSD_GROUP_HEADS, SSD_HEAD_DIM, SSD_STATE)
        return carry

    lax.fori_loop(0, seqs_per_step, one_sequence, 0)


def _ssd_call(u, dt, conv_w, conv_b, dtb, alog, dsk, y_prev, h0, *, n_seq, seq_len, seqs_per_step, row_block0,
              want_final, name):
    t, n_u = u.shape
    e = dsk.shape[1]
    groups = e // SSD_GROUP_CH
    xb0 = e // SSD_GROUP_CH
    bb0 = 2 * e // SSD_STATE
    cb0 = bb0 + groups
    has_h0 = h0 is not None
    seq = lambda b: row_block0 + b
    rows = seqs_per_step * seq_len
    in_specs = [pl.BlockSpec((rows, SSD_GROUP_CH), lambda b, g: (seq(b), xb0 + g)),
                pl.BlockSpec((rows, SSD_STATE), lambda b, g: (seq(b), bb0 + g)),
                pl.BlockSpec((rows, SSD_STATE), lambda b, g: (seq(b), cb0 + g)),
                pl.BlockSpec((rows, LANES), lambda b, g: (seq(b), g)),
                pl.BlockSpec((CONV_W, SSD_GROUP_CH), lambda b, g: (0, g)),
                pl.BlockSpec((CONV_W, SSD_STATE), lambda b, g: (0, bb0 - xb0 * 2 + g)),
                pl.BlockSpec((CONV_W, SSD_STATE), lambda b, g: (0, cb0 - xb0 * 2 + g)),
                pl.BlockSpec((1, SSD_GROUP_CH), lambda b, g: (0, g)),
                pl.BlockSpec((1, SSD_STATE), lambda b, g: (0, bb0 - xb0 * 2 + g)),
                pl.BlockSpec((1, SSD_STATE), lambda b, g: (0, cb0 - xb0 * 2 + g)),
                pl.BlockSpec((1, LANES), lambda b, g: (0, g)),
                pl.BlockSpec((1, LANES), lambda b, g: (0, g)),
                pl.BlockSpec((1, SSD_GROUP_CH), lambda b, g: (0, g))]
    args = [u, u, u, dt, conv_w, conv_w, conv_w, conv_b, conv_b, conv_b, dtb, alog, dsk]
    state_block = (seqs_per_step, 2, SSD_GROUP_HEADS, SSD_HEAD_DIM, SSD_STATE)
    if has_h0:
        in_specs.append(pl.BlockSpec(state_block, lambda b, g: (b, 0, g, 0, 0)))
        args.append(h0)
    aliases = {}
    if y_prev is not None:
        in_specs.append(pl.BlockSpec(memory_space=pl.ANY))
        args.append(y_prev)
        aliases = {len(args) - 1: 0}
    out_specs = [pl.BlockSpec((rows, SSD_GROUP_CH), lambda b, g: (seq(b), g))]
    out_shape = [jax.ShapeDtypeStruct((t, e), ACT)]
    if want_final:
        out_specs.append(pl.BlockSpec(state_block, lambda b, g: (b, 0, g, 0, 0)))
        out_shape.append(jax.ShapeDtypeStruct((n_seq, 2, e // SSD_HEAD_DIM, SSD_HEAD_DIM, SSD_STATE), F32))
    width = SSD_GROUP_CH + 2 * SSD_STATE
    nc = seq_len // CHUNK

    def body(*refs):
        refs = list(refs)
        if y_prev is not None:
            n_in = len(args)
            del refs[n_in - 1]
        _ssd_body(*refs, seq_len=seq_len, seqs_per_step=seqs_per_step, has_h0=has_h0, want_final=want_final)

    return pl.pallas_call(
        body,
        grid=(n_seq // seqs_per_step, groups),
        in_specs=in_specs,
        out_specs=out_specs,
        out_shape=out_shape,
        input_output_aliases=aliases,
        scratch_shapes=[pltpu.VMEM((seq_len + CONV_WIN - CHUNK, width), ACT),
                        pltpu.VMEM((CONV_W - 1, CHUNK, CONV_WIN), BF16),
                        pltpu.VMEM((seq_len, SSD_GROUP_CH), BF16),
                        pltpu.VMEM((nc, SSD_GROUP_CH, CHUNK), F32),
                        pltpu.VMEM((seq_len, SSD_STATE), BF16),
                        pltpu.VMEM((nc, SSD_STATE, CHUNK), BF16),
                        pltpu.VMEM((seq_len, SSD_STATE), BF16),
                        pltpu.VMEM((seq_len, LANES), F32),
                        pltpu.VMEM((nc, 2 * SSD_GROUP_HEADS, CHUNK), F32),
                        pltpu.VMEM((nc, 2 * SSD_GROUP_HEADS, CHUNK), F32),
                        pltpu.VMEM((nc, 2 * SSD_GROUP_HEADS, CHUNK), F32),
                        pltpu.VMEM((seq_len, SSD_GROUP_CH), F32),
                        pltpu.VMEM((SSD_GROUP_CH, SSD_STATE), F32),
                        pltpu.VMEM((SSD_GROUP_CH, SSD_STATE), F32)],
        compiler_params=_params("parallel", "parallel"),
        name=name,
    )(*args)


def _group_lanes(v, groups):
    per_group = v.reshape(2, groups, SSD_GROUP_HEADS).transpose(1, 0, 2).reshape(groups, 2 * SSD_GROUP_HEADS)
    return jnp.pad(per_group, ((0, 0), (0, LANES - 2 * SSD_GROUP_HEADS))).reshape(1, groups * LANES)


def _gmlp_body(u_ref, v_ref, lng_ref, lnb_ref, ws_ref, bs_ref, o_ref, *, tm):
    groups = ws_ref.shape[0]
    gch = u_ref.shape[1] // groups
    for c in range(tm // CHUNK):
        rows = slice(c * CHUNK, (c + 1) * CHUNK)
        v = _gelu(v_ref[rows, :].astype(F32))
        vc = v - jnp.mean(v, axis=-1, keepdims=True)
        vn = vc * lax.rsqrt(jnp.mean(vc * vc, axis=-1, keepdims=True) + EPS) * lng_ref[...] + lnb_ref[...]
        vb = vn.astype(BF16)
        for g in range(groups):
            cols = slice(g * gch, (g + 1) * gch)
            s = _dot(ws_ref[g].astype(BF16), vb[:, cols]) + bs_ref[:, cols]
            o_ref[rows, cols] = (_gelu(u_ref[rows, cols].astype(F32)) * s).astype(o_ref.dtype)


def _gmlp(u, ln_g, ln_b, w_s, b_lanes, *, tm=256):
    t = u.shape[0]
    e = ln_g.shape[1]
    return pl.pallas_call(
        functools.partial(_gmlp_body, tm=tm),
        grid=(t // tm,),
        in_specs=[pl.BlockSpec((tm, e), lambda i: (i, 0)),
                  pl.BlockSpec((tm, e), lambda i: (i, 1)),
                  pl.BlockSpec((1, e), lambda i: (0, 0)),
                  pl.BlockSpec((1, e), lambda i: (0, 0)),
                  pl.BlockSpec(w_s.shape, lambda i: (0, 0, 0)),
                  pl.BlockSpec((CHUNK, e), lambda i: (0, 0))],
        out_specs=pl.BlockSpec((tm, e), lambda i: (i, 0)),
        out_shape=jax.ShapeDtypeStruct((t, e), ACT),
        compiler_params=_params("parallel"),
        name="gmlp",
    )(u, u, ln_g, ln_b, w_s, b_lanes)


def _s5_body(lr_ref, li_ref, ls_ref, brt_ref, bit_ref, cre_ref, cim_ref, h0_ref, u_ref, y_ref, fin_ref,
             win_s, t_s, ef_s, eb_s, z_s, a_s, zin_s, d_s, spf_s, spb_s, yo_s, uf_s,
             *, seqs, l_ctx, n_lat, l_lat, ctx_parts):
    bst = S5_BLOCK_ST
    ng = LANES // S5_GROUP
    kw = S5_T * LANES
    part = pl.program_id(1)

    @pl.when(part == 0)
    def _build():
        own = (lax.broadcasted_iota(jnp.int32, (ng, S5_GROUP, bst), 2) // S5_STATE
               == lax.broadcasted_iota(jnp.int32, (ng, S5_GROUP, bst), 0))

        def spread(v):
            return jnp.where(own, v[None], 0.0).reshape(LANES, bst).astype(BF16)

        tau = lax.broadcasted_iota(jnp.int32, (S5_T + 8, 1), 0).astype(F32)
        e_refs = (ef_s, eb_s)
        for d in range(2):
            lr, li = lr_ref[d], li_ref[d]
            step = jnp.exp(ls_ref[d])
            mag = jnp.exp(tau * (lr * step))
            p_re = mag * jnp.cos(tau * (li * step))
            p_im = mag * jnp.sin(tau * (li * step))
            ab_re, ab_im = p_re[1:2], p_im[1:2]
            den = lr * lr + li * li
            nr = ab_re - 1.0
            cr = (nr * lr + ab_im * li) / den
            ci = (ab_im * lr - nr * li) / den
            brt, bit = brt_ref[d], bit_ref[d]
            bb_re = cr * brt - ci * bit
            bb_im = cr * bit + ci * brt
            cre, cim = cre_ref[d], cim_ref[d]
            for k in range(S5_T):
                rows = slice(k * LANES, (k + 1) * LANES)
                tq = S5_T - 1 - k if d == 0 else k
                te = k + 1 if d == 0 else S5_T - k
                pr, pi = p_re[tq:tq + 1], p_im[tq:tq + 1]
                win_s[rows, 2 * d * bst:(2 * d + 1) * bst] = spread(pr * bb_re - pi * bb_im)
                win_s[rows, (2 * d + 1) * bst:(2 * d + 2) * bst] = spread(pr * bb_im + pi * bb_re)
                pr, pi = p_re[te:te + 1], p_im[te:te + 1]
                e_refs[d][rows, 0:bst] = spread(cre * pr - cim * pi)
                e_refs[d][rows, bst:2 * bst] = spread(-(cre * pi + cim * pr))
            c_own = jnp.concatenate([spread(cre), spread(-cim)], axis=1)
            z = _dot_nt(win_s[:, 2 * d * bst:(2 * d + 2) * bst], c_own)
            if d == 0:
                z_s[0:kw, :] = z
            else:
                z_s[kw - LANES:kw, :] += z[0:LANES]
                z_s[kw:2 * kw - LANES, :] = z[LANES:kw]
            a_s[2 * d:2 * d + 1, :] = p_re[S5_T:S5_T + 1]
            a_s[2 * d + 1:2 * d + 2, :] = p_im[S5_T:S5_T + 1]
        for k in range(S5_T):
            r0 = (S5_T - 1 - k) * LANES
            t_s[:, k * LANES:(k + 1) * LANES] = z_s[r0:r0 + kw, :].astype(BF16)

    def outputs():
        yo_s[...] = (_dot(zin_s[...], t_s[...]) + _dot_nt(spf_s[...].astype(BF16), ef_s[...])
                     + _dot_nt(spb_s[...].astype(BF16), eb_s[...]))

    @pl.when(part < ctx_parts)
    def _context():
        nj = l_ctx // S5_T
        uf_s[...] = u_ref[...].astype(F32)
        for j in range(nj):
            for k in range(S5_T):
                zin_s[j * seqs:(j + 1) * seqs, k * LANES:(k + 1) * LANES] = (
                    uf_s[pl.ds(j * S5_T + k, seqs, stride=l_ctx), :].astype(BF16))
        d_s[...] = _dot(zin_s[...], win_s[...])
        coef = [jnp.broadcast_to(a_s[i:i + 1, :], (seqs, bst)) for i in range(4)]
        zero = jnp.zeros((seqs, bst), F32)

        def step(j, carry):
            fr, fi, br, bi = carry
            rf = pl.ds(pl.multiple_of(j * seqs, seqs), seqs)
            rb = pl.ds(pl.multiple_of((nj - 1 - j) * seqs, seqs), seqs)
            spf_s[rf, 0:bst] = fr
            spf_s[rf, bst:2 * bst] = fi
            spb_s[rb, 0:bst] = br
            spb_s[rb, bst:2 * bst] = bi
            return (coef[0] * fr - coef[1] * fi + d_s[rf, 0:bst],
                    coef[0] * fi + coef[1] * fr + d_s[rf, bst:2 * bst],
                    coef[2] * br - coef[3] * bi + d_s[rb, 2 * bst:3 * bst],
                    coef[2] * bi + coef[3] * br + d_s[rb, 3 * bst:4 * bst])

        fr, fi, br, bi = lax.fori_loop(0, nj, step, (zero, zero, zero, zero))
        fin_ref[0, 0] = fr
        fin_ref[0, 1] = fi
        fin_ref[1, 0] = br
        fin_ref[1, 1] = bi
        outputs()
        for j in range(nj):
            for k in range(S5_T):
                y_ref[pl.ds(j * S5_T + k, seqs, stride=l_ctx), :] = (
                    yo_s[j * seqs:(j + 1) * seqs, k * LANES:(k + 1) * LANES])

    @pl.when(part == ctx_parts)
    def _latent():
        nj = l_lat // S5_T
        uf_s[...] = u_ref[...].astype(F32)
        for b in range(n_lat):
            for k in range(S5_T):
                zin_s[b * nj:(b + 1) * nj, k * LANES:(k + 1) * LANES] = (
                    uf_s[pl.ds(b * l_lat + k, nj, stride=S5_T), :].astype(BF16))
        d_s[...] = _dot(zin_s[...], win_s[...])
        coef_re = jnp.concatenate([jnp.broadcast_to(a_s[0:1, :], (n_lat, bst)),
                                   jnp.broadcast_to(a_s[2:3, :], (n_lat, bst))], axis=0)
        coef_im = jnp.concatenate([jnp.broadcast_to(a_s[1:2, :], (n_lat, bst)),
                                   jnp.broadcast_to(a_s[3:4, :], (n_lat, bst))], axis=0)
        s_re0 = jnp.concatenate([h0_ref[0, 0], h0_ref[1, 0]], axis=0)
        s_im0 = jnp.concatenate([h0_ref[0, 1], h0_ref[1, 1]], axis=0)

        def step(j, carry):
            s_re, s_im = carry
            d_re, d_im = [], []
            for b in range(n_lat):
                row = pl.ds(b * nj + j, 1)
                spf_s[row, 0:bst] = s_re[b:b + 1]
                spf_s[row, bst:2 * bst] = s_im[b:b + 1]
                d_re.append(d_s[row, 0:bst])
                d_im.append(d_s[row, bst:2 * bst])
            for b in range(n_lat):
                row = pl.ds(b * nj + nj - 1 - j, 1)
                spb_s[row, 0:bst] = s_re[n_lat + b:n_lat + b + 1]
                spb_s[row, bst:2 * bst] = s_im[n_lat + b:n_lat + b + 1]
                d_re.append(d_s[row, 2 * bst:3 * bst])
                d_im.append(d_s[row, 3 * bst:4 * bst])
            d_re = jnp.concatenate(d_re, axis=0)
            d_im = jnp.concatenate(d_im, axis=0)
            return coef_re * s_re - coef_im * s_im + d_re, coef_re * s_im + coef_im * s_re + d_im

        lax.fori_loop(0, nj, step, (s_re0, s_im0))
        outputs()
        for b in range(n_lat):
            for k in range(S5_T):
                y_ref[pl.ds(b * l_lat + k, nj, stride=S5_T), :] = yo_s[b * nj:(b + 1) * nj, k * LANES:(k + 1) * LANES]


def _s5_core(prm, h0, u, *, n_ctx, l_ctx, n_lat, l_lat, e):
    blocks = prm[0].shape[0]
    part_tokens = n_lat * l_lat
    t = u.shape[0]
    n_parts = t // part_tokens
    seqs = part_tokens // l_ctx
    r = part_tokens // S5_T
    vec = pl.BlockSpec((None, 2, 1, S5_BLOCK_ST), lambda g, p: (g, 0, 0, 0))
    mat = pl.BlockSpec((None, 2, S5_GROUP, S5_BLOCK_ST), lambda g, p: (g, 0, 0, 0))
    k_in = S5_T * LANES
    return pl.pallas_call(
        functools.partial(_s5_body, seqs=seqs, l_ctx=l_ctx, n_lat=n_lat, l_lat=l_lat, ctx_parts=n_parts - 1),
        grid=(blocks, n_parts),
        in_specs=[vec, vec, vec, mat, mat, mat, mat,
                  pl.BlockSpec((None, 2, 2, n_lat, S5_BLOCK_ST), lambda g, p: (g, 0, 0, 0, 0)),
                  pl.BlockSpec((part_tokens, LANES), lambda g, p: (p, g))],
        out_specs=[pl.BlockSpec((part_tokens, LANES), lambda g, p: (p, g)),
                   pl.BlockSpec((None, 2, 2, seqs, S5_BLOCK_ST),
                                lambda g, p: (g, 0, 0, jnp.minimum(p, n_parts - 2), 0))],
        out_shape=[jax.ShapeDtypeStruct((t, e), F32),
                   jax.ShapeDtypeStruct((blocks, 2, 2, n_ctx, S5_BLOCK_ST), F32)],
        scratch_shapes=[pltpu.VMEM((k_in, 4 * S5_BLOCK_ST), BF16),
                        pltpu.VMEM((k_in, k_in), BF16),
                        pltpu.VMEM((k_in, 2 * S5_BLOCK_ST), BF16),
                        pltpu.VMEM((k_in, 2 * S5_BLOCK_ST), BF16),
                        pltpu.VMEM(((2 * S5_T - 1) * LANES, LANES), F32),
                        pltpu.VMEM((8, S5_BLOCK_ST), F32),
                        pltpu.VMEM((r, k_in), BF16),
                        pltpu.VMEM((r, 4 * S5_BLOCK_ST), F32),
                        pltpu.VMEM((r, 2 * S5_BLOCK_ST), F32),
                        pltpu.VMEM((r, 2 * S5_BLOCK_ST), F32),
                        pltpu.VMEM((r, k_in), F32),
                        pltpu.VMEM((part_tokens, LANES), F32)],
        compiler_params=_params("parallel", "arbitrary"),
        name="s5_core",
    )(*prm, h0, u)


def _s5_glu_body(ys_ref, u_ref, dsk_ref, w_ref, b_ref, o_ref):
    y = _gelu(ys_ref[...] + dsk_ref[...] * u_ref[...].astype(F32))
    o_ref[...] = (y * jax.nn.sigmoid(_dot(y.astype(BF16), w_ref[...]) + b_ref[...])).astype(o_ref.dtype)


def _s5_glu(ys, u, dsk, w, b, *, tm=512):
    t, e = ys.shape
    return pl.pallas_call(
        _s5_glu_body,
        grid=(t // tm,),
        in_specs=[pl.BlockSpec((tm, e), lambda i: (i, 0)),
                  pl.BlockSpec((tm, e), lambda i: (i, 0)),
                  pl.BlockSpec((1, e), lambda i: (0, 0)),
                  pl.BlockSpec((e, e), lambda i: (0, 0)),
                  pl.BlockSpec((1, e), lambda i: (0, 0))],
        out_specs=pl.BlockSpec((tm, e), lambda i: (i, 0)),
        out_shape=jax.ShapeDtypeStruct((t, e), ACT),
        compiler_params=_params("parallel"),
        name="s5_glu",
    )(ys, u, dsk, w, b)


def _ctx_attn_body(q_ref, k_ref, v_ref, o_ref, ko_ref, vo_ref):
    scale = HEAD_DIM ** -0.5
    seq_len = q_ref.shape[0]
    first = lax.broadcasted_iota(jnp.int32, (seq_len, LANES), 1) < HEAD_DIM
    masks = (first, jnp.logical_not(first))
    n_pairs = q_ref.shape[1] // LANES
    logits, values = [], []
    for pair in range(n_pairs):
        cols = slice(pair * LANES, (pair + 1) * LANES)
        q2, k2, v2 = q_ref[:, cols].astype(F32) * scale, k_ref[:, cols], v_ref[:, cols]
        kb = k2.astype(BF16)
        values.append(v2.astype(BF16))
        for h in range(LANES // HEAD_DIM):
            ko_ref[2 * pair + h] = k2[:, h * HEAD_DIM:(h + 1) * HEAD_DIM].astype(F32)
            vo_ref[2 * pair + h] = v2[:, h * HEAD_DIM:(h + 1) * HEAD_DIM].astype(F32)
            logits.append(_dot_nt(jnp.where(masks[h], q2, 0.0).astype(BF16), kb))
    probs = []
    for s in logits:
        p = jnp.exp(s - jnp.max(s, axis=-1, keepdims=True))
        probs.append((p * (1.0 / jnp.sum(p, axis=-1, keepdims=True))).astype(BF16))
    for pair in range(n_pairs):
        o0, o1 = _dot(probs[2 * pair], values[pair]), _dot(probs[2 * pair + 1], values[pair])
        o_ref[:, pair * LANES:(pair + 1) * LANES] = jnp.where(first, o0, o1).astype(o_ref.dtype)


def _ctx_attention(u, y_prev, *, n_seq, seq_len, e, width=512):
    t = u.shape[0]
    hp = e // width
    heads = e // HEAD_DIM
    kv_block = (None, width // HEAD_DIM, seq_len, HEAD_DIM)
    in_specs = [pl.BlockSpec((seq_len, width), lambda b, h: (b, h)),
                pl.BlockSpec((seq_len, width), lambda b, h: (b, hp + h)),
                pl.BlockSpec((seq_len, width), lambda b, h: (b, 2 * hp + h))]
    return pl.pallas_call(
        _ctx_attn_body,
        grid=(n_seq, hp),
        in_specs=in_specs,
        out_specs=[pl.BlockSpec((seq_len, width), lambda b, h: (b, h)),
                   pl.BlockSpec(kv_block, lambda b, h: (b, h, 0, 0)),
                   pl.BlockSpec(kv_block, lambda b, h: (b, h, 0, 0))],
        out_shape=[jax.ShapeDtypeStruct((t, e), ACT),
                   jax.ShapeDtypeStruct((n_seq, heads, seq_len, HEAD_DIM), F32),
                   jax.ShapeDtypeStruct((n_seq, heads, seq_len, HEAD_DIM), F32)],
        compiler_params=_params("parallel", "parallel"),
        name="ctx_attention",
    )(u, u, u)


def _nat_bias_body(rpb_ref, o_ref, *, n_rel_rows, n_rel_cols):
    h = pl.program_id(0)
    q = lax.broadcasted_iota(jnp.int32, (GRID_W, GRID_W), 0)
    kc = lax.broadcasted_iota(jnp.int32, (GRID_W, GRID_W), 1)
    rel = kc - q + (WIN_COLS - 1)
    c_start = jnp.clip(q - WIN_COLS // 2, 0, GRID_W - WIN_COLS)
    ok = (kc >= c_start) & (kc < c_start + WIN_COLS)
    for i in range(n_rel_rows):
        acc = jnp.zeros((GRID_W, GRID_W), F32)
        for dcol in range(n_rel_cols):
            acc = jnp.where(rel == dcol, rpb_ref[(h * n_rel_rows + i) * n_rel_cols + dcol], acc)
        tile = jnp.where(ok, acc, MASKED)
        if i < n_rel_rows - 1:
            o_ref[i, :, 0:GRID_W] = tile
        if i > 0:
            o_ref[i - 1, :, GRID_W:2 * GRID_W] = tile


def _nat_bias(rpb):
    heads, n_rel_rows, n_rel_cols = rpb.shape
    return pl.pallas_call(
        functools.partial(_nat_bias_body, n_rel_rows=n_rel_rows, n_rel_cols=n_rel_cols),
        grid=(heads,),
        in_specs=[pl.BlockSpec(memory_space=pltpu.SMEM)],
        out_specs=pl.BlockSpec((None, n_rel_rows - 1, GRID_W, 2 * GRID_W), lambda h: (h, 0, 0, 0)),
        out_shape=jax.ShapeDtypeStruct((heads, n_rel_rows - 1, GRID_W, 2 * GRID_W), F32),
        compiler_params=_params("parallel"),
        name="nat_bias",
    )(rpb.reshape(-1))


def _nat_body(q_ref, k_ref, v_ref, ck_ref, cv_ref, bias_ref, o_ref, ckb_s, cvb_s, *, rows):
    scale = HEAD_DIM ** -0.5
    wr = min(WIN_ROWS, rows)
    nw = wr * GRID_W
    ckb_s[...] = jnp.concatenate([ck_ref[0], ck_ref[1]], axis=1).astype(BF16)
    cvb_s[...] = jnp.concatenate([cv_ref[0], cv_ref[1]], axis=1).astype(BF16)
    first = lax.broadcasted_iota(jnp.int32, (GRID_W, LANES), 1) < HEAD_DIM

    heads = tuple(range(LANES // HEAD_DIM))
    masks = (first, jnp.logical_not(first))

    def row_group(g, carry):
        rws = [g * NAT_ROWS_PER_STEP + i for i in range(NAT_ROWS_PER_STEP)]
        starts = [jnp.clip(r - wr // 2, 0, rows - wr) for r in rws]
        q_rows = [pl.ds(pl.multiple_of(r * GRID_W, GRID_W), GRID_W) for r in rws]
        k_rows = [pl.ds(pl.multiple_of(rs * GRID_W, GRID_W), nw) for rs in starts]
        logits = []
        for r, rs, qr, kr in zip(rws, starts, q_rows, k_rows):
            q2 = q_ref[qr, :].astype(F32) * scale
            kb = k_ref[kr, :].astype(BF16)
            for h in heads:
                q = jnp.where(masks[h], q2, 0.0).astype(BF16)
                logits.append((_dot_nt(q, kb), _dot_nt(q, ckb_s[...])))
        probs = []
        for idx, (s_win, s_ctx) in enumerate(logits):
            r, rs, h = rws[idx // 2], starts[idx // 2], heads[idx % 2]
            i0 = (WIN_ROWS - 1) - (r - rs)
            s_win = s_win + jnp.concatenate([bias_ref[h, i0 + 2 * jj] for jj in range(wr // 2)], axis=1)
            m = jnp.maximum(jnp.max(s_win, axis=-1, keepdims=True), jnp.max(s_ctx, axis=-1, keepdims=True))
            p_win = jnp.exp(s_win - m)
            p_ctx = jnp.exp(s_ctx - m)
            inv = 1.0 / (jnp.sum(p_win, axis=-1, keepdims=True) + jnp.sum(p_ctx, axis=-1, keepdims=True))
            probs.append(((p_win * inv).astype(BF16), (p_ctx * inv).astype(BF16)))
        for i, (qr, kr) in enumerate(zip(q_rows, k_rows)):
            vb = v_ref[kr, :].astype(BF16)
            outs = [_dot(probs[2 * i + h][0], vb) + _dot(probs[2 * i + h][1], cvb_s[...]) for h in heads]
            o_ref[qr, :] = jnp.where(first, outs[0], outs[1]).astype(o_ref.dtype)
        return carry

    lax.fori_loop(0, rows // NAT_ROWS_PER_STEP, row_group, 0)


def _nat_attention(u, cache_k, cache_v, bias, y_prev, *, n_seq, seq_len, row_block0, e):
    t = u.shape[0]
    hp = e // LANES
    hpb = LANES // HEAD_DIM
    past = cache_k.shape[2]
    seq = lambda b: row_block0 + b
    cache_block = (None, hpb, past, HEAD_DIM)
    return pl.pallas_call(
        lambda q, k, v, ck, cv, bs, yp, o, *scratch: _nat_body(q, k, v, ck, cv, bs, o, *scratch,
                                                               rows=seq_len // GRID_W),
        grid=(n_seq, hp),
        in_specs=[pl.BlockSpec((seq_len, LANES), lambda b, h: (seq(b), h)),
                  pl.BlockSpec((seq_len, LANES), lambda b, h: (seq(b), hp + h)),
                  pl.BlockSpec((seq_len, LANES), lambda b, h: (seq(b), 2 * hp + h)),
                  pl.BlockSpec(cache_block, lambda b, h: (b, h, 0, 0)),
                  pl.BlockSpec(cache_block, lambda b, h: (b, h, 0, 0)),
                  pl.BlockSpec((hpb,) + bias.shape[1:], lambda b, h: (h, 0, 0, 0)),
                  pl.BlockSpec(memory_space=pl.ANY)],
        out_specs=pl.BlockSpec((seq_len, LANES), lambda b, h: (seq(b), h)),
        out_shape=jax.ShapeDtypeStruct((t, e), ACT),
        input_output_aliases={6: 0},
        scratch_shapes=[pltpu.VMEM((past, LANES), BF16), pltpu.VMEM((past, LANES), BF16)],
        compiler_params=_params("parallel", "parallel"),
        name="nat_attention",
    )(u, u, u, cache_k, cache_v, bias, y_prev)


def kernel(x_prompt, x_sample, state_ssd, state_s5, cache_k, cache_v, c, c_ctx, norm_g, w_mod, b_mod, w_out, final_g, ssd_w_in, ssd_conv_w, ssd_conv_b, ssd_dt_bias, ssd_a_log, ssd_d, ssd_norm_g, mlp_w_in, mlp_ln_g, mlp_ln_b, mlp_w_s, mlp_b_s, s5_w_in, s5_lam_re, s5_lam_im, s5_log_step, s5_b_re, s5_b_im, s5_c_re, s5_c_im, s5_d, s5_w_glu, s5_b_glu, nat_w_in, nat_rpb):
    n_ctx, l_ctx, d = x_prompt.shape
    n_lat, l_lat, _ = x_sample.shape
    t_ctx = n_ctx * l_ctx
    depth = norm_g.shape[0]
    e = w_out.shape[1]
    assert depth == 4 and l_lat % l_ctx == 0 and t_ctx % (n_lat * l_lat) == 0
    tiles = dict(tm=512, t_ctx=t_ctx, l_lat=l_lat)
    tiles_in = dict(tm=2048, tn=1024, t_ctx=t_ctx, l_lat=l_lat)

    x = jnp.concatenate([x_prompt.reshape(t_ctx, d), x_sample.reshape(n_lat * l_lat, d)], axis=0)
    cond8 = jnp.concatenate([c_ctx[None], c, jnp.zeros((8 - 1 - n_lat, d), F32)], axis=0)
    mods = _modulation(cond8, w_mod, b_mod).reshape(depth, 8, 3, 1, d)
    w_out_b = w_out.astype(BF16)

    groups = e // SSD_GROUP_CH
    n_main = 3 * e
    w_dt = ssd_w_in[0][:, n_main:].reshape(d, 2, groups, SSD_GROUP_HEADS).transpose(0, 2, 1, 3)
    w_dt = jnp.pad(w_dt.reshape(d, groups, 2 * SSD_GROUP_HEADS), ((0, 0), (0, 0), (0, LANES - 2 * SSD_GROUP_HEADS)))
    u = _inproj(x, mods, 0, norm_g[0:1], ssd_w_in[0][:, :n_main].astype(BF16), name="ssd_in", **tiles_in)
    dt = _inproj(x, mods, 0, norm_g[0:1], w_dt.reshape(d, groups * LANES).astype(BF16), name="ssd_dt_in", out_dtype=F32,
                 **tiles_in)
    ssd_args = (ssd_conv_w[0], ssd_conv_b[0:1], _group_lanes(ssd_dt_bias[0], groups),
                _group_lanes(ssd_a_log[0], groups), jnp.repeat(ssd_d[0], SSD_HEAD_DIM)[None])
    y, new_ssd = _ssd_call(u, dt, *ssd_args, None, None, n_seq=n_ctx, seq_len=l_ctx, seqs_per_step=l_lat // l_ctx,
                           row_block0=0, want_final=True, name="ssd_ctx")
    y, = _ssd_call(u, dt, *ssd_args, y, state_ssd[:, 0], n_seq=n_lat, seq_len=l_lat, seqs_per_step=1,
                   row_block0=t_ctx // l_lat, want_final=False, name="ssd_lat")
    x = _outproj(y, u, 0, x, mods, 0, w_out_b[0], norm_g=ssd_norm_g[0:1], name="ssd_out", **tiles)

    u = _inproj(x, mods, 1, norm_g[1:2], mlp_w_in[0].astype(BF16), name="mlp_in", **tiles_in)
    b_lanes = jnp.repeat(mlp_b_s[0].T, e // mlp_b_s.shape[1], axis=1)
    y = _gmlp(u, mlp_ln_g[0:1], mlp_ln_b[0:1], mlp_w_s[0], b_lanes)
    x = _outproj(y, u, 2, x, mods, 1, w_out_b[1], name="mlp_out", **tiles)

    u = _inproj(x, mods, 2, norm_g[2:3], s5_w_in[0].astype(BF16), name="s5_in", **tiles_in)
    s5_groups = e // S5_GROUP
    bg = LANES // S5_GROUP
    blocks = s5_groups // bg

    def block_vec(v):
        return v.reshape(2, blocks, 1, S5_BLOCK_ST).transpose(1, 0, 2, 3)

    def block_mat_b(v):
        v = v.reshape(2, blocks, bg, S5_STATE, S5_GROUP).transpose(1, 0, 4, 2, 3)
        return v.reshape(blocks, 2, S5_GROUP, S5_BLOCK_ST)

    def block_mat_c(v):
        v = v.reshape(2, blocks, bg, S5_GROUP, S5_STATE).transpose(1, 0, 3, 2, 4)
        return v.reshape(blocks, 2, S5_GROUP, S5_BLOCK_ST)

    log_step = jnp.repeat(s5_log_step[0][:, :, None], S5_STATE, axis=2)
    prm = (block_vec(s5_lam_re[0]), block_vec(s5_lam_im[0]), block_vec(log_step), block_mat_b(s5_b_re[0]),
           block_mat_b(s5_b_im[0]), block_mat_c(s5_c_re[0]), block_mat_c(s5_c_im[0]))
    h0 = state_s5[:, 0].reshape(n_lat, 2, 2, blocks, S5_BLOCK_ST).transpose(3, 1, 2, 0, 4)
    ys, fin = _s5_core(prm, h0, u, n_ctx=n_ctx, l_ctx=l_ctx, n_lat=n_lat, l_lat=l_lat, e=e)
    new_s5 = fin.transpose(3, 1, 2, 0, 4).reshape(n_ctx, 1, 2, 2, s5_groups, S5_STATE)
    y = _s5_glu(ys, u, s5_d[0:1], s5_w_glu[0].astype(BF16), s5_b_glu[0:1])
    x = _outproj(y, u, 1, x, mods, 2, w_out_b[2], name="s5_out", **tiles)

    u = _inproj(x, mods, 3, norm_g[3:4], nat_w_in[0].astype(BF16), name="nat_in", **tiles_in)
    y, new_k, new_v = _ctx_attention(u, None, n_seq=n_ctx, seq_len=l_ctx, e=e)
    y = _nat_attention(u, cache_k[:, 0], cache_v[:, 0], _nat_bias(nat_rpb[0]), y, n_seq=n_lat, seq_len=l_lat,
                       row_block0=t_ctx // l_lat, e=e)
    out = _outproj(y, u, 3, x, mods, 3, w_out_b[3], final_g=final_g[None], name="nat_out", **tiles)

    return (out[:t_ctx].reshape(n_ctx, l_ctx, d), out[t_ctx:].reshape(n_lat, l_lat, d),
            new_ssd[:, None], new_s5, new_k[:, None], new_v[:, None])
```

```python
import functools

import jax
import jax.numpy as jnp
from jax import lax
from jax.experimental import pallas as pl
from jax.experimental.pallas import tpu as pltpu

F32 = jnp.float32
BF16 = jnp.bfloat16
ACT = jnp.bfloat16
EPS = 1e-6
HIGHEST = lax.Precision.HIGHEST

LANES = 128
CHUNK = 128
SSD_HEAD_DIM = 64
SSD_GROUP_HEADS = 4
SSD_GROUP_CH = SSD_HEAD_DIM * SSD_GROUP_HEADS
SSD_STATE = 128
CONV_W = 5
CONV_HALO = 16
SSD_PREP_CHUNKS = 2
CONV_WIN = 2 * CHUNK
S5_T = 8
S5_GROUP = 16
S5_STATE = 64
S5_BLOCK_ST = (LANES // S5_GROUP) * S5_STATE
HEAD_DIM = 64
GRID_W = 64
WIN_ROWS = 8
WIN_COLS = 16
NAT_ROWS_PER_STEP = 8
MASKED = -1e30
VMEM_LIMIT = 52 * 1024 * 1024


def _silu(x):
    return x * jax.nn.sigmoid(x)


def _gelu(x):
    return 0.5 * x * (1.0 + jnp.tanh(0.7978845608028654 * (x + 0.044715 * (x * x * x))))


def _softplus(x):
    return jnp.maximum(x, 0.0) + jnp.log1p(jnp.exp(-jnp.abs(x)))


def _dot(a, b):
    return jnp.dot(a, b, preferred_element_type=F32)


def _dot_nt(a, b):
    return lax.dot_general(a, b, (((1,), (1,)), ((), ())), preferred_element_type=F32)


def _cumsum_rows(lower_tri, x):
    hi = x.astype(BF16)
    r1 = x - hi.astype(F32)
    mid = r1.astype(BF16)
    lo = (r1 - mid.astype(F32)).astype(BF16)
    return _dot(lower_tri, hi) + _dot(lower_tri, mid) + _dot(lower_tri, lo)


def _params(*sem):
    return pltpu.CompilerParams(dimension_semantics=sem, vmem_limit_bytes=VMEM_LIMIT)


def _mod_row(i, *, tm, t_ctx, l_lat):
    start = i * tm
    return jnp.where(start < t_ctx, 0, 1 + (start - t_ctx) // l_lat)


def _mod_body(c_ref, w_ref, b_ref, o_ref):
    cond = _silu(c_ref[...]).astype(BF16)
    o_ref[...] = _dot(cond, w_ref[...].astype(BF16)) + b_ref[...]


def _modulation(cond8, w_mod, b_mod, *, tn=1024):
    depth, d, n = w_mod.shape
    return pl.pallas_call(
        _mod_body,
        grid=(depth, n // tn),
        in_specs=[pl.BlockSpec((8, d), lambda l, j: (0, 0)),
                  pl.BlockSpec((None, d, tn), lambda l, j: (l, 0, j)),
                  pl.BlockSpec((None, 1, tn), lambda l, j: (l, 0, j))],
        out_specs=pl.BlockSpec((None, 8, tn), lambda l, j: (l, 0, j)),
        out_shape=jax.ShapeDtypeStruct((depth, 8, n), F32),
        compiler_params=_params("parallel", "parallel"),
        name="modulation",
    )(cond8, w_mod, b_mod.reshape(depth, 1, n))


def _stream_specs(x, tm, width, n_grid):
    ids = (lambda i, *_: i) if n_grid == 1 else (lambda i, j: i)
    if not isinstance(x, tuple):
        return [x], [pl.BlockSpec((tm, width), lambda *g: (ids(*g), 0))], None
    split = x[0].shape[0] // tm
    last = x[1].shape[0] // tm - 1
    return (list(x),
            [pl.BlockSpec((tm, width), lambda *g: (jnp.minimum(ids(*g), split - 1), 0)),
             pl.BlockSpec((tm, width), lambda *g: (jnp.clip(ids(*g) - split, 0, last), 0))],
            split)


def _inproj_body(*refs, split):
    n_x = 1 if split is None else 2
    x_refs = refs[:n_x]
    shift_ref, scale_ref, g_ref, w_ref, o_ref, h_ref = refs[n_x:]

    def normalise(x_ref):
        x = x_ref[...]
        y = x * lax.rsqrt(jnp.mean(x * x, axis=-1, keepdims=True) + EPS) * g_ref[...]
        h_ref[...] = (y * (1.0 + scale_ref[...]) + shift_ref[...]).astype(BF16)

    first_col = pl.program_id(1) == 0
    if split is None:
        pl.when(first_col)(lambda: normalise(x_refs[0]))
    else:
        in_ctx = pl.program_id(0) < split
        pl.when(first_col & in_ctx)(lambda: normalise(x_refs[0]))
        pl.when(first_col & jnp.logical_not(in_ctx))(lambda: normalise(x_refs[1]))
    o_ref[...] = _dot(h_ref[...], w_ref[...]).astype(o_ref.dtype)


def _inproj(x, mods, layer, g, w, *, tm, tn, t_ctx, l_lat, name, out_dtype=ACT, n_out=None):
    d = w.shape[0]
    n = w.shape[1] if n_out is None else n_out
    row = functools.partial(_mod_row, tm=tm, t_ctx=t_ctx, l_lat=l_lat)
    x_args, x_specs, split = _stream_specs(x, tm, d, 2)
    t = sum(a.shape[0] for a in x_args)

    def mod_spec(part):
        return pl.BlockSpec((None, None, None, 1, d), lambda i, j: (layer, row(i), part, 0, 0))

    return pl.pallas_call(
        functools.partial(_inproj_body, split=split),
        grid=(t // tm, n // tn),
        in_specs=x_specs + [mod_spec(0), mod_spec(1),
                            pl.BlockSpec((1, d), lambda i, j: (0, 0)),
                            pl.BlockSpec((d, tn), lambda i, j: (0, j))],
        out_specs=pl.BlockSpec((tm, tn), lambda i, j: (i, j)),
        out_shape=jax.ShapeDtypeStruct((t, n), out_dtype),
        scratch_shapes=[pltpu.VMEM((tm, d), BF16)],
        compiler_params=_params("parallel", "arbitrary"),
        name=name,
    )(*x_args, mods, mods, g, w)


def _outproj_body(*refs, split_in, split_out, gated_norm, final_norm):
    refs = list(refs)
    y_ref, z_ref = refs[:2]
    x_refs = refs[2:3] if split_in is None else refs[2:4]
    rest = refs[2 + len(x_refs):]
    gate_ref, w_ref = rest[:2]
    rest = rest[2:]
    ng_ref = rest.pop(0) if gated_norm else None
    fg_ref = rest.pop(0) if final_norm else None
    o_refs = rest
    t = y_ref[...].astype(F32) * _silu(z_ref[...].astype(F32))
    if gated_norm:
        t = t * lax.rsqrt(jnp.mean(t * t, axis=-1, keepdims=True) + EPS) * ng_ref[...]
    branch = gate_ref[...] * _dot(t.astype(BF16), w_ref[...])

    def emit(x_ref, o_ref):
        xn = x_ref[...] + branch
        if final_norm:
            xn = xn * lax.rsqrt(jnp.mean(xn * xn, axis=-1, keepdims=True) + EPS) * fg_ref[...]
        o_ref[...] = xn

    split = split_in if split_in is not None else split_out
    if split is None:
        emit(x_refs[0], o_refs[0])
    else:
        in_ctx = pl.program_id(0) < split
        pl.when(in_ctx)(lambda: emit(x_refs[0], o_refs[0]))
        pl.when(jnp.logical_not(in_ctx))(lambda: emit(x_refs[-1], o_refs[-1]))


def _outproj(y, u, z_block, x, mods, layer, w, *, norm_g=None, final_g=None, split_rows=None, tm, t_ctx, l_lat,
             name):
    t, e = y.shape
    d = w.shape[1]
    row = functools.partial(_mod_row, tm=tm, t_ctx=t_ctx, l_lat=l_lat)
    x_args, x_specs, split_in = _stream_specs(x, tm, d, 1)
    in_specs = ([pl.BlockSpec((tm, e), lambda i: (i, 0)), pl.BlockSpec((tm, e), lambda i: (i, z_block))] + x_specs
                + [pl.BlockSpec((None, None, None, 1, d), lambda i: (layer, row(i), 2, 0, 0)),
                   pl.BlockSpec((e, d), lambda i: (0, 0))])
    args = [y, u] + x_args + [mods, w]
    if norm_g is not None:
        in_specs.append(pl.BlockSpec((1, e), lambda i: (0, 0)))
        args.append(norm_g)
    if final_g is not None:
        in_specs.append(pl.BlockSpec((1, d), lambda i: (0, 0)))
        args.append(final_g)
    if split_rows is None:
        split_out = None
        out_specs = pl.BlockSpec((tm, d), lambda i: (i, 0))
        out_shape = jax.ShapeDtypeStruct((t, d), F32)
    else:
        split_out = split_rows[0] // tm
        last = split_rows[1] // tm - 1
        out_specs = [pl.BlockSpec((tm, d), lambda i: (jnp.minimum(i, split_out - 1), 0)),
                     pl.BlockSpec((tm, d), lambda i: (jnp.clip(i - split_out, 0, last), 0))]
        out_shape = [jax.ShapeDtypeStruct((rows, d), F32) for rows in split_rows]
    return pl.pallas_call(
        functools.partial(_outproj_body, split_in=split_in, split_out=split_out, gated_norm=norm_g is not None,
                          final_norm=final_g is not None),
        grid=(t // tm,),
        in_specs=in_specs,
        out_specs=out_specs,
        out_shape=out_shape,
        compiler_params=_params("arbitrary"),
        name=name,
    )(*args)


def _ssd_body(*refs, seq_len, seqs_per_step, has_h0, want_final):
    refs = list(refs)
    x_ref, b_ref, c_ref, dt_ref, wx_ref, wb_ref, wc_ref, bx_ref, bb_ref, bc_ref, dtb_ref, alog_ref, dsk_ref = refs[:13]
    rest = refs[13:]
    h0_ref = rest.pop(0) if has_h0 else None
    if want_final:
        y_ref, hf_ref = rest[0], rest[1]
        rest = rest[2:]
    else:
        y_ref, hf_ref = rest[0], None
        rest = rest[1:]
    pad_s, shift_s, xb_s, xt_s, bm_s, bt_s, cm_s, e_s, et_s, ct_s, dtt_s, y_s, sf_s, sb_s = rest
    nc = seq_len // CHUNK
    width = SSD_GROUP_CH + 2 * SSD_STATE
    nh = SSD_GROUP_HEADS

    pad_s[0:CONV_HALO, :] = jnp.zeros((CONV_HALO, width), ACT)
    tail = CONV_WIN - CHUNK - CONV_HALO
    pad_s[CONV_HALO + seq_len:CONV_HALO + seq_len + tail, :] = jnp.zeros((tail, width), ACT)
    win_row = lax.broadcasted_iota(jnp.int32, (CHUNK, CONV_WIN), 1)
    tok_row = lax.broadcasted_iota(jnp.int32, (CHUNK, CONV_WIN), 0)
    taps = [j for j in range(CONV_W) if j != CONV_W // 2]
    for n, j in enumerate(taps):
        shift_s[n] = (win_row == tok_row + (CONV_HALO + j - CONV_W // 2)).astype(BF16)

    def copy_chunk(off, ci, carry):
        base = pl.multiple_of(ci * CHUNK, CHUNK)
        src = pl.ds(off + base, CHUNK)
        dst = pl.ds(base + CONV_HALO, CHUNK)
        pad_s[dst, 0:SSD_GROUP_CH] = x_ref[src, :]
        pad_s[dst, SSD_GROUP_CH:SSD_GROUP_CH + SSD_STATE] = b_ref[src, :]
        pad_s[dst, SSD_GROUP_CH + SSD_STATE:width] = c_ref[src, :]
        return carry

    conv_w = jnp.concatenate([wx_ref[...], wb_ref[...], wc_ref[...]], axis=1)
    conv_b = jnp.concatenate([bx_ref[...], bb_ref[...], bc_ref[...]], axis=1)
    a_row = -jnp.exp(alog_ref[...])
    row_i = lax.broadcasted_iota(jnp.int32, (CHUNK, CHUNK), 0)
    col_i = lax.broadcasted_iota(jnp.int32, (CHUNK, CHUNK), 1)
    lower = row_i >= col_i
    upper = row_i <= col_i
    lower_b = lower.astype(BF16)
    fwd_lane = lax.broadcasted_iota(jnp.int32, (1, LANES), 1) < nh

    def prep_chunks(off, pi, carry):
        cis = [pi * SSD_PREP_CHUNKS + n for n in range(SSD_PREP_CHUNKS)]
        rows = [pl.ds(pl.multiple_of(ci * CHUNK, CHUNK), CHUNK) for ci in cis]
        mid = CONV_W // 2
        half = width // 2
        units = [(n, h) for n in range(SSD_PREP_CHUNKS) for h in range(2)]

        def window(unit):
            n, h = unit
            return pad_s[pl.ds(pl.multiple_of(cis[n] * CHUNK, CHUNK), CONV_WIN), h * half:(h + 1) * half]

        def shift_dots(unit):
            win = window(unit)
            return [_dot(shift_s[t], win) for t in range(len(taps))]

        def finish(unit, shifted):
            n, h = unit
            ci, r = cis[n], rows[n]
            cols = slice(h * half, (h + 1) * half)
            acc = conv_b[:, cols] + window(unit)[CONV_HALO:CONV_HALO + CHUNK].astype(F32) * conv_w[mid:mid + 1, cols]
            for t, j in enumerate(taps):
                acc = acc + shifted[t] * conv_w[j:j + 1, cols]
            v = _silu(acc)
            if h == 0:
                y_s[r, :] = v * dsk_ref[...]
                xb_s[r, :] = v.astype(BF16)
                xt_s[ci] = v.T
            else:
                bm = v[:, 0:SSD_STATE]
                bm_s[r, :] = bm.astype(BF16)
                bt_s[ci] = bm.T.astype(BF16)
                cm_s[r, :] = v[:, SSD_STATE:half].astype(BF16)

        dts = [_softplus(dt_ref[pl.ds(off + pl.multiple_of(ci * CHUNK, CHUNK), CHUNK), :] + dtb_ref[...]) for ci in cis]
        das = [dt * a_row for dt in dts]
        pending = shift_dots(units[0])
        cums = []
        for i, unit in enumerate(units):
            following = shift_dots(units[i + 1]) if i + 1 < len(units) else None
            if unit[1] == 0:
                cums.append(_cumsum_rows(lower_b, das[unit[0]]))
            finish(unit, pending)
            pending = following
        for ci, r, dt, da, cum in zip(cis, rows, dts, das, cums):
            e = jnp.where(fwd_lane, cum, cum - da)
            e_s[r, :] = e
            et_s[ci] = e.T[0:2 * nh]
            ct_s[ci] = cum.T[0:2 * nh]
            dtt_s[ci] = dt.T[0:2 * nh]
        return carry

    head_of_lane = lax.broadcasted_iota(jnp.int32, (CHUNK, SSD_GROUP_CH), 1) // SSD_HEAD_DIM
    st_refs = (sf_s, sb_s)

    def scan_chunk(i, carry):
        chunks = (i, nc - 1 - i)
        rows = [pl.ds(pl.multiple_of(ci * CHUNK, CHUNK), CHUNK) for ci in chunks]
        s_prev = [st_refs[d][...] for d in range(2)]
        g, cs = [], []
        for d in range(2):
            cm = cm_s[rows[d], :]
            g.append(_dot(cm, bt_s[chunks[d]]))
            cs.append(_dot_nt(cm, s_prev[d].astype(BF16)))
        w_rows, keeps, ys = [], [], []
        for d in range(2):
            e = e_s[rows[d], :]
            e_t, c_t, dt_t = et_s[chunks[d]], ct_s[chunks[d]], dtt_s[chunks[d]]
            xb = xb_s[rows[d], :]
            y = jnp.zeros((CHUNK, SSD_GROUP_CH), F32)
            w_d, keep_d = [], []
            for r in range(nh):
                col = r + nh * d
                ec = jnp.broadcast_to(e[:, col:col + 1], (CHUNK, CHUNK))
                er = e_t[col:col + 1, :]
                dt_r = dt_t[col:col + 1, :]
                tot = c_t[col:col + 1, CHUNK - 1:CHUNK]
                if d == 0:
                    decay = jnp.exp(jnp.where(lower, ec - er, -jnp.inf))
                    off = jnp.exp(ec)
                    w_d.append(jnp.broadcast_to(dt_r * jnp.exp(tot - er), (SSD_HEAD_DIM, CHUNK)))
                else:
                    decay = jnp.exp(jnp.where(upper, er - ec, -jnp.inf))
                    off = jnp.exp(tot - ec)
                    w_d.append(jnp.broadcast_to(dt_r * jnp.exp(er), (SSD_HEAD_DIM, CHUNK)))
                keep_d.append(jnp.broadcast_to(jnp.exp(tot), (SSD_HEAD_DIM, SSD_STATE)))
                yd = _dot((g[d] * decay * dt_r).astype(BF16), xb)
                off2 = jnp.concatenate([off] * (SSD_GROUP_CH // CHUNK), axis=1)
                y = jnp.where(head_of_lane == r, yd + cs[d] * off2, y)
            ys.append(y)
            w_rows.append(jnp.concatenate(w_d, axis=0))
            keeps.append(jnp.concatenate(keep_d, axis=0))
        for d in range(2):
            w_t = (xt_s[chunks[d]] * w_rows[d]).astype(BF16)
            st_refs[d][...] = keeps[d] * s_prev[d] + _dot(w_t, bm_s[rows[d], :])
            y_s[rows[d], :] += ys[d]
        return carry

    def emit_chunk(off, ci, carry):
        base = pl.multiple_of(ci * CHUNK, CHUNK)
        y_ref[pl.ds(off + base, CHUNK), :] = y_s[pl.ds(base, CHUNK), :].astype(y_ref.dtype)
        return carry

    def one_sequence(s, carry):
        off = pl.multiple_of(s * seq_len, CHUNK)
        lax.fori_loop(0, nc, functools.partial(copy_chunk, off), 0)
        lax.fori_loop(0, nc // SSD_PREP_CHUNKS, functools.partial(prep_chunks, off), 0)
        if has_h0:
            sf_s[...] = h0_ref[s, 0].reshape(SSD_GROUP_CH, SSD_STATE)
            sb_s[...] = h0_ref[s, 1].reshape(SSD_GROUP_CH, SSD_STATE)
        else:
            sf_s[...] = jnp.zeros((SSD_GROUP_CH, SSD_STATE), F32)
            sb_s[...] = jnp.zeros((SSD_GROUP_CH, SSD_STATE), F32)
        lax.fori_loop(0, nc, scan_chunk, 0)
        lax.fori_loop(0, nc, functools.partial(emit_chunk, off), 0)
        if want_final:
            hf_ref[s, 0] = sf_s[...].reshape(SSD_GROUP_HEADS, SSD_HEAD_DIM, SSD_STATE)
            hf_ref[s, 1] = sb_s[...].reshape(SSD_GROUP_HEADS, SSD_HEAD_DIM, SSD_STATE)
        return carry

    lax.fori_loop(0, seqs_per_step, one_sequence, 0)


def _ssd_call(u, dt, conv_w, conv_b, dtb, alog, dsk, y_prev, h0, *, n_seq, seq_len, seqs_per_step, row_block0,
              want_final, name):
    t, n_u = u.shape
    e = dsk.shape[1]
    groups = e // SSD_GROUP_CH
    xb0 = e // SSD_GROUP_CH
    bb0 = 2 * e // SSD_STATE
    cb0 = bb0 + groups
    has_h0 = h0 is not None
    seq = lambda b: row_block0 + b
    rows = seqs_per_step * seq_len
    in_specs = [pl.BlockSpec((rows, SSD_GROUP_CH), lambda b, g: (seq(b), xb0 + g)),
                pl.BlockSpec((rows, SSD_STATE), lambda b, g: (seq(b), bb0 + g)),
                pl.BlockSpec((rows, SSD_STATE), lambda b, g: (seq(b), cb0 + g)),
                pl.BlockSpec((rows, LANES), lambda b, g: (seq(b), g)),
                pl.BlockSpec((CONV_W, SSD_GROUP_CH), lambda b, g: (0, g)),
                pl.BlockSpec((CONV_W, SSD_STATE), lambda b, g: (0, bb0 - xb0 * 2 + g)),
                pl.BlockSpec((CONV_W, SSD_STATE), lambda b, g: (0, cb0 - xb0 * 2 + g)),
                pl.BlockSpec((1, SSD_GROUP_CH), lambda b, g: (0, g)),
                pl.BlockSpec((1, SSD_STATE), lambda b, g: (0, bb0 - xb0 * 2 + g)),
                pl.BlockSpec((1, SSD_STATE), lambda b, g: (0, cb0 - xb0 * 2 + g)),
                pl.BlockSpec((1, LANES), lambda b, g: (0, g)),
                pl.BlockSpec((1, LANES), lambda b, g: (0, g)),
                pl.BlockSpec((1, SSD_GROUP_CH), lambda b, g: (0, g))]
    args = [u, u, u, dt, conv_w, conv_w, conv_w, conv_b, conv_b, conv_b, dtb, alog, dsk]
    state_block = (seqs_per_step, 2, SSD_GROUP_HEADS, SSD_HEAD_DIM, SSD_STATE)
    if has_h0:
        in_specs.append(pl.BlockSpec(state_block, lambda b, g: (b, 0, g, 0, 0)))
        args.append(h0)
    aliases = {}
    if y_prev is not None:
        in_specs.append(pl.BlockSpec(memory_space=pl.ANY))
        args.append(y_prev)
        aliases = {len(args) - 1: 0}
    out_specs = [pl.BlockSpec((rows, SSD_GROUP_CH), lambda b, g: (seq(b), g))]
    out_shape = [jax.ShapeDtypeStruct((t, e), ACT)]
    if want_final:
        out_specs.append(pl.BlockSpec(state_block, lambda b, g: (b, 0, g, 0, 0)))
        out_shape.append(jax.ShapeDtypeStruct((n_seq, 2, e // SSD_HEAD_DIM, SSD_HEAD_DIM, SSD_STATE), F32))
    width = SSD_GROUP_CH + 2 * SSD_STATE
    nc = seq_len // CHUNK

    def body(*refs):
        refs = list(refs)
        if y_prev is not None:
            n_in = len(args)
            del refs[n_in - 1]
        _ssd_body(*refs, seq_len=seq_len, seqs_per_step=seqs_per_step, has_h0=has_h0, want_final=want_final)

    return pl.pallas_call(
        body,
        grid=(n_seq // seqs_per_step, groups),
        in_specs=in_specs,
        out_specs=out_specs,
        out_shape=out_shape,
        input_output_aliases=aliases,
        scratch_shapes=[pltpu.VMEM((seq_len + CONV_WIN - CHUNK, width), ACT),
                        pltpu.VMEM((CONV_W - 1, CHUNK, CONV_WIN), BF16),
                        pltpu.VMEM((seq_len, SSD_GROUP_CH), BF16),
                        pltpu.VMEM((nc, SSD_GROUP_CH, CHUNK), F32),
                        pltpu.VMEM((seq_len, SSD_STATE), BF16),
                        pltpu.VMEM((nc, SSD_STATE, CHUNK), BF16),
                        pltpu.VMEM((seq_len, SSD_STATE), BF16),
                        pltpu.VMEM((seq_len, LANES), F32),
                        pltpu.VMEM((nc, 2 * SSD_GROUP_HEADS, CHUNK), F32),
                        pltpu.VMEM((nc, 2 * SSD_GROUP_HEADS, CHUNK), F32),
                        pltpu.VMEM((nc, 2 * SSD_GROUP_HEADS, CHUNK), F32),
                        pltpu.VMEM((seq_len, SSD_GROUP_CH), F32),
                        pltpu.VMEM((SSD_GROUP_CH, SSD_STATE), F32),
                        pltpu.VMEM((SSD_GROUP_CH, SSD_STATE), F32)],
        compiler_params=_params("parallel", "parallel"),
        name=name,
    )(*args)


def _group_lanes(v, groups):
    per_group = v.reshape(2, groups, SSD_GROUP_HEADS).transpose(1, 0, 2).reshape(groups, 2 * SSD_GROUP_HEADS)
    return jnp.pad(per_group, ((0, 0), (0, LANES - 2 * SSD_GROUP_HEADS))).reshape(1, groups * LANES)


def _gmlp_body(u_ref, v_ref, lng_ref, lnb_ref, ws_ref, bs_ref, o_ref, *, tm):
    groups = ws_ref.shape[0]
    gch = u_ref.shape[1] // groups
    for c in range(tm // CHUNK):
        rows = slice(c * CHUNK, (c + 1) * CHUNK)
        v = _gelu(v_ref[rows, :].astype(F32))
        vc = v - jnp.mean(v, axis=-1, keepdims=True)
        vn = vc * lax.rsqrt(jnp.mean(vc * vc, axis=-1, keepdims=True) + EPS) * lng_ref[...] + lnb_ref[...]
        vb = vn.astype(BF16)
        for g in range(groups):
            cols = slice(g * gch, (g + 1) * gch)
            s = _dot(ws_ref[g].astype(BF16), vb[:, cols]) + bs_ref[:, cols]
            o_ref[rows, cols] = (_gelu(u_ref[rows, cols].astype(F32)) * s).astype(o_ref.dtype)


def _gmlp(u, ln_g, ln_b, w_s, b_lanes, *, tm=256):
    t = u.shape[0]
    e = ln_g.shape[1]
    return pl.pallas_call(
        functools.partial(_gmlp_body, tm=tm),
        grid=(t // tm,),
        in_specs=[pl.BlockSpec((tm, e), lambda i: (i, 0)),
                  pl.BlockSpec((tm, e), lambda i: (i, 1)),
                  pl.BlockSpec((1, e), lambda i: (0, 0)),
                  pl.BlockSpec((1, e), lambda i: (0, 0)),
                  pl.BlockSpec(w_s.shape, lambda i: (0, 0, 0)),
                  pl.BlockSpec((CHUNK, e), lambda i: (0, 0))],
        out_specs=pl.BlockSpec((tm, e), lambda i: (i, 0)),
        out_shape=jax.ShapeDtypeStruct((t, e), ACT),
        compiler_params=_params("parallel"),
        name="gmlp",
    )(u, u, ln_g, ln_b, w_s, b_lanes)


def _s5_body(lr_ref, li_ref, ls_ref, brt_ref, bit_ref, cre_ref, cim_ref, h0_ref, u_ref, y_ref, fin_ref,
             win_s, t_s, ef_s, eb_s, z_s, a_s, zin_s, d_s, spf_s, spb_s, yo_s, uf_s,
             *, seqs, l_ctx, n_lat, l_lat, ctx_parts):
    bst = S5_BLOCK_ST
    ng = LANES // S5_GROUP
    kw = S5_T * LANES
    part = pl.program_id(1)

    @pl.when(part == 0)
    def _build():
        own = (lax.broadcasted_iota(jnp.int32, (ng, S5_GROUP, bst), 2) // S5_STATE
               == lax.broadcasted_iota(jnp.int32, (ng, S5_GROUP, bst), 0))

        def spread(v):
            return jnp.where(own, v[None], 0.0).reshape(LANES, bst).astype(BF16)

        tau = lax.broadcasted_iota(jnp.int32, (S5_T + 8, 1), 0).astype(F32)
        e_refs = (ef_s, eb_s)
        for d in range(2):
            lr, li = lr_ref[d], li_ref[d]
            step = jnp.exp(ls_ref[d])
            mag = jnp.exp(tau * (lr * step))
            p_re = mag * jnp.cos(tau * (li * step))
            p_im = mag * jnp.sin(tau * (li * step))
            ab_re, ab_im = p_re[1:2], p_im[1:2]
            den = lr * lr + li * li
            nr = ab_re - 1.0
            cr = (nr * lr + ab_im * li) / den
            ci = (ab_im * lr - nr * li) / den
            brt, bit = brt_ref[d], bit_ref[d]
            bb_re = cr * brt - ci * bit
            bb_im = cr * bit + ci * brt
            cre, cim = cre_ref[d], cim_ref[d]
            for k in range(S5_T):
                rows = slice(k * LANES, (k + 1) * LANES)
                tq = S5_T - 1 - k if d == 0 else k
                te = k + 1 if d == 0 else S5_T - k
                pr, pi = p_re[tq:tq + 1], p_im[tq:tq + 1]
                win_s[rows, 2 * d * bst:(2 * d + 1) * bst] = spread(pr * bb_re - pi * bb_im)
                win_s[rows, (2 * d + 1) * bst:(2 * d + 2) * bst] = spread(pr * bb_im + pi * bb_re)
                pr, pi = p_re[te:te + 1], p_im[te:te + 1]
                e_refs[d][rows, 0:bst] = spread(cre * pr - cim * pi)
                e_refs[d][rows, bst:2 * bst] = spread(-(cre * pi + cim * pr))
            c_own = jnp.concatenate([spread(cre), spread(-cim)], axis=1)
            z = _dot_nt(win_s[:, 2 * d * bst:(2 * d + 2) * bst], c_own)
            if d == 0:
                z_s[0:kw, :] = z
            else:
                z_s[kw - LANES:kw, :] += z[0:LANES]
                z_s[kw:2 * kw - LANES, :] = z[LANES:kw]
            a_s[2 * d:2 * d + 1, :] = p_re[S5_T:S5_T + 1]
            a_s[2 * d + 1:2 * d + 2, :] = p_im[S5_T:S5_T + 1]
        for k in range(S5_T):
            r0 = (S5_T - 1 - k) * LANES
            t_s[:, k * LANES:(k + 1) * LANES] = z_s[r0:r0 + kw, :].astype(BF16)

    def outputs():
        yo_s[...] = (_dot(zin_s[...], t_s[...]) + _dot_nt(spf_s[...].astype(BF16), ef_s[...])
                     + _dot_nt(spb_s[...].astype(BF16), eb_s[...]))

    @pl.when(part < ctx_parts)
    def _context():
        nj = l_ctx // S5_T
        uf_s[...] = u_ref[...].astype(F32)
        for j in range(nj):
            for k in range(S5_T):
                zin_s[j * seqs:(j + 1) * seqs, k * LANES:(k + 1) * LANES] = (
                    uf_s[pl.ds(j * S5_T + k, seqs, stride=l_ctx), :].astype(BF16))
        d_s[...] = _dot(zin_s[...], win_s[...])
        coef = [jnp.broadcast_to(a_s[i:i + 1, :], (seqs, bst)) for i in range(4)]
        zero = jnp.zeros((seqs, bst), F32)

        def step(j, carry):
            fr, fi, br, bi = carry
            rf = pl.ds(pl.multiple_of(j * seqs, seqs), seqs)
            rb = pl.ds(pl.multiple_of((nj - 1 - j) * seqs, seqs), seqs)
            spf_s[rf, 0:bst] = fr
            spf_s[rf, bst:2 * bst] = fi
            spb_s[rb, 0:bst] = br
            spb_s[rb, bst:2 * bst] = bi
            return (coef[0] * fr - coef[1] * fi + d_s[rf, 0:bst],
                    coef[0] * fi + coef[1] * fr + d_s[rf, bst:2 * bst],
                    coef[2] * br - coef[3] * bi + d_s[rb, 2 * bst:3 * bst],
                    coef[2] * bi + coef[3] * br + d_s[rb, 3 * bst:4 * bst])

        fr, fi, br, bi = lax.fori_loop(0, nj, step, (zero, zero, zero, zero))
        fin_ref[0, 0] = fr
        fin_ref[0, 1] = fi
        fin_ref[1, 0] = br
        fin_ref[1, 1] = bi
        outputs()
        for j in range(nj):
            for k in range(S5_T):
                y_ref[pl.ds(j * S5_T + k, seqs, stride=l_ctx), :] = (
                    yo_s[j * seqs:(j + 1) * seqs, k * LANES:(k + 1) * LANES])

    @pl.when(part == ctx_parts)
    def _latent():
        nj = l_lat // S5_T
        uf_s[...] = u_ref[...].astype(F32)
        for b in range(n_lat):
            for k in range(S5_T):
                zin_s[b * nj:(b + 1) * nj, k * LANES:(k + 1) * LANES] = (
                    uf_s[pl.ds(b * l_lat + k, nj, stride=S5_T), :].astype(BF16))
        d_s[...] = _dot(zin_s[...], win_s[...])
        coef_re = jnp.concatenate([jnp.broadcast_to(a_s[0:1, :], (n_lat, bst)),
                                   jnp.broadcast_to(a_s[2:3, :], (n_lat, bst))], axis=0)
        coef_im = jnp.concatenate([jnp.broadcast_to(a_s[1:2, :], (n_lat, bst)),
                                   jnp.broadcast_to(a_s[3:4, :], (n_lat, bst))], axis=0)
        s_re0 = jnp.concatenate([h0_ref[0, 0], h0_ref[1, 0]], axis=0)
        s_im0 = jnp.concatenate([h0_ref[0, 1], h0_ref[1, 1]], axis=0)

        def step(j, carry):
            s_re, s_im = carry
            d_re, d_im = [], []
            for b in range(n_lat):
                row = pl.ds(b * nj + j, 1)
                spf_s[row, 0:bst] = s_re[b:b + 1]
                spf_s[row, bst:2 * bst] = s_im[b:b + 1]
                d_re.append(d_s[row, 0:bst])
                d_im.append(d_s[row, bst:2 * bst])
            for b in range(n_lat):
                row = pl.ds(b * nj + nj - 1 - j, 1)
                spb_s[row, 0:bst] = s_re[n_lat + b:n_lat + b + 1]
                spb_s[row, bst:2 * bst] = s_im[n_lat + b:n_lat + b + 1]
                d_re.append(d_s[row, 2 * bst:3 * bst])
                d_im.append(d_s[row, 3 * bst:4 * bst])
            d_re = jnp.concatenate(d_re, axis=0)
            d_im = jnp.concatenate(d_im, axis=0)
            return coef_re * s_re - coef_im * s_im + d_re, coef_re * s_im + coef_im * s_re + d_im

        lax.fori_loop(0, nj, step, (s_re0, s_im0))
        outputs()
        for b in range(n_lat):
            for k in range(S5_T):
                y_ref[pl.ds(b * l_lat + k, nj, stride=S5_T), :] = yo_s[b * nj:(b + 1) * nj, k * LANES:(k + 1) * LANES]


def _s5_core(prm, h0, u, *, n_ctx, l_ctx, n_lat, l_lat, e):
    blocks = prm[0].shape[0]
    part_tokens = n_lat * l_lat
    t = u.shape[0]
    n_parts = t // part_tokens
    seqs = part_tokens // l_ctx
    r = part_tokens // S5_T
    vec = pl.BlockSpec((None, 2, 1, S5_BLOCK_ST), lambda g, p: (g, 0, 0, 0))
    mat = pl.BlockSpec((None, 2, S5_GROUP, S5_BLOCK_ST), lambda g, p: (g, 0, 0, 0))
    k_in = S5_T * LANES
    return pl.pallas_call(
        functools.partial(_s5_body, seqs=seqs, l_ctx=l_ctx, n_lat=n_lat, l_lat=l_lat, ctx_parts=n_parts - 1),
        grid=(blocks, n_parts),
        in_specs=[vec, vec, vec, mat, mat, mat, mat,
                  pl.BlockSpec((None, 2, 2, n_lat, S5_BLOCK_ST), lambda g, p: (g, 0, 0, 0, 0)),
                  pl.BlockSpec((part_tokens, LANES), lambda g, p: (p, g))],
        out_specs=[pl.BlockSpec((part_tokens, LANES), lambda g, p: (p, g)),
                   pl.BlockSpec((None, 2, 2, seqs, S5_BLOCK_ST),
                                lambda g, p: (g, 0, 0, jnp.minimum(p, n_parts - 2), 0))],
        out_shape=[jax.ShapeDtypeStruct((t, e), F32),
                   jax.ShapeDtypeStruct((blocks, 2, 2, n_ctx, S5_BLOCK_ST), F32)],
        scratch_shapes=[pltpu.VMEM((k_in, 4 * S5_BLOCK_ST), BF16),
                        pltpu.VMEM((k_in, k_in), BF16),
                        pltpu.VMEM((k_in, 2 * S5_BLOCK_ST), BF16),
                        pltpu.VMEM((k_in, 2 * S5_BLOCK_ST), BF16),
                        pltpu.VMEM(((2 * S5_T - 1) * LANES, LANES), F32),
                        pltpu.VMEM((8, S5_BLOCK_ST), F32),
                        pltpu.VMEM((r, k_in), BF16),
                        pltpu.VMEM((r, 4 * S5_BLOCK_ST), F32),
                        pltpu.VMEM((r, 2 * S5_BLOCK_ST), F32),
                        pltpu.VMEM((r, 2 * S5_BLOCK_ST), F32),
                        pltpu.VMEM((r, k_in), F32),
                        pltpu.VMEM((part_tokens, LANES), F32)],
        compiler_params=_params("parallel", "arbitrary"),
        name="s5_core",
    )(*prm, h0, u)


def _s5_glu_body(ys_ref, u_ref, dsk_ref, w_ref, b_ref, o_ref):
    y = _gelu(ys_ref[...] + dsk_ref[...] * u_ref[...].astype(F32))
    o_ref[...] = (y * jax.nn.sigmoid(_dot(y.astype(BF16), w_ref[...]) + b_ref[...])).astype(o_ref.dtype)


def _s5_glu(ys, u, dsk, w, b, *, tm=512):
    t, e = ys.shape
    return pl.pallas_call(
        _s5_glu_body,
        grid=(t // tm,),
        in_specs=[pl.BlockSpec((tm, e), lambda i: (i, 0)),
                  pl.BlockSpec((tm, e), lambda i: (i, 0)),
                  pl.BlockSpec((1, e), lambda i: (0, 0)),
                  pl.BlockSpec((e, e), lambda i: (0, 0)),
                  pl.BlockSpec((1, e), lambda i: (0, 0))],
        out_specs=pl.BlockSpec((tm, e), lambda i: (i, 0)),
        out_shape=jax.ShapeDtypeStruct((t, e), ACT),
        compiler_params=_params("parallel"),
        name="s5_glu",
    )(ys, u, dsk, w, b)


def _ctx_attn_body(q_ref, k_ref, v_ref, o_ref, ko_ref, vo_ref):
    scale = HEAD_DIM ** -0.5
    seq_len = q_ref.shape[0]
    first = lax.broadcasted_iota(jnp.int32, (seq_len, LANES), 1) < HEAD_DIM
    masks = (first, jnp.logical_not(first))
    n_pairs = q_ref.shape[1] // LANES
    logits, values = [], []
    for pair in range(n_pairs):
        cols = slice(pair * LANES, (pair + 1) * LANES)
        q2, k2, v2 = q_ref[:, cols].astype(F32) * scale, k_ref[:, cols], v_ref[:, cols]
        kb = k2.astype(BF16)
        values.append(v2.astype(BF16))
        for h in range(LANES // HEAD_DIM):
            ko_ref[2 * pair + h] = k2[:, h * HEAD_DIM:(h + 1) * HEAD_DIM].astype(F32)
            vo_ref[2 * pair + h] = v2[:, h * HEAD_DIM:(h + 1) * HEAD_DIM].astype(F32)
            logits.append(_dot_nt(jnp.where(masks[h], q2, 0.0).astype(BF16), kb))
    probs = []
    for s in logits:
        p = jnp.exp(s - jnp.max(s, axis=-1, keepdims=True))
        probs.append((p * (1.0 / jnp.sum(p, axis=-1, keepdims=True))).astype(BF16))
    for pair in range(n_pairs):
        o0, o1 = _dot(probs[2 * pair], values[pair]), _dot(probs[2 * pair + 1], values[pair])
        o_ref[:, pair * LANES:(pair + 1) * LANES] = jnp.where(first, o0, o1).astype(o_ref.dtype)


def _ctx_attention(u, y_prev, *, n_seq, seq_len, e, width=512):
    t = u.shape[0]
    hp = e // width
    heads = e // HEAD_DIM
    kv_block = (None, width // HEAD_DIM, seq_len, HEAD_DIM)
    in_specs = [pl.BlockSpec((seq_len, width), lambda b, h: (b, h)),
                pl.BlockSpec((seq_len, width), lambda b, h: (b, hp + h)),
                pl.BlockSpec((seq_len, width), lambda b, h: (b, 2 * hp + h))]
    return pl.pallas_call(
        _ctx_attn_body,
        grid=(n_seq, hp),
        in_specs=in_specs,
        out_specs=[pl.BlockSpec((seq_len, width), lambda b, h: (b, h)),
                   pl.BlockSpec(kv_block, lambda b, h: (b, h, 0, 0)),
                   pl.BlockSpec(kv_block, lambda b, h: (b, h, 0, 0))],
        out_shape=[jax.ShapeDtypeStruct((t, e), ACT),
                   jax.ShapeDtypeStruct((n_seq, heads, seq_len, HEAD_DIM), F32),
                   jax.ShapeDtypeStruct((n_seq, heads, seq_len, HEAD_DIM), F32)],
        compiler_params=_params("parallel", "parallel"),
        name="ctx_attention",
    )(u, u, u)


def _nat_bias_body(rpb_ref, o_ref, *, n_rel_rows, n_rel_cols):
    h = pl.program_id(0)
    q = lax.broadcasted_iota(jnp.int32, (GRID_W, GRID_W), 0)
    kc = lax.broadcasted_iota(jnp.int32, (GRID_W, GRID_W), 1)
    rel = kc - q + (WIN_COLS - 1)
    c_start = jnp.clip(q - WIN_COLS // 2, 0, GRID_W - WIN_COLS)
    ok = (kc >= c_start) & (kc < c_start + WIN_COLS)
    for i in range(n_rel_rows):
        acc = jnp.zeros((GRID_W, GRID_W), F32)
        for dcol in range(n_rel_cols):
            acc = jnp.where(rel == dcol, rpb_ref[(h * n_rel_rows + i) * n_rel_cols + dcol], acc)
        tile = jnp.where(ok, acc, MASKED)
        if i < n_rel_rows - 1:
            o_ref[i, :, 0:GRID_W] = tile
        if i > 0:
            o_ref[i - 1, :, GRID_W:2 * GRID_W] = tile


def _nat_bias(rpb):
    heads, n_rel_rows, n_rel_cols = rpb.shape
    return pl.pallas_call(
        functools.partial(_nat_bias_body, n_rel_rows=n_rel_rows, n_rel_cols=n_rel_cols),
        grid=(heads,),
        in_specs=[pl.BlockSpec(memory_space=pltpu.SMEM)],
        out_specs=pl.BlockSpec((None, n_rel_rows - 1, GRID_W, 2 * GRID_W), lambda h: (h, 0, 0, 0)),
        out_shape=jax.ShapeDtypeStruct((heads, n_rel_rows - 1, GRID_W, 2 * GRID_W), F32),
        compiler_params=_params("parallel"),
        name="nat_bias",
    )(rpb.reshape(-1))


def _nat_body(q_ref, k_ref, v_ref, ck_ref, cv_ref, bias_ref, o_ref, ckb_s, cvb_s, *, rows):
    scale = HEAD_DIM ** -0.5
    wr = min(WIN_ROWS, rows)
    nw = wr * GRID_W
    ckb_s[...] = jnp.concatenate([ck_ref[0], ck_ref[1]], axis=1).astype(BF16)
    cvb_s[...] = jnp.concatenate([cv_ref[0], cv_ref[1]], axis=1).astype(BF16)
    first = lax.broadcasted_iota(jnp.int32, (GRID_W, LANES), 1) < HEAD_DIM

    heads = tuple(range(LANES // HEAD_DIM))
    masks = (first, jnp.logical_not(first))

    def row_group(g, carry):
        rws = [g * NAT_ROWS_PER_STEP + i for i in range(NAT_ROWS_PER_STEP)]
        starts = [jnp.clip(r - wr // 2, 0, rows - wr) for r in rws]
        q_rows = [pl.ds(pl.multiple_of(r * GRID_W, GRID_W), GRID_W) for r in rws]
        k_rows = [pl.ds(pl.multiple_of(rs * GRID_W, GRID_W), nw) for rs in starts]
        logits = []
        for r, rs, qr, kr in zip(rws, starts, q_rows, k_rows):
            q2 = q_ref[qr, :].astype(F32) * scale
            kb = k_ref[kr, :].astype(BF16)
            for h in heads:
                q = jnp.where(masks[h], q2, 0.0).astype(BF16)
                logits.append((_dot_nt(q, kb), _dot_nt(q, ckb_s[...])))
        probs = []
        for idx, (s_win, s_ctx) in enumerate(logits):
            r, rs, h = rws[idx // 2], starts[idx // 2], heads[idx % 2]
            i0 = (WIN_ROWS - 1) - (r - rs)
            s_win = s_win + jnp.concatenate([bias_ref[h, i0 + 2 * jj] for jj in range(wr // 2)], axis=1)
            m = jnp.maximum(jnp.max(s_win, axis=-1, keepdims=True), jnp.max(s_ctx, axis=-1, keepdims=True))
            p_win = jnp.exp(s_win - m)
            p_ctx = jnp.exp(s_ctx - m)
            inv = 1.0 / (jnp.sum(p_win, axis=-1, keepdims=True) + jnp.sum(p_ctx, axis=-1, keepdims=True))
            probs.append(((p_win * inv).astype(BF16), (p_ctx * inv).astype(BF16)))
        for i, (qr, kr) in enumerate(zip(q_rows, k_rows)):
            vb = v_ref[kr, :].astype(BF16)
            outs = [_dot(probs[2 * i + h][0], vb) + _dot(probs[2 * i + h][1], cvb_s[...]) for h in heads]
            o_ref[qr, :] = jnp.where(first, outs[0], outs[1]).astype(o_ref.dtype)
        return carry

    lax.fori_loop(0, rows // NAT_ROWS_PER_STEP, row_group, 0)


def _nat_attention(u, cache_k, cache_v, cache_layer, bias, y_prev, *, n_seq, seq_len, row_block0, e):
    t = u.shape[0]
    hp = e // LANES
    hpb = LANES // HEAD_DIM
    past = cache_k.shape[3]
    seq = lambda b: row_block0 + b
    cache_block = (None, None, hpb, past, HEAD_DIM)
    return pl.pallas_call(
        lambda q, k, v, ck, cv, bs, yp, o, *scratch: _nat_body(q, k, v, ck, cv, bs, o, *scratch,
                                                               rows=seq_len // GRID_W),
        grid=(n_seq, hp),
        in_specs=[pl.BlockSpec((seq_len, LANES), lambda b, h: (seq(b), h)),
                  pl.BlockSpec((seq_len, LANES), lambda b, h: (seq(b), hp + h)),
                  pl.BlockSpec((seq_len, LANES), lambda b, h: (seq(b), 2 * hp + h)),
                  pl.BlockSpec(cache_block, lambda b, h: (b, cache_layer, h, 0, 0)),
                  pl.BlockSpec(cache_block, lambda b, h: (b, cache_layer, h, 0, 0)),
                  pl.BlockSpec((hpb,) + bias.shape[1:], lambda b, h: (h, 0, 0, 0)),
                  pl.BlockSpec(memory_space=pl.ANY)],
        out_specs=pl.BlockSpec((seq_len, LANES), lambda b, h: (seq(b), h)),
        out_shape=jax.ShapeDtypeStruct((t, e), ACT),
        input_output_aliases={6: 0},
        scratch_shapes=[pltpu.VMEM((past, LANES), BF16), pltpu.VMEM((past, LANES), BF16)],
        compiler_params=_params("parallel", "parallel"),
        name="nat_attention",
    )(u, u, u, cache_k, cache_v, bias, y_prev)


def kernel(x_prompt, x_sample, state_ssd, state_s5, cache_k, cache_v, c, c_ctx, norm_g, w_mod, b_mod, w_out, final_g, ssd_w_in, ssd_conv_w, ssd_conv_b, ssd_dt_bias, ssd_a_log, ssd_d, ssd_norm_g, mlp_w_in, mlp_ln_g, mlp_ln_b, mlp_w_s, mlp_b_s, s5_w_in, s5_lam_re, s5_lam_im, s5_log_step, s5_b_re, s5_b_im, s5_c_re, s5_c_im, s5_d, s5_w_glu, s5_b_glu, nat_w_in, nat_rpb):
    n_ctx, l_ctx, d = x_prompt.shape
    n_lat, l_lat, _ = x_sample.shape
    t_ctx = n_ctx * l_ctx
    depth = norm_g.shape[0]
    e = w_out.shape[1]
    assert depth == 4 and l_lat % l_ctx == 0 and t_ctx % (n_lat * l_lat) == 0
    tiles = dict(tm=512, t_ctx=t_ctx, l_lat=l_lat)
    tiles_in = dict(tm=2048, tn=1024, t_ctx=t_ctx, l_lat=l_lat)

    t_lat = n_lat * l_lat
    x = (x_prompt.reshape(t_ctx, d), x_sample.reshape(t_lat, d))
    cond8 = jnp.concatenate([c_ctx[None], c, jnp.zeros((8 - 1 - n_lat, d), F32)], axis=0)
    mods = _modulation(cond8, w_mod, b_mod).reshape(depth, 8, 3, 1, d)
    w_out_b = w_out.astype(BF16)

    groups = e // SSD_GROUP_CH
    n_main = 3 * e
    w_dt = ssd_w_in[0][:, n_main:].reshape(d, 2, groups, SSD_GROUP_HEADS).transpose(0, 2, 1, 3)
    w_dt = jnp.pad(w_dt.reshape(d, groups, 2 * SSD_GROUP_HEADS), ((0, 0), (0, 0), (0, LANES - 2 * SSD_GROUP_HEADS)))
    tiles_two = dict(tiles_in, tm=1024)
    u = _inproj(x, mods, 0, norm_g[0:1], ssd_w_in[0].astype(BF16), n_out=n_main, name="ssd_in", **tiles_two)
    dt = _inproj(x, mods, 0, norm_g[0:1], w_dt.reshape(d, groups * LANES).astype(BF16), name="ssd_dt_in", out_dtype=F32,
                 **tiles_two)
    ssd_args = (ssd_conv_w[0], ssd_conv_b[0:1], _group_lanes(ssd_dt_bias[0], groups),
                _group_lanes(ssd_a_log[0], groups), jnp.repeat(ssd_d[0], SSD_HEAD_DIM)[None])
    y, new_ssd = _ssd_call(u, dt, *ssd_args, None, None, n_seq=n_ctx, seq_len=l_ctx, seqs_per_step=l_lat // l_ctx,
                           row_block0=0, want_final=True, name="ssd_ctx")
    y, = _ssd_call(u, dt, *ssd_args, y, state_ssd[:, 0], n_seq=n_lat, seq_len=l_lat, seqs_per_step=1,
                   row_block0=t_ctx // l_lat, want_final=False, name="ssd_lat")
    x = _outproj(y, u, 0, x, mods, 0, w_out_b[0], norm_g=ssd_norm_g[0:1], name="ssd_out", **tiles)

    u = _inproj(x, mods, 1, norm_g[1:2], mlp_w_in[0].astype(BF16), name="mlp_in", **tiles_in)
    b_lanes = jnp.repeat(mlp_b_s[0].T, e // mlp_b_s.shape[1], axis=1)
    y = _gmlp(u, mlp_ln_g[0:1], mlp_ln_b[0:1], mlp_w_s[0], b_lanes)
    x = _outproj(y, u, 2, x, mods, 1, w_out_b[1], name="mlp_out", **tiles)

    u = _inproj(x, mods, 2, norm_g[2:3], s5_w_in[0].astype(BF16), name="s5_in", **tiles_in)
    s5_groups = e // S5_GROUP
    bg = LANES // S5_GROUP
    blocks = s5_groups // bg

    def block_vec(v):
        return v.reshape(2, blocks, 1, S5_BLOCK_ST).transpose(1, 0, 2, 3)

    def block_mat_b(v):
        v = v.reshape(2, blocks, bg, S5_STATE, S5_GROUP).transpose(1, 0, 4, 2, 3)
        return v.reshape(blocks, 2, S5_GROUP, S5_BLOCK_ST)

    def block_mat_c(v):
        v = v.reshape(2, blocks, bg, S5_GROUP, S5_STATE).transpose(1, 0, 3, 2, 4)
        return v.reshape(blocks, 2, S5_GROUP, S5_BLOCK_ST)

    log_step = jnp.repeat(s5_log_step[0][:, :, None], S5_STATE, axis=2)
    prm = (block_vec(s5_lam_re[0]), block_vec(s5_lam_im[0]), block_vec(log_step), block_mat_b(s5_b_re[0]),
           block_mat_b(s5_b_im[0]), block_mat_c(s5_c_re[0]), block_mat_c(s5_c_im[0]))
    h0 = state_s5[:, 0].reshape(n_lat, 2, 2, blocks, S5_BLOCK_ST).transpose(3, 1, 2, 0, 4)
    ys, fin = _s5_core(prm, h0, u, n_ctx=n_ctx, l_ctx=l_ctx, n_lat=n_lat, l_lat=l_lat, e=e)
    new_s5 = fin.transpose(3, 1, 2, 0, 4).reshape(n_ctx, 1, 2, 2, s5_groups, S5_STATE)
    y = _s5_glu(ys, u, s5_d[0:1], s5_w_glu[0].astype(BF16), s5_b_glu[0:1])
    x = _outproj(y, u, 1, x, mods, 2, w_out_b[2], name="s5_out", **tiles)

    u = _inproj(x, mods, 3, norm_g[3:4], nat_w_in[0].astype(BF16), name="nat_in", **tiles_in)
    y, new_k, new_v = _ctx_attention(u, None, n_seq=n_ctx, seq_len=l_ctx, e=e)
    y = _nat_attention(u, cache_k, cache_v, 0, _nat_bias(nat_rpb[0]), y, n_seq=n_lat, seq_len=l_lat,
                       row_block0=t_ctx // l_lat, e=e)
    out_ctx, out_lat = _outproj(y, u, 3, x, mods, 3, w_out_b[3], final_g=final_g[None], split_rows=(t_ctx, t_lat),
                                name="nat_out", **tiles)

    return (out_ctx.reshape(n_ctx, l_ctx, d), out_lat.reshape(n_lat, l_lat, d),
            new_ssd[:, None], new_s5, new_k[:, None], new_v[:, None])
```

```python
import functools

import jax
import jax.numpy as jnp
from jax import lax
from jax.experimental import pallas as pl
from jax.experimental.pallas import tpu as pltpu

F32 = jnp.float32
BF16 = jnp.bfloat16
ACT = jnp.bfloat16
EPS = 1e-6
HIGHEST = lax.Precision.HIGHEST

LANES = 128
CHUNK = 128
SSD_HEAD_DIM = 64
SSD_GROUP_HEADS = 4
SSD_GROUP_CH = SSD_HEAD_DIM * SSD_GROUP_HEADS
SSD_STATE = 128
CONV_W = 5
CONV_HALO = 16
SSD_PREP_CHUNKS = 2
CONV_WIN = 2 * CHUNK
S5_T = 8
S5_GROUP = 16
S5_STATE = 64
S5_BLOCK_ST = (LANES // S5_GROUP) * S5_STATE
HEAD_DIM = 64
GRID_W = 64
WIN_ROWS = 8
WIN_COLS = 16
NAT_ROWS_PER_STEP = 8
MASKED = -1e30
VMEM_LIMIT = 52 * 1024 * 1024


def _silu(x):
    return x * jax.nn.sigmoid(x)


def _gelu(x):
    return 0.5 * x * (1.0 + jnp.tanh(0.7978845608028654 * (x + 0.044715 * (x * x * x))))


def _softplus(x):
    return jnp.maximum(x, 0.0) + jnp.log1p(jnp.exp(-jnp.abs(x)))


def _dot(a, b):
    return jnp.dot(a, b, preferred_element_type=F32)


def _dot_nt(a, b):
    return lax.dot_general(a, b, (((1,), (1,)), ((), ())), preferred_element_type=F32)


def _cumsum_rows(lower_tri, x):
    hi = x.astype(BF16)
    r1 = x - hi.astype(F32)
    mid = r1.astype(BF16)
    lo = (r1 - mid.astype(F32)).astype(BF16)
    return _dot(lower_tri, hi) + _dot(lower_tri, mid) + _dot(lower_tri, lo)


def _params(*sem):
    return pltpu.CompilerParams(dimension_semantics=sem, vmem_limit_bytes=VMEM_LIMIT)


def _mod_row(i, *, tm, t_ctx, l_lat):
    start = i * tm
    return jnp.where(start < t_ctx, 0, 1 + (start - t_ctx) // l_lat)


def _mod_body(c_ref, w_ref, b_ref, o_ref):
    cond = _silu(c_ref[...]).astype(BF16)
    o_ref[...] = _dot(cond, w_ref[...].astype(BF16)) + b_ref[...]


def _modulation(cond8, w_mod, b_mod, *, tn=1024):
    depth, d, n = w_mod.shape
    return pl.pallas_call(
        _mod_body,
        grid=(depth, n // tn),
        in_specs=[pl.BlockSpec((8, d), lambda l, j: (0, 0)),
                  pl.BlockSpec((None, d, tn), lambda l, j: (l, 0, j)),
                  pl.BlockSpec((None, 1, tn), lambda l, j: (l, 0, j))],
        out_specs=pl.BlockSpec((None, 8, tn), lambda l, j: (l, 0, j)),
        out_shape=jax.ShapeDtypeStruct((depth, 8, n), F32),
        compiler_params=_params("parallel", "parallel"),
        name="modulation",
    )(cond8, w_mod, b_mod.reshape(depth, 1, n))


def _stream_specs(x, tm, width, n_grid):
    ids = (lambda i, *_: i) if n_grid == 1 else (lambda i, j: i)
    if not isinstance(x, tuple):
        return [x], [pl.BlockSpec((tm, width), lambda *g: (ids(*g), 0))], None
    split = x[0].shape[0] // tm
    last = x[1].shape[0] // tm - 1
    return (list(x),
            [pl.BlockSpec((tm, width), lambda *g: (jnp.minimum(ids(*g), split - 1), 0)),
             pl.BlockSpec((tm, width), lambda *g: (jnp.clip(ids(*g) - split, 0, last), 0))],
            split)


def _inproj_body(*refs, split):
    n_x = 1 if split is None else 2
    x_refs = refs[:n_x]
    shift_ref, scale_ref, g_ref, w_ref, o_ref, h_ref = refs[n_x:]

    def normalise(x_ref):
        x = x_ref[...]
        y = x * lax.rsqrt(jnp.mean(x * x, axis=-1, keepdims=True) + EPS) * g_ref[...]
        h_ref[...] = (y * (1.0 + scale_ref[...]) + shift_ref[...]).astype(BF16)

    first_col = pl.program_id(1) == 0
    if split is None:
        pl.when(first_col)(lambda: normalise(x_refs[0]))
    else:
        in_ctx = pl.program_id(0) < split
        pl.when(first_col & in_ctx)(lambda: normalise(x_refs[0]))
        pl.when(first_col & jnp.logical_not(in_ctx))(lambda: normalise(x_refs[1]))
    o_ref[...] = _dot(h_ref[...], w_ref[...]).astype(o_ref.dtype)


def _inproj(x, mods, layer, g, w, *, tm, tn, t_ctx, l_lat, name, out_dtype=ACT, n_out=None):
    d = w.shape[0]
    n = w.shape[1] if n_out is None else n_out
    row = functools.partial(_mod_row, tm=tm, t_ctx=t_ctx, l_lat=l_lat)
    x_args, x_specs, split = _stream_specs(x, tm, d, 2)
    t = sum(a.shape[0] for a in x_args)

    def mod_spec(part):
        return pl.BlockSpec((None, None, None, 1, d), lambda i, j: (layer, row(i), part, 0, 0))

    return pl.pallas_call(
        functools.partial(_inproj_body, split=split),
        grid=(t // tm, n // tn),
        in_specs=x_specs + [mod_spec(0), mod_spec(1),
                            pl.BlockSpec((1, d), lambda i, j: (0, 0)),
                            pl.BlockSpec((d, tn), lambda i, j: (0, j))],
        out_specs=pl.BlockSpec((tm, tn), lambda i, j: (i, j)),
        out_shape=jax.ShapeDtypeStruct((t, n), out_dtype),
        scratch_shapes=[pltpu.VMEM((tm, d), BF16)],
        compiler_params=_params("parallel", "arbitrary"),
        name=name,
    )(*x_args, mods, mods, g, w)


def _outproj_body(*refs, n_y, n_x, split, gated_norm, final_norm):
    refs = list(refs)
    y_refs, z_ref, x_refs = refs[:n_y], refs[n_y], refs[n_y + 1:n_y + 1 + n_x]
    rest = refs[n_y + 1 + n_x:]
    gate_ref, w_ref = rest[:2]
    rest = rest[2:]
    ng_ref = rest.pop(0) if gated_norm else None
    fg_ref = rest.pop(0) if final_norm else None
    o_refs = rest

    def emit(y_ref, x_ref, o_ref):
        t = y_ref[...].astype(F32) * _silu(z_ref[...].astype(F32))
        if gated_norm:
            t = t * lax.rsqrt(jnp.mean(t * t, axis=-1, keepdims=True) + EPS) * ng_ref[...]
        xn = x_ref[...] + gate_ref[...] * _dot(t.astype(BF16), w_ref[...])
        if final_norm:
            xn = xn * lax.rsqrt(jnp.mean(xn * xn, axis=-1, keepdims=True) + EPS) * fg_ref[...]
        o_ref[...] = xn

    if split is None:
        emit(y_refs[0], x_refs[0], o_refs[0])
    else:
        in_ctx = pl.program_id(0) < split
        pl.when(in_ctx)(lambda: emit(y_refs[0], x_refs[0], o_refs[0]))
        pl.when(jnp.logical_not(in_ctx))(lambda: emit(y_refs[-1], x_refs[-1], o_refs[-1]))


def _outproj(y, u, z_block, x, mods, layer, w, *, norm_g=None, final_g=None, split_rows=None, tm, t_ctx, l_lat,
             name):
    t, e = u.shape[0], w.shape[0]
    d = w.shape[1]
    row = functools.partial(_mod_row, tm=tm, t_ctx=t_ctx, l_lat=l_lat)
    y_args, y_specs, split_y = _stream_specs(y, tm, e, 1)
    x_args, x_specs, split_in = _stream_specs(x, tm, d, 1)
    in_specs = (y_specs + [pl.BlockSpec((tm, e), lambda i: (i, z_block))] + x_specs
                + [pl.BlockSpec((None, None, None, 1, d), lambda i: (layer, row(i), 2, 0, 0)),
                   pl.BlockSpec((e, d), lambda i: (0, 0))])
    args = y_args + [u] + x_args + [mods, w]
    if norm_g is not None:
        in_specs.append(pl.BlockSpec((1, e), lambda i: (0, 0)))
        args.append(norm_g)
    if final_g is not None:
        in_specs.append(pl.BlockSpec((1, d), lambda i: (0, 0)))
        args.append(final_g)
    if split_rows is None:
        split_out = None
        out_specs = pl.BlockSpec((tm, d), lambda i: (i, 0))
        out_shape = jax.ShapeDtypeStruct((t, d), F32)
    else:
        split_out = split_rows[0] // tm
        last = split_rows[1] // tm - 1
        out_specs = [pl.BlockSpec((tm, d), lambda i: (jnp.minimum(i, split_out - 1), 0)),
                     pl.BlockSpec((tm, d), lambda i: (jnp.clip(i - split_out, 0, last), 0))]
        out_shape = [jax.ShapeDtypeStruct((rows, d), F32) for rows in split_rows]
    return pl.pallas_call(
        functools.partial(_outproj_body, n_y=len(y_args), n_x=len(x_args),
                          split=next((s for s in (split_y, split_in, split_out) if s is not None), None),
                          gated_norm=norm_g is not None, final_norm=final_g is not None),
        grid=(t // tm,),
        in_specs=in_specs,
        out_specs=out_specs,
        out_shape=out_shape,
        compiler_params=_params("arbitrary"),
        name=name,
    )(*args)


def _ssd_body(*refs, seq_len, seqs_per_step, has_h0, want_final):
    refs = list(refs)
    x_ref, b_ref, c_ref, dt_ref, wx_ref, wb_ref, wc_ref, bx_ref, bb_ref, bc_ref, dtb_ref, alog_ref, dsk_ref = refs[:13]
    rest = refs[13:]
    h0_ref = rest.pop(0) if has_h0 else None
    if want_final:
        y_ref, hf_ref = rest[0], rest[1]
        rest = rest[2:]
    else:
        y_ref, hf_ref = rest[0], None
        rest = rest[1:]
    pad_s, shift_s, xb_s, xt_s, bm_s, bt_s, cm_s, e_s, et_s, ct_s, dtt_s, y_s, sf_s, sb_s = rest
    nc = seq_len // CHUNK
    width = SSD_GROUP_CH + 2 * SSD_STATE
    nh = SSD_GROUP_HEADS

    pad_s[0:CONV_HALO, :] = jnp.zeros((CONV_HALO, width), ACT)
    tail = CONV_WIN - CHUNK - CONV_HALO
    pad_s[CONV_HALO + seq_len:CONV_HALO + seq_len + tail, :] = jnp.zeros((tail, width), ACT)
    win_row = lax.broadcasted_iota(jnp.int32, (CHUNK, CONV_WIN), 1)
    tok_row = lax.broadcasted_iota(jnp.int32, (CHUNK, CONV_WIN), 0)
    taps = [j for j in range(CONV_W) if j != CONV_W // 2]
    for n, j in enumerate(taps):
        shift_s[n] = (win_row == tok_row + (CONV_HALO + j - CONV_W // 2)).astype(BF16)

    def copy_chunk(off, ci, carry):
        base = pl.multiple_of(ci * CHUNK, CHUNK)
        src = pl.ds(off + base, CHUNK)
        dst = pl.ds(base + CONV_HALO, CHUNK)
        pad_s[dst, 0:SSD_GROUP_CH] = x_ref[src, :]
        pad_s[dst, SSD_GROUP_CH:SSD_GROUP_CH + SSD_STATE] = b_ref[src, :]
        pad_s[dst, SSD_GROUP_CH + SSD_STATE:width] = c_ref[src, :]
        return carry

    conv_w = jnp.concatenate([wx_ref[...], wb_ref[...], wc_ref[...]], axis=1)
    conv_b = jnp.concatenate([bx_ref[...], bb_ref[...], bc_ref[...]], axis=1)
    a_row = -jnp.exp(alog_ref[...])
    row_i = lax.broadcasted_iota(jnp.int32, (CHUNK, CHUNK), 0)
    col_i = lax.broadcasted_iota(jnp.int32, (CHUNK, CHUNK), 1)
    lower = row_i >= col_i
    upper = row_i <= col_i
    lower_b = lower.astype(BF16)
    fwd_lane = lax.broadcasted_iota(jnp.int32, (1, LANES), 1) < nh

    def prep_chunks(off, pi, carry):
        cis = [pi * SSD_PREP_CHUNKS + n for n in range(SSD_PREP_CHUNKS)]
        rows = [pl.ds(pl.multiple_of(ci * CHUNK, CHUNK), CHUNK) for ci in cis]
        mid = CONV_W // 2
        half = width // 2
        units = [(n, h) for n in range(SSD_PREP_CHUNKS) for h in range(2)]

        def window(unit):
            n, h = unit
            return pad_s[pl.ds(pl.multiple_of(cis[n] * CHUNK, CHUNK), CONV_WIN), h * half:(h + 1) * half]

        def shift_dots(unit):
            win = window(unit)
            return [_dot(shift_s[t], win) for t in range(len(taps))]

        def finish(unit, shifted):
            n, h = unit
            ci, r = cis[n], rows[n]
            cols = slice(h * half, (h + 1) * half)
            acc = conv_b[:, cols] + window(unit)[CONV_HALO:CONV_HALO + CHUNK].astype(F32) * conv_w[mid:mid + 1, cols]
            for t, j in enumerate(taps):
                acc = acc + shifted[t] * conv_w[j:j + 1, cols]
            v = _silu(acc)
            if h == 0:
                y_s[r, :] = v * dsk_ref[...]
                xb_s[r, :] = v.astype(BF16)
                xt_s[ci] = v.T
            else:
                bm = v[:, 0:SSD_STATE]
                bm_s[r, :] = bm.astype(BF16)
                bt_s[ci] = bm.T.astype(BF16)
                cm_s[r, :] = v[:, SSD_STATE:half].astype(BF16)

        dts = [_softplus(dt_ref[pl.ds(off + pl.multiple_of(ci * CHUNK, CHUNK), CHUNK), :] + dtb_ref[...]) for ci in cis]
        das = [dt * a_row for dt in dts]
        pending = shift_dots(units[0])
        cums = []
        for i, unit in enumerate(units):
            following = shift_dots(units[i + 1]) if i + 1 < len(units) else None
            if unit[1] == 0:
                cums.append(_cumsum_rows(lower_b, das[unit[0]]))
            finish(unit, pending)
            pending = following
        for ci, r, dt, da, cum in zip(cis, rows, dts, das, cums):
            e = jnp.where(fwd_lane, cum, cum - da)
            e_s[r, :] = e
            et_s[ci] = e.T[0:2 * nh]
            ct_s[ci] = cum.T[0:2 * nh]
            dtt_s[ci] = dt.T[0:2 * nh]
        return carry

    head_of_lane = lax.broadcasted_iota(jnp.int32, (CHUNK, SSD_GROUP_CH), 1) // SSD_HEAD_DIM
    st_refs = (sf_s, sb_s)

    def scan_chunk(i, carry):
        chunks = (i, nc - 1 - i)
        rows = [pl.ds(pl.multiple_of(ci * CHUNK, CHUNK), CHUNK) for ci in chunks]
        s_prev = [st_refs[d][...] for d in range(2)]
        g, cs = [], []
        for d in range(2):
            cm = cm_s[rows[d], :]
            g.append(_dot(cm, bt_s[chunks[d]]))
            cs.append(_dot_nt(cm, s_prev[d].astype(BF16)))
        w_rows, keeps, ys = [], [], []
        for d in range(2):
            e = e_s[rows[d], :]
            e_t, c_t, dt_t = et_s[chunks[d]], ct_s[chunks[d]], dtt_s[chunks[d]]
            xb = xb_s[rows[d], :]
            y = jnp.zeros((CHUNK, SSD_GROUP_CH), F32)
            w_d, keep_d = [], []
            for r in range(nh):
                col = r + nh * d
                ec = jnp.broadcast_to(e[:, col:col + 1], (CHUNK, CHUNK))
                er = e_t[col:col + 1, :]
                dt_r = dt_t[col:col + 1, :]
                tot = c_t[col:col + 1, CHUNK - 1:CHUNK]
                if d == 0:
                    decay = jnp.exp(jnp.where(lower, ec - er, -jnp.inf))
                    off = jnp.exp(ec)
                    w_d.append(jnp.broadcast_to(dt_r * jnp.exp(tot - er), (SSD_HEAD_DIM, CHUNK)))
                else:
                    decay = jnp.exp(jnp.where(upper, er - ec, -jnp.inf))
                    off = jnp.exp(tot - ec)
                    w_d.append(jnp.broadcast_to(dt_r * jnp.exp(er), (SSD_HEAD_DIM, CHUNK)))
                keep_d.append(jnp.broadcast_to(jnp.exp(tot), (SSD_HEAD_DIM, SSD_STATE)))
                yd = _dot((g[d] * decay * dt_r).astype(BF16), xb)
                off2 = jnp.concatenate([off] * (SSD_GROUP_CH // CHUNK), axis=1)
                y = jnp.where(head_of_lane == r, yd + cs[d] * off2, y)
            ys.append(y)
            w_rows.append(jnp.concatenate(w_d, axis=0))
            keeps.append(jnp.concatenate(keep_d, axis=0))
        for d in range(2):
            w_t = (xt_s[chunks[d]] * w_rows[d]).astype(BF16)
            st_refs[d][...] = keeps[d] * s_prev[d] + _dot(w_t, bm_s[rows[d], :])
            y_s[rows[d], :] += ys[d]
        return carry

    def emit_chunk(off, ci, carry):
        base = pl.multiple_of(ci * CHUNK, CHUNK)
        y_ref[pl.ds(off + base, CHUNK), :] = y_s[pl.ds(base, CHUNK), :].astype(y_ref.dtype)
        return carry

    def one_sequence(s, carry):
        off = pl.multiple_of(s * seq_len, CHUNK)
        lax.fori_loop(0, nc, functools.partial(copy_chunk, off), 0)
        lax.fori_loop(0, nc // SSD_PREP_CHUNKS, functools.partial(prep_chunks, off), 0)
        if has_h0:
            sf_s[...] = h0_ref[s, 0].reshape(SSD_GROUP_CH, SSD_STATE)
            sb_s[...] = h0_ref[s, 1].reshape(SSD_GROUP_CH, SSD_STATE)
        else:
            sf_s[...] = jnp.zeros((SSD_GROUP_CH, SSD_STATE), F32)
            sb_s[...] = jnp.zeros((SSD_GROUP_CH, SSD_STATE), F32)
        lax.fori_loop(0, nc, scan_chunk, 0)
        lax.fori_loop(0, nc, functools.partial(emit_chunk, off), 0)
        if want_final:
            hf_ref[s, 0] = sf_s[...].reshape(SSD_GROUP_HEADS, SSD_HEAD_DIM, SSD_STATE)
            hf_ref[s, 1] = sb_s[...].reshape(SSD_GROUP_HEADS, SSD_HEAD_DIM, SSD_STATE)
        return carry

    lax.fori_loop(0, seqs_per_step, one_sequence, 0)


def _ssd_call(u, dt, conv_w, conv_b, dtb, alog, dsk, h0, *, n_seq, seq_len, seqs_per_step, row_block0,
              want_final, name):
    t, n_u = u.shape
    e = dsk.shape[1]
    groups = e // SSD_GROUP_CH
    xb0 = e // SSD_GROUP_CH
    bb0 = 2 * e // SSD_STATE
    cb0 = bb0 + groups
    has_h0 = h0 is not None
    seq = lambda b: row_block0 + b
    rows = seqs_per_step * seq_len
    in_specs = [pl.BlockSpec((rows, SSD_GROUP_CH), lambda b, g: (seq(b), xb0 + g)),
                pl.BlockSpec((rows, SSD_STATE), lambda b, g: (seq(b), bb0 + g)),
                pl.BlockSpec((rows, SSD_STATE), lambda b, g: (seq(b), cb0 + g)),
                pl.BlockSpec((rows, LANES), lambda b, g: (seq(b), g)),
                pl.BlockSpec((CONV_W, SSD_GROUP_CH), lambda b, g: (0, g)),
                pl.BlockSpec((CONV_W, SSD_STATE), lambda b, g: (0, bb0 - xb0 * 2 + g)),
                pl.BlockSpec((CONV_W, SSD_STATE), lambda b, g: (0, cb0 - xb0 * 2 + g)),
                pl.BlockSpec((1, SSD_GROUP_CH), lambda b, g: (0, g)),
                pl.BlockSpec((1, SSD_STATE), lambda b, g: (0, bb0 - xb0 * 2 + g)),
                pl.BlockSpec((1, SSD_STATE), lambda b, g: (0, cb0 - xb0 * 2 + g)),
                pl.BlockSpec((1, LANES), lambda b, g: (0, g)),
                pl.BlockSpec((1, LANES), lambda b, g: (0, g)),
                pl.BlockSpec((1, SSD_GROUP_CH), lambda b, g: (0, g))]
    args = [u, u, u, dt, conv_w, conv_w, conv_w, conv_b, conv_b, conv_b, dtb, alog, dsk]
    state_block = (seqs_per_step, 2, SSD_GROUP_HEADS, SSD_HEAD_DIM, SSD_STATE)
    if has_h0:
        in_specs.append(pl.BlockSpec(state_block, lambda b, g: (b, 0, g, 0, 0)))
        args.append(h0)
    out_specs = [pl.BlockSpec((rows, SSD_GROUP_CH), lambda b, g: (b, g))]
    out_shape = [jax.ShapeDtypeStruct((n_seq * seq_len, e), ACT)]
    if want_final:
        out_specs.append(pl.BlockSpec(state_block, lambda b, g: (b, 0, g, 0, 0)))
        out_shape.append(jax.ShapeDtypeStruct((n_seq, 2, e // SSD_HEAD_DIM, SSD_HEAD_DIM, SSD_STATE), F32))
    width = SSD_GROUP_CH + 2 * SSD_STATE
    nc = seq_len // CHUNK

    return pl.pallas_call(
        functools.partial(_ssd_body, seq_len=seq_len, seqs_per_step=seqs_per_step, has_h0=has_h0,
                          want_final=want_final),
        grid=(n_seq // seqs_per_step, groups),
        in_specs=in_specs,
        out_specs=out_specs,
        out_shape=out_shape,
        scratch_shapes=[pltpu.VMEM((seq_len + CONV_WIN - CHUNK, width), ACT),
                        pltpu.VMEM((CONV_W - 1, CHUNK, CONV_WIN), BF16),
                        pltpu.VMEM((seq_len, SSD_GROUP_CH), BF16),
                        pltpu.VMEM((nc, SSD_GROUP_CH, CHUNK), F32),
                        pltpu.VMEM((seq_len, SSD_STATE), BF16),
                        pltpu.VMEM((nc, SSD_STATE, CHUNK), BF16),
                        pltpu.VMEM((seq_len, SSD_STATE), BF16),
                        pltpu.VMEM((seq_len, LANES), F32),
                        pltpu.VMEM((nc, 2 * SSD_GROUP_HEADS, CHUNK), F32),
                        pltpu.VMEM((nc, 2 * SSD_GROUP_HEADS, CHUNK), F32),
                        pltpu.VMEM((nc, 2 * SSD_GROUP_HEADS, CHUNK), F32),
                        pltpu.VMEM((seq_len, SSD_GROUP_CH), F32),
                        pltpu.VMEM((SSD_GROUP_CH, SSD_STATE), F32),
                        pltpu.VMEM((SSD_GROUP_CH, SSD_STATE), F32)],
        compiler_params=_params("parallel", "parallel"),
        name=name,
    )(*args)


def _group_lanes(v, groups):
    per_group = v.reshape(2, groups, SSD_GROUP_HEADS).transpose(1, 0, 2).reshape(groups, 2 * SSD_GROUP_HEADS)
    return jnp.pad(per_group, ((0, 0), (0, LANES - 2 * SSD_GROUP_HEADS))).reshape(1, groups * LANES)


def _gmlp_body(u_ref, v_ref, lng_ref, lnb_ref, ws_ref, bs_ref, o_ref, *, tm):
    groups = ws_ref.shape[0]
    gch = u_ref.shape[1] // groups
    for c in range(tm // CHUNK):
        rows = slice(c * CHUNK, (c + 1) * CHUNK)
        v = _gelu(v_ref[rows, :].astype(F32))
        vc = v - jnp.mean(v, axis=-1, keepdims=True)
        vn = vc * lax.rsqrt(jnp.mean(vc * vc, axis=-1, keepdims=True) + EPS) * lng_ref[...] + lnb_ref[...]
        vb = vn.astype(BF16)
        for g in range(groups):
            cols = slice(g * gch, (g + 1) * gch)
            s = _dot(ws_ref[g].astype(BF16), vb[:, cols]) + bs_ref[:, cols]
            o_ref[rows, cols] = (_gelu(u_ref[rows, cols].astype(F32)) * s).astype(o_ref.dtype)


def _gmlp(u, ln_g, ln_b, w_s, b_lanes, *, tm=256):
    t = u.shape[0]
    e = ln_g.shape[1]
    return pl.pallas_call(
        functools.partial(_gmlp_body, tm=tm),
        grid=(t // tm,),
        in_specs=[pl.BlockSpec((tm, e), lambda i: (i, 0)),
                  pl.BlockSpec((tm, e), lambda i: (i, 1)),
                  pl.BlockSpec((1, e), lambda i: (0, 0)),
                  pl.BlockSpec((1, e), lambda i: (0, 0)),
                  pl.BlockSpec(w_s.shape, lambda i: (0, 0, 0)),
                  pl.BlockSpec((CHUNK, e), lambda i: (0, 0))],
        out_specs=pl.BlockSpec((tm, e), lambda i: (i, 0)),
        out_shape=jax.ShapeDtypeStruct((t, e), ACT),
        compiler_params=_params("parallel"),
        name="gmlp",
    )(u, u, ln_g, ln_b, w_s, b_lanes)


def _s5_body(lr_ref, li_ref, ls_ref, brt_ref, bit_ref, cre_ref, cim_ref, h0_ref, u_ref, y_ref, fin_ref,
             win_s, t_s, ef_s, eb_s, z_s, a_s, zin_s, d_s, spf_s, spb_s, yo_s, uf_s,
             *, seqs, l_ctx, n_lat, l_lat, ctx_parts):
    bst = S5_BLOCK_ST
    ng = LANES // S5_GROUP
    kw = S5_T * LANES
    part = pl.program_id(1)

    @pl.when(part == 0)
    def _build():
        own = (lax.broadcasted_iota(jnp.int32, (ng, S5_GROUP, bst), 2) // S5_STATE
               == lax.broadcasted_iota(jnp.int32, (ng, S5_GROUP, bst), 0))

        def spread(v):
            return jnp.where(own, v[None], 0.0).reshape(LANES, bst).astype(BF16)

        tau = lax.broadcasted_iota(jnp.int32, (S5_T + 8, 1), 0).astype(F32)
        e_refs = (ef_s, eb_s)
        for d in range(2):
            lr, li = lr_ref[d], li_ref[d]
            step = jnp.exp(ls_ref[d])
            mag = jnp.exp(tau * (lr * step))
            p_re = mag * jnp.cos(tau * (li * step))
            p_im = mag * jnp.sin(tau * (li * step))
            ab_re, ab_im = p_re[1:2], p_im[1:2]
            den = lr * lr + li * li
            nr = ab_re - 1.0
            cr = (nr * lr + ab_im * li) / den
            ci = (ab_im * lr - nr * li) / den
            brt, bit = brt_ref[d], bit_ref[d]
            bb_re = cr * brt - ci * bit
            bb_im = cr * bit + ci * brt
            cre, cim = cre_ref[d], cim_ref[d]
            for k in range(S5_T):
                rows = slice(k * LANES, (k + 1) * LANES)
                tq = S5_T - 1 - k if d == 0 else k
                te = k + 1 if d == 0 else S5_T - k
                pr, pi = p_re[tq:tq + 1], p_im[tq:tq + 1]
                win_s[rows, 2 * d * bst:(2 * d + 1) * bst] = spread(pr * bb_re - pi * bb_im)
                win_s[rows, (2 * d + 1) * bst:(2 * d + 2) * bst] = spread(pr * bb_im + pi * bb_re)
                pr, pi = p_re[te:te + 1], p_im[te:te + 1]
                e_refs[d][rows, 0:bst] = spread(cre * pr - cim * pi)
                e_refs[d][rows, bst:2 * bst] = spread(-(cre * pi + cim * pr))
            c_own = jnp.concatenate([spread(cre), spread(-cim)], axis=1)
            z = _dot_nt(win_s[:, 2 * d * bst:(2 * d + 2) * bst], c_own)
            if d == 0:
                z_s[0:kw, :] = z
            else:
                z_s[kw - LANES:kw, :] += z[0:LANES]
                z_s[kw:2 * kw - LANES, :] = z[LANES:kw]
            a_s[2 * d:2 * d + 1, :] = p_re[S5_T:S5_T + 1]
            a_s[2 * d + 1:2 * d + 2, :] = p_im[S5_T:S5_T + 1]
        for k in range(S5_T):
            r0 = (S5_T - 1 - k) * LANES
            t_s[:, k * LANES:(k + 1) * LANES] = z_s[r0:r0 + kw, :].astype(BF16)

    def outputs():
        yo_s[...] = (_dot(zin_s[...], t_s[...]) + _dot_nt(spf_s[...].astype(BF16), ef_s[...])
                     + _dot_nt(spb_s[...].astype(BF16), eb_s[...]))

    @pl.when(part < ctx_parts)
    def _context():
        nj = l_ctx // S5_T
        uf_s[...] = u_ref[...].astype(F32)
        for j in range(nj):
            for k in range(S5_T):
                zin_s[j * seqs:(j + 1) * seqs, k * LANES:(k + 1) * LANES] = (
                    uf_s[pl.ds(j * S5_T + k, seqs, stride=l_ctx), :].astype(BF16))
        d_s[...] = _dot(zin_s[...], win_s[...])
        coef = [jnp.broadcast_to(a_s[i:i + 1, :], (seqs, bst)) for i in range(4)]
        zero = jnp.zeros((seqs, bst), F32)

        def step(j, carry):
            fr, fi, br, bi = carry
            rf = pl.ds(pl.multiple_of(j * seqs, seqs), seqs)
            rb = pl.ds(pl.multiple_of((nj - 1 - j) * seqs, seqs), seqs)
            spf_s[rf, 0:bst] = fr
            spf_s[rf, bst:2 * bst] = fi
            spb_s[rb, 0:bst] = br
            spb_s[rb, bst:2 * bst] = bi
            return (coef[0] * fr - coef[1] * fi + d_s[rf, 0:bst],
                    coef[0] * fi + coef[1] * fr + d_s[rf, bst:2 * bst],
                    coef[2] * br - coef[3] * bi + d_s[rb, 2 * bst:3 * bst],
                    coef[2] * bi + coef[3] * br + d_s[rb, 3 * bst:4 * bst])

        fr, fi, br, bi = lax.fori_loop(0, nj, step, (zero, zero, zero, zero))
        fin_ref[0, 0] = fr
        fin_ref[0, 1] = fi
        fin_ref[1, 0] = br
        fin_ref[1, 1] = bi
        outputs()
        for j in range(nj):
            for k in range(S5_T):
                y_ref[pl.ds(j * S5_T + k, seqs, stride=l_ctx), :] = (
                    yo_s[j * seqs:(j + 1) * seqs, k * LANES:(k + 1) * LANES])

    @pl.when(part == ctx_parts)
    def _latent():
        nj = l_lat // S5_T
        uf_s[...] = u_ref[...].astype(F32)
        for b in range(n_lat):
            for k in range(S5_T):
                zin_s[b * nj:(b + 1) * nj, k * LANES:(k + 1) * LANES] = (
                    uf_s[pl.ds(b * l_lat + k, nj, stride=S5_T), :].astype(BF16))
        d_s[...] = _dot(zin_s[...], win_s[...])
        coef_re = jnp.concatenate([jnp.broadcast_to(a_s[0:1, :], (n_lat, bst)),
                                   jnp.broadcast_to(a_s[2:3, :], (n_lat, bst))], axis=0)
        coef_im = jnp.concatenate([jnp.broadcast_to(a_s[1:2, :], (n_lat, bst)),
                                   jnp.broadcast_to(a_s[3:4, :], (n_lat, bst))], axis=0)
        s_re0 = jnp.concatenate([h0_ref[0, 0], h0_ref[1, 0]], axis=0)
        s_im0 = jnp.concatenate([h0_ref[0, 1], h0_ref[1, 1]], axis=0)

        def step(j, carry):
            s_re, s_im = carry
            d_re, d_im = [], []
            for b in range(n_lat):
                row = pl.ds(b * nj + j, 1)
                spf_s[row, 0:bst] = s_re[b:b + 1]
                spf_s[row, bst:2 * bst] = s_im[b:b + 1]
                d_re.append(d_s[row, 0:bst])
                d_im.append(d_s[row, bst:2 * bst])
            for b in range(n_lat):
                row = pl.ds(b * nj + nj - 1 - j, 1)
                spb_s[row, 0:bst] = s_re[n_lat + b:n_lat + b + 1]
                spb_s[row, bst:2 * bst] = s_im[n_lat + b:n_lat + b + 1]
                d_re.append(d_s[row, 2 * bst:3 * bst])
                d_im.append(d_s[row, 3 * bst:4 * bst])
            d_re = jnp.concatenate(d_re, axis=0)
            d_im = jnp.concatenate(d_im, axis=0)
            return coef_re * s_re - coef_im * s_im + d_re, coef_re * s_im + coef_im * s_re + d_im

        lax.fori_loop(0, nj, step, (s_re0, s_im0))
        outputs()
        for b in range(n_lat):
            for k in range(S5_T):
                y_ref[pl.ds(b * l_lat + k, nj, stride=S5_T), :] = yo_s[b * nj:(b + 1) * nj, k * LANES:(k + 1) * LANES]


def _s5_core(prm, h0, u, *, n_ctx, l_ctx, n_lat, l_lat, e):
    blocks = prm[0].shape[0]
    part_tokens = n_lat * l_lat
    t = u.shape[0]
    n_parts = t // part_tokens
    seqs = part_tokens // l_ctx
    r = part_tokens // S5_T
    vec = pl.BlockSpec((None, 2, 1, S5_BLOCK_ST), lambda g, p: (g, 0, 0, 0))
    mat = pl.BlockSpec((None, 2, S5_GROUP, S5_BLOCK_ST), lambda g, p: (g, 0, 0, 0))
    k_in = S5_T * LANES
    return pl.pallas_call(
        functools.partial(_s5_body, seqs=seqs, l_ctx=l_ctx, n_lat=n_lat, l_lat=l_lat, ctx_parts=n_parts - 1),
        grid=(blocks, n_parts),
        in_specs=[vec, vec, vec, mat, mat, mat, mat,
                  pl.BlockSpec((None, 2, 2, n_lat, S5_BLOCK_ST), lambda g, p: (g, 0, 0, 0, 0)),
                  pl.BlockSpec((part_tokens, LANES), lambda g, p: (p, g))],
        out_specs=[pl.BlockSpec((part_tokens, LANES), lambda g, p: (p, g)),
                   pl.BlockSpec((None, 2, 2, seqs, S5_BLOCK_ST),
                                lambda g, p: (g, 0, 0, jnp.minimum(p, n_parts - 2), 0))],
        out_shape=[jax.ShapeDtypeStruct((t, e), F32),
                   jax.ShapeDtypeStruct((blocks, 2, 2, n_ctx, S5_BLOCK_ST), F32)],
        scratch_shapes=[pltpu.VMEM((k_in, 4 * S5_BLOCK_ST), BF16),
                        pltpu.VMEM((k_in, k_in), BF16),
                        pltpu.VMEM((k_in, 2 * S5_BLOCK_ST), BF16),
                        pltpu.VMEM((k_in, 2 * S5_BLOCK_ST), BF16),
                        pltpu.VMEM(((2 * S5_T - 1) * LANES, LANES), F32),
                        pltpu.VMEM((8, S5_BLOCK_ST), F32),
                        pltpu.VMEM((r, k_in), BF16),
                        pltpu.VMEM((r, 4 * S5_BLOCK_ST), F32),
                        pltpu.VMEM((r, 2 * S5_BLOCK_ST), F32),
                        pltpu.VMEM((r, 2 * S5_BLOCK_ST), F32),
                        pltpu.VMEM((r, k_in), F32),
                        pltpu.VMEM((part_tokens, LANES), F32)],
        compiler_params=_params("parallel", "arbitrary"),
        name="s5_core",
    )(*prm, h0, u)


def _s5_glu_body(ys_ref, u_ref, dsk_ref, w_ref, b_ref, o_ref):
    y = _gelu(ys_ref[...] + dsk_ref[...] * u_ref[...].astype(F32))
    o_ref[...] = (y * jax.nn.sigmoid(_dot(y.astype(BF16), w_ref[...]) + b_ref[...])).astype(o_ref.dtype)


def _s5_glu(ys, u, dsk, w, b, *, tm=512):
    t, e = ys.shape
    return pl.pallas_call(
        _s5_glu_body,
        grid=(t // tm,),
        in_specs=[pl.BlockSpec((tm, e), lambda i: (i, 0)),
                  pl.BlockSpec((tm, e), lambda i: (i, 0)),
                  pl.BlockSpec((1, e), lambda i: (0, 0)),
                  pl.BlockSpec((e, e), lambda i: (0, 0)),
                  pl.BlockSpec((1, e), lambda i: (0, 0))],
        out_specs=pl.BlockSpec((tm, e), lambda i: (i, 0)),
        out_shape=jax.ShapeDtypeStruct((t, e), ACT),
        compiler_params=_params("parallel"),
        name="s5_glu",
    )(ys, u, dsk, w, b)


def _ctx_attn_body(q_ref, k_ref, v_ref, o_ref, ko_ref, vo_ref):
    scale = HEAD_DIM ** -0.5
    seq_len = q_ref.shape[0]
    first = lax.broadcasted_iota(jnp.int32, (seq_len, LANES), 1) < HEAD_DIM
    masks = (first, jnp.logical_not(first))
    n_pairs = q_ref.shape[1] // LANES
    logits, values = [], []
    for pair in range(n_pairs):
        cols = slice(pair * LANES, (pair + 1) * LANES)
        q2, k2, v2 = q_ref[:, cols].astype(F32) * scale, k_ref[:, cols], v_ref[:, cols]
        kb = k2.astype(BF16)
        values.append(v2.astype(BF16))
        for h in range(LANES // HEAD_DIM):
            ko_ref[2 * pair + h] = k2[:, h * HEAD_DIM:(h + 1) * HEAD_DIM].astype(F32)
            vo_ref[2 * pair + h] = v2[:, h * HEAD_DIM:(h + 1) * HEAD_DIM].astype(F32)
            logits.append(_dot_nt(jnp.where(masks[h], q2, 0.0).astype(BF16), kb))
    probs = []
    for s in logits:
        p = jnp.exp(s - jnp.max(s, axis=-1, keepdims=True))
        probs.append((p * (1.0 / jnp.sum(p, axis=-1, keepdims=True))).astype(BF16))
    for pair in range(n_pairs):
        o0, o1 = _dot(probs[2 * pair], values[pair]), _dot(probs[2 * pair + 1], values[pair])
        o_ref[:, pair * LANES:(pair + 1) * LANES] = jnp.where(first, o0, o1).astype(o_ref.dtype)


def _ctx_attention(u, *, n_seq, seq_len, e, width=512):
    t = n_seq * seq_len
    hp = e // width
    heads = e // HEAD_DIM
    kv_block = (None, width // HEAD_DIM, seq_len, HEAD_DIM)
    in_specs = [pl.BlockSpec((seq_len, width), lambda b, h: (b, h)),
                pl.BlockSpec((seq_len, width), lambda b, h: (b, hp + h)),
                pl.BlockSpec((seq_len, width), lambda b, h: (b, 2 * hp + h))]
    return pl.pallas_call(
        _ctx_attn_body,
        grid=(n_seq, hp),
        in_specs=in_specs,
        out_specs=[pl.BlockSpec((seq_len, width), lambda b, h: (b, h)),
                   pl.BlockSpec(kv_block, lambda b, h: (b, h, 0, 0)),
                   pl.BlockSpec(kv_block, lambda b, h: (b, h, 0, 0))],
        out_shape=[jax.ShapeDtypeStruct((t, e), ACT),
                   jax.ShapeDtypeStruct((n_seq, heads, seq_len, HEAD_DIM), F32),
                   jax.ShapeDtypeStruct((n_seq, heads, seq_len, HEAD_DIM), F32)],
        compiler_params=_params("parallel", "parallel"),
        name="ctx_attention",
    )(u, u, u)


def _nat_bias_body(rp_ref, o_ref):
    n_rel_rows = rp_ref.shape[1]
    q = lax.broadcasted_iota(jnp.int32, (GRID_W, LANES), 0)
    lane = lax.broadcasted_iota(jnp.int32, (GRID_W, LANES), 1)
    kc = lane % GRID_W
    c_start = jnp.clip(q - WIN_COLS // 2, 0, GRID_W - WIN_COLS)
    ok = (kc >= c_start) & (kc < c_start + WIN_COLS)
    left = lane < GRID_W
    for h in range(o_ref.shape[0]):
        halves = []
        for i in range(n_rel_rows):
            row = jnp.broadcast_to(rp_ref[h, i:i + 1, :], (GRID_W, LANES))
            halves.append((pltpu.roll(row, 0, 1, stride=1, stride_axis=0),
                           pltpu.roll(row, GRID_W, 1, stride=1, stride_axis=0)))
        for i in range(n_rel_rows - 1):
            o_ref[h, i] = jnp.where(ok, jnp.where(left, halves[i][0], halves[i + 1][1]), MASKED)


def _nat_bias(rpb, *, heads_per_step=4):
    heads, n_rel_rows, n_rel_cols = rpb.shape
    rows = jnp.roll(jnp.pad(rpb, ((0, 0), (0, 0), (0, LANES - n_rel_cols))), -(WIN_COLS - 1), axis=-1)
    return pl.pallas_call(
        _nat_bias_body,
        grid=(heads // heads_per_step,),
        in_specs=[pl.BlockSpec((heads_per_step, n_rel_rows, LANES), lambda h: (h, 0, 0))],
        out_specs=pl.BlockSpec((heads_per_step, n_rel_rows - 1, GRID_W, 2 * GRID_W), lambda h: (h, 0, 0, 0)),
        out_shape=jax.ShapeDtypeStruct((heads, n_rel_rows - 1, GRID_W, 2 * GRID_W), F32),
        compiler_params=_params("parallel"),
        name="nat_bias",
    )(rows)


def _nat_body(q_ref, k_ref, v_ref, ck_ref, cv_ref, bias_ref, o_ref, ckb_s, cvb_s, *, rows):
    scale = HEAD_DIM ** -0.5
    wr = min(WIN_ROWS, rows)
    nw = wr * GRID_W
    ckb_s[...] = jnp.concatenate([ck_ref[0], ck_ref[1]], axis=1).astype(BF16)
    cvb_s[...] = jnp.concatenate([cv_ref[0], cv_ref[1]], axis=1).astype(BF16)
    first = lax.broadcasted_iota(jnp.int32, (GRID_W, LANES), 1) < HEAD_DIM

    heads = tuple(range(LANES // HEAD_DIM))
    masks = (first, jnp.logical_not(first))

    def row_group(g, carry):
        rws = [g * NAT_ROWS_PER_STEP + i for i in range(NAT_ROWS_PER_STEP)]
        starts = [jnp.clip(r - wr // 2, 0, rows - wr) for r in rws]
        q_rows = [pl.ds(pl.multiple_of(r * GRID_W, GRID_W), GRID_W) for r in rws]
        k_rows = [pl.ds(pl.multiple_of(rs * GRID_W, GRID_W), nw) for rs in starts]
        logits = []
        for r, rs, qr, kr in zip(rws, starts, q_rows, k_rows):
            q2 = q_ref[qr, :].astype(F32) * scale
            kb = k_ref[kr, :].astype(BF16)
            for h in heads:
                q = jnp.where(masks[h], q2, 0.0).astype(BF16)
                logits.append((_dot_nt(q, kb), _dot_nt(q, ckb_s[...])))
        probs = []
        for idx, (s_win, s_ctx) in enumerate(logits):
            r, rs, h = rws[idx // 2], starts[idx // 2], heads[idx % 2]
            i0 = (WIN_ROWS - 1) - (r - rs)
            s_win = s_win + jnp.concatenate([bias_ref[h, i0 + 2 * jj] for jj in range(wr // 2)], axis=1)
            m = jnp.maximum(jnp.max(s_win, axis=-1, keepdims=True), jnp.max(s_ctx, axis=-1, keepdims=True))
            p_win = jnp.exp(s_win - m)
            p_ctx = jnp.exp(s_ctx - m)
            inv = 1.0 / (jnp.sum(p_win, axis=-1, keepdims=True) + jnp.sum(p_ctx, axis=-1, keepdims=True))
            probs.append(((p_win * inv).astype(BF16), (p_ctx * inv).astype(BF16)))
        for i, (qr, kr) in enumerate(zip(q_rows, k_rows)):
            vb = v_ref[kr, :].astype(BF16)
            outs = [_dot(probs[2 * i + h][0], vb) + _dot(probs[2 * i + h][1], cvb_s[...]) for h in heads]
            o_ref[qr, :] = jnp.where(first, outs[0], outs[1]).astype(o_ref.dtype)
        return carry

    lax.fori_loop(0, rows // NAT_ROWS_PER_STEP, row_group, 0)


def _nat_attention(u, cache_k, cache_v, cache_layer, bias, *, n_seq, seq_len, row_block0, e):
    hp = e // LANES
    hpb = LANES // HEAD_DIM
    past = cache_k.shape[3]
    seq = lambda b: row_block0 + b
    cache_block = (None, None, hpb, past, HEAD_DIM)
    return pl.pallas_call(
        functools.partial(_nat_body, rows=seq_len // GRID_W),
        grid=(n_seq, hp),
        in_specs=[pl.BlockSpec((seq_len, LANES), lambda b, h: (seq(b), h)),
                  pl.BlockSpec((seq_len, LANES), lambda b, h: (seq(b), hp + h)),
                  pl.BlockSpec((seq_len, LANES), lambda b, h: (seq(b), 2 * hp + h)),
                  pl.BlockSpec(cache_block, lambda b, h: (b, cache_layer, h, 0, 0)),
                  pl.BlockSpec(cache_block, lambda b, h: (b, cache_layer, h, 0, 0)),
                  pl.BlockSpec((hpb,) + bias.shape[1:], lambda b, h: (h, 0, 0, 0))],
        out_specs=pl.BlockSpec((seq_len, LANES), lambda b, h: (b, h)),
        out_shape=jax.ShapeDtypeStruct((n_seq * seq_len, e), ACT),
        scratch_shapes=[pltpu.VMEM((past, LANES), BF16), pltpu.VMEM((past, LANES), BF16)],
        compiler_params=_params("parallel", "parallel"),
        name="nat_attention",
    )(u, u, u, cache_k, cache_v, bias)


def kernel(x_prompt, x_sample, state_ssd, state_s5, cache_k, cache_v, c, c_ctx, norm_g, w_mod, b_mod, w_out, final_g, ssd_w_in, ssd_conv_w, ssd_conv_b, ssd_dt_bias, ssd_a_log, ssd_d, ssd_norm_g, mlp_w_in, mlp_ln_g, mlp_ln_b, mlp_w_s, mlp_b_s, s5_w_in, s5_lam_re, s5_lam_im, s5_log_step, s5_b_re, s5_b_im, s5_c_re, s5_c_im, s5_d, s5_w_glu, s5_b_glu, nat_w_in, nat_rpb):
    n_ctx, l_ctx, d = x_prompt.shape
    n_lat, l_lat, _ = x_sample.shape
    t_ctx = n_ctx * l_ctx
    depth = norm_g.shape[0]
    e = w_out.shape[1]
    assert depth == 4 and l_lat % l_ctx == 0 and t_ctx % (n_lat * l_lat) == 0
    tiles = dict(tm=512, t_ctx=t_ctx, l_lat=l_lat)
    tiles_in = dict(tm=2048, tn=1024, t_ctx=t_ctx, l_lat=l_lat)

    t_lat = n_lat * l_lat
    x = (x_prompt.reshape(t_ctx, d), x_sample.reshape(t_lat, d))
    cond8 = jnp.concatenate([c_ctx[None], c, jnp.zeros((8 - 1 - n_lat, d), F32)], axis=0)
    mods = _modulation(cond8, w_mod, b_mod).reshape(depth, 8, 3, 1, d)
    w_out_b = w_out.astype(BF16)

    groups = e // SSD_GROUP_CH
    n_main = 3 * e
    w_dt = ssd_w_in[0][:, n_main:].reshape(d, 2, groups, SSD_GROUP_HEADS).transpose(0, 2, 1, 3)
    w_dt = jnp.pad(w_dt.reshape(d, groups, 2 * SSD_GROUP_HEADS), ((0, 0), (0, 0), (0, LANES - 2 * SSD_GROUP_HEADS)))
    tiles_two = dict(tiles_in, tm=1024)
    u = _inproj(x, mods, 0, norm_g[0:1], ssd_w_in[0].astype(BF16), n_out=n_main, name="ssd_in", **tiles_two)
    dt = _inproj(x, mods, 0, norm_g[0:1], w_dt.reshape(d, groups * LANES).astype(BF16), name="ssd_dt_in", out_dtype=F32,
                 **tiles_two)
    ssd_args = (ssd_conv_w[0], ssd_conv_b[0:1], _group_lanes(ssd_dt_bias[0], groups),
                _group_lanes(ssd_a_log[0], groups), jnp.repeat(ssd_d[0], SSD_HEAD_DIM)[None])
    y_ctx, new_ssd = _ssd_call(u, dt, *ssd_args, None, n_seq=n_ctx, seq_len=l_ctx, seqs_per_step=l_lat // l_ctx,
                               row_block0=0, want_final=True, name="ssd_ctx")
    y_lat, = _ssd_call(u, dt, *ssd_args, state_ssd[:, 0], n_seq=n_lat, seq_len=l_lat, seqs_per_step=1,
                       row_block0=t_ctx // l_lat, want_final=False, name="ssd_lat")
    x = _outproj((y_ctx, y_lat), u, 0, x, mods, 0, w_out_b[0], norm_g=ssd_norm_g[0:1], name="ssd_out", **tiles)

    u = _inproj(x, mods, 1, norm_g[1:2], mlp_w_in[0].astype(BF16), name="mlp_in", **tiles_in)
    b_lanes = jnp.repeat(mlp_b_s[0].T, e // mlp_b_s.shape[1], axis=1)
    y = _gmlp(u, mlp_ln_g[0:1], mlp_ln_b[0:1], mlp_w_s[0], b_lanes)
    x = _outproj(y, u, 2, x, mods, 1, w_out_b[1], name="mlp_out", **tiles)

    u = _inproj(x, mods, 2, norm_g[2:3], s5_w_in[0].astype(BF16), name="s5_in", **tiles_in)
    s5_groups = e // S5_GROUP
    bg = LANES // S5_GROUP
    blocks = s5_groups // bg

    def block_vec(v):
        return v.reshape(2, blocks, 1, S5_BLOCK_ST).transpose(1, 0, 2, 3)

    def block_mat_b(v):
        v = v.reshape(2, blocks, bg, S5_STATE, S5_GROUP).transpose(1, 0, 4, 2, 3)
        return v.reshape(blocks, 2, S5_GROUP, S5_BLOCK_ST)

    def block_mat_c(v):
        v = v.reshape(2, blocks, bg, S5_GROUP, S5_STATE).transpose(1, 0, 3, 2, 4)
        return v.reshape(blocks, 2, S5_GROUP, S5_BLOCK_ST)

    log_step = jnp.repeat(s5_log_step[0][:, :, None], S5_STATE, axis=2)
    prm = (block_vec(s5_lam_re[0]), block_vec(s5_lam_im[0]), block_vec(log_step), block_mat_b(s5_b_re[0]),
           block_mat_b(s5_b_im[0]), block_mat_c(s5_c_re[0]), block_mat_c(s5_c_im[0]))
    h0 = state_s5[:, 0].reshape(n_lat, 2, 2, blocks, S5_BLOCK_ST).transpose(3, 1, 2, 0, 4)
    ys, fin = _s5_core(prm, h0, u, n_ctx=n_ctx, l_ctx=l_ctx, n_lat=n_lat, l_lat=l_lat, e=e)
    new_s5 = fin.transpose(3, 1, 2, 0, 4).reshape(n_ctx, 1, 2, 2, s5_groups, S5_STATE)
    y = _s5_glu(ys, u, s5_d[0:1], s5_w_glu[0].astype(BF16), s5_b_glu[0:1])
    x = _outproj(y, u, 1, x, mods, 2, w_out_b[2], name="s5_out", **tiles)

    u = _inproj(x, mods, 3, norm_g[3:4], nat_w_in[0].astype(BF16), name="nat_in", **tiles_in)
    y_ctx, new_k, new_v = _ctx_attention(u, n_seq=n_ctx, seq_len=l_ctx, e=e)
    y_lat = _nat_attention(u, cache_k, cache_v, 0, _nat_bias(nat_rpb[0]), n_seq=n_lat, seq_len=l_lat,
                           row_block0=t_ctx // l_lat, e=e)
    out_ctx, out_lat = _outproj((y_ctx, y_lat), u, 3, x, mods, 3, w_out_b[3], final_g=final_g[None], split_rows=(t_ctx, t_lat),
                                name="nat_out", **tiles)

    return (out_ctx.reshape(n_ctx, l_ctx, d), out_lat.reshape(n_lat, l_lat, d),
            new_ssd[:, None], new_s5, new_k[:, None], new_v[:, None])
```

```python
import collections
import functools

import jax
import jax.numpy as jnp
from jax import lax
from jax.experimental import pallas as pl
from jax.experimental.pallas import tpu as pltpu

F32 = jnp.float32
BF16 = jnp.bfloat16
ACT = jnp.bfloat16
EPS = 1e-6
HIGHEST = lax.Precision.HIGHEST

LANES = 128
CHUNK = 128
SSD_HEAD_DIM = 64
SSD_GROUP_HEADS = 4
SSD_GROUP_CH = SSD_HEAD_DIM * SSD_GROUP_HEADS
SSD_STATE = 128
CONV_W = 5
CONV_HALO = 16
SSD_INTERLEAVE = 2
CONV_WIN = 2 * CHUNK
S5_T = 8
S5_GROUP = 16
S5_STATE = 64
S5_BLOCK_ST = (LANES // S5_GROUP) * S5_STATE
HEAD_DIM = 64
GRID_W = 64
WIN_ROWS = 8
WIN_COLS = 16
NAT_ROWS_PER_STEP = 8
MASKED = -1e30
VMEM_LIMIT = 52 * 1024 * 1024


def _silu(x):
    return x * jax.nn.sigmoid(x)


def _gelu(x):
    return 0.5 * x * (1.0 + jnp.tanh(0.7978845608028654 * (x + 0.044715 * (x * x * x))))


def _softplus(x):
    return jnp.maximum(x, 0.0) + jnp.log1p(jnp.exp(-jnp.abs(x)))


def _dot(a, b):
    return jnp.dot(a, b, preferred_element_type=F32)


def _dot_nt(a, b):
    return lax.dot_general(a, b, (((1,), (1,)), ((), ())), preferred_element_type=F32)


def _cumsum_rows(lower_tri, x):
    hi = x.astype(BF16)
    r1 = x - hi.astype(F32)
    mid = r1.astype(BF16)
    lo = (r1 - mid.astype(F32)).astype(BF16)
    return _dot(lower_tri, hi) + _dot(lower_tri, mid) + _dot(lower_tri, lo)


def _params(*sem):
    return pltpu.CompilerParams(dimension_semantics=sem, vmem_limit_bytes=VMEM_LIMIT)


def _mod_row(i, *, tm, t_ctx, l_lat):
    start = i * tm
    return jnp.where(start < t_ctx, 0, 1 + (start - t_ctx) // l_lat)


def _mod_body(c_ref, w_ref, b_ref, o_ref):
    cond = _silu(c_ref[...]).astype(BF16)
    o_ref[...] = _dot(cond, w_ref[...].astype(BF16)) + b_ref[...]


def _modulation(cond8, w_mod, b_mod, *, tn=1024):
    depth, d, n = w_mod.shape
    return pl.pallas_call(
        _mod_body,
        grid=(depth, n // tn),
        in_specs=[pl.BlockSpec((8, d), lambda l, j: (0, 0)),
                  pl.BlockSpec((None, d, tn), lambda l, j: (l, 0, j)),
                  pl.BlockSpec((None, 1, tn), lambda l, j: (l, 0, j))],
        out_specs=pl.BlockSpec((None, 8, tn), lambda l, j: (l, 0, j)),
        out_shape=jax.ShapeDtypeStruct((depth, 8, n), F32),
        compiler_params=_params("parallel", "parallel"),
        name="modulation",
    )(cond8, w_mod, b_mod.reshape(depth, 1, n))


def _stream_specs(x, tm, width, n_grid):
    ids = (lambda i, *_: i) if n_grid == 1 else (lambda i, j: i)
    if not isinstance(x, tuple):
        return [x], [pl.BlockSpec((tm, width), lambda *g: (ids(*g), 0))], None
    split = x[0].shape[0] // tm
    last = x[1].shape[0] // tm - 1
    return (list(x),
            [pl.BlockSpec((tm, width), lambda *g: (jnp.minimum(ids(*g), split - 1), 0)),
             pl.BlockSpec((tm, width), lambda *g: (jnp.clip(ids(*g) - split, 0, last), 0))],
            split)


def _inproj_body(*refs, split):
    n_x = 1 if split is None else 2
    x_refs = refs[:n_x]
    shift_ref, scale_ref, g_ref, w_ref, o_ref, h_ref = refs[n_x:]

    def normalise(x_ref):
        x = x_ref[...]
        y = x * lax.rsqrt(jnp.mean(x * x, axis=-1, keepdims=True) + EPS) * g_ref[...]
        h_ref[...] = (y * (1.0 + scale_ref[...]) + shift_ref[...]).astype(BF16)

    first_col = pl.program_id(1) == 0
    if split is None:
        pl.when(first_col)(lambda: normalise(x_refs[0]))
    else:
        in_ctx = pl.program_id(0) < split
        pl.when(first_col & in_ctx)(lambda: normalise(x_refs[0]))
        pl.when(first_col & jnp.logical_not(in_ctx))(lambda: normalise(x_refs[1]))
    o_ref[...] = _dot(h_ref[...], w_ref[...]).astype(o_ref.dtype)


def _inproj(x, mods, layer, g, w, *, tm, tn, t_ctx, l_lat, name, out_dtype=ACT, n_out=None):
    d = w.shape[0]
    n = w.shape[1] if n_out is None else n_out
    row = functools.partial(_mod_row, tm=tm, t_ctx=t_ctx, l_lat=l_lat)
    x_args, x_specs, split = _stream_specs(x, tm, d, 2)
    t = sum(a.shape[0] for a in x_args)

    def mod_spec(part):
        return pl.BlockSpec((None, None, None, 1, d), lambda i, j: (layer, row(i), part, 0, 0))

    return pl.pallas_call(
        functools.partial(_inproj_body, split=split),
        grid=(t // tm, n // tn),
        in_specs=x_specs + [mod_spec(0), mod_spec(1),
                            pl.BlockSpec((1, d), lambda i, j: (0, 0)),
                            pl.BlockSpec((d, tn), lambda i, j: (0, j))],
        out_specs=pl.BlockSpec((tm, tn), lambda i, j: (i, j)),
        out_shape=jax.ShapeDtypeStruct((t, n), out_dtype),
        scratch_shapes=[pltpu.VMEM((tm, d), BF16)],
        compiler_params=_params("parallel", "arbitrary"),
        name=name,
    )(*x_args, mods, mods, g, w)


def _outproj_body(*refs, n_y, n_x, split, gated_norm, final_norm):
    refs = list(refs)
    y_refs, z_ref, x_refs = refs[:n_y], refs[n_y], refs[n_y + 1:n_y + 1 + n_x]
    rest = refs[n_y + 1 + n_x:]
    gate_ref, w_ref = rest[:2]
    rest = rest[2:]
    ng_ref = rest.pop(0) if gated_norm else None
    fg_ref = rest.pop(0) if final_norm else None
    o_refs = rest

    def emit(y_ref, x_ref, o_ref):
        t = y_ref[...].astype(F32) * _silu(z_ref[...].astype(F32))
        if gated_norm:
            t = t * lax.rsqrt(jnp.mean(t * t, axis=-1, keepdims=True) + EPS) * ng_ref[...]
        xn = x_ref[...] + gate_ref[...] * _dot(t.astype(BF16), w_ref[...])
        if final_norm:
            xn = xn * lax.rsqrt(jnp.mean(xn * xn, axis=-1, keepdims=True) + EPS) * fg_ref[...]
        o_ref[...] = xn

    if split is None:
        emit(y_refs[0], x_refs[0], o_refs[0])
    else:
        in_ctx = pl.program_id(0) < split
        pl.when(in_ctx)(lambda: emit(y_refs[0], x_refs[0], o_refs[0]))
        pl.when(jnp.logical_not(in_ctx))(lambda: emit(y_refs[-1], x_refs[-1], o_refs[-1]))


def _outproj(y, u, z_block, x, mods, layer, w, *, norm_g=None, final_g=None, split_rows=None, tm, t_ctx, l_lat,
             name):
    t, e = u.shape[0], w.shape[0]
    d = w.shape[1]
    row = functools.partial(_mod_row, tm=tm, t_ctx=t_ctx, l_lat=l_lat)
    y_args, y_specs, split_y = _stream_specs(y, tm, e, 1)
    x_args, x_specs, split_in = _stream_specs(x, tm, d, 1)
    in_specs = (y_specs + [pl.BlockSpec((tm, e), lambda i: (i, z_block))] + x_specs
                + [pl.BlockSpec((None, None, None, 1, d), lambda i: (layer, row(i), 2, 0, 0)),
                   pl.BlockSpec((e, d), lambda i: (0, 0))])
    args = y_args + [u] + x_args + [mods, w]
    if norm_g is not None:
        in_specs.append(pl.BlockSpec((1, e), lambda i: (0, 0)))
        args.append(norm_g)
    if final_g is not None:
        in_specs.append(pl.BlockSpec((1, d), lambda i: (0, 0)))
        args.append(final_g)
    if split_rows is None:
        split_out = None
        out_specs = pl.BlockSpec((tm, d), lambda i: (i, 0))
        out_shape = jax.ShapeDtypeStruct((t, d), F32)
    else:
        split_out = split_rows[0] // tm
        last = split_rows[1] // tm - 1
        out_specs = [pl.BlockSpec((tm, d), lambda i: (jnp.minimum(i, split_out - 1), 0)),
                     pl.BlockSpec((tm, d), lambda i: (jnp.clip(i - split_out, 0, last), 0))]
        out_shape = [jax.ShapeDtypeStruct((rows, d), F32) for rows in split_rows]
    return pl.pallas_call(
        functools.partial(_outproj_body, n_y=len(y_args), n_x=len(x_args),
                          split=next((s for s in (split_y, split_in, split_out) if s is not None), None),
                          gated_norm=norm_g is not None, final_norm=final_g is not None),
        grid=(t // tm,),
        in_specs=in_specs,
        out_specs=out_specs,
        out_shape=out_shape,
        compiler_params=_params("arbitrary"),
        name=name,
    )(*args)


def _ssd_body_one_at_a_time(*refs, seq_len, seqs_per_step, has_h0, want_final):
    refs = list(refs)
    x_ref, b_ref, c_ref, dt_ref, wx_ref, wb_ref, wc_ref, bx_ref, bb_ref, bc_ref, dtb_ref, alog_ref, dsk_ref = refs[:13]
    rest = refs[13:]
    h0_ref = rest.pop(0) if has_h0 else None
    if want_final:
        y_ref, hf_ref = rest[0], rest[1]
        rest = rest[2:]
    else:
        y_ref, hf_ref = rest[0], None
        rest = rest[1:]
    pad_s, shift_s, xbd_s, xt_s, bm_s, bt_s, cm_s, e_s, et_s, ct_s, dtt_s, y_s, sf_s, sb_s = rest
    nc = seq_len // CHUNK
    width = SSD_GROUP_CH + 2 * SSD_STATE
    nh = SSD_GROUP_HEADS

    pad_s[0:CONV_HALO, :] = jnp.zeros((CONV_HALO, width), ACT)
    tail = CONV_WIN - CHUNK - CONV_HALO
    pad_s[CONV_HALO + seq_len:CONV_HALO + seq_len + tail, :] = jnp.zeros((tail, width), ACT)
    win_row = lax.broadcasted_iota(jnp.int32, (CHUNK, CONV_WIN), 1)
    tok_row = lax.broadcasted_iota(jnp.int32, (CHUNK, CONV_WIN), 0)
    taps = [j for j in range(CONV_W) if j != CONV_W // 2]
    for n, j in enumerate(taps):
        shift_s[n] = (win_row == tok_row + (CONV_HALO + j - CONV_W // 2)).astype(BF16)

    def copy_chunk(off, ci, carry):
        base = pl.multiple_of(ci * CHUNK, CHUNK)
        src = pl.ds(off + base, CHUNK)
        dst = pl.ds(base + CONV_HALO, CHUNK)
        pad_s[dst, 0:SSD_GROUP_CH] = x_ref[src, :]
        pad_s[dst, SSD_GROUP_CH:SSD_GROUP_CH + SSD_STATE] = b_ref[src, :]
        pad_s[dst, SSD_GROUP_CH + SSD_STATE:width] = c_ref[src, :]
        return carry

    conv_w = jnp.concatenate([wx_ref[...], wb_ref[...], wc_ref[...]], axis=1)
    conv_b = jnp.concatenate([bx_ref[...], bb_ref[...], bc_ref[...]], axis=1)
    a_row = -jnp.exp(alog_ref[...])
    row_i = lax.broadcasted_iota(jnp.int32, (CHUNK, CHUNK), 0)
    col_i = lax.broadcasted_iota(jnp.int32, (CHUNK, CHUNK), 1)
    lower = row_i >= col_i
    upper = row_i <= col_i
    lower_b = lower.astype(BF16)
    fwd_lane = lax.broadcasted_iota(jnp.int32, (1, LANES), 1) < nh

    def prep_chunks(off, pi, carry):
        cis = [pi * SSD_PREP_CHUNKS + n for n in range(SSD_PREP_CHUNKS)]
        rows = [pl.ds(pl.multiple_of(ci * CHUNK, CHUNK), CHUNK) for ci in cis]
        mid = CONV_W // 2
        half = width // 2
        units = [(n, h) for n in range(SSD_PREP_CHUNKS) for h in range(2)]

        def window(unit):
            n, h = unit
            return pad_s[pl.ds(pl.multiple_of(cis[n] * CHUNK, CHUNK), CONV_WIN), h * half:(h + 1) * half]

        def shift_dots(unit):
            win = window(unit)
            return [_dot(shift_s[t], win) for t in range(len(taps))]

        def finish(unit, shifted):
            n, h = unit
            ci, r = cis[n], rows[n]
            cols = slice(h * half, (h + 1) * half)
            acc = conv_b[:, cols] + window(unit)[CONV_HALO:CONV_HALO + CHUNK].astype(F32) * conv_w[mid:mid + 1, cols]
            for t, j in enumerate(taps):
                acc = acc + shifted[t] * conv_w[j:j + 1, cols]
            v = _silu(acc)
            if h == 0:
                y_s[r, :] = v * dsk_ref[...]
                vb = v.astype(BF16)
                for hd in range(nh):
                    xbd_s[ci, hd * CHUNK:(hd + 1) * CHUNK, :] = jnp.where(head_of_lane == hd, vb, jnp.zeros_like(vb))
                xt_s[ci] = v.T
            else:
                bm = v[:, 0:SSD_STATE]
                bm_s[r, :] = bm.astype(BF16)
                bt_s[ci] = bm.T.astype(BF16)
                cm_s[r, :] = v[:, SSD_STATE:half].astype(BF16)

        dts = [_softplus(dt_ref[pl.ds(off + pl.multiple_of(ci * CHUNK, CHUNK), CHUNK), :] + dtb_ref[...]) for ci in cis]
        das = [dt * a_row for dt in dts]
        pending = shift_dots(units[0])
        cums = []
        for i, unit in enumerate(units):
            following = shift_dots(units[i + 1]) if i + 1 < len(units) else None
            if unit[1] == 0:
                cums.append(_cumsum_rows(lower_b, das[unit[0]]))
            finish(unit, pending)
            pending = following
        for ci, r, dt, da, cum in zip(cis, rows, dts, das, cums):
            e = jnp.where(fwd_lane, cum, cum - da)
            e_s[r, :] = e
            et_s[ci] = e.T[0:2 * nh]
            ct_s[ci] = cum.T[0:2 * nh]
            dtt_s[ci] = dt.T[0:2 * nh]
        return carry

    head_of_lane = lax.broadcasted_iota(jnp.int32, (CHUNK, SSD_GROUP_CH), 1) // SSD_HEAD_DIM
    st_refs = (sf_s, sb_s)

    def out_exponent(e, col):
        return jnp.broadcast_to(e[:, col:col + 1], (CHUNK, CHUNK))

    def diag_chunks(pi, carry):
        cis = [pi * SSD_PREP_CHUNKS + n for n in range(SSD_PREP_CHUNKS)]
        rows = [pl.ds(pl.multiple_of(ci * CHUNK, CHUNK), CHUNK) for ci in cis]
        gs = [_dot(cm_s[r, :], bt_s[ci]) for ci, r in zip(cis, rows)]
        for ci, r, g in zip(cis, rows, gs):
            e = e_s[r, :]
            e_t, log_dt = et_s[ci], jnp.log(dtt_s[ci])
            mixes = []
            for h in range(nh):
                ec_f, ec_b = out_exponent(e, h), out_exponent(e, nh + h)
                er_f = e_t[h:h + 1, :] - log_dt[h:h + 1, :]
                er_b = e_t[nh + h:nh + h + 1, :] + log_dt[nh + h:nh + h + 1, :]
                decay = (jnp.exp(jnp.where(lower, ec_f - er_f, -jnp.inf))
                         + jnp.exp(jnp.where(upper, er_b - ec_b, -jnp.inf)))
                mixes.append((g * decay).astype(BF16))
            y_s[r, :] += _dot(jnp.concatenate(mixes, axis=1), xbd_s[ci])
        return carry

    def scan_chunk(i, carry):
        chunks = (i, nc - 1 - i)
        rows = [pl.ds(pl.multiple_of(ci * CHUNK, CHUNK), CHUNK) for ci in chunks]
        s_prev = [st_refs[d][...] for d in range(2)]
        cs = [_dot_nt(cm_s[rows[d], :], s_prev[d].astype(BF16)) for d in range(2)]
        w_rows, keeps = [], []
        for d in range(2):
            e = e_s[rows[d], :]
            e_t, c_t, dt_t = et_s[chunks[d]], ct_s[chunks[d]], dtt_s[chunks[d]]
            w_d, keep_d, offs = [], [], []
            for r in range(nh):
                col = r + nh * d
                ec = out_exponent(e, col)
                er = e_t[col:col + 1, :]
                dt_r = dt_t[col:col + 1, :]
                tot = c_t[col:col + 1, CHUNK - 1:CHUNK]
                if d == 0:
                    offs.append(jnp.exp(ec))
                    w_d.append(jnp.broadcast_to(dt_r * jnp.exp(tot - er), (SSD_HEAD_DIM, CHUNK)))
                else:
                    offs.append(jnp.exp(tot - ec))
                    w_d.append(jnp.broadcast_to(dt_r * jnp.exp(er), (SSD_HEAD_DIM, CHUNK)))
                keep_d.append(jnp.broadcast_to(jnp.exp(tot), (SSD_HEAD_DIM, SSD_STATE)))
            off = jnp.concatenate([offs[nh - 1]] * (SSD_GROUP_CH // CHUNK), axis=1)
            for r in range(nh - 2, -1, -1):
                off = jnp.where(head_of_lane == r, jnp.concatenate([offs[r]] * (SSD_GROUP_CH // CHUNK), axis=1), off)
            y_s[rows[d], :] += cs[d] * off
            w_rows.append(jnp.concatenate(w_d, axis=0))
            keeps.append(jnp.concatenate(keep_d, axis=0))
        for d in range(2):
            w_t = (xt_s[chunks[d]] * w_rows[d]).astype(BF16)
            st_refs[d][...] = keeps[d] * s_prev[d] + _dot(w_t, bm_s[rows[d], :])
        return carry

    def emit_chunk(off, ci, carry):
        base = pl.multiple_of(ci * CHUNK, CHUNK)
        y_ref[pl.ds(off + base, CHUNK), :] = y_s[pl.ds(base, CHUNK), :].astype(y_ref.dtype)
        return carry

    def one_sequence(s, carry):
        off = pl.multiple_of(s * seq_len, CHUNK)
        lax.fori_loop(0, nc, functools.partial(copy_chunk, off), 0)
        lax.fori_loop(0, nc // SSD_PREP_CHUNKS, functools.partial(prep_chunks, off), 0)
        lax.fori_loop(0, nc // SSD_PREP_CHUNKS, diag_chunks, 0)
        if has_h0:
            sf_s[...] = h0_ref[s, 0].reshape(SSD_GROUP_CH, SSD_STATE)
            sb_s[...] = h0_ref[s, 1].reshape(SSD_GROUP_CH, SSD_STATE)
        else:
            sf_s[...] = jnp.zeros((SSD_GROUP_CH, SSD_STATE), F32)
            sb_s[...] = jnp.zeros((SSD_GROUP_CH, SSD_STATE), F32)
        lax.fori_loop(0, nc, scan_chunk, 0)
        lax.fori_loop(0, nc, functools.partial(emit_chunk, off), 0)
        if want_final:
            hf_ref[s, 0] = sf_s[...].reshape(SSD_GROUP_HEADS, SSD_HEAD_DIM, SSD_STATE)
            hf_ref[s, 1] = sb_s[...].reshape(SSD_GROUP_HEADS, SSD_HEAD_DIM, SSD_STATE)
        return carry

    lax.fori_loop(0, seqs_per_step, one_sequence, 0)


_SsdScratch = collections.namedtuple("_SsdScratch", "pad xbd xt bm bt cm e et ct dtt y sf sb")


def _ssd_scratch_shapes(seq_len):
    nc = seq_len // CHUNK
    width = SSD_GROUP_CH + 2 * SSD_STATE
    return [pltpu.VMEM((seq_len + CONV_WIN - CHUNK, width), ACT),
            pltpu.VMEM((nc, SSD_GROUP_HEADS * CHUNK, SSD_GROUP_CH), BF16),
            pltpu.VMEM((nc, SSD_GROUP_CH, CHUNK), F32),
            pltpu.VMEM((seq_len, SSD_STATE), BF16),
            pltpu.VMEM((nc, SSD_STATE, CHUNK), BF16),
            pltpu.VMEM((seq_len, SSD_STATE), BF16),
            pltpu.VMEM((seq_len, LANES), F32),
            pltpu.VMEM((nc, 2 * SSD_GROUP_HEADS, CHUNK), F32),
            pltpu.VMEM((nc, 2 * SSD_GROUP_HEADS, CHUNK), F32),
            pltpu.VMEM((nc, 2 * SSD_GROUP_HEADS, CHUNK), F32),
            pltpu.VMEM((seq_len, SSD_GROUP_CH), F32),
            pltpu.VMEM((SSD_GROUP_CH, SSD_STATE), F32),
            pltpu.VMEM((SSD_GROUP_CH, SSD_STATE), F32)]


def _ssd_body(*refs, seq_len, seqs_per_step, has_h0, want_final):
    refs = list(refs)
    x_ref, b_ref, c_ref, dt_ref, wx_ref, wb_ref, wc_ref, bx_ref, bb_ref, bc_ref, dtb_ref, alog_ref, dsk_ref = refs[:13]
    rest = refs[13:]
    h0_ref = rest.pop(0) if has_h0 else None
    y_ref = rest.pop(0)
    hf_ref = rest.pop(0) if want_final else None
    shift_s = rest.pop(0)
    il = min(SSD_INTERLEAVE, seqs_per_step)
    per_seq = len(rest) // il
    sc = [_SsdScratch(*rest[q * per_seq:(q + 1) * per_seq]) for q in range(il)]
    lanes_q = range(il)
    nc = seq_len // CHUNK
    width = SSD_GROUP_CH + 2 * SSD_STATE
    half = width // 2
    nh = SSD_GROUP_HEADS
    mid = CONV_W // 2

    tail = CONV_WIN - CHUNK - CONV_HALO
    for q in lanes_q:
        sc[q].pad[0:CONV_HALO, :] = jnp.zeros((CONV_HALO, width), ACT)
        sc[q].pad[CONV_HALO + seq_len:CONV_HALO + seq_len + tail, :] = jnp.zeros((tail, width), ACT)
    win_row = lax.broadcasted_iota(jnp.int32, (CHUNK, CONV_WIN), 1)
    tok_row = lax.broadcasted_iota(jnp.int32, (CHUNK, CONV_WIN), 0)
    taps = [j for j in range(CONV_W) if j != mid]
    for n, j in enumerate(taps):
        shift_s[n] = (win_row == tok_row + (CONV_HALO + j - mid)).astype(BF16)

    conv_w = jnp.concatenate([wx_ref[...], wb_ref[...], wc_ref[...]], axis=1)
    conv_b = jnp.concatenate([bx_ref[...], bb_ref[...], bc_ref[...]], axis=1)
    a_row = -jnp.exp(alog_ref[...])
    row_i = lax.broadcasted_iota(jnp.int32, (CHUNK, CHUNK), 0)
    col_i = lax.broadcasted_iota(jnp.int32, (CHUNK, CHUNK), 1)
    lower = row_i >= col_i
    upper = row_i <= col_i
    lower_b = lower.astype(BF16)
    fwd_lane = lax.broadcasted_iota(jnp.int32, (1, LANES), 1) < nh
    head_of_lane = lax.broadcasted_iota(jnp.int32, (CHUNK, SSD_GROUP_CH), 1) // SSD_HEAD_DIM

    def chunk_rows(ci):
        return pl.ds(pl.multiple_of(ci * CHUNK, CHUNK), CHUNK)

    def copy_chunk(offs, ci, carry):
        base = pl.multiple_of(ci * CHUNK, CHUNK)
        dst = pl.ds(base + CONV_HALO, CHUNK)
        for q in lanes_q:
            src = pl.ds(offs[q] + base, CHUNK)
            sc[q].pad[dst, 0:SSD_GROUP_CH] = x_ref[src, :]
            sc[q].pad[dst, SSD_GROUP_CH:SSD_GROUP_CH + SSD_STATE] = b_ref[src, :]
            sc[q].pad[dst, SSD_GROUP_CH + SSD_STATE:width] = c_ref[src, :]
        return carry

    def prep_chunk(offs, ci, carry):
        r = chunk_rows(ci)
        units = [(q, h) for h in range(2) for q in lanes_q]

        def window(unit):
            q, h = unit
            return sc[q].pad[pl.ds(pl.multiple_of(ci * CHUNK, CHUNK), CONV_WIN), h * half:(h + 1) * half]

        def shift_dots(unit):
            win = window(unit)
            return [_dot(shift_s[t], win) for t in range(len(taps))]

        def finish(unit, shifted):
            q, h = unit
            cols = slice(h * half, (h + 1) * half)
            acc = conv_b[:, cols] + window(unit)[CONV_HALO:CONV_HALO + CHUNK].astype(F32) * conv_w[mid:mid + 1, cols]
            for t, j in enumerate(taps):
                acc = acc + shifted[t] * conv_w[j:j + 1, cols]
            v = _silu(acc)
            if h == 0:
                sc[q].y[r, :] = v * dsk_ref[...]
                vb = v.astype(BF16)
                for hd in range(nh):
                    sc[q].xbd[ci, hd * CHUNK:(hd + 1) * CHUNK, :] = jnp.where(head_of_lane == hd, vb,
                                                                              jnp.zeros_like(vb))
                sc[q].xt[ci] = v.T
            else:
                bm = v[:, 0:SSD_STATE]
                sc[q].bm[r, :] = bm.astype(BF16)
                sc[q].bt[ci] = bm.T.astype(BF16)
                sc[q].cm[r, :] = v[:, SSD_STATE:half].astype(BF16)

        dts = [_softplus(dt_ref[pl.ds(offs[q] + pl.multiple_of(ci * CHUNK, CHUNK), CHUNK), :] + dtb_ref[...])
               for q in lanes_q]
        das = [dt * a_row for dt in dts]
        pending = shift_dots(units[0])
        cums = []
        for i, unit in enumerate(units):
            following = shift_dots(units[i + 1]) if i + 1 < len(units) else None
            if unit[1] == 0:
                cums.append(_cumsum_rows(lower_b, das[unit[0]]))
            finish(unit, pending)
            pending = following
        for q in lanes_q:
            e = jnp.where(fwd_lane, cums[q], cums[q] - das[q])
            sc[q].e[r, :] = e
            sc[q].et[ci] = e.T[0:2 * nh]
            sc[q].ct[ci] = cums[q].T[0:2 * nh]
            sc[q].dtt[ci] = dts[q].T[0:2 * nh]
        return carry

    def scan_chunk(i, carry):
        chunks = (i, nc - 1 - i)
        rows = [chunk_rows(ci) for ci in chunks]
        streams = [(q, d) for q in lanes_q for d in range(2)]
        st = {(q, d): (sc[q].sf if d == 0 else sc[q].sb) for q, d in streams}
        s_prev = {k: st[k][...] for k in streams}
        g, cs = {}, {}
        for q, d in streams:
            cm = sc[q].cm[rows[d], :]
            g[q, d] = _dot(cm, sc[q].bt[chunks[d]])
            cs[q, d] = _dot_nt(cm, s_prev[q, d].astype(BF16))
        w_rows, keeps, ys = {}, {}, {}
        for q, d in streams:
            e = sc[q].e[rows[d], :]
            e_t, c_t, dt_t = sc[q].et[chunks[d]], sc[q].ct[chunks[d]], sc[q].dtt[chunks[d]]
            w_d, keep_d, mixes, offs = [], [], [], []
            for h in range(nh):
                col = h + nh * d
                ec = jnp.broadcast_to(e[:, col:col + 1], (CHUNK, CHUNK))
                er = e_t[col:col + 1, :]
                dt_h = dt_t[col:col + 1, :]
                log_dt = jnp.log(dt_h)
                tot = c_t[col:col + 1, CHUNK - 1:CHUNK]
                if d == 0:
                    decay_dt = jnp.exp(jnp.where(lower, ec - (er - log_dt), -jnp.inf))
                    offs.append(jnp.exp(ec))
                    w_d.append(jnp.broadcast_to(dt_h * jnp.exp(tot - er), (SSD_HEAD_DIM, CHUNK)))
                else:
                    decay_dt = jnp.exp(jnp.where(upper, (er + log_dt) - ec, -jnp.inf))
                    offs.append(jnp.exp(tot - ec))
                    w_d.append(jnp.broadcast_to(dt_h * jnp.exp(er), (SSD_HEAD_DIM, CHUNK)))
                keep_d.append(jnp.broadcast_to(jnp.exp(tot), (SSD_HEAD_DIM, SSD_STATE)))
                mixes.append((g[q, d] * decay_dt).astype(BF16))
            y_diag = _dot(jnp.concatenate(mixes, axis=1), sc[q].xbd[chunks[d]])
            off = jnp.concatenate([offs[nh - 1]] * (SSD_GROUP_CH // CHUNK), axis=1)
            for h in range(nh - 2, -1, -1):
                off = jnp.where(head_of_lane == h, jnp.concatenate([offs[h]] * (SSD_GROUP_CH // CHUNK), axis=1), off)
            ys[q, d] = y_diag + cs[q, d] * off
            w_rows[q, d] = jnp.concatenate(w_d, axis=0)
            keeps[q, d] = jnp.concatenate(keep_d, axis=0)
        for q, d in streams:
            w_t = (sc[q].xt[chunks[d]] * w_rows[q, d]).astype(BF16)
            st[q, d][...] = keeps[q, d] * s_prev[q, d] + _dot(w_t, sc[q].bm[rows[d], :])
            sc[q].y[rows[d], :] += ys[q, d]
        return carry

    def emit_chunk(offs, ci, carry):
        base = pl.multiple_of(ci * CHUNK, CHUNK)
        for q in lanes_q:
            y_ref[pl.ds(offs[q] + base, CHUNK), :] = sc[q].y[pl.ds(base, CHUNK), :].astype(y_ref.dtype)
        return carry

    def one_group(s, carry):
        seqs = [s * il + q for q in lanes_q]
        offs = [pl.multiple_of(sq * seq_len, CHUNK) for sq in seqs]
        lax.fori_loop(0, nc, functools.partial(copy_chunk, offs), 0)
        lax.fori_loop(0, nc, functools.partial(prep_chunk, offs), 0)
        for q in lanes_q:
            if has_h0:
                sc[q].sf[...] = h0_ref[seqs[q], 0].reshape(SSD_GROUP_CH, SSD_STATE)
                sc[q].sb[...] = h0_ref[seqs[q], 1].reshape(SSD_GROUP_CH, SSD_STATE)
            else:
                sc[q].sf[...] = jnp.zeros((SSD_GROUP_CH, SSD_STATE), F32)
                sc[q].sb[...] = jnp.zeros((SSD_GROUP_CH, SSD_STATE), F32)
        lax.fori_loop(0, nc, scan_chunk, 0)
        lax.fori_loop(0, nc, functools.partial(emit_chunk, offs), 0)
        if want_final:
            for q in lanes_q:
                hf_ref[seqs[q], 0] = sc[q].sf[...].reshape(SSD_GROUP_HEADS, SSD_HEAD_DIM, SSD_STATE)
                hf_ref[seqs[q], 1] = sc[q].sb[...].reshape(SSD_GROUP_HEADS, SSD_HEAD_DIM, SSD_STATE)
        return carry

    lax.fori_loop(0, seqs_per_step // il, one_group, 0)


def _ssd_call(u, dt, conv_w, conv_b, dtb, alog, dsk, h0, *, n_seq, seq_len, seqs_per_step, row_block0,
              want_final, name):
    t, n_u = u.shape
    e = dsk.shape[1]
    groups = e // SSD_GROUP_CH
    xb0 = e // SSD_GROUP_CH
    bb0 = 2 * e // SSD_STATE
    cb0 = bb0 + groups
    has_h0 = h0 is not None
    seq = lambda b: row_block0 + b
    rows = seqs_per_step * seq_len
    in_specs = [pl.BlockSpec((rows, SSD_GROUP_CH), lambda b, g: (seq(b), xb0 + g)),
                pl.BlockSpec((rows, SSD_STATE), lambda b, g: (seq(b), bb0 + g)),
                pl.BlockSpec((rows, SSD_STATE), lambda b, g: (seq(b), cb0 + g)),
                pl.BlockSpec((rows, LANES), lambda b, g: (seq(b), g)),
                pl.BlockSpec((CONV_W, SSD_GROUP_CH), lambda b, g: (0, g)),
                pl.BlockSpec((CONV_W, SSD_STATE), lambda b, g: (0, bb0 - xb0 * 2 + g)),
                pl.BlockSpec((CONV_W, SSD_STATE), lambda b, g: (0, cb0 - xb0 * 2 + g)),
                pl.BlockSpec((1, SSD_GROUP_CH), lambda b, g: (0, g)),
                pl.BlockSpec((1, SSD_STATE), lambda b, g: (0, bb0 - xb0 * 2 + g)),
                pl.BlockSpec((1, SSD_STATE), lambda b, g: (0, cb0 - xb0 * 2 + g)),
                pl.BlockSpec((1, LANES), lambda b, g: (0, g)),
                pl.BlockSpec((1, LANES), lambda b, g: (0, g)),
                pl.BlockSpec((1, SSD_GROUP_CH), lambda b, g: (0, g))]
    args = [u, u, u, dt, conv_w, conv_w, conv_w, conv_b, conv_b, conv_b, dtb, alog, dsk]
    state_block = (seqs_per_step, 2, SSD_GROUP_HEADS, SSD_HEAD_DIM, SSD_STATE)
    if has_h0:
        in_specs.append(pl.BlockSpec(state_block, lambda b, g: (b, 0, g, 0, 0)))
        args.append(h0)
    out_specs = [pl.BlockSpec((rows, SSD_GROUP_CH), lambda b, g: (b, g))]
    out_shape = [jax.ShapeDtypeStruct((n_seq * seq_len, e), ACT)]
    if want_final:
        out_specs.append(pl.BlockSpec(state_block, lambda b, g: (b, 0, g, 0, 0)))
        out_shape.append(jax.ShapeDtypeStruct((n_seq, 2, e // SSD_HEAD_DIM, SSD_HEAD_DIM, SSD_STATE), F32))
    width = SSD_GROUP_CH + 2 * SSD_STATE
    nc = seq_len // CHUNK

    return pl.pallas_call(
        functools.partial(_ssd_body, seq_len=seq_len, seqs_per_step=seqs_per_step, has_h0=has_h0,
                          want_final=want_final),
        grid=(n_seq // seqs_per_step, groups),
        in_specs=in_specs,
        out_specs=out_specs,
        out_shape=out_shape,
        scratch_shapes=([pltpu.VMEM((CONV_W - 1, CHUNK, CONV_WIN), BF16)]
                        + _ssd_scratch_shapes(seq_len) * min(SSD_INTERLEAVE, seqs_per_step)),
        compiler_params=_params("parallel", "parallel"),
        name=name,
    )(*args)


def _group_lanes(v, groups):
    per_group = v.reshape(2, groups, SSD_GROUP_HEADS).transpose(1, 0, 2).reshape(groups, 2 * SSD_GROUP_HEADS)
    return jnp.pad(per_group, ((0, 0), (0, LANES - 2 * SSD_GROUP_HEADS))).reshape(1, groups * LANES)


def _gmlp_body(u_ref, v_ref, lng_ref, lnb_ref, ws_ref, bs_ref, o_ref, *, tm):
    groups = ws_ref.shape[0]
    gch = u_ref.shape[1] // groups
    for c in range(tm // CHUNK):
        rows = slice(c * CHUNK, (c + 1) * CHUNK)
        v = _gelu(v_ref[rows, :].astype(F32))
        vc = v - jnp.mean(v, axis=-1, keepdims=True)
        vn = vc * lax.rsqrt(jnp.mean(vc * vc, axis=-1, keepdims=True) + EPS) * lng_ref[...] + lnb_ref[...]
        vb = vn.astype(BF16)
        for g in range(groups):
            cols = slice(g * gch, (g + 1) * gch)
            s = _dot(ws_ref[g].astype(BF16), vb[:, cols]) + bs_ref[:, cols]
            o_ref[rows, cols] = (_gelu(u_ref[rows, cols].astype(F32)) * s).astype(o_ref.dtype)


def _gmlp(u, ln_g, ln_b, w_s, b_lanes, *, tm=256):
    t = u.shape[0]
    e = ln_g.shape[1]
    return pl.pallas_call(
        functools.partial(_gmlp_body, tm=tm),
        grid=(t // tm,),
        in_specs=[pl.BlockSpec((tm, e), lambda i: (i, 0)),
                  pl.BlockSpec((tm, e), lambda i: (i, 1)),
                  pl.BlockSpec((1, e), lambda i: (0, 0)),
                  pl.BlockSpec((1, e), lambda i: (0, 0)),
                  pl.BlockSpec(w_s.shape, lambda i: (0, 0, 0)),
                  pl.BlockSpec((CHUNK, e), lambda i: (0, 0))],
        out_specs=pl.BlockSpec((tm, e), lambda i: (i, 0)),
        out_shape=jax.ShapeDtypeStruct((t, e), ACT),
        compiler_params=_params("parallel"),
        name="gmlp",
    )(u, u, ln_g, ln_b, w_s, b_lanes)


def _s5_body(lr_ref, li_ref, ls_ref, brt_ref, bit_ref, cre_ref, cim_ref, h0_ref, u_ref, y_ref, fin_ref,
             win_s, t_s, ef_s, eb_s, z_s, a_s, zin_s, d_s, spf_s, spb_s, yo_s, uf_s,
             *, seqs, l_ctx, n_lat, l_lat, ctx_parts):
    bst = S5_BLOCK_ST
    ng = LANES // S5_GROUP
    kw = S5_T * LANES
    part = pl.program_id(1)

    @pl.when(part == 0)
    def _build():
        own = (lax.broadcasted_iota(jnp.int32, (ng, S5_GROUP, bst), 2) // S5_STATE
               == lax.broadcasted_iota(jnp.int32, (ng, S5_GROUP, bst), 0))

        def spread(v):
            return jnp.where(own, v[None], 0.0).reshape(LANES, bst).astype(BF16)

        tau = lax.broadcasted_iota(jnp.int32, (S5_T + 8, 1), 0).astype(F32)
        e_refs = (ef_s, eb_s)
        for d in range(2):
            lr, li = lr_ref[d], li_ref[d]
            step = jnp.exp(ls_ref[d])
            mag = jnp.exp(tau * (lr * step))
            p_re = mag * jnp.cos(tau * (li * step))
            p_im = mag * jnp.sin(tau * (li * step))
            ab_re, ab_im = p_re[1:2], p_im[1:2]
            den = lr * lr + li * li
            nr = ab_re - 1.0
            cr = (nr * lr + ab_im * li) / den
            ci = (ab_im * lr - nr * li) / den
            brt, bit = brt_ref[d], bit_ref[d]
            bb_re = cr * brt - ci * bit
            bb_im = cr * bit + ci * brt
            cre, cim = cre_ref[d], cim_ref[d]
            for k in range(S5_T):
                rows = slice(k * LANES, (k + 1) * LANES)
                tq = S5_T - 1 - k if d == 0 else k
                te = k + 1 if d == 0 else S5_T - k
                pr, pi = p_re[tq:tq + 1], p_im[tq:tq + 1]
                win_s[rows, 2 * d * bst:(2 * d + 1) * bst] = spread(pr * bb_re - pi * bb_im)
                win_s[rows, (2 * d + 1) * bst:(2 * d + 2) * bst] = spread(pr * bb_im + pi * bb_re)
                pr, pi = p_re[te:te + 1], p_im[te:te + 1]
                e_refs[d][rows, 0:bst] = spread(cre * pr - cim * pi)
                e_refs[d][rows, bst:2 * bst] = spread(-(cre * pi + cim * pr))
            c_own = jnp.concatenate([spread(cre), spread(-cim)], axis=1)
            z = _dot_nt(win_s[:, 2 * d * bst:(2 * d + 2) * bst], c_own)
            if d == 0:
                z_s[0:kw, :] = z
            else:
                z_s[kw - LANES:kw, :] += z[0:LANES]
                z_s[kw:2 * kw - LANES, :] = z[LANES:kw]
            a_s[2 * d:2 * d + 1, :] = p_re[S5_T:S5_T + 1]
            a_s[2 * d + 1:2 * d + 2, :] = p_im[S5_T:S5_T + 1]
        for k in range(S5_T):
            r0 = (S5_T - 1 - k) * LANES
            t_s[:, k * LANES:(k + 1) * LANES] = z_s[r0:r0 + kw, :].astype(BF16)

    def outputs():
        yo_s[...] = (_dot(zin_s[...], t_s[...]) + _dot_nt(spf_s[...].astype(BF16), ef_s[...])
                     + _dot_nt(spb_s[...].astype(BF16), eb_s[...]))

    @pl.when(part < ctx_parts)
    def _context():
        nj = l_ctx // S5_T
        uf_s[...] = u_ref[...].astype(F32)
        for j in range(nj):
            for k in range(S5_T):
                zin_s[j * seqs:(j + 1) * seqs, k * LANES:(k + 1) * LANES] = (
                    uf_s[pl.ds(j * S5_T + k, seqs, stride=l_ctx), :].astype(BF16))
        d_s[...] = _dot(zin_s[...], win_s[...])
        coef = [jnp.broadcast_to(a_s[i:i + 1, :], (seqs, bst)) for i in range(4)]
        zero = jnp.zeros((seqs, bst), F32)

        def step(j, carry):
            fr, fi, br, bi = carry
            rf = pl.ds(pl.multiple_of(j * seqs, seqs), seqs)
            rb = pl.ds(pl.multiple_of((nj - 1 - j) * seqs, seqs), seqs)
            spf_s[rf, 0:bst] = fr
            spf_s[rf, bst:2 * bst] = fi
            spb_s[rb, 0:bst] = br
            spb_s[rb, bst:2 * bst] = bi
            return (coef[0] * fr - coef[1] * fi + d_s[rf, 0:bst],
                    coef[0] * fi + coef[1] * fr + d_s[rf, bst:2 * bst],
                    coef[2] * br - coef[3] * bi + d_s[rb, 2 * bst:3 * bst],
                    coef[2] * bi + coef[3] * br + d_s[rb, 3 * bst:4 * bst])

        fr, fi, br, bi = lax.fori_loop(0, nj, step, (zero, zero, zero, zero))
        fin_ref[0, 0] = fr
        fin_ref[0, 1] = fi
        fin_ref[1, 0] = br
        fin_ref[1, 1] = bi
        outputs()
        for j in range(nj):
            for k in range(S5_T):
                y_ref[pl.ds(j * S5_T + k, seqs, stride=l_ctx), :] = (
                    yo_s[j * seqs:(j + 1) * seqs, k * LANES:(k + 1) * LANES])

    @pl.when(part == ctx_parts)
    def _latent():
        nj = l_lat // S5_T
        uf_s[...] = u_ref[...].astype(F32)
        for b in range(n_lat):
            for k in range(S5_T):
                zin_s[b * nj:(b + 1) * nj, k * LANES:(k + 1) * LANES] = (
                    uf_s[pl.ds(b * l_lat + k, nj, stride=S5_T), :].astype(BF16))
        d_s[...] = _dot(zin_s[...], win_s[...])
        coef_re = jnp.concatenate([jnp.broadcast_to(a_s[0:1, :], (n_lat, bst)),
                                   jnp.broadcast_to(a_s[2:3, :], (n_lat, bst))], axis=0)
        coef_im = jnp.concatenate([jnp.broadcast_to(a_s[1:2, :], (n_lat, bst)),
                                   jnp.broadcast_to(a_s[3:4, :], (n_lat, bst))], axis=0)
        s_re0 = jnp.concatenate([h0_ref[0, 0], h0_ref[1, 0]], axis=0)
        s_im0 = jnp.concatenate([h0_ref[0, 1], h0_ref[1, 1]], axis=0)

        def step(j, carry):
            s_re, s_im = carry
            d_re, d_im = [], []
            for b in range(n_lat):
                row = pl.ds(b * nj + j, 1)
                spf_s[row, 0:bst] = s_re[b:b + 1]
                spf_s[row, bst:2 * bst] = s_im[b:b + 1]
                d_re.append(d_s[row, 0:bst])
                d_im.append(d_s[row, bst:2 * bst])
            for b in range(n_lat):
                row = pl.ds(b * nj + nj - 1 - j, 1)
                spb_s[row, 0:bst] = s_re[n_lat + b:n_lat + b + 1]
                spb_s[row, bst:2 * bst] = s_im[n_lat + b:n_lat + b + 1]
                d_re.append(d_s[row, 2 * bst:3 * bst])
                d_im.append(d_s[row, 3 * bst:4 * bst])
            d_re = jnp.concatenate(d_re, axis=0)
            d_im = jnp.concatenate(d_im, axis=0)
            return coef_re * s_re - coef_im * s_im + d_re, coef_re * s_im + coef_im * s_re + d_im

        lax.fori_loop(0, nj, step, (s_re0, s_im0))
        outputs()
        for b in range(n_lat):
            for k in range(S5_T):
                y_ref[pl.ds(b * l_lat + k, nj, stride=S5_T), :] = yo_s[b * nj:(b + 1) * nj, k * LANES:(k + 1) * LANES]


def _s5_core(prm, h0, u, *, n_ctx, l_ctx, n_lat, l_lat, e):
    blocks = prm[0].shape[0]
    part_tokens = n_lat * l_lat
    t = u.shape[0]
    n_parts = t // part_tokens
    seqs = part_tokens // l_ctx
    r = part_tokens // S5_T
    vec = pl.BlockSpec((None, 2, 1, S5_BLOCK_ST), lambda g, p: (g, 0, 0, 0))
    mat = pl.BlockSpec((None, 2, S5_GROUP, S5_BLOCK_ST), lambda g, p: (g, 0, 0, 0))
    k_in = S5_T * LANES
    return pl.pallas_call(
        functools.partial(_s5_body, seqs=seqs, l_ctx=l_ctx, n_lat=n_lat, l_lat=l_lat, ctx_parts=n_parts - 1),
        grid=(blocks, n_parts),
        in_specs=[vec, vec, vec, mat, mat, mat, mat,
                  pl.BlockSpec((None, 2, 2, n_lat, S5_BLOCK_ST), lambda g, p: (g, 0, 0, 0, 0)),
                  pl.BlockSpec((part_tokens, LANES), lambda g, p: (p, g))],
        out_specs=[pl.BlockSpec((part_tokens, LANES), lambda g, p: (p, g)),
                   pl.BlockSpec((None, 2, 2, seqs, S5_BLOCK_ST),
                                lambda g, p: (g, 0, 0, jnp.minimum(p, n_parts - 2), 0))],
        out_shape=[jax.ShapeDtypeStruct((t, e), F32),
                   jax.ShapeDtypeStruct((blocks, 2, 2, n_ctx, S5_BLOCK_ST), F32)],
        scratch_shapes=[pltpu.VMEM((k_in, 4 * S5_BLOCK_ST), BF16),
                        pltpu.VMEM((k_in, k_in), BF16),
                        pltpu.VMEM((k_in, 2 * S5_BLOCK_ST), BF16),
                        pltpu.VMEM((k_in, 2 * S5_BLOCK_ST), BF16),
                        pltpu.VMEM(((2 * S5_T - 1) * LANES, LANES), F32),
                        pltpu.VMEM((8, S5_BLOCK_ST), F32),
                        pltpu.VMEM((r, k_in), BF16),
                        pltpu.VMEM((r, 4 * S5_BLOCK_ST), F32),
                        pltpu.VMEM((r, 2 * S5_BLOCK_ST), F32),
                        pltpu.VMEM((r, 2 * S5_BLOCK_ST), F32),
                        pltpu.VMEM((r, k_in), F32),
                        pltpu.VMEM((part_tokens, LANES), F32)],
        compiler_params=_params("parallel", "arbitrary"),
        name="s5_core",
    )(*prm, h0, u)


def _s5_glu_body(ys_ref, u_ref, dsk_ref, w_ref, b_ref, o_ref):
    y = _gelu(ys_ref[...] + dsk_ref[...] * u_ref[...].astype(F32))
    o_ref[...] = (y * jax.nn.sigmoid(_dot(y.astype(BF16), w_ref[...]) + b_ref[...])).astype(o_ref.dtype)


def _s5_glu(ys, u, dsk, w, b, *, tm=512):
    t, e = ys.shape
    return pl.pallas_call(
        _s5_glu_body,
        grid=(t // tm,),
        in_specs=[pl.BlockSpec((tm, e), lambda i: (i, 0)),
                  pl.BlockSpec((tm, e), lambda i: (i, 0)),
                  pl.BlockSpec((1, e), lambda i: (0, 0)),
                  pl.BlockSpec((e, e), lambda i: (0, 0)),
                  pl.BlockSpec((1, e), lambda i: (0, 0))],
        out_specs=pl.BlockSpec((tm, e), lambda i: (i, 0)),
        out_shape=jax.ShapeDtypeStruct((t, e), ACT),
        compiler_params=_params("parallel"),
        name="s5_glu",
    )(ys, u, dsk, w, b)


def _ctx_attn_body(q_ref, k_ref, v_ref, o_ref, ko_ref, vo_ref):
    scale = HEAD_DIM ** -0.5
    seq_len = q_ref.shape[0]
    first = lax.broadcasted_iota(jnp.int32, (seq_len, LANES), 1) < HEAD_DIM
    masks = (first, jnp.logical_not(first))
    n_pairs = q_ref.shape[1] // LANES
    logits, values = [], []
    for pair in range(n_pairs):
        cols = slice(pair * LANES, (pair + 1) * LANES)
        q2, k2, v2 = q_ref[:, cols].astype(F32) * scale, k_ref[:, cols], v_ref[:, cols]
        kb = k2.astype(BF16)
        values.append(v2.astype(BF16))
        for h in range(LANES // HEAD_DIM):
            ko_ref[2 * pair + h] = k2[:, h * HEAD_DIM:(h + 1) * HEAD_DIM].astype(F32)
            vo_ref[2 * pair + h] = v2[:, h * HEAD_DIM:(h + 1) * HEAD_DIM].astype(F32)
            logits.append(_dot_nt(jnp.where(masks[h], q2, 0.0).astype(BF16), kb))
    probs = []
    for s in logits:
        p = jnp.exp(s - jnp.max(s, axis=-1, keepdims=True))
        probs.append((p * (1.0 / jnp.sum(p, axis=-1, keepdims=True))).astype(BF16))
    for pair in range(n_pairs):
        o0, o1 = _dot(probs[2 * pair], values[pair]), _dot(probs[2 * pair + 1], values[pair])
        o_ref[:, pair * LANES:(pair + 1) * LANES] = jnp.where(first, o0, o1).astype(o_ref.dtype)


def _ctx_attention(u, *, n_seq, seq_len, e, width=512):
    t = n_seq * seq_len
    hp = e // width
    heads = e // HEAD_DIM
    kv_block = (None, width // HEAD_DIM, seq_len, HEAD_DIM)
    in_specs = [pl.BlockSpec((seq_len, width), lambda b, h: (b, h)),
                pl.BlockSpec((seq_len, width), lambda b, h: (b, hp + h)),
                pl.BlockSpec((seq_len, width), lambda b, h: (b, 2 * hp + h))]
    return pl.pallas_call(
        _ctx_attn_body,
        grid=(n_seq, hp),
        in_specs=in_specs,
        out_specs=[pl.BlockSpec((seq_len, width), lambda b, h: (b, h)),
                   pl.BlockSpec(kv_block, lambda b, h: (b, h, 0, 0)),
                   pl.BlockSpec(kv_block, lambda b, h: (b, h, 0, 0))],
        out_shape=[jax.ShapeDtypeStruct((t, e), ACT),
                   jax.ShapeDtypeStruct((n_seq, heads, seq_len, HEAD_DIM), F32),
                   jax.ShapeDtypeStruct((n_seq, heads, seq_len, HEAD_DIM), F32)],
        compiler_params=_params("parallel", "parallel"),
        name="ctx_attention",
    )(u, u, u)


def _nat_bias_body(rp_ref, o_ref):
    n_rel_rows = rp_ref.shape[1]
    q = lax.broadcasted_iota(jnp.int32, (GRID_W, LANES), 0)
    lane = lax.broadcasted_iota(jnp.int32, (GRID_W, LANES), 1)
    kc = lane % GRID_W
    c_start = jnp.clip(q - WIN_COLS // 2, 0, GRID_W - WIN_COLS)
    ok = (kc >= c_start) & (kc < c_start + WIN_COLS)
    left = lane < GRID_W
    for h in range(o_ref.shape[0]):
        halves = []
        for i in range(n_rel_rows):
            row = jnp.broadcast_to(rp_ref[h, i:i + 1, :], (GRID_W, LANES))
            halves.append((pltpu.roll(row, 0, 1, stride=1, stride_axis=0),
                           pltpu.roll(row, GRID_W, 1, stride=1, stride_axis=0)))
        for i in range(n_rel_rows - 1):
            o_ref[h, i] = jnp.where(ok, jnp.where(left, halves[i][0], halves[i + 1][1]), MASKED)


def _nat_bias(rpb, *, heads_per_step=4):
    heads, n_rel_rows, n_rel_cols = rpb.shape
    rows = jnp.roll(jnp.pad(rpb, ((0, 0), (0, 0), (0, LANES - n_rel_cols))), -(WIN_COLS - 1), axis=-1)
    return pl.pallas_call(
        _nat_bias_body,
        grid=(heads // heads_per_step,),
        in_specs=[pl.BlockSpec((heads_per_step, n_rel_rows, LANES), lambda h: (h, 0, 0))],
        out_specs=pl.BlockSpec((heads_per_step, n_rel_rows - 1, GRID_W, 2 * GRID_W), lambda h: (h, 0, 0, 0)),
        out_shape=jax.ShapeDtypeStruct((heads, n_rel_rows - 1, GRID_W, 2 * GRID_W), F32),
        compiler_params=_params("parallel"),
        name="nat_bias",
    )(rows)


def _nat_body(q_ref, k_ref, v_ref, ck_ref, cv_ref, bias_ref, o_ref, ckb_s, cvb_s, *, rows):
    scale = HEAD_DIM ** -0.5
    wr = min(WIN_ROWS, rows)
    nw = wr * GRID_W
    ckb_s[...] = jnp.concatenate([ck_ref[0], ck_ref[1]], axis=1).astype(BF16)
    cvb_s[...] = jnp.concatenate([cv_ref[0], cv_ref[1]], axis=1).astype(BF16)
    first = lax.broadcasted_iota(jnp.int32, (GRID_W, LANES), 1) < HEAD_DIM

    heads = tuple(range(LANES // HEAD_DIM))
    masks = (first, jnp.logical_not(first))

    def row_group(g, carry):
        rws = [g * NAT_ROWS_PER_STEP + i for i in range(NAT_ROWS_PER_STEP)]
        starts = [jnp.clip(r - wr // 2, 0, rows - wr) for r in rws]
        q_rows = [pl.ds(pl.multiple_of(r * GRID_W, GRID_W), GRID_W) for r in rws]
        k_rows = [pl.ds(pl.multiple_of(rs * GRID_W, GRID_W), nw) for rs in starts]
        logits = []
        for r, rs, qr, kr in zip(rws, starts, q_rows, k_rows):
            q2 = q_ref[qr, :].astype(F32) * scale
            kb = k_ref[kr, :].astype(BF16)
            for h in heads:
                q = jnp.where(masks[h], q2, 0.0).astype(BF16)
                logits.append((_dot_nt(q, kb), _dot_nt(q, ckb_s[...])))
        probs = []
        for idx, (s_win, s_ctx) in enumerate(logits):
            r, rs, h = rws[idx // 2], starts[idx // 2], heads[idx % 2]
            i0 = (WIN_ROWS - 1) - (r - rs)
            s_win = s_win + jnp.concatenate([bias_ref[h, i0 + 2 * jj] for jj in range(wr // 2)], axis=1)
            m = jnp.maximum(jnp.max(s_win, axis=-1, keepdims=True), jnp.max(s_ctx, axis=-1, keepdims=True))
            p_win = jnp.exp(s_win - m)
            p_ctx = jnp.exp(s_ctx - m)
            inv = 1.0 / (jnp.sum(p_win, axis=-1, keepdims=True) + jnp.sum(p_ctx, axis=-1, keepdims=True))
            probs.append(((p_win * inv).astype(BF16), (p_ctx * inv).astype(BF16)))
        for i, (qr, kr) in enumerate(zip(q_rows, k_rows)):
            vb = v_ref[kr, :].astype(BF16)
            outs = [_dot(probs[2 * i + h][0], vb) + _dot(probs[2 * i + h][1], cvb_s[...]) for h in heads]
            o_ref[qr, :] = jnp.where(first, outs[0], outs[1]).astype(o_ref.dtype)
        return carry

    lax.fori_loop(0, rows // NAT_ROWS_PER_STEP, row_group, 0)


def _nat_attention(u, cache_k, cache_v, cache_layer, bias, *, n_seq, seq_len, row_block0, e):
    hp = e // LANES
    hpb = LANES // HEAD_DIM
    past = cache_k.shape[3]
    seq = lambda b: row_block0 + b
    cache_block = (None, None, hpb, past, HEAD_DIM)
    return pl.pallas_call(
        functools.partial(_nat_body, rows=seq_len // GRID_W),
        grid=(n_seq, hp),
        in_specs=[pl.BlockSpec((seq_len, LANES), lambda b, h: (seq(b), h)),
                  pl.BlockSpec((seq_len, LANES), lambda b, h: (seq(b), hp + h)),
                  pl.BlockSpec((seq_len, LANES), lambda b, h: (seq(b), 2 * hp + h)),
                  pl.BlockSpec(cache_block, lambda b, h: (b, cache_layer, h, 0, 0)),
                  pl.BlockSpec(cache_block, lambda b, h: (b, cache_layer, h, 0, 0)),
                  pl.BlockSpec((hpb,) + bias.shape[1:], lambda b, h: (h, 0, 0, 0))],
        out_specs=pl.BlockSpec((seq_len, LANES), lambda b, h: (b, h)),
        out_shape=jax.ShapeDtypeStruct((n_seq * seq_len, e), ACT),
        scratch_shapes=[pltpu.VMEM((past, LANES), BF16), pltpu.VMEM((past, LANES), BF16)],
        compiler_params=_params("parallel", "parallel"),
        name="nat_attention",
    )(u, u, u, cache_k, cache_v, bias)


def kernel(x_prompt, x_sample, state_ssd, state_s5, cache_k, cache_v, c, c_ctx, norm_g, w_mod, b_mod, w_out, final_g, ssd_w_in, ssd_conv_w, ssd_conv_b, ssd_dt_bias, ssd_a_log, ssd_d, ssd_norm_g, mlp_w_in, mlp_ln_g, mlp_ln_b, mlp_w_s, mlp_b_s, s5_w_in, s5_lam_re, s5_lam_im, s5_log_step, s5_b_re, s5_b_im, s5_c_re, s5_c_im, s5_d, s5_w_glu, s5_b_glu, nat_w_in, nat_rpb):
    n_ctx, l_ctx, d = x_prompt.shape
    n_lat, l_lat, _ = x_sample.shape
    t_ctx = n_ctx * l_ctx
    depth = norm_g.shape[0]
    e = w_out.shape[1]
    assert depth == 4 and l_lat % l_ctx == 0 and t_ctx % (n_lat * l_lat) == 0
    tiles = dict(tm=512, t_ctx=t_ctx, l_lat=l_lat)
    tiles_in = dict(tm=2048, tn=1024, t_ctx=t_ctx, l_lat=l_lat)

    t_lat = n_lat * l_lat
    x = (x_prompt.reshape(t_ctx, d), x_sample.reshape(t_lat, d))
    cond8 = jnp.concatenate([c_ctx[None], c, jnp.zeros((8 - 1 - n_lat, d), F32)], axis=0)
    mods = _modulation(cond8, w_mod, b_mod).reshape(depth, 8, 3, 1, d)
    w_out_b = w_out.astype(BF16)

    groups = e // SSD_GROUP_CH
    n_main = 3 * e
    w_dt = ssd_w_in[0][:, n_main:].reshape(d, 2, groups, SSD_GROUP_HEADS).transpose(0, 2, 1, 3)
    w_dt = jnp.pad(w_dt.reshape(d, groups, 2 * SSD_GROUP_HEADS), ((0, 0), (0, 0), (0, LANES - 2 * SSD_GROUP_HEADS)))
    tiles_two = dict(tiles_in, tm=1024)
    u = _inproj(x, mods, 0, norm_g[0:1], ssd_w_in[0].astype(BF16), n_out=n_main, name="ssd_in", **tiles_two)
    dt = _inproj(x, mods, 0, norm_g[0:1], w_dt.reshape(d, groups * LANES).astype(BF16), name="ssd_dt_in", out_dtype=F32,
                 **tiles_two)
    ssd_args = (ssd_conv_w[0], ssd_conv_b[0:1], _group_lanes(ssd_dt_bias[0], groups),
                _group_lanes(ssd_a_log[0], groups), jnp.repeat(ssd_d[0], SSD_HEAD_DIM)[None])
    y_ctx, new_ssd = _ssd_call(u, dt, *ssd_args, None, n_seq=n_ctx, seq_len=l_ctx, seqs_per_step=l_lat // l_ctx,
                               row_block0=0, want_final=True, name="ssd_ctx")
    y_lat, = _ssd_call(u, dt, *ssd_args, state_ssd[:, 0], n_seq=n_lat, seq_len=l_lat, seqs_per_step=n_lat,
                       row_block0=t_ctx // t_lat, want_final=False, name="ssd_lat")
    x = _outproj((y_ctx, y_lat), u, 0, x, mods, 0, w_out_b[0], norm_g=ssd_norm_g[0:1], name="ssd_out", **tiles)

    u = _inproj(x, mods, 1, norm_g[1:2], mlp_w_in[0].astype(BF16), name="mlp_in", **tiles_in)
    b_lanes = jnp.repeat(mlp_b_s[0].T, e // mlp_b_s.shape[1], axis=1)
    y = _gmlp(u, mlp_ln_g[0:1], mlp_ln_b[0:1], mlp_w_s[0], b_lanes)
    x = _outproj(y, u, 2, x, mods, 1, w_out_b[1], name="mlp_out", **tiles)

    u = _inproj(x, mods, 2, norm_g[2:3], s5_w_in[0].astype(BF16), name="s5_in", **tiles_in)
    s5_groups = e // S5_GROUP
    bg = LANES // S5_GROUP
    blocks = s5_groups // bg

    def block_vec(v):
        return v.reshape(2, blocks, 1, S5_BLOCK_ST).transpose(1, 0, 2, 3)

    def block_mat_b(v):
        v = v.reshape(2, blocks, bg, S5_STATE, S5_GROUP).transpose(1, 0, 4, 2, 3)
        return v.reshape(blocks, 2, S5_GROUP, S5_BLOCK_ST)

    def block_mat_c(v):
        v = v.reshape(2, blocks, bg, S5_GROUP, S5_STATE).transpose(1, 0, 3, 2, 4)
        return v.reshape(blocks, 2, S5_GROUP, S5_BLOCK_ST)

    log_step = jnp.repeat(s5_log_step[0][:, :, None], S5_STATE, axis=2)
    prm = (block_vec(s5_lam_re[0]), block_vec(s5_lam_im[0]), block_vec(log_step), block_mat_b(s5_b_re[0]),
           block_mat_b(s5_b_im[0]), block_mat_c(s5_c_re[0]), block_mat_c(s5_c_im[0]))
    h0 = state_s5[:, 0].reshape(n_lat, 2, 2, blocks, S5_BLOCK_ST).transpose(3, 1, 2, 0, 4)
    ys, fin = _s5_core(prm, h0, u, n_ctx=n_ctx, l_ctx=l_ctx, n_lat=n_lat, l_lat=l_lat, e=e)
    new_s5 = fin.transpose(3, 1, 2, 0, 4).reshape(n_ctx, 1, 2, 2, s5_groups, S5_STATE)
    y = _s5_glu(ys, u, s5_d[0:1], s5_w_glu[0].astype(BF16), s5_b_glu[0:1])
    x = _outproj(y, u, 1, x, mods, 2, w_out_b[2], name="s5_out", **tiles)

    u = _inproj(x, mods, 3, norm_g[3:4], nat_w_in[0].astype(BF16), name="nat_in", **tiles_in)
    y_ctx, new_k, new_v = _ctx_attention(u, n_seq=n_ctx, seq_len=l_ctx, e=e)
    y_lat = _nat_attention(u, cache_k, cache_v, 0, _nat_bias(nat_rpb[0]), n_seq=n_lat, seq_len=l_lat,
                           row_block0=t_ctx // l_lat, e=e)
    out_ctx, out_lat = _outproj((y_ctx, y_lat), u, 3, x, mods, 3, w_out_b[3], final_g=final_g[None], split_rows=(t_ctx, t_lat),
                                name="nat_out", **tiles)

    return (out_ctx.reshape(n_ctx, l_ctx, d), out_lat.reshape(n_lat, l_lat, d),
            new_ssd[:, None], new_s5, new_k[:, None], new_v[:, None])
```

```python
import collections
import functools

import jax
import jax.numpy as jnp
from jax import lax
from jax.experimental import pallas as pl
from jax.experimental.pallas import tpu as pltpu

F32 = jnp.float32
BF16 = jnp.bfloat16
ACT = jnp.bfloat16
EPS = 1e-6

LANES = 128
CHUNK = 128
SSD_HEAD_DIM = 64
SSD_GROUP_HEADS = 4
SSD_GROUP_CH = SSD_HEAD_DIM * SSD_GROUP_HEADS
SSD_STATE = 128
CONV_W = 5
CONV_HALO = 16
SSD_INTERLEAVE = 4
CONV_WIN = 2 * CHUNK
S5_T = 8
S5_GROUP = 16
S5_STATE = 64
S5_BLOCK_ST = (LANES // S5_GROUP) * S5_STATE
HEAD_DIM = 64
GRID_W = 64
WIN_ROWS = 8
WIN_COLS = 16
NAT_ROWS_PER_STEP = 8
MASKED = -1e30
VMEM_LIMIT = 52 * 1024 * 1024


def _silu(x):
    return x * jax.nn.sigmoid(x)


def _gelu(x):
    return 0.5 * x * (1.0 + jnp.tanh(0.7978845608028654 * (x + 0.044715 * (x * x * x))))


def _softplus(x):
    return jnp.maximum(x, 0.0) + jnp.log1p(jnp.exp(-jnp.abs(x)))


def _dot(a, b):
    return jnp.dot(a, b, preferred_element_type=F32)


def _dot_nt(a, b):
    return lax.dot_general(a, b, (((1,), (1,)), ((), ())), preferred_element_type=F32)


def _cumsum_rows(lower_tri, x):
    hi = x.astype(BF16)
    r1 = x - hi.astype(F32)
    mid = r1.astype(BF16)
    lo = (r1 - mid.astype(F32)).astype(BF16)
    return _dot(lower_tri, hi) + _dot(lower_tri, mid) + _dot(lower_tri, lo)


def _params(*sem):
    return pltpu.CompilerParams(dimension_semantics=sem, vmem_limit_bytes=VMEM_LIMIT)


def _mod_row(i, *, tm, t_ctx, l_lat):
    start = i * tm
    return jnp.where(start < t_ctx, 0, 1 + (start - t_ctx) // l_lat)


def _mod_body(c_ref, w_ref, b_ref, o_ref):
    cond = _silu(c_ref[...]).astype(BF16)
    o_ref[...] = _dot(cond, w_ref[...].astype(BF16)) + b_ref[...]


def _modulation(cond8, w_mod, b_mod, *, tn=1024):
    depth, d, n = w_mod.shape
    return pl.pallas_call(
        _mod_body,
        grid=(depth, n // tn),
        in_specs=[pl.BlockSpec((8, d), lambda l, j: (0, 0)),
                  pl.BlockSpec((None, d, tn), lambda l, j: (l, 0, j)),
                  pl.BlockSpec((None, 1, tn), lambda l, j: (l, 0, j))],
        out_specs=pl.BlockSpec((None, 8, tn), lambda l, j: (l, 0, j)),
        out_shape=jax.ShapeDtypeStruct((depth, 8, n), F32),
        compiler_params=_params("parallel", "parallel"),
        name="modulation",
    )(cond8, w_mod, b_mod.reshape(depth, 1, n))


def _stream_specs(x, tm, width, n_grid):
    ids = (lambda i, *_: i) if n_grid == 1 else (lambda i, j: i)
    if not isinstance(x, tuple):
        return [x], [pl.BlockSpec((tm, width), lambda *g: (ids(*g), 0))], None
    split = x[0].shape[0] // tm
    last = x[1].shape[0] // tm - 1
    return (list(x),
            [pl.BlockSpec((tm, width), lambda *g: (jnp.minimum(ids(*g), split - 1), 0)),
             pl.BlockSpec((tm, width), lambda *g: (jnp.clip(ids(*g) - split, 0, last), 0))],
            split)


def _inproj_body(*refs, split):
    n_x = 1 if split is None else 2
    x_refs = refs[:n_x]
    shift_ref, scale_ref, g_ref, w_ref, o_ref, h_ref = refs[n_x:]

    def normalise(x_ref):
        x = x_ref[...]
        y = x * lax.rsqrt(jnp.mean(x * x, axis=-1, keepdims=True) + EPS) * g_ref[...]
        h_ref[...] = (y * (1.0 + scale_ref[...]) + shift_ref[...]).astype(BF16)

    first_col = pl.program_id(1) == 0
    if split is None:
        pl.when(first_col)(lambda: normalise(x_refs[0]))
    else:
        in_ctx = pl.program_id(0) < split
        pl.when(first_col & in_ctx)(lambda: normalise(x_refs[0]))
        pl.when(first_col & jnp.logical_not(in_ctx))(lambda: normalise(x_refs[1]))
    o_ref[...] = _dot(h_ref[...], w_ref[...]).astype(o_ref.dtype)


def _inproj(x, mods, layer, g, w, *, tm, tn, t_ctx, l_lat, name, out_dtype=ACT, n_out=None):
    d = w.shape[0]
    n = w.shape[1] if n_out is None else n_out
    row = functools.partial(_mod_row, tm=tm, t_ctx=t_ctx, l_lat=l_lat)
    x_args, x_specs, split = _stream_specs(x, tm, d, 2)
    t = sum(a.shape[0] for a in x_args)

    def mod_spec(part):
        return pl.BlockSpec((None, None, None, 1, d), lambda i, j: (layer, row(i), part, 0, 0))

    return pl.pallas_call(
        functools.partial(_inproj_body, split=split),
        grid=(t // tm, n // tn),
        in_specs=x_specs + [mod_spec(0), mod_spec(1),
                            pl.BlockSpec((1, d), lambda i, j: (0, 0)),
                            pl.BlockSpec((d, tn), lambda i, j: (0, j))],
        out_specs=pl.BlockSpec((tm, tn), lambda i, j: (i, j)),
        out_shape=jax.ShapeDtypeStruct((t, n), out_dtype),
        scratch_shapes=[pltpu.VMEM((tm, d), BF16)],
        compiler_params=_params("parallel", "arbitrary"),
        name=name,
    )(*x_args, mods, mods, g, w)


def _outproj_body(*refs, n_y, n_x, split, gated_norm, final_norm):
    refs = list(refs)
    y_refs, z_ref, x_refs = refs[:n_y], refs[n_y], refs[n_y + 1:n_y + 1 + n_x]
    rest = refs[n_y + 1 + n_x:]
    gate_ref, w_ref = rest[:2]
    rest = rest[2:]
    ng_ref = rest.pop(0) if gated_norm else None
    fg_ref = rest.pop(0) if final_norm else None
    o_refs = rest

    def emit(y_ref, x_ref, o_ref):
        t = y_ref[...].astype(F32) * _silu(z_ref[...].astype(F32))
        if gated_norm:
            t = t * lax.rsqrt(jnp.mean(t * t, axis=-1, keepdims=True) + EPS) * ng_ref[...]
        xn = x_ref[...] + gate_ref[...] * _dot(t.astype(BF16), w_ref[...])
        if final_norm:
            xn = xn * lax.rsqrt(jnp.mean(xn * xn, axis=-1, keepdims=True) + EPS) * fg_ref[...]
        o_ref[...] = xn

    if split is None:
        emit(y_refs[0], x_refs[0], o_refs[0])
    else:
        in_ctx = pl.program_id(0) < split
        pl.when(in_ctx)(lambda: emit(y_refs[0], x_refs[0], o_refs[0]))
        pl.when(jnp.logical_not(in_ctx))(lambda: emit(y_refs[-1], x_refs[-1], o_refs[-1]))


def _outproj(y, u, z_block, x, mods, layer, w, *, norm_g=None, final_g=None, split_rows=None, tm, t_ctx, l_lat,
             name):
    t, e = u.shape[0], w.shape[0]
    d = w.shape[1]
    row = functools.partial(_mod_row, tm=tm, t_ctx=t_ctx, l_lat=l_lat)
    y_args, y_specs, split_y = _stream_specs(y, tm, e, 1)
    x_args, x_specs, split_in = _stream_specs(x, tm, d, 1)
    in_specs = (y_specs + [pl.BlockSpec((tm, e), lambda i: (i, z_block))] + x_specs
                + [pl.BlockSpec((None, None, None, 1, d), lambda i: (layer, row(i), 2, 0, 0)),
                   pl.BlockSpec((e, d), lambda i: (0, 0))])
    args = y_args + [u] + x_args + [mods, w]
    if norm_g is not None:
        in_specs.append(pl.BlockSpec((1, e), lambda i: (0, 0)))
        args.append(norm_g)
    if final_g is not None:
        in_specs.append(pl.BlockSpec((1, d), lambda i: (0, 0)))
        args.append(final_g)
    if split_rows is None:
        split_out = None
        out_specs = pl.BlockSpec((tm, d), lambda i: (i, 0))
        out_shape = jax.ShapeDtypeStruct((t, d), F32)
    else:
        split_out = split_rows[0] // tm
        last = split_rows[1] // tm - 1
        out_specs = [pl.BlockSpec((tm, d), lambda i: (jnp.minimum(i, split_out - 1), 0)),
                     pl.BlockSpec((tm, d), lambda i: (jnp.clip(i - split_out, 0, last), 0))]
        out_shape = [jax.ShapeDtypeStruct((rows, d), F32) for rows in split_rows]
    return pl.pallas_call(
        functools.partial(_outproj_body, n_y=len(y_args), n_x=len(x_args),
                          split=next((s for s in (split_y, split_in, split_out) if s is not None), None),
                          gated_norm=norm_g is not None, final_norm=final_g is not None),
        grid=(t // tm,),
        in_specs=in_specs,
        out_specs=out_specs,
        out_shape=out_shape,
        compiler_params=_params("arbitrary"),
        name=name,
    )(*args)


_SsdScratch = collections.namedtuple("_SsdScratch", "pad xbd xt bm bt cm e et ct dtt y sf sb")


def _ssd_scratch_shapes(seq_len):
    nc = seq_len // CHUNK
    width = SSD_GROUP_CH + 2 * SSD_STATE
    return [pltpu.VMEM((seq_len + CONV_WIN - CHUNK, width), ACT),
            pltpu.VMEM((nc, SSD_GROUP_HEADS * CHUNK, SSD_GROUP_CH), BF16),
            pltpu.VMEM((nc, SSD_GROUP_CH, CHUNK), F32),
            pltpu.VMEM((seq_len, SSD_STATE), BF16),
            pltpu.VMEM((nc, SSD_STATE, CHUNK), BF16),
            pltpu.VMEM((seq_len, SSD_STATE), BF16),
            pltpu.VMEM((seq_len, LANES), F32),
            pltpu.VMEM((nc, 2 * SSD_GROUP_HEADS, CHUNK), F32),
            pltpu.VMEM((nc, 2 * SSD_GROUP_HEADS, CHUNK), F32),
            pltpu.VMEM((nc, 2 * SSD_GROUP_HEADS, CHUNK), F32),
            pltpu.VMEM((seq_len, SSD_GROUP_CH), F32),
            pltpu.VMEM((SSD_GROUP_CH, SSD_STATE), F32),
            pltpu.VMEM((SSD_GROUP_CH, SSD_STATE), F32)]


def _ssd_body(*refs, seq_len, seqs_per_step, has_h0, want_final):
    refs = list(refs)
    x_ref, b_ref, c_ref, dt_ref, wx_ref, wb_ref, wc_ref, bx_ref, bb_ref, bc_ref, dtb_ref, alog_ref, dsk_ref = refs[:13]
    rest = refs[13:]
    h0_ref = rest.pop(0) if has_h0 else None
    y_ref = rest.pop(0)
    hf_ref = rest.pop(0) if want_final else None
    shift_s = rest.pop(0)
    il = min(SSD_INTERLEAVE, seqs_per_step)
    per_seq = len(rest) // il
    sc = [_SsdScratch(*rest[q * per_seq:(q + 1) * per_seq]) for q in range(il)]
    lanes_q = range(il)
    nc = seq_len // CHUNK
    width = SSD_GROUP_CH + 2 * SSD_STATE
    half = width // 2
    nh = SSD_GROUP_HEADS
    mid = CONV_W // 2

    tail = CONV_WIN - CHUNK - CONV_HALO
    for q in lanes_q:
        sc[q].pad[0:CONV_HALO, :] = jnp.zeros((CONV_HALO, width), ACT)
        sc[q].pad[CONV_HALO + seq_len:CONV_HALO + seq_len + tail, :] = jnp.zeros((tail, width), ACT)
    win_row = lax.broadcasted_iota(jnp.int32, (CHUNK, CONV_WIN), 1)
    tok_row = lax.broadcasted_iota(jnp.int32, (CHUNK, CONV_WIN), 0)
    taps = [j for j in range(CONV_W) if j != mid]
    for n, j in enumerate(taps):
        shift_s[n] = (win_row == tok_row + (CONV_HALO + j - mid)).astype(BF16)

    conv_w = jnp.concatenate([wx_ref[...], wb_ref[...], wc_ref[...]], axis=1)
    conv_b = jnp.concatenate([bx_ref[...], bb_ref[...], bc_ref[...]], axis=1)
    a_row = -jnp.exp(alog_ref[...])
    row_i = lax.broadcasted_iota(jnp.int32, (CHUNK, CHUNK), 0)
    col_i = lax.broadcasted_iota(jnp.int32, (CHUNK, CHUNK), 1)
    lower = row_i >= col_i
    upper = row_i <= col_i
    lower_b = lower.astype(BF16)
    fwd_lane = lax.broadcasted_iota(jnp.int32, (1, LANES), 1) < nh
    head_of_lane = lax.broadcasted_iota(jnp.int32, (CHUNK, SSD_GROUP_CH), 1) // SSD_HEAD_DIM

    def chunk_rows(ci):
        return pl.ds(pl.multiple_of(ci * CHUNK, CHUNK), CHUNK)

    def copy_chunk(offs, ci, carry):
        base = pl.multiple_of(ci * CHUNK, CHUNK)
        dst = pl.ds(base + CONV_HALO, CHUNK)
        for q in lanes_q:
            src = pl.ds(offs[q] + base, CHUNK)
            sc[q].pad[dst, 0:SSD_GROUP_CH] = x_ref[src, :]
            sc[q].pad[dst, SSD_GROUP_CH:SSD_GROUP_CH + SSD_STATE] = b_ref[src, :]
            sc[q].pad[dst, SSD_GROUP_CH + SSD_STATE:width] = c_ref[src, :]
        return carry

    def prep_chunk(offs, ci, carry):
        r = chunk_rows(ci)
        units = [(q, h) for h in range(2) for q in lanes_q]

        def window(unit):
            q, h = unit
            return sc[q].pad[pl.ds(pl.multiple_of(ci * CHUNK, CHUNK), CONV_WIN), h * half:(h + 1) * half]

        def shift_dots(unit):
            win = window(unit)
            return [_dot(shift_s[t], win) for t in range(len(taps))]

        def finish(unit, shifted):
            q, h = unit
            cols = slice(h * half, (h + 1) * half)
            acc = conv_b[:, cols] + window(unit)[CONV_HALO:CONV_HALO + CHUNK].astype(F32) * conv_w[mid:mid + 1, cols]
            for t, j in enumerate(taps):
                acc = acc + shifted[t] * conv_w[j:j + 1, cols]
            v = _silu(acc)
            if h == 0:
                sc[q].y[r, :] = v * dsk_ref[...]
                vb = v.astype(BF16)
                for hd in range(nh):
                    sc[q].xbd[ci, hd * CHUNK:(hd + 1) * CHUNK, :] = jnp.where(head_of_lane == hd, vb,
                                                                              jnp.zeros_like(vb))
                sc[q].xt[ci] = v.T
            else:
                bm = v[:, 0:SSD_STATE]
                sc[q].bm[r, :] = bm.astype(BF16)
                sc[q].bt[ci] = bm.T.astype(BF16)
                sc[q].cm[r, :] = v[:, SSD_STATE:half].astype(BF16)

        dts = [_softplus(dt_ref[pl.ds(offs[q] + pl.multiple_of(ci * CHUNK, CHUNK), CHUNK), :] + dtb_ref[...])
               for q in lanes_q]
        das = [dt * a_row for dt in dts]
        pending = shift_dots(units[0])
        cums = []
        for i, unit in enumerate(units):
            following = shift_dots(units[i + 1]) if i + 1 < len(units) else None
            if unit[1] == 0:
                cums.append(_cumsum_rows(lower_b, das[unit[0]]))
            finish(unit, pending)
            pending = following
        for q in lanes_q:
            e = jnp.where(fwd_lane, cums[q], cums[q] - das[q])
            sc[q].e[r, :] = e
            sc[q].et[ci] = e.T[0:2 * nh]
            sc[q].ct[ci] = cums[q].T[0:2 * nh]
            sc[q].dtt[ci] = dts[q].T[0:2 * nh]
        return carry

    def scan_chunk(i, carry):
        chunks = (i, nc - 1 - i)
        rows = [chunk_rows(ci) for ci in chunks]
        streams = [(q, d) for q in lanes_q for d in range(2)]
        st = {(q, d): (sc[q].sf if d == 0 else sc[q].sb) for q, d in streams}
        s_prev = {k: st[k][...] for k in streams}
        g, cs = {}, {}
        for q, d in streams:
            cm = sc[q].cm[rows[d], :]
            g[q, d] = _dot(cm, sc[q].bt[chunks[d]])
            cs[q, d] = _dot_nt(cm, s_prev[q, d].astype(BF16))
        w_rows, keeps, ys = {}, {}, {}
        for q, d in streams:
            e = sc[q].e[rows[d], :]
            e_t, c_t, dt_t = sc[q].et[chunks[d]], sc[q].ct[chunks[d]], sc[q].dtt[chunks[d]]
            w_d, keep_d, mixes, offs = [], [], [], []
            for h in range(nh):
                col = h + nh * d
                ec = jnp.broadcast_to(e[:, col:col + 1], (CHUNK, CHUNK))
                er = e_t[col:col + 1, :]
                dt_h = dt_t[col:col + 1, :]
                log_dt = jnp.log(dt_h)
                tot = c_t[col:col + 1, CHUNK - 1:CHUNK]
                if d == 0:
                    decay_dt = jnp.exp(jnp.where(lower, ec - (er - log_dt), -jnp.inf))
                    offs.append(jnp.exp(ec))
                    w_d.append(jnp.broadcast_to(dt_h * jnp.exp(tot - er), (SSD_HEAD_DIM, CHUNK)))
                else:
                    decay_dt = jnp.exp(jnp.where(upper, (er + log_dt) - ec, -jnp.inf))
                    offs.append(jnp.exp(tot - ec))
                    w_d.append(jnp.broadcast_to(dt_h * jnp.exp(er), (SSD_HEAD_DIM, CHUNK)))
                keep_d.append(jnp.broadcast_to(jnp.exp(tot), (SSD_HEAD_DIM, SSD_STATE)))
                mixes.append((g[q, d] * decay_dt).astype(BF16))
            y_diag = _dot(jnp.concatenate(mixes, axis=1), sc[q].xbd[chunks[d]])
            off = jnp.concatenate([offs[nh - 1]] * (SSD_GROUP_CH // CHUNK), axis=1)
            for h in range(nh - 2, -1, -1):
                off = jnp.where(head_of_lane == h, jnp.concatenate([offs[h]] * (SSD_GROUP_CH // CHUNK), axis=1), off)
            ys[q, d] = y_diag + cs[q, d] * off
            w_rows[q, d] = jnp.concatenate(w_d, axis=0)
            keeps[q, d] = jnp.concatenate(keep_d, axis=0)
        for q, d in streams:
            w_t = (sc[q].xt[chunks[d]] * w_rows[q, d]).astype(BF16)
            st[q, d][...] = keeps[q, d] * s_prev[q, d] + _dot(w_t, sc[q].bm[rows[d], :])
            sc[q].y[rows[d], :] += ys[q, d]
        return carry

    def emit_chunk(offs, ci, carry):
        base = pl.multiple_of(ci * CHUNK, CHUNK)
        for q in lanes_q:
            y_ref[pl.ds(offs[q] + base, CHUNK), :] = sc[q].y[pl.ds(base, CHUNK), :].astype(y_ref.dtype)
        return carry

    def one_group(s, carry):
        seqs = [s * il + q for q in lanes_q]
        offs = [pl.multiple_of(sq * seq_len, CHUNK) for sq in seqs]
        lax.fori_loop(0, nc, functools.partial(copy_chunk, offs), 0)
        lax.fori_loop(0, nc, functools.partial(prep_chunk, offs), 0)
        for q in lanes_q:
            if has_h0:
                sc[q].sf[...] = h0_ref[seqs[q], 0].reshape(SSD_GROUP_CH, SSD_STATE)
                sc[q].sb[...] = h0_ref[seqs[q], 1].reshape(SSD_GROUP_CH, SSD_STATE)
            else:
                sc[q].sf[...] = jnp.zeros((SSD_GROUP_CH, SSD_STATE), F32)
                sc[q].sb[...] = jnp.zeros((SSD_GROUP_CH, SSD_STATE), F32)
        lax.fori_loop(0, nc, scan_chunk, 0)
        lax.fori_loop(0, nc, functools.partial(emit_chunk, offs), 0)
        if want_final:
            for q in lanes_q:
                hf_ref[seqs[q], 0] = sc[q].sf[...].reshape(SSD_GROUP_HEADS, SSD_HEAD_DIM, SSD_STATE)
                hf_ref[seqs[q], 1] = sc[q].sb[...].reshape(SSD_GROUP_HEADS, SSD_HEAD_DIM, SSD_STATE)
        return carry

    lax.fori_loop(0, seqs_per_step // il, one_group, 0)


def _ssd_call(u, dt, conv_w, conv_b, dtb, alog, dsk, h0, *, n_seq, seq_len, seqs_per_step, row_block0,
              want_final, name):
    t, n_u = u.shape
    e = dsk.shape[1]
    groups = e // SSD_GROUP_CH
    xb0 = e // SSD_GROUP_CH
    bb0 = 2 * e // SSD_STATE
    cb0 = bb0 + groups
    has_h0 = h0 is not None
    seq = lambda b: row_block0 + b
    rows = seqs_per_step * seq_len
    in_specs = [pl.BlockSpec((rows, SSD_GROUP_CH), lambda b, g: (seq(b), xb0 + g)),
                pl.BlockSpec((rows, SSD_STATE), lambda b, g: (seq(b), bb0 + g)),
                pl.BlockSpec((rows, SSD_STATE), lambda b, g: (seq(b), cb0 + g)),
                pl.BlockSpec((rows, LANES), lambda b, g: (seq(b), g)),
                pl.BlockSpec((CONV_W, SSD_GROUP_CH), lambda b, g: (0, g)),
                pl.BlockSpec((CONV_W, SSD_STATE), lambda b, g: (0, bb0 - xb0 * 2 + g)),
                pl.BlockSpec((CONV_W, SSD_STATE), lambda b, g: (0, cb0 - xb0 * 2 + g)),
                pl.BlockSpec((1, SSD_GROUP_CH), lambda b, g: (0, g)),
                pl.BlockSpec((1, SSD_STATE), lambda b, g: (0, bb0 - xb0 * 2 + g)),
                pl.BlockSpec((1, SSD_STATE), lambda b, g: (0, cb0 - xb0 * 2 + g)),
                pl.BlockSpec((1, LANES), lambda b, g: (0, g)),
                pl.BlockSpec((1, LANES), lambda b, g: (0, g)),
                pl.BlockSpec((1, SSD_GROUP_CH), lambda b, g: (0, g))]
    args = [u, u, u, dt, conv_w, conv_w, conv_w, conv_b, conv_b, conv_b, dtb, alog, dsk]
    state_block = (seqs_per_step, 2, SSD_GROUP_HEADS, SSD_HEAD_DIM, SSD_STATE)
    if has_h0:
        in_specs.append(pl.BlockSpec(state_block, lambda b, g: (b, 0, g, 0, 0)))
        args.append(h0)
    out_specs = [pl.BlockSpec((rows, SSD_GROUP_CH), lambda b, g: (b, g))]
    out_shape = [jax.ShapeDtypeStruct((n_seq * seq_len, e), ACT)]
    if want_final:
        out_specs.append(pl.BlockSpec(state_block, lambda b, g: (b, 0, g, 0, 0)))
        out_shape.append(jax.ShapeDtypeStruct((n_seq, 2, e // SSD_HEAD_DIM, SSD_HEAD_DIM, SSD_STATE), F32))
    return pl.pallas_call(
        functools.partial(_ssd_body, seq_len=seq_len, seqs_per_step=seqs_per_step, has_h0=has_h0,
                          want_final=want_final),
        grid=(n_seq // seqs_per_step, groups),
        in_specs=in_specs,
        out_specs=out_specs,
        out_shape=out_shape,
        scratch_shapes=([pltpu.VMEM((CONV_W - 1, CHUNK, CONV_WIN), BF16)]
                        + _ssd_scratch_shapes(seq_len) * min(SSD_INTERLEAVE, seqs_per_step)),
        compiler_params=_params("parallel", "parallel"),
        name=name,
    )(*args)


def _group_lanes(v, groups):
    per_group = v.reshape(2, groups, SSD_GROUP_HEADS).transpose(1, 0, 2).reshape(groups, 2 * SSD_GROUP_HEADS)
    return jnp.pad(per_group, ((0, 0), (0, LANES - 2 * SSD_GROUP_HEADS))).reshape(1, groups * LANES)


def _gmlp_body(u_ref, v_ref, lng_ref, lnb_ref, ws_ref, bs_ref, o_ref, *, tm):
    groups = ws_ref.shape[0]
    gch = u_ref.shape[1] // groups
    for c in range(tm // CHUNK):
        rows = slice(c * CHUNK, (c + 1) * CHUNK)
        v = _gelu(v_ref[rows, :].astype(F32))
        vc = v - jnp.mean(v, axis=-1, keepdims=True)
        vn = vc * lax.rsqrt(jnp.mean(vc * vc, axis=-1, keepdims=True) + EPS) * lng_ref[...] + lnb_ref[...]
        vb = vn.astype(BF16)
        for g in range(groups):
            cols = slice(g * gch, (g + 1) * gch)
            s = _dot(ws_ref[g].astype(BF16), vb[:, cols]) + bs_ref[:, cols]
            o_ref[rows, cols] = (_gelu(u_ref[rows, cols].astype(F32)) * s).astype(o_ref.dtype)


def _gmlp(u, ln_g, ln_b, w_s, b_lanes, *, tm=256):
    t = u.shape[0]
    e = ln_g.shape[1]
    return pl.pallas_call(
        functools.partial(_gmlp_body, tm=tm),
        grid=(t // tm,),
        in_specs=[pl.BlockSpec((tm, e), lambda i: (i, 0)),
                  pl.BlockSpec((tm, e), lambda i: (i, 1)),
                  pl.BlockSpec((1, e), lambda i: (0, 0)),
                  pl.BlockSpec((1, e), lambda i: (0, 0)),
                  pl.BlockSpec(w_s.shape, lambda i: (0, 0, 0)),
                  pl.BlockSpec((CHUNK, e), lambda i: (0, 0))],
        out_specs=pl.BlockSpec((tm, e), lambda i: (i, 0)),
        out_shape=jax.ShapeDtypeStruct((t, e), ACT),
        compiler_params=_params("parallel"),
        name="gmlp",
    )(u, u, ln_g, ln_b, w_s, b_lanes)


def _s5_body(lr_ref, li_ref, ls_ref, brt_ref, bit_ref, cre_ref, cim_ref, h0_ref, u_ref, y_ref, fin_ref,
             win_s, t_s, ef_s, eb_s, z_s, a_s, zin_s, d_s, spf_s, spb_s, yo_s, uf_s,
             *, seqs, l_ctx, n_lat, l_lat, ctx_parts):
    bst = S5_BLOCK_ST
    ng = LANES // S5_GROUP
    kw = S5_T * LANES
    part = pl.program_id(1)

    @pl.when(part == 0)
    def _build():
        own = (lax.broadcasted_iota(jnp.int32, (ng, S5_GROUP, bst), 2) // S5_STATE
               == lax.broadcasted_iota(jnp.int32, (ng, S5_GROUP, bst), 0))

        def spread(v):
            return jnp.where(own, v[None], 0.0).reshape(LANES, bst).astype(BF16)

        tau = lax.broadcasted_iota(jnp.int32, (S5_T + 8, 1), 0).astype(F32)
        e_refs = (ef_s, eb_s)
        for d in range(2):
            lr, li = lr_ref[d], li_ref[d]
            step = jnp.exp(ls_ref[d])
            mag = jnp.exp(tau * (lr * step))
            p_re = mag * jnp.cos(tau * (li * step))
            p_im = mag * jnp.sin(tau * (li * step))
            ab_re, ab_im = p_re[1:2], p_im[1:2]
            den = lr * lr + li * li
            nr = ab_re - 1.0
            cr = (nr * lr + ab_im * li) / den
            ci = (ab_im * lr - nr * li) / den
            brt, bit = brt_ref[d], bit_ref[d]
            bb_re = cr * brt - ci * bit
            bb_im = cr * bit + ci * brt
            cre, cim = cre_ref[d], cim_ref[d]
            for k in range(S5_T):
                rows = slice(k * LANES, (k + 1) * LANES)
                tq = S5_T - 1 - k if d == 0 else k
                te = k + 1 if d == 0 else S5_T - k
                pr, pi = p_re[tq:tq + 1], p_im[tq:tq + 1]
                win_s[rows, 2 * d * bst:(2 * d + 1) * bst] = spread(pr * bb_re - pi * bb_im)
                win_s[rows, (2 * d + 1) * bst:(2 * d + 2) * bst] = spread(pr * bb_im + pi * bb_re)
                pr, pi = p_re[te:te + 1], p_im[te:te + 1]
                e_refs[d][rows, 0:bst] = spread(cre * pr - cim * pi)
                e_refs[d][rows, bst:2 * bst] = spread(-(cre * pi + cim * pr))
            c_own = jnp.concatenate([spread(cre), spread(-cim)], axis=1)
            z = _dot_nt(win_s[:, 2 * d * bst:(2 * d + 2) * bst], c_own)
            if d == 0:
                z_s[0:kw, :] = z
            else:
                z_s[kw - LANES:kw, :] += z[0:LANES]
                z_s[kw:2 * kw - LANES, :] = z[LANES:kw]
            a_s[2 * d:2 * d + 1, :] = p_re[S5_T:S5_T + 1]
            a_s[2 * d + 1:2 * d + 2, :] = p_im[S5_T:S5_T + 1]
        for k in range(S5_T):
            r0 = (S5_T - 1 - k) * LANES
            t_s[:, k * LANES:(k + 1) * LANES] = z_s[r0:r0 + kw, :].astype(BF16)

    def outputs():
        yo_s[...] = (_dot(zin_s[...], t_s[...]) + _dot_nt(spf_s[...].astype(BF16), ef_s[...])
                     + _dot_nt(spb_s[...].astype(BF16), eb_s[...]))

    @pl.when(part < ctx_parts)
    def _context():
        nj = l_ctx // S5_T
        uf_s[...] = u_ref[...].astype(F32)
        for j in range(nj):
            for k in range(S5_T):
                zin_s[j * seqs:(j + 1) * seqs, k * LANES:(k + 1) * LANES] = (
                    uf_s[pl.ds(j * S5_T + k, seqs, stride=l_ctx), :].astype(BF16))
        d_s[...] = _dot(zin_s[...], win_s[...])
        coef = [jnp.broadcast_to(a_s[i:i + 1, :], (seqs, bst)) for i in range(4)]
        zero = jnp.zeros((seqs, bst), F32)

        def step(j, carry):
            fr, fi, br, bi = carry
            rf = pl.ds(pl.multiple_of(j * seqs, seqs), seqs)
            rb = pl.ds(pl.multiple_of((nj - 1 - j) * seqs, seqs), seqs)
            spf_s[rf, 0:bst] = fr
            spf_s[rf, bst:2 * bst] = fi
            spb_s[rb, 0:bst] = br
            spb_s[rb, bst:2 * bst] = bi
            return (coef[0] * fr - coef[1] * fi + d_s[rf, 0:bst],
                    coef[0] * fi + coef[1] * fr + d_s[rf, bst:2 * bst],
                    coef[2] * br - coef[3] * bi + d_s[rb, 2 * bst:3 * bst],
                    coef[2] * bi + coef[3] * br + d_s[rb, 3 * bst:4 * bst])

        fr, fi, br, bi = lax.fori_loop(0, nj, step, (zero, zero, zero, zero))
        for d, ri, val in ((0, 0, fr), (0, 1, fi), (1, 0, br), (1, 1, bi)):
            for gg in range(ng):
                fin_ref[:, d, ri, gg, :] = val[:, gg * S5_STATE:(gg + 1) * S5_STATE]
        outputs()
        for j in range(nj):
            for k in range(S5_T):
                y_ref[pl.ds(j * S5_T + k, seqs, stride=l_ctx), :] = (
                    yo_s[j * seqs:(j + 1) * seqs, k * LANES:(k + 1) * LANES])

    @pl.when(part == ctx_parts)
    def _latent():
        nj = l_lat // S5_T
        uf_s[...] = u_ref[...].astype(F32)
        for b in range(n_lat):
            for k in range(S5_T):
                zin_s[b * nj:(b + 1) * nj, k * LANES:(k + 1) * LANES] = (
                    uf_s[pl.ds(b * l_lat + k, nj, stride=S5_T), :].astype(BF16))
        d_s[...] = _dot(zin_s[...], win_s[...])
        coef_re = jnp.concatenate([jnp.broadcast_to(a_s[0:1, :], (n_lat, bst)),
                                   jnp.broadcast_to(a_s[2:3, :], (n_lat, bst))], axis=0)
        coef_im = jnp.concatenate([jnp.broadcast_to(a_s[1:2, :], (n_lat, bst)),
                                   jnp.broadcast_to(a_s[3:4, :], (n_lat, bst))], axis=0)
        s_re0 = jnp.concatenate([h0_ref[0, 0], h0_ref[1, 0]], axis=0)
        s_im0 = jnp.concatenate([h0_ref[0, 1], h0_ref[1, 1]], axis=0)

        def step(j, carry):
            s_re, s_im = carry
            d_re, d_im = [], []
            for b in range(n_lat):
                row = pl.ds(b * nj + j, 1)
                spf_s[row, 0:bst] = s_re[b:b + 1]
                spf_s[row, bst:2 * bst] = s_im[b:b + 1]
                d_re.append(d_s[row, 0:bst])
                d_im.append(d_s[row, bst:2 * bst])
            for b in range(n_lat):
                row = pl.ds(b * nj + nj - 1 - j, 1)
                spb_s[row, 0:bst] = s_re[n_lat + b:n_lat + b + 1]
                spb_s[row, bst:2 * bst] = s_im[n_lat + b:n_lat + b + 1]
                d_re.append(d_s[row, 2 * bst:3 * bst])
                d_im.append(d_s[row, 3 * bst:4 * bst])
            d_re = jnp.concatenate(d_re, axis=0)
            d_im = jnp.concatenate(d_im, axis=0)
            return coef_re * s_re - coef_im * s_im + d_re, coef_re * s_im + coef_im * s_re + d_im

        lax.fori_loop(0, nj, step, (s_re0, s_im0))
        outputs()
        for b in range(n_lat):
            for k in range(S5_T):
                y_ref[pl.ds(b * l_lat + k, nj, stride=S5_T), :] = yo_s[b * nj:(b + 1) * nj, k * LANES:(k + 1) * LANES]


def _s5_core(prm, h0, u, *, n_ctx, l_ctx, n_lat, l_lat, e):
    blocks = prm[0].shape[0]
    part_tokens = n_lat * l_lat
    t = u.shape[0]
    n_parts = t // part_tokens
    seqs = part_tokens // l_ctx
    r = part_tokens // S5_T
    vec = pl.BlockSpec((None, 2, 1, S5_BLOCK_ST), lambda g, p: (g, 0, 0, 0))
    mat = pl.BlockSpec((None, 2, S5_GROUP, S5_BLOCK_ST), lambda g, p: (g, 0, 0, 0))
    k_in = S5_T * LANES
    return pl.pallas_call(
        functools.partial(_s5_body, seqs=seqs, l_ctx=l_ctx, n_lat=n_lat, l_lat=l_lat, ctx_parts=n_parts - 1),
        grid=(blocks, n_parts),
        in_specs=[vec, vec, vec, mat, mat, mat, mat,
                  pl.BlockSpec((None, 2, 2, n_lat, S5_BLOCK_ST), lambda g, p: (g, 0, 0, 0, 0)),
                  pl.BlockSpec((part_tokens, LANES), lambda g, p: (p, g))],
        out_specs=[pl.BlockSpec((part_tokens, LANES), lambda g, p: (p, g)),
                   pl.BlockSpec((seqs, 2, 2, LANES // S5_GROUP, S5_STATE),
                                lambda g, p: (jnp.minimum(p, n_parts - 2), 0, 0, g, 0))],
        out_shape=[jax.ShapeDtypeStruct((t, e), F32),
                   jax.ShapeDtypeStruct((n_ctx, 2, 2, e // S5_GROUP, S5_STATE), F32)],
        scratch_shapes=[pltpu.VMEM((k_in, 4 * S5_BLOCK_ST), BF16),
                        pltpu.VMEM((k_in, k_in), BF16),
                        pltpu.VMEM((k_in, 2 * S5_BLOCK_ST), BF16),
                        pltpu.VMEM((k_in, 2 * S5_BLOCK_ST), BF16),
                        pltpu.VMEM(((2 * S5_T - 1) * LANES, LANES), F32),
                        pltpu.VMEM((8, S5_BLOCK_ST), F32),
                        pltpu.VMEM((r, k_in), BF16),
                        pltpu.VMEM((r, 4 * S5_BLOCK_ST), F32),
                        pltpu.VMEM((r, 2 * S5_BLOCK_ST), F32),
                        pltpu.VMEM((r, 2 * S5_BLOCK_ST), F32),
                        pltpu.VMEM((r, k_in), F32),
                        pltpu.VMEM((part_tokens, LANES), F32)],
        compiler_params=_params("parallel", "arbitrary"),
        name="s5_core",
    )(*prm, h0, u)


def _s5_glu_body(ys_ref, u_ref, dsk_ref, w_ref, b_ref, o_ref):
    y = _gelu(ys_ref[...] + dsk_ref[...] * u_ref[...].astype(F32))
    o_ref[...] = (y * jax.nn.sigmoid(_dot(y.astype(BF16), w_ref[...]) + b_ref[...])).astype(o_ref.dtype)


def _s5_glu(ys, u, dsk, w, b, *, tm=512):
    t, e = ys.shape
    return pl.pallas_call(
        _s5_glu_body,
        grid=(t // tm,),
        in_specs=[pl.BlockSpec((tm, e), lambda i: (i, 0)),
                  pl.BlockSpec((tm, e), lambda i: (i, 0)),
                  pl.BlockSpec((1, e), lambda i: (0, 0)),
                  pl.BlockSpec((e, e), lambda i: (0, 0)),
                  pl.BlockSpec((1, e), lambda i: (0, 0))],
        out_specs=pl.BlockSpec((tm, e), lambda i: (i, 0)),
        out_shape=jax.ShapeDtypeStruct((t, e), ACT),
        compiler_params=_params("parallel"),
        name="s5_glu",
    )(ys, u, dsk, w, b)


def _ctx_attn_body(q_ref, k_ref, v_ref, o_ref, ko_ref, vo_ref):
    scale = HEAD_DIM ** -0.5
    seq_len = q_ref.shape[0]
    first = lax.broadcasted_iota(jnp.int32, (seq_len, LANES), 1) < HEAD_DIM
    masks = (first, jnp.logical_not(first))
    n_pairs = q_ref.shape[1] // LANES
    logits, values = [], []
    for pair in range(n_pairs):
        cols = slice(pair * LANES, (pair + 1) * LANES)
        q2, k2, v2 = q_ref[:, cols].astype(F32) * scale, k_ref[:, cols], v_ref[:, cols]
        kb, vb = k2.astype(BF16), v2.astype(BF16)
        for h in range(LANES // HEAD_DIM):
            ko_ref[2 * pair + h] = k2[:, h * HEAD_DIM:(h + 1) * HEAD_DIM].astype(F32)
            vo_ref[2 * pair + h] = v2[:, h * HEAD_DIM:(h + 1) * HEAD_DIM].astype(F32)
            logits.append(_dot_nt(jnp.where(masks[h], q2, 0.0).astype(BF16), kb))
            values.append(jnp.where(masks[h], vb, jnp.ones_like(vb)))
    probs = [jnp.exp(s - jnp.max(s, axis=-1, keepdims=True)).astype(BF16) for s in logits]
    for pair in range(n_pairs):
        o0, o1 = _dot(probs[2 * pair], values[2 * pair]), _dot(probs[2 * pair + 1], values[2 * pair + 1])
        num = jnp.where(first, o0, o1)
        den = jnp.where(first, pltpu.roll(o0, HEAD_DIM, 1), pltpu.roll(o1, HEAD_DIM, 1))
        o_ref[:, pair * LANES:(pair + 1) * LANES] = (num / den).astype(o_ref.dtype)


def _ctx_attention(u, *, n_seq, seq_len, e, width=512):
    t = n_seq * seq_len
    hp = e // width
    heads = e // HEAD_DIM
    kv_block = (None, width // HEAD_DIM, seq_len, HEAD_DIM)
    in_specs = [pl.BlockSpec((seq_len, width), lambda b, h: (b, h)),
                pl.BlockSpec((seq_len, width), lambda b, h: (b, hp + h)),
                pl.BlockSpec((seq_len, width), lambda b, h: (b, 2 * hp + h))]
    return pl.pallas_call(
        _ctx_attn_body,
        grid=(n_seq, hp),
        in_specs=in_specs,
        out_specs=[pl.BlockSpec((seq_len, width), lambda b, h: (b, h)),
                   pl.BlockSpec(kv_block, lambda b, h: (b, h, 0, 0)),
                   pl.BlockSpec(kv_block, lambda b, h: (b, h, 0, 0))],
        out_shape=[jax.ShapeDtypeStruct((t, e), ACT),
                   jax.ShapeDtypeStruct((n_seq, heads, seq_len, HEAD_DIM), F32),
                   jax.ShapeDtypeStruct((n_seq, heads, seq_len, HEAD_DIM), F32)],
        compiler_params=_params("parallel", "parallel"),
        name="ctx_attention",
    )(u, u, u)


def _nat_bias_body(rp_ref, o_ref):
    n_rel_rows = rp_ref.shape[1]
    q = lax.broadcasted_iota(jnp.int32, (GRID_W, LANES), 0)
    lane = lax.broadcasted_iota(jnp.int32, (GRID_W, LANES), 1)
    kc = lane % GRID_W
    c_start = jnp.clip(q - WIN_COLS // 2, 0, GRID_W - WIN_COLS)
    ok = (kc >= c_start) & (kc < c_start + WIN_COLS)
    left = lane < GRID_W
    for h in range(o_ref.shape[0]):
        halves = []
        for i in range(n_rel_rows):
            row = jnp.broadcast_to(rp_ref[h, i:i + 1, :], (GRID_W, LANES))
            halves.append((pltpu.roll(row, 0, 1, stride=1, stride_axis=0),
                           pltpu.roll(row, GRID_W, 1, stride=1, stride_axis=0)))
        for i in range(n_rel_rows - 1):
            o_ref[h, i] = jnp.where(ok, jnp.where(left, halves[i][0], halves[i + 1][1]), MASKED)


def _nat_bias(rpb, *, heads_per_step=4):
    heads, n_rel_rows, n_rel_cols = rpb.shape
    rows = jnp.roll(jnp.pad(rpb, ((0, 0), (0, 0), (0, LANES - n_rel_cols))), -(WIN_COLS - 1), axis=-1)
    return pl.pallas_call(
        _nat_bias_body,
        grid=(heads // heads_per_step,),
        in_specs=[pl.BlockSpec((heads_per_step, n_rel_rows, LANES), lambda h: (h, 0, 0))],
        out_specs=pl.BlockSpec((heads_per_step, n_rel_rows - 1, GRID_W, 2 * GRID_W), lambda h: (h, 0, 0, 0)),
        out_shape=jax.ShapeDtypeStruct((heads, n_rel_rows - 1, GRID_W, 2 * GRID_W), F32),
        compiler_params=_params("parallel"),
        name="nat_bias",
    )(rows)


def _nat_body(q_ref, k_ref, v_ref, ck_ref, cv_ref, bias_ref, o_ref, ckb_s, cvb_s, *, rows):
    scale = HEAD_DIM ** -0.5
    wr = min(WIN_ROWS, rows)
    nw = wr * GRID_W
    heads = tuple(range(LANES // HEAD_DIM))
    first = lax.broadcasted_iota(jnp.int32, (GRID_W, LANES), 1) < HEAD_DIM
    masks = (first, jnp.logical_not(first))
    ckb_s[...] = jnp.concatenate([ck_ref[0], ck_ref[1]], axis=1).astype(BF16)
    cvb_s[...] = jnp.concatenate([cv_ref[0], cv_ref[1]], axis=1).astype(BF16)

    def row_group(g, carry):
        rws = [g * NAT_ROWS_PER_STEP + i for i in range(NAT_ROWS_PER_STEP)]
        starts = [jnp.clip(r - wr // 2, 0, rows - wr) for r in rws]
        q_rows = [pl.ds(pl.multiple_of(r * GRID_W, GRID_W), GRID_W) for r in rws]
        k_rows = [pl.ds(pl.multiple_of(rs * GRID_W, GRID_W), nw) for rs in starts]
        logits = []
        for r, rs, qr, kr in zip(rws, starts, q_rows, k_rows):
            q2 = q_ref[qr, :].astype(F32) * scale
            kb = k_ref[kr, :].astype(BF16)
            for h in heads:
                q = jnp.where(masks[h], q2, 0.0).astype(BF16)
                logits.append((_dot_nt(q, kb), _dot_nt(q, ckb_s[...])))
        probs = []
        for idx, (s_win, s_ctx) in enumerate(logits):
            r, rs, h = rws[idx // 2], starts[idx // 2], heads[idx % 2]
            i0 = (WIN_ROWS - 1) - (r - rs)
            s_win = s_win + jnp.concatenate([bias_ref[h, i0 + 2 * jj] for jj in range(wr // 2)], axis=1)
            m = jnp.maximum(jnp.max(s_win, axis=-1, keepdims=True), jnp.max(s_ctx, axis=-1, keepdims=True))
            p_win = jnp.exp(s_win - m)
            p_ctx = jnp.exp(s_ctx - m)
            inv = 1.0 / (jnp.sum(p_win, axis=-1, keepdims=True) + jnp.sum(p_ctx, axis=-1, keepdims=True))
            probs.append(((p_win * inv).astype(BF16), (p_ctx * inv).astype(BF16)))
        for i, (qr, kr) in enumerate(zip(q_rows, k_rows)):
            vb = v_ref[kr, :].astype(BF16)
            outs = [_dot(probs[2 * i + h][0], vb) + _dot(probs[2 * i + h][1], cvb_s[...]) for h in heads]
            o_ref[qr, :] = jnp.where(first, outs[0], outs[1]).astype(o_ref.dtype)
        return carry

    lax.fori_loop(0, rows // NAT_ROWS_PER_STEP, row_group, 0)


def _nat_attention(u, cache_k, cache_v, cache_layer, bias, *, n_seq, seq_len, row_block0, e):
    hp = e // LANES
    hpb = LANES // HEAD_DIM
    past = cache_k.shape[3]
    seq = lambda b: row_block0 + b
    cache_block = (None, None, hpb, past, HEAD_DIM)
    return pl.pallas_call(
        functools.partial(_nat_body, rows=seq_len // GRID_W),
        grid=(n_seq, hp),
        in_specs=[pl.BlockSpec((seq_len, LANES), lambda b, h: (seq(b), h)),
                  pl.BlockSpec((seq_len, LANES), lambda b, h: (seq(b), hp + h)),
                  pl.BlockSpec((seq_len, LANES), lambda b, h: (seq(b), 2 * hp + h)),
                  pl.BlockSpec(cache_block, lambda b, h: (b, cache_layer, h, 0, 0)),
                  pl.BlockSpec(cache_block, lambda b, h: (b, cache_layer, h, 0, 0)),
                  pl.BlockSpec((hpb,) + bias.shape[1:], lambda b, h: (h, 0, 0, 0))],
        out_specs=pl.BlockSpec((seq_len, LANES), lambda b, h: (b, h)),
        out_shape=jax.ShapeDtypeStruct((n_seq * seq_len, e), ACT),
        scratch_shapes=[pltpu.VMEM((past, LANES), BF16), pltpu.VMEM((past, LANES), BF16)],
        compiler_params=_params("parallel", "parallel"),
        name="nat_attention",
    )(u, u, u, cache_k, cache_v, bias)


def kernel(x_prompt, x_sample, state_ssd, state_s5, cache_k, cache_v, c, c_ctx, norm_g, w_mod, b_mod, w_out, final_g, ssd_w_in, ssd_conv_w, ssd_conv_b, ssd_dt_bias, ssd_a_log, ssd_d, ssd_norm_g, mlp_w_in, mlp_ln_g, mlp_ln_b, mlp_w_s, mlp_b_s, s5_w_in, s5_lam_re, s5_lam_im, s5_log_step, s5_b_re, s5_b_im, s5_c_re, s5_c_im, s5_d, s5_w_glu, s5_b_glu, nat_w_in, nat_rpb):
    n_ctx, l_ctx, d = x_prompt.shape
    n_lat, l_lat, _ = x_sample.shape
    t_ctx = n_ctx * l_ctx
    depth = norm_g.shape[0]
    e = w_out.shape[1]
    assert depth == 4 and l_lat % l_ctx == 0 and t_ctx % (n_lat * l_lat) == 0
    tiles = dict(tm=512, t_ctx=t_ctx, l_lat=l_lat)
    tiles_in = dict(tm=2048, tn=1024, t_ctx=t_ctx, l_lat=l_lat)

    t_lat = n_lat * l_lat
    x = (x_prompt.reshape(t_ctx, d), x_sample.reshape(t_lat, d))
    cond8 = jnp.concatenate([c_ctx[None], c, jnp.zeros((8 - 1 - n_lat, d), F32)], axis=0)
    mods = _modulation(cond8, w_mod, b_mod).reshape(depth, 8, 3, 1, d)
    w_out_b = w_out.astype(BF16)

    groups = e // SSD_GROUP_CH
    n_main = 3 * e
    w_dt = ssd_w_in[0][:, n_main:].reshape(d, 2, groups, SSD_GROUP_HEADS).transpose(0, 2, 1, 3)
    w_dt = jnp.pad(w_dt.reshape(d, groups, 2 * SSD_GROUP_HEADS), ((0, 0), (0, 0), (0, LANES - 2 * SSD_GROUP_HEADS)))
    tiles_two = dict(tiles_in, tm=1024)
    u = _inproj(x, mods, 0, norm_g[0:1], ssd_w_in[0].astype(BF16), n_out=n_main, name="ssd_in", **tiles_two)
    dt = _inproj(x, mods, 0, norm_g[0:1], w_dt.reshape(d, groups * LANES).astype(BF16), name="ssd_dt_in", out_dtype=F32,
                 **tiles_two)
    ssd_args = (ssd_conv_w[0], ssd_conv_b[0:1], _group_lanes(ssd_dt_bias[0], groups),
                _group_lanes(ssd_a_log[0], groups), jnp.repeat(ssd_d[0], SSD_HEAD_DIM)[None])
    y_ctx, new_ssd = _ssd_call(u, dt, *ssd_args, None, n_seq=n_ctx, seq_len=l_ctx, seqs_per_step=l_lat // l_ctx,
                               row_block0=0, want_final=True, name="ssd_ctx")
    y_lat, = _ssd_call(u, dt, *ssd_args, state_ssd[:, 0], n_seq=n_lat, seq_len=l_lat, seqs_per_step=n_lat,
                       row_block0=t_ctx // t_lat, want_final=False, name="ssd_lat")
    x = _outproj((y_ctx, y_lat), u, 0, x, mods, 0, w_out_b[0], norm_g=ssd_norm_g[0:1], name="ssd_out", **tiles)

    u = _inproj(x, mods, 1, norm_g[1:2], mlp_w_in[0].astype(BF16), name="mlp_in", **tiles_in)
    b_lanes = jnp.repeat(mlp_b_s[0].T, e // mlp_b_s.shape[1], axis=1)
    y = _gmlp(u, mlp_ln_g[0:1], mlp_ln_b[0:1], mlp_w_s[0], b_lanes)
    x = _outproj(y, u, 2, x, mods, 1, w_out_b[1], name="mlp_out", **tiles)

    u = _inproj(x, mods, 2, norm_g[2:3], s5_w_in[0].astype(BF16), name="s5_in", **tiles_in)
    s5_groups = e // S5_GROUP
    bg = LANES // S5_GROUP
    blocks = s5_groups // bg

    def block_vec(v):
        return v.reshape(2, blocks, 1, S5_BLOCK_ST).transpose(1, 0, 2, 3)

    def block_mat_b(v):
        v = v.reshape(2, blocks, bg, S5_STATE, S5_GROUP).transpose(1, 0, 4, 2, 3)
        return v.reshape(blocks, 2, S5_GROUP, S5_BLOCK_ST)

    def block_mat_c(v):
        v = v.reshape(2, blocks, bg, S5_GROUP, S5_STATE).transpose(1, 0, 3, 2, 4)
        return v.reshape(blocks, 2, S5_GROUP, S5_BLOCK_ST)

    log_step = jnp.repeat(s5_log_step[0][:, :, None], S5_STATE, axis=2)
    prm = (block_vec(s5_lam_re[0]), block_vec(s5_lam_im[0]), block_vec(log_step), block_mat_b(s5_b_re[0]),
           block_mat_b(s5_b_im[0]), block_mat_c(s5_c_re[0]), block_mat_c(s5_c_im[0]))
    h0 = state_s5[:, 0].reshape(n_lat, 2, 2, blocks, S5_BLOCK_ST).transpose(3, 1, 2, 0, 4)
    ys, fin = _s5_core(prm, h0, u, n_ctx=n_ctx, l_ctx=l_ctx, n_lat=n_lat, l_lat=l_lat, e=e)
    new_s5 = fin[:, None]
    y = _s5_glu(ys, u, s5_d[0:1], s5_w_glu[0].astype(BF16), s5_b_glu[0:1])
    x = _outproj(y, u, 1, x, mods, 2, w_out_b[2], name="s5_out", **tiles)

    u = _inproj(x, mods, 3, norm_g[3:4], nat_w_in[0].astype(BF16), name="nat_in", **tiles_in)
    y_ctx, new_k, new_v = _ctx_attention(u, n_seq=n_ctx, seq_len=l_ctx, e=e)
    y_lat = _nat_attention(u, cache_k, cache_v, 0, _nat_bias(nat_rpb[0]), n_seq=n_lat, seq_len=l_lat,
                           row_block0=t_ctx // l_lat, e=e)
    out_ctx, out_lat = _outproj((y_ctx, y_lat), u, 3, x, mods, 3, w_out_b[3], final_g=final_g[None], split_rows=(t_ctx, t_lat),
                                name="nat_out", **tiles)

    return (out_ctx.reshape(n_ctx, l_ctx, d), out_lat.reshape(n_lat, l_lat, d),
            new_ssd[:, None], new_s5, new_k[:, None], new_v[:, None])
```

```python
import collections
import functools

import jax
import jax.numpy as jnp
from jax import lax
from jax.experimental import pallas as pl
from jax.experimental.pallas import tpu as pltpu

F32 = jnp.float32
BF16 = jnp.bfloat16
ACT = jnp.bfloat16
EPS = 1e-6

LANES = 128
CHUNK = 128
SSD_HEAD_DIM = 64
SSD_GROUP_HEADS = 4
SSD_GROUP_CH = SSD_HEAD_DIM * SSD_GROUP_HEADS
SSD_STATE = 128
CONV_W = 5
CONV_HALO = 16
SSD_INTERLEAVE = 4
CONV_WIN = 2 * CHUNK
S5_T = 8
S5_GROUP = 16
S5_STATE = 64
S5_BLOCK_ST = (LANES // S5_GROUP) * S5_STATE
HEAD_DIM = 64
GRID_W = 64
WIN_ROWS = 8
WIN_COLS = 16
NAT_ROWS_PER_STEP = 8
MASKED = -1e30
MOD_ROWS = 8

VMEM_LIMIT = 52 * 1024 * 1024
IN_TM, IN_TN = 2048, 1024
IN_TM_PAIR = 1024
OUT_TM = 512
GMLP_TM = 256
GLU_TM = 512
CTX_ATTN_WIDTH = 512


def _silu(x):
    return x * jax.nn.sigmoid(x)


def _gelu(x):
    return 0.5 * x * (1.0 + jnp.tanh(0.7978845608028654 * (x + 0.044715 * (x * x * x))))


def _softplus(x):
    return jnp.maximum(x, 0.0) + jnp.log1p(jnp.exp(-jnp.abs(x)))


def _dot(a, b):
    return jnp.dot(a, b, preferred_element_type=F32)


def _dot_nt(a, b):
    return lax.dot_general(a, b, (((1,), (1,)), ((), ())), preferred_element_type=F32)


def _cumsum_rows(lower_tri, x):
    hi = x.astype(BF16)
    r1 = x - hi.astype(F32)
    mid = r1.astype(BF16)
    lo = (r1 - mid.astype(F32)).astype(BF16)
    return _dot(lower_tri, hi) + _dot(lower_tri, mid) + _dot(lower_tri, lo)


def _params(*sem):
    return pltpu.CompilerParams(dimension_semantics=sem, vmem_limit_bytes=VMEM_LIMIT)


def _mod_row(i, *, tm, t_ctx, l_lat):
    start = i * tm
    return jnp.where(start < t_ctx, 0, 1 + (start - t_ctx) // l_lat)


def _mod_body(c_ref, w_ref, b_ref, o_ref):
    cond = _silu(c_ref[...]).astype(BF16)
    o_ref[...] = _dot(cond, w_ref[...].astype(BF16)) + b_ref[...]


def _modulation(cond, w_mod, b_mod, *, tn=IN_TN):
    depth, d, n = w_mod.shape
    rows = cond.shape[0]
    return pl.pallas_call(
        _mod_body,
        grid=(depth, n // tn),
        in_specs=[pl.BlockSpec((rows, d), lambda l, j: (0, 0)),
                  pl.BlockSpec((None, d, tn), lambda l, j: (l, 0, j)),
                  pl.BlockSpec((None, 1, tn), lambda l, j: (l, 0, j))],
        out_specs=pl.BlockSpec((None, rows, tn), lambda l, j: (l, 0, j)),
        out_shape=jax.ShapeDtypeStruct((depth, rows, n), F32),
        compiler_params=_params("parallel", "parallel"),
        name="modulation",
    )(cond, w_mod, b_mod.reshape(depth, 1, n))


def _stream_specs(x, tm, width, n_grid):
    ids = (lambda i, *_: i) if n_grid == 1 else (lambda i, j: i)
    if not isinstance(x, tuple):
        return [x], [pl.BlockSpec((tm, width), lambda *g: (ids(*g), 0))], None
    split = x[0].shape[0] // tm
    last = x[1].shape[0] // tm - 1
    return (list(x),
            [pl.BlockSpec((tm, width), lambda *g: (jnp.minimum(ids(*g), split - 1), 0)),
             pl.BlockSpec((tm, width), lambda *g: (jnp.clip(ids(*g) - split, 0, last), 0))],
            split)


def _inproj_body(*refs, split):
    n_x = 1 if split is None else 2
    x_refs = refs[:n_x]
    shift_ref, scale_ref, g_ref, w_ref, o_ref, h_ref = refs[n_x:]

    def normalise(x_ref):
        x = x_ref[...]
        y = x * lax.rsqrt(jnp.mean(x * x, axis=-1, keepdims=True) + EPS) * g_ref[...]
        h_ref[...] = (y * (1.0 + scale_ref[...]) + shift_ref[...]).astype(BF16)

    first_col = pl.program_id(1) == 0
    if split is None:
        pl.when(first_col)(lambda: normalise(x_refs[0]))
    else:
        in_ctx = pl.program_id(0) < split
        pl.when(first_col & in_ctx)(lambda: normalise(x_refs[0]))
        pl.when(first_col & jnp.logical_not(in_ctx))(lambda: normalise(x_refs[1]))
    o_ref[...] = _dot(h_ref[...], w_ref[...]).astype(o_ref.dtype)


def _inproj(x, mods, layer, g, w, *, tm, tn, t_ctx, l_lat, name, out_dtype=ACT, n_out=None):
    d = w.shape[0]
    n = w.shape[1] if n_out is None else n_out
    row = functools.partial(_mod_row, tm=tm, t_ctx=t_ctx, l_lat=l_lat)
    x_args, x_specs, split = _stream_specs(x, tm, d, 2)
    t = sum(a.shape[0] for a in x_args)

    def mod_spec(part):
        return pl.BlockSpec((None, None, None, 1, d), lambda i, j: (layer, row(i), part, 0, 0))

    return pl.pallas_call(
        functools.partial(_inproj_body, split=split),
        grid=(t // tm, n // tn),
        in_specs=x_specs + [mod_spec(0), mod_spec(1),
                            pl.BlockSpec((1, d), lambda i, j: (0, 0)),
                            pl.BlockSpec((d, tn), lambda i, j: (0, j))],
        out_specs=pl.BlockSpec((tm, tn), lambda i, j: (i, j)),
        out_shape=jax.ShapeDtypeStruct((t, n), out_dtype),
        scratch_shapes=[pltpu.VMEM((tm, d), BF16)],
        compiler_params=_params("parallel", "arbitrary"),
        name=name,
    )(*x_args, mods, mods, g, w)


def _outproj_body(*refs, n_y, n_x, split, gated_norm, final_norm):
    refs = list(refs)
    y_refs, z_ref, x_refs = refs[:n_y], refs[n_y], refs[n_y + 1:n_y + 1 + n_x]
    rest = refs[n_y + 1 + n_x:]
    gate_ref, w_ref = rest[:2]
    rest = rest[2:]
    ng_ref = rest.pop(0) if gated_norm else None
    fg_ref = rest.pop(0) if final_norm else None
    o_refs = rest

    def emit(y_ref, x_ref, o_ref):
        t = y_ref[...].astype(F32) * _silu(z_ref[...].astype(F32))
        if gated_norm:
            t = t * lax.rsqrt(jnp.mean(t * t, axis=-1, keepdims=True) + EPS) * ng_ref[...]
        xn = x_ref[...] + gate_ref[...] * _dot(t.astype(BF16), w_ref[...])
        if final_norm:
            xn = xn * lax.rsqrt(jnp.mean(xn * xn, axis=-1, keepdims=True) + EPS) * fg_ref[...]
        o_ref[...] = xn

    if split is None:
        emit(y_refs[0], x_refs[0], o_refs[0])
    else:
        in_ctx = pl.program_id(0) < split
        pl.when(in_ctx)(lambda: emit(y_refs[0], x_refs[0], o_refs[0]))
        pl.when(jnp.logical_not(in_ctx))(lambda: emit(y_refs[-1], x_refs[-1], o_refs[-1]))


def _outproj(y, u, z_block, x, mods, layer, w, *, norm_g=None, final_g=None, split_rows=None, tm, t_ctx, l_lat,
             name):
    t, e = u.shape[0], w.shape[0]
    d = w.shape[1]
    row = functools.partial(_mod_row, tm=tm, t_ctx=t_ctx, l_lat=l_lat)
    y_args, y_specs, split_y = _stream_specs(y, tm, e, 1)
    x_args, x_specs, split_in = _stream_specs(x, tm, d, 1)
    in_specs = (y_specs + [pl.BlockSpec((tm, e), lambda i: (i, z_block))] + x_specs
                + [pl.BlockSpec((None, None, None, 1, d), lambda i: (layer, row(i), 2, 0, 0)),
                   pl.BlockSpec((e, d), lambda i: (0, 0))])
    args = y_args + [u] + x_args + [mods, w]
    if norm_g is not None:
        in_specs.append(pl.BlockSpec((1, e), lambda i: (0, 0)))
        args.append(norm_g)
    if final_g is not None:
        in_specs.append(pl.BlockSpec((1, d), lambda i: (0, 0)))
        args.append(final_g)
    if split_rows is None:
        split_out = None
        out_specs = pl.BlockSpec((tm, d), lambda i: (i, 0))
        out_shape = jax.ShapeDtypeStruct((t, d), F32)
    else:
        split_out = split_rows[0] // tm
        last = split_rows[1] // tm - 1
        out_specs = [pl.BlockSpec((tm, d), lambda i: (jnp.minimum(i, split_out - 1), 0)),
                     pl.BlockSpec((tm, d), lambda i: (jnp.clip(i - split_out, 0, last), 0))]
        out_shape = [jax.ShapeDtypeStruct((rows, d), F32) for rows in split_rows]
    return pl.pallas_call(
        functools.partial(_outproj_body, n_y=len(y_args), n_x=len(x_args),
                          split=next((s for s in (split_y, split_in, split_out) if s is not None), None),
                          gated_norm=norm_g is not None, final_norm=final_g is not None),
        grid=(t // tm,),
        in_specs=in_specs,
        out_specs=out_specs,
        out_shape=out_shape,
        compiler_params=_params("arbitrary"),
        name=name,
    )(*args)


_SsdScratch = collections.namedtuple("_SsdScratch", "pad xbd xt bm bt cm e et ct dtt y sf sb")


def _ssd_scratch_shapes(seq_len):
    nc = seq_len // CHUNK
    width = SSD_GROUP_CH + 2 * SSD_STATE
    return [pltpu.VMEM((seq_len + CONV_WIN - CHUNK, width), ACT),
            pltpu.VMEM((nc, SSD_GROUP_HEADS * CHUNK, SSD_GROUP_CH), BF16),
            pltpu.VMEM((nc, SSD_GROUP_CH, CHUNK), F32),
            pltpu.VMEM((seq_len, SSD_STATE), BF16),
            pltpu.VMEM((nc, SSD_STATE, CHUNK), BF16),
            pltpu.VMEM((seq_len, SSD_STATE), BF16),
            pltpu.VMEM((seq_len, LANES), F32),
            pltpu.VMEM((nc, 2 * SSD_GROUP_HEADS, CHUNK), F32),
            pltpu.VMEM((nc, 2 * SSD_GROUP_HEADS, CHUNK), F32),
            pltpu.VMEM((nc, 2 * SSD_GROUP_HEADS, CHUNK), F32),
            pltpu.VMEM((seq_len, SSD_GROUP_CH), F32),
            pltpu.VMEM((SSD_GROUP_CH, SSD_STATE), F32),
            pltpu.VMEM((SSD_GROUP_CH, SSD_STATE), F32)]


def _ssd_body(*refs, seq_len, seqs_per_step, has_h0, want_final):
    refs = list(refs)
    x_ref, b_ref, c_ref, dt_ref, wx_ref, wb_ref, wc_ref, bx_ref, bb_ref, bc_ref, dtb_ref, alog_ref, dsk_ref = refs[:13]
    rest = refs[13:]
    h0_ref = rest.pop(0) if has_h0 else None
    y_ref = rest.pop(0)
    hf_ref = rest.pop(0) if want_final else None
    shift_s = rest.pop(0)
    il = min(SSD_INTERLEAVE, seqs_per_step)
    per_seq = len(rest) // il
    sc = [_SsdScratch(*rest[q * per_seq:(q + 1) * per_seq]) for q in range(il)]
    lanes_q = range(il)
    nc = seq_len // CHUNK
    width = SSD_GROUP_CH + 2 * SSD_STATE
    half = width // 2
    nh = SSD_GROUP_HEADS
    mid = CONV_W // 2

    tail = CONV_WIN - CHUNK - CONV_HALO
    for q in lanes_q:
        sc[q].pad[0:CONV_HALO, :] = jnp.zeros((CONV_HALO, width), ACT)
        sc[q].pad[CONV_HALO + seq_len:CONV_HALO + seq_len + tail, :] = jnp.zeros((tail, width), ACT)
    win_row = lax.broadcasted_iota(jnp.int32, (CHUNK, CONV_WIN), 1)
    tok_row = lax.broadcasted_iota(jnp.int32, (CHUNK, CONV_WIN), 0)
    taps = [j for j in range(CONV_W) if j != mid]
    for n, j in enumerate(taps):
        shift_s[n] = (win_row == tok_row + (CONV_HALO + j - mid)).astype(BF16)

    conv_w = jnp.concatenate([wx_ref[...], wb_ref[...], wc_ref[...]], axis=1)
    conv_b = jnp.concatenate([bx_ref[...], bb_ref[...], bc_ref[...]], axis=1)
    a_row = -jnp.exp(alog_ref[...])
    row_i = lax.broadcasted_iota(jnp.int32, (CHUNK, CHUNK), 0)
    col_i = lax.broadcasted_iota(jnp.int32, (CHUNK, CHUNK), 1)
    lower = row_i >= col_i
    upper = row_i <= col_i
    lower_b = lower.astype(BF16)
    fwd_lane = lax.broadcasted_iota(jnp.int32, (1, LANES), 1) < nh
    head_of_lane = lax.broadcasted_iota(jnp.int32, (CHUNK, SSD_GROUP_CH), 1) // SSD_HEAD_DIM

    def chunk_rows(ci):
        return pl.ds(pl.multiple_of(ci * CHUNK, CHUNK), CHUNK)

    def copy_chunk(offs, ci, carry):
        base = pl.multiple_of(ci * CHUNK, CHUNK)
        dst = pl.ds(base + CONV_HALO, CHUNK)
        for q in lanes_q:
            src = pl.ds(offs[q] + base, CHUNK)
            sc[q].pad[dst, 0:SSD_GROUP_CH] = x_ref[src, :]
            sc[q].pad[dst, SSD_GROUP_CH:SSD_GROUP_CH + SSD_STATE] = b_ref[src, :]
            sc[q].pad[dst, SSD_GROUP_CH + SSD_STATE:width] = c_ref[src, :]
        return carry

    def prep_chunk(offs, ci, carry):
        r = chunk_rows(ci)
        units = [(q, h) for h in range(2) for q in lanes_q]

        def window(unit):
            q, h = unit
            return sc[q].pad[pl.ds(pl.multiple_of(ci * CHUNK, CHUNK), CONV_WIN), h * half:(h + 1) * half]

        def shift_dots(unit):
            win = window(unit)
            return [_dot(shift_s[t], win) for t in range(len(taps))]

        def finish(unit, shifted):
            q, h = unit
            cols = slice(h * half, (h + 1) * half)
            acc = conv_b[:, cols] + window(unit)[CONV_HALO:CONV_HALO + CHUNK].astype(F32) * conv_w[mid:mid + 1, cols]
            for t, j in enumerate(taps):
                acc = acc + shifted[t] * conv_w[j:j + 1, cols]
            v = _silu(acc)
            if h == 0:
                sc[q].y[r, :] = v * dsk_ref[...]
                vb = v.astype(BF16)
                for hd in range(nh):
                    sc[q].xbd[ci, hd * CHUNK:(hd + 1) * CHUNK, :] = jnp.where(head_of_lane == hd, vb,
                                                                              jnp.zeros_like(vb))
                sc[q].xt[ci] = v.T
            else:
                bm = v[:, 0:SSD_STATE]
                sc[q].bm[r, :] = bm.astype(BF16)
                sc[q].bt[ci] = bm.T.astype(BF16)
                sc[q].cm[r, :] = v[:, SSD_STATE:half].astype(BF16)

        dts = [_softplus(dt_ref[pl.ds(offs[q] + pl.multiple_of(ci * CHUNK, CHUNK), CHUNK), :] + dtb_ref[...])
               for q in lanes_q]
        das = [dt * a_row for dt in dts]
        pending = shift_dots(units[0])
        cums = []
        for i, unit in enumerate(units):
            following = shift_dots(units[i + 1]) if i + 1 < len(units) else None
            if unit[1] == 0:
                cums.append(_cumsum_rows(lower_b, das[unit[0]]))
            finish(unit, pending)
            pending = following
        for q in lanes_q:
            e = jnp.where(fwd_lane, cums[q], cums[q] - das[q])
            sc[q].e[r, :] = e
            sc[q].et[ci] = e.T[0:2 * nh]
            sc[q].ct[ci] = cums[q].T[0:2 * nh]
            sc[q].dtt[ci] = dts[q].T[0:2 * nh]
        return carry

    def scan_chunk(i, carry):
        chunks = (i, nc - 1 - i)
        rows = [chunk_rows(ci) for ci in chunks]
        streams = [(q, d) for q in lanes_q for d in range(2)]
        st = {(q, d): (sc[q].sf if d == 0 else sc[q].sb) for q, d in streams}
        s_prev = {k: st[k][...] for k in streams}
        g, cs = {}, {}
        for q, d in streams:
            cm = sc[q].cm[rows[d], :]
            g[q, d] = _dot(cm, sc[q].bt[chunks[d]])
            cs[q, d] = _dot_nt(cm, s_prev[q, d].astype(BF16))
        w_rows, keeps, ys = {}, {}, {}
        for q, d in streams:
            e = sc[q].e[rows[d], :]
            e_t, c_t, dt_t = sc[q].et[chunks[d]], sc[q].ct[chunks[d]], sc[q].dtt[chunks[d]]
            w_d, keep_d, mixes, offs = [], [], [], []
            for h in range(nh):
                col = h + nh * d
                ec = jnp.broadcast_to(e[:, col:col + 1], (CHUNK, CHUNK))
                er = e_t[col:col + 1, :]
                dt_h = dt_t[col:col + 1, :]
                log_dt = jnp.log(dt_h)
                tot = c_t[col:col + 1, CHUNK - 1:CHUNK]
                if d == 0:
                    decay_dt = jnp.exp(jnp.where(lower, ec - (er - log_dt), -jnp.inf))
                    offs.append(jnp.exp(ec))
                    w_d.append(jnp.broadcast_to(dt_h * jnp.exp(tot - er), (SSD_HEAD_DIM, CHUNK)))
                else:
                    decay_dt = jnp.exp(jnp.where(upper, (er + log_dt) - ec, -jnp.inf))
                    offs.append(jnp.exp(tot - ec))
                    w_d.append(jnp.broadcast_to(dt_h * jnp.exp(er), (SSD_HEAD_DIM, CHUNK)))
                keep_d.append(jnp.broadcast_to(jnp.exp(tot), (SSD_HEAD_DIM, SSD_STATE)))
                mixes.append((g[q, d] * decay_dt).astype(BF16))
            y_diag = _dot(jnp.concatenate(mixes, axis=1), sc[q].xbd[chunks[d]])
            off = jnp.concatenate([offs[nh - 1]] * (SSD_GROUP_CH // CHUNK), axis=1)
            for h in range(nh - 2, -1, -1):
                off = jnp.where(head_of_lane == h, jnp.concatenate([offs[h]] * (SSD_GROUP_CH // CHUNK), axis=1), off)
            ys[q, d] = y_diag + cs[q, d] * off
            w_rows[q, d] = jnp.concatenate(w_d, axis=0)
            keeps[q, d] = jnp.concatenate(keep_d, axis=0)
        for q, d in streams:
            w_t = (sc[q].xt[chunks[d]] * w_rows[q, d]).astype(BF16)
            st[q, d][...] = keeps[q, d] * s_prev[q, d] + _dot(w_t, sc[q].bm[rows[d], :])
            sc[q].y[rows[d], :] += ys[q, d]
        return carry

    def emit_chunk(offs, ci, carry):
        base = pl.multiple_of(ci * CHUNK, CHUNK)
        for q in lanes_q:
            y_ref[pl.ds(offs[q] + base, CHUNK), :] = sc[q].y[pl.ds(base, CHUNK), :].astype(y_ref.dtype)
        return carry

    def one_group(s, carry):
        seqs = [s * il + q for q in lanes_q]
        offs = [pl.multiple_of(sq * seq_len, CHUNK) for sq in seqs]
        lax.fori_loop(0, nc, functools.partial(copy_chunk, offs), 0)
        lax.fori_loop(0, nc, functools.partial(prep_chunk, offs), 0)
        for q in lanes_q:
            if has_h0:
                sc[q].sf[...] = h0_ref[seqs[q], 0].reshape(SSD_GROUP_CH, SSD_STATE)
                sc[q].sb[...] = h0_ref[seqs[q], 1].reshape(SSD_GROUP_CH, SSD_STATE)
            else:
                sc[q].sf[...] = jnp.zeros((SSD_GROUP_CH, SSD_STATE), F32)
                sc[q].sb[...] = jnp.zeros((SSD_GROUP_CH, SSD_STATE), F32)
        lax.fori_loop(0, nc, scan_chunk, 0)
        lax.fori_loop(0, nc, functools.partial(emit_chunk, offs), 0)
        if want_final:
            for q in lanes_q:
                hf_ref[seqs[q], 0] = sc[q].sf[...].reshape(SSD_GROUP_HEADS, SSD_HEAD_DIM, SSD_STATE)
                hf_ref[seqs[q], 1] = sc[q].sb[...].reshape(SSD_GROUP_HEADS, SSD_HEAD_DIM, SSD_STATE)
        return carry

    lax.fori_loop(0, seqs_per_step // il, one_group, 0)


def _ssd_call(u, dt, conv_w, conv_b, dtb, alog, dsk, h0, *, n_seq, seq_len, seqs_per_step, row_block0,
              want_final, name):
    t, n_u = u.shape
    e = dsk.shape[1]
    groups = e // SSD_GROUP_CH
    xb0 = e // SSD_GROUP_CH
    bb0 = 2 * e // SSD_STATE
    cb0 = bb0 + groups
    has_h0 = h0 is not None
    seq = lambda b: row_block0 + b
    rows = seqs_per_step * seq_len
    in_specs = [pl.BlockSpec((rows, SSD_GROUP_CH), lambda b, g: (seq(b), xb0 + g)),
                pl.BlockSpec((rows, SSD_STATE), lambda b, g: (seq(b), bb0 + g)),
                pl.BlockSpec((rows, SSD_STATE), lambda b, g: (seq(b), cb0 + g)),
                pl.BlockSpec((rows, LANES), lambda b, g: (seq(b), g)),
                pl.BlockSpec((CONV_W, SSD_GROUP_CH), lambda b, g: (0, g)),
                pl.BlockSpec((CONV_W, SSD_STATE), lambda b, g: (0, bb0 - xb0 * 2 + g)),
                pl.BlockSpec((CONV_W, SSD_STATE), lambda b, g: (0, cb0 - xb0 * 2 + g)),
                pl.BlockSpec((1, SSD_GROUP_CH), lambda b, g: (0, g)),
                pl.BlockSpec((1, SSD_STATE), lambda b, g: (0, bb0 - xb0 * 2 + g)),
                pl.BlockSpec((1, SSD_STATE), lambda b, g: (0, cb0 - xb0 * 2 + g)),
                pl.BlockSpec((1, LANES), lambda b, g: (0, g)),
                pl.BlockSpec((1, LANES), lambda b, g: (0, g)),
                pl.BlockSpec((1, SSD_GROUP_CH), lambda b, g: (0, g))]
    args = [u, u, u, dt, conv_w, conv_w, conv_w, conv_b, conv_b, conv_b, dtb, alog, dsk]
    state_block = (seqs_per_step, 2, SSD_GROUP_HEADS, SSD_HEAD_DIM, SSD_STATE)
    if has_h0:
        in_specs.append(pl.BlockSpec(state_block, lambda b, g: (b, 0, g, 0, 0)))
        args.append(h0)
    out_specs = [pl.BlockSpec((rows, SSD_GROUP_CH), lambda b, g: (b, g))]
    out_shape = [jax.ShapeDtypeStruct((n_seq * seq_len, e), ACT)]
    if want_final:
        out_specs.append(pl.BlockSpec(state_block, lambda b, g: (b, 0, g, 0, 0)))
        out_shape.append(jax.ShapeDtypeStruct((n_seq, 2, e // SSD_HEAD_DIM, SSD_HEAD_DIM, SSD_STATE), F32))
    return pl.pallas_call(
        functools.partial(_ssd_body, seq_len=seq_len, seqs_per_step=seqs_per_step, has_h0=has_h0,
                          want_final=want_final),
        grid=(n_seq // seqs_per_step, groups),
        in_specs=in_specs,
        out_specs=out_specs,
        out_shape=out_shape,
        scratch_shapes=([pltpu.VMEM((CONV_W - 1, CHUNK, CONV_WIN), BF16)]
                        + _ssd_scratch_shapes(seq_len) * min(SSD_INTERLEAVE, seqs_per_step)),
        compiler_params=_params("parallel", "parallel"),
        name=name,
    )(*args)


def _group_lanes(v, groups):
    per_group = v.reshape(2, groups, SSD_GROUP_HEADS).transpose(1, 0, 2).reshape(groups, 2 * SSD_GROUP_HEADS)
    return jnp.pad(per_group, ((0, 0), (0, LANES - 2 * SSD_GROUP_HEADS))).reshape(1, groups * LANES)


def _gmlp_body(u_ref, v_ref, lng_ref, lnb_ref, ws_ref, bs_ref, o_ref, *, tm):
    groups = ws_ref.shape[0]
    gch = u_ref.shape[1] // groups
    for c in range(tm // CHUNK):
        rows = slice(c * CHUNK, (c + 1) * CHUNK)
        v = _gelu(v_ref[rows, :].astype(F32))
        vc = v - jnp.mean(v, axis=-1, keepdims=True)
        vn = vc * lax.rsqrt(jnp.mean(vc * vc, axis=-1, keepdims=True) + EPS) * lng_ref[...] + lnb_ref[...]
        vb = vn.astype(BF16)
        for g in range(groups):
            cols = slice(g * gch, (g + 1) * gch)
            s = _dot(ws_ref[g].astype(BF16), vb[:, cols]) + bs_ref[:, cols]
            o_ref[rows, cols] = (_gelu(u_ref[rows, cols].astype(F32)) * s).astype(o_ref.dtype)


def _gmlp(u, ln_g, ln_b, w_s, b_lanes, *, tm=GMLP_TM):
    t = u.shape[0]
    e = ln_g.shape[1]
    return pl.pallas_call(
        functools.partial(_gmlp_body, tm=tm),
        grid=(t // tm,),
        in_specs=[pl.BlockSpec((tm, e), lambda i: (i, 0)),
                  pl.BlockSpec((tm, e), lambda i: (i, 1)),
                  pl.BlockSpec((1, e), lambda i: (0, 0)),
                  pl.BlockSpec((1, e), lambda i: (0, 0)),
                  pl.BlockSpec(w_s.shape, lambda i: (0, 0, 0)),
                  pl.BlockSpec((CHUNK, e), lambda i: (0, 0))],
        out_specs=pl.BlockSpec((tm, e), lambda i: (i, 0)),
        out_shape=jax.ShapeDtypeStruct((t, e), ACT),
        compiler_params=_params("parallel"),
        name="gmlp",
    )(u, u, ln_g, ln_b, w_s, b_lanes)


def _s5_body(lr_ref, li_ref, ls_ref, brt_ref, bit_ref, cre_ref, cim_ref, h0_ref, u_ref, y_ref, fin_ref,
             win_s, t_s, ef_s, eb_s, z_s, a_s, zin_s, d_s, spf_s, spb_s, yo_s, uf_s,
             *, seqs, l_ctx, n_lat, l_lat, ctx_parts):
    bst = S5_BLOCK_ST
    ng = LANES // S5_GROUP
    kw = S5_T * LANES
    part = pl.program_id(1)

    @pl.when(part == 0)
    def _build():
        own = (lax.broadcasted_iota(jnp.int32, (ng, S5_GROUP, bst), 2) // S5_STATE
               == lax.broadcasted_iota(jnp.int32, (ng, S5_GROUP, bst), 0))

        def spread(v):
            return jnp.where(own, v[None], 0.0).reshape(LANES, bst).astype(BF16)

        tau = lax.broadcasted_iota(jnp.int32, (S5_T + 8, 1), 0).astype(F32)
        e_refs = (ef_s, eb_s)
        for d in range(2):
            lr, li = lr_ref[d], li_ref[d]
            step = jnp.exp(ls_ref[d])
            mag = jnp.exp(tau * (lr * step))
            p_re = mag * jnp.cos(tau * (li * step))
            p_im = mag * jnp.sin(tau * (li * step))
            ab_re, ab_im = p_re[1:2], p_im[1:2]
            den = lr * lr + li * li
            nr = ab_re - 1.0
            cr = (nr * lr + ab_im * li) / den
            ci = (ab_im * lr - nr * li) / den
            brt, bit = brt_ref[d], bit_ref[d]
            bb_re = cr * brt - ci * bit
            bb_im = cr * bit + ci * brt
            cre, cim = cre_ref[d], cim_ref[d]
            for k in range(S5_T):
                rows = slice(k * LANES, (k + 1) * LANES)
                tq = S5_T - 1 - k if d == 0 else k
                te = k + 1 if d == 0 else S5_T - k
                pr, pi = p_re[tq:tq + 1], p_im[tq:tq + 1]
                win_s[rows, 2 * d * bst:(2 * d + 1) * bst] = spread(pr * bb_re - pi * bb_im)
                win_s[rows, (2 * d + 1) * bst:(2 * d + 2) * bst] = spread(pr * bb_im + pi * bb_re)
                pr, pi = p_re[te:te + 1], p_im[te:te + 1]
                e_refs[d][rows, 0:bst] = spread(cre * pr - cim * pi)
                e_refs[d][rows, bst:2 * bst] = spread(-(cre * pi + cim * pr))
            c_own = jnp.concatenate([spread(cre), spread(-cim)], axis=1)
            z = _dot_nt(win_s[:, 2 * d * bst:(2 * d + 2) * bst], c_own)
            if d == 0:
                z_s[0:kw, :] = z
            else:
                z_s[kw - LANES:kw, :] += z[0:LANES]
                z_s[kw:2 * kw - LANES, :] = z[LANES:kw]
            a_s[2 * d:2 * d + 1, :] = p_re[S5_T:S5_T + 1]
            a_s[2 * d + 1:2 * d + 2, :] = p_im[S5_T:S5_T + 1]
        for k in range(S5_T):
            r0 = (S5_T - 1 - k) * LANES
            t_s[:, k * LANES:(k + 1) * LANES] = z_s[r0:r0 + kw, :].astype(BF16)

    def outputs():
        yo_s[...] = (_dot(zin_s[...], t_s[...]) + _dot_nt(spf_s[...].astype(BF16), ef_s[...])
                     + _dot_nt(spb_s[...].astype(BF16), eb_s[...]))

    @pl.when(part < ctx_parts)
    def _context():
        nj = l_ctx // S5_T
        uf_s[...] = u_ref[...].astype(F32)
        for j in range(nj):
            for k in range(S5_T):
                zin_s[j * seqs:(j + 1) * seqs, k * LANES:(k + 1) * LANES] = (
                    uf_s[pl.ds(j * S5_T + k, seqs, stride=l_ctx), :].astype(BF16))
        d_s[...] = _dot(zin_s[...], win_s[...])
        yo_s[...] = _dot(zin_s[...], t_s[...])
        coef = [jnp.broadcast_to(a_s[i:i + 1, :], (seqs, bst)) for i in range(4)]
        fr = fi = br = bi = jnp.zeros((seqs, bst), F32)
        for j in range(nj):
            rf = slice(j * seqs, (j + 1) * seqs)
            rb = slice((nj - 1 - j) * seqs, (nj - j) * seqs)
            spf_s[rf, 0:bst] = fr
            spf_s[rf, bst:2 * bst] = fi
            spb_s[rb, 0:bst] = br
            spb_s[rb, bst:2 * bst] = bi
            fr, fi, br, bi = (coef[0] * fr - coef[1] * fi + d_s[rf, 0:bst],
                              coef[0] * fi + coef[1] * fr + d_s[rf, bst:2 * bst],
                              coef[2] * br - coef[3] * bi + d_s[rb, 2 * bst:3 * bst],
                              coef[2] * bi + coef[3] * br + d_s[rb, 3 * bst:4 * bst])
        for d, ri, val in ((0, 0, fr), (0, 1, fi), (1, 0, br), (1, 1, bi)):
            for gg in range(ng):
                fin_ref[:, d, ri, gg, :] = val[:, gg * S5_STATE:(gg + 1) * S5_STATE]
        yo_s[...] += (_dot_nt(spf_s[...].astype(BF16), ef_s[...]) + _dot_nt(spb_s[...].astype(BF16), eb_s[...]))
        for j in range(nj):
            for k in range(S5_T):
                y_ref[pl.ds(j * S5_T + k, seqs, stride=l_ctx), :] = (
                    yo_s[j * seqs:(j + 1) * seqs, k * LANES:(k + 1) * LANES])

    @pl.when(part == ctx_parts)
    def _latent():
        nj = l_lat // S5_T
        uf_s[...] = u_ref[...].astype(F32)
        for b in range(n_lat):
            for k in range(S5_T):
                zin_s[b * nj:(b + 1) * nj, k * LANES:(k + 1) * LANES] = (
                    uf_s[pl.ds(b * l_lat + k, nj, stride=S5_T), :].astype(BF16))
        d_s[...] = _dot(zin_s[...], win_s[...])
        coef_re = jnp.concatenate([jnp.broadcast_to(a_s[0:1, :], (n_lat, bst)),
                                   jnp.broadcast_to(a_s[2:3, :], (n_lat, bst))], axis=0)
        coef_im = jnp.concatenate([jnp.broadcast_to(a_s[1:2, :], (n_lat, bst)),
                                   jnp.broadcast_to(a_s[3:4, :], (n_lat, bst))], axis=0)
        s_re0 = jnp.concatenate([h0_ref[0, 0], h0_ref[1, 0]], axis=0)
        s_im0 = jnp.concatenate([h0_ref[0, 1], h0_ref[1, 1]], axis=0)

        def step(j, carry):
            s_re, s_im = carry
            d_re, d_im = [], []
            for b in range(n_lat):
                row = pl.ds(b * nj + j, 1)
                spf_s[row, 0:bst] = s_re[b:b + 1]
                spf_s[row, bst:2 * bst] = s_im[b:b + 1]
                d_re.append(d_s[row, 0:bst])
                d_im.append(d_s[row, bst:2 * bst])
            for b in range(n_lat):
                row = pl.ds(b * nj + nj - 1 - j, 1)
                spb_s[row, 0:bst] = s_re[n_lat + b:n_lat + b + 1]
                spb_s[row, bst:2 * bst] = s_im[n_lat + b:n_lat + b + 1]
                d_re.append(d_s[row, 2 * bst:3 * bst])
                d_im.append(d_s[row, 3 * bst:4 * bst])
            d_re = jnp.concatenate(d_re, axis=0)
            d_im = jnp.concatenate(d_im, axis=0)
            return coef_re * s_re - coef_im * s_im + d_re, coef_re * s_im + coef_im * s_re + d_im

        lax.fori_loop(0, nj, step, (s_re0, s_im0))
        outputs()
        for b in range(n_lat):
            for k in range(S5_T):
                y_ref[pl.ds(b * l_lat + k, nj, stride=S5_T), :] = yo_s[b * nj:(b + 1) * nj, k * LANES:(k + 1) * LANES]


def _s5_core(prm, h0, u, *, n_ctx, l_ctx, n_lat, l_lat, e):
    blocks = prm[0].shape[0]
    part_tokens = n_lat * l_lat
    t = u.shape[0]
    n_parts = t // part_tokens
    seqs = part_tokens // l_ctx
    r = part_tokens // S5_T
    vec = pl.BlockSpec((None, 2, 1, S5_BLOCK_ST), lambda g, p: (g, 0, 0, 0))
    mat = pl.BlockSpec((None, 2, S5_GROUP, S5_BLOCK_ST), lambda g, p: (g, 0, 0, 0))
    k_in = S5_T * LANES
    return pl.pallas_call(
        functools.partial(_s5_body, seqs=seqs, l_ctx=l_ctx, n_lat=n_lat, l_lat=l_lat, ctx_parts=n_parts - 1),
        grid=(blocks, n_parts),
        in_specs=[vec, vec, vec, mat, mat, mat, mat,
                  pl.BlockSpec((None, 2, 2, n_lat, S5_BLOCK_ST), lambda g, p: (g, 0, 0, 0, 0)),
                  pl.BlockSpec((part_tokens, LANES), lambda g, p: (p, g))],
        out_specs=[pl.BlockSpec((part_tokens, LANES), lambda g, p: (p, g)),
                   pl.BlockSpec((seqs, 2, 2, LANES // S5_GROUP, S5_STATE),
                                lambda g, p: (jnp.minimum(p, n_parts - 2), 0, 0, g, 0))],
        out_shape=[jax.ShapeDtypeStruct((t, e), F32),
                   jax.ShapeDtypeStruct((n_ctx, 2, 2, e // S5_GROUP, S5_STATE), F32)],
        scratch_shapes=[pltpu.VMEM((k_in, 4 * S5_BLOCK_ST), BF16),
                        pltpu.VMEM((k_in, k_in), BF16),
                        pltpu.VMEM((k_in, 2 * S5_BLOCK_ST), BF16),
                        pltpu.VMEM((k_in, 2 * S5_BLOCK_ST), BF16),
                        pltpu.VMEM(((2 * S5_T - 1) * LANES, LANES), F32),
                        pltpu.VMEM((8, S5_BLOCK_ST), F32),
                        pltpu.VMEM((r, k_in), BF16),
                        pltpu.VMEM((r, 4 * S5_BLOCK_ST), F32),
                        pltpu.VMEM((r, 2 * S5_BLOCK_ST), F32),
                        pltpu.VMEM((r, 2 * S5_BLOCK_ST), F32),
                        pltpu.VMEM((r, k_in), F32),
                        pltpu.VMEM((part_tokens, LANES), F32)],
        compiler_params=_params("parallel", "arbitrary"),
        name="s5_core",
    )(*prm, h0, u)


def _s5_glu_body(ys_ref, u_ref, dsk_ref, w_ref, b_ref, o_ref):
    y = _gelu(ys_ref[...] + dsk_ref[...] * u_ref[...].astype(F32))
    o_ref[...] = (y * jax.nn.sigmoid(_dot(y.astype(BF16), w_ref[...]) + b_ref[...])).astype(o_ref.dtype)


def _s5_glu(ys, u, dsk, w, b, *, tm=GLU_TM):
    t, e = ys.shape
    return pl.pallas_call(
        _s5_glu_body,
        grid=(t // tm,),
        in_specs=[pl.BlockSpec((tm, e), lambda i: (i, 0)),
                  pl.BlockSpec((tm, e), lambda i: (i, 0)),
                  pl.BlockSpec((1, e), lambda i: (0, 0)),
                  pl.BlockSpec((e, e), lambda i: (0, 0)),
                  pl.BlockSpec((1, e), lambda i: (0, 0))],
        out_specs=pl.BlockSpec((tm, e), lambda i: (i, 0)),
        out_shape=jax.ShapeDtypeStruct((t, e), ACT),
        compiler_params=_params("parallel"),
        name="s5_glu",
    )(ys, u, dsk, w, b)


def _ctx_attn_body(q_ref, k_ref, v_ref, o_ref, ko_ref, vo_ref):
    scale = HEAD_DIM ** -0.5
    seq_len = q_ref.shape[0]
    first = lax.broadcasted_iota(jnp.int32, (seq_len, LANES), 1) < HEAD_DIM
    masks = (first, jnp.logical_not(first))
    n_pairs = q_ref.shape[1] // LANES
    logits, values = [], []
    for pair in range(n_pairs):
        cols = slice(pair * LANES, (pair + 1) * LANES)
        q2, k2, v2 = q_ref[:, cols].astype(F32) * scale, k_ref[:, cols], v_ref[:, cols]
        kb, vb = k2.astype(BF16), v2.astype(BF16)
        for h in range(LANES // HEAD_DIM):
            ko_ref[2 * pair + h] = k2[:, h * HEAD_DIM:(h + 1) * HEAD_DIM].astype(F32)
            vo_ref[2 * pair + h] = v2[:, h * HEAD_DIM:(h + 1) * HEAD_DIM].astype(F32)
            logits.append(_dot_nt(jnp.where(masks[h], q2, 0.0).astype(BF16), kb))
            values.append(jnp.where(masks[h], vb, jnp.ones_like(vb)))
    probs = [jnp.exp(s - jnp.max(s, axis=-1, keepdims=True)).astype(BF16) for s in logits]
    for pair in range(n_pairs):
        o0, o1 = _dot(probs[2 * pair], values[2 * pair]), _dot(probs[2 * pair + 1], values[2 * pair + 1])
        num = jnp.where(first, o0, o1)
        den = jnp.where(first, pltpu.roll(o0, HEAD_DIM, 1), pltpu.roll(o1, HEAD_DIM, 1))
        o_ref[:, pair * LANES:(pair + 1) * LANES] = (num / den).astype(o_ref.dtype)


def _ctx_attention(u, *, n_seq, seq_len, e, width=CTX_ATTN_WIDTH):
    t = n_seq * seq_len
    hp = e // width
    heads = e // HEAD_DIM
    kv_block = (None, width // HEAD_DIM, seq_len, HEAD_DIM)
    in_specs = [pl.BlockSpec((seq_len, width), lambda b, h: (b, h)),
                pl.BlockSpec((seq_len, width), lambda b, h: (b, hp + h)),
                pl.BlockSpec((seq_len, width), lambda b, h: (b, 2 * hp + h))]
    return pl.pallas_call(
        _ctx_attn_body,
        grid=(n_seq, hp),
        in_specs=in_specs,
        out_specs=[pl.BlockSpec((seq_len, width), lambda b, h: (b, h)),
                   pl.BlockSpec(kv_block, lambda b, h: (b, h, 0, 0)),
                   pl.BlockSpec(kv_block, lambda b, h: (b, h, 0, 0))],
        out_shape=[jax.ShapeDtypeStruct((t, e), ACT),
                   jax.ShapeDtypeStruct((n_seq, heads, seq_len, HEAD_DIM), F32),
                   jax.ShapeDtypeStruct((n_seq, heads, seq_len, HEAD_DIM), F32)],
        compiler_params=_params("parallel", "parallel"),
        name="ctx_attention",
    )(u, u, u)


def _nat_bias_body(rp_ref, o_ref):
    n_rel_rows = rp_ref.shape[1]
    q = lax.broadcasted_iota(jnp.int32, (GRID_W, LANES), 0)
    lane = lax.broadcasted_iota(jnp.int32, (GRID_W, LANES), 1)
    kc = lane % GRID_W
    c_start = jnp.clip(q - WIN_COLS // 2, 0, GRID_W - WIN_COLS)
    ok = (kc >= c_start) & (kc < c_start + WIN_COLS)
    left = lane < GRID_W
    for h in range(o_ref.shape[0]):
        halves = []
        for i in range(n_rel_rows):
            row = jnp.broadcast_to(rp_ref[h, i:i + 1, :], (GRID_W, LANES))
            halves.append((pltpu.roll(row, 0, 1, stride=1, stride_axis=0),
                           pltpu.roll(row, GRID_W, 1, stride=1, stride_axis=0)))
        for i in range(n_rel_rows - 1):
            o_ref[h, i] = jnp.where(ok, jnp.where(left, halves[i][0], halves[i + 1][1]), MASKED)


def _nat_bias(rpb, *, heads_per_step=4):
    heads, n_rel_rows, n_rel_cols = rpb.shape
    rows = jnp.roll(jnp.pad(rpb, ((0, 0), (0, 0), (0, LANES - n_rel_cols))), -(WIN_COLS - 1), axis=-1)
    return pl.pallas_call(
        _nat_bias_body,
        grid=(heads // heads_per_step,),
        in_specs=[pl.BlockSpec((heads_per_step, n_rel_rows, LANES), lambda h: (h, 0, 0))],
        out_specs=pl.BlockSpec((heads_per_step, n_rel_rows - 1, GRID_W, 2 * GRID_W), lambda h: (h, 0, 0, 0)),
        out_shape=jax.ShapeDtypeStruct((heads, n_rel_rows - 1, GRID_W, 2 * GRID_W), F32),
        compiler_params=_params("parallel"),
        name="nat_bias",
    )(rows)


def _nat_body(q_ref, k_ref, v_ref, ck_ref, cv_ref, bias_ref, o_ref, ckb_s, cvb_s, *, rows):
    scale = HEAD_DIM ** -0.5
    wr = min(WIN_ROWS, rows)
    nw = wr * GRID_W
    heads = tuple(range(LANES // HEAD_DIM))
    first = lax.broadcasted_iota(jnp.int32, (GRID_W, LANES), 1) < HEAD_DIM
    masks = (first, jnp.logical_not(first))
    ckb_s[...] = jnp.concatenate([ck_ref[0], ck_ref[1]], axis=1).astype(BF16)
    cvb_s[...] = jnp.concatenate([cv_ref[0], cv_ref[1]], axis=1).astype(BF16)

    def row_group(g, carry):
        rws = [g * NAT_ROWS_PER_STEP + i for i in range(NAT_ROWS_PER_STEP)]
        starts = [jnp.clip(r - wr // 2, 0, rows - wr) for r in rws]
        q_rows = [pl.ds(pl.multiple_of(r * GRID_W, GRID_W), GRID_W) for r in rws]
        k_rows = [pl.ds(pl.multiple_of(rs * GRID_W, GRID_W), nw) for rs in starts]
        logits = []
        for r, rs, qr, kr in zip(rws, starts, q_rows, k_rows):
            q2 = q_ref[qr, :].astype(F32) * scale
            kb = k_ref[kr, :].astype(BF16)
            for h in heads:
                q = jnp.where(masks[h], q2, 0.0).astype(BF16)
                logits.append((_dot_nt(q, kb), _dot_nt(q, ckb_s[...])))
        probs = []
        for idx, (s_win, s_ctx) in enumerate(logits):
            r, rs, h = rws[idx // 2], starts[idx // 2], heads[idx % 2]
            i0 = (WIN_ROWS - 1) - (r - rs)
            s_win = s_win + jnp.concatenate([bias_ref[h, i0 + 2 * jj] for jj in range(wr // 2)], axis=1)
            m = jnp.maximum(jnp.max(s_win, axis=-1, keepdims=True), jnp.max(s_ctx, axis=-1, keepdims=True))
            p_win = jnp.exp(s_win - m)
            p_ctx = jnp.exp(s_ctx - m)
            inv = 1.0 / (jnp.sum(p_win, axis=-1, keepdims=True) + jnp.sum(p_ctx, axis=-1, keepdims=True))
            probs.append(((p_win * inv).astype(BF16), (p_ctx * inv).astype(BF16)))
        for i, (qr, kr) in enumerate(zip(q_rows, k_rows)):
            vb = v_ref[kr, :].astype(BF16)
            outs = [_dot(probs[2 * i + h][0], vb) + _dot(probs[2 * i + h][1], cvb_s[...]) for h in heads]
            o_ref[qr, :] = jnp.where(first, outs[0], outs[1]).astype(o_ref.dtype)
        return carry

    lax.fori_loop(0, rows // NAT_ROWS_PER_STEP, row_group, 0)


def _nat_attention(u, cache_k, cache_v, cache_layer, bias, *, n_seq, seq_len, row_block0, e):
    hp = e // LANES
    hpb = LANES // HEAD_DIM
    past = cache_k.shape[3]
    seq = lambda b: row_block0 + b
    cache_block = (None, None, hpb, past, HEAD_DIM)
    return pl.pallas_call(
        functools.partial(_nat_body, rows=seq_len // GRID_W),
        grid=(n_seq, hp),
        in_specs=[pl.BlockSpec((seq_len, LANES), lambda b, h: (seq(b), h)),
                  pl.BlockSpec((seq_len, LANES), lambda b, h: (seq(b), hp + h)),
                  pl.BlockSpec((seq_len, LANES), lambda b, h: (seq(b), 2 * hp + h)),
                  pl.BlockSpec(cache_block, lambda b, h: (b, cache_layer, h, 0, 0)),
                  pl.BlockSpec(cache_block, lambda b, h: (b, cache_layer, h, 0, 0)),
                  pl.BlockSpec((hpb,) + bias.shape[1:], lambda b, h: (h, 0, 0, 0))],
        out_specs=pl.BlockSpec((seq_len, LANES), lambda b, h: (b, h)),
        out_shape=jax.ShapeDtypeStruct((n_seq * seq_len, e), ACT),
        scratch_shapes=[pltpu.VMEM((past, LANES), BF16), pltpu.VMEM((past, LANES), BF16)],
        compiler_params=_params("parallel", "parallel"),
        name="nat_attention",
    )(u, u, u, cache_k, cache_v, bias)


def kernel(x_prompt, x_sample, state_ssd, state_s5, cache_k, cache_v, c, c_ctx, norm_g, w_mod, b_mod, w_out, final_g, ssd_w_in, ssd_conv_w, ssd_conv_b, ssd_dt_bias, ssd_a_log, ssd_d, ssd_norm_g, mlp_w_in, mlp_ln_g, mlp_ln_b, mlp_w_s, mlp_b_s, s5_w_in, s5_lam_re, s5_lam_im, s5_log_step, s5_b_re, s5_b_im, s5_c_re, s5_c_im, s5_d, s5_w_glu, s5_b_glu, nat_w_in, nat_rpb):
    n_ctx, l_ctx, d = x_prompt.shape
    n_lat, l_lat, _ = x_sample.shape
    t_ctx = n_ctx * l_ctx
    depth = norm_g.shape[0]
    e = w_out.shape[1]
    assert depth == 4 and l_lat % l_ctx == 0 and t_ctx % (n_lat * l_lat) == 0
    tiles = dict(tm=OUT_TM, t_ctx=t_ctx, l_lat=l_lat)
    tiles_in = dict(tm=IN_TM, tn=IN_TN, t_ctx=t_ctx, l_lat=l_lat)

    t_lat = n_lat * l_lat
    x = (x_prompt.reshape(t_ctx, d), x_sample.reshape(t_lat, d))
    cond = jnp.concatenate([c_ctx[None], c, jnp.zeros((MOD_ROWS - 1 - n_lat, d), F32)], axis=0)
    mods = _modulation(cond, w_mod, b_mod).reshape(depth, MOD_ROWS, 3, 1, d)
    w_out_b = w_out.astype(BF16)

    groups = e // SSD_GROUP_CH
    n_main = 3 * e
    w_dt = ssd_w_in[0][:, n_main:].reshape(d, 2, groups, SSD_GROUP_HEADS).transpose(0, 2, 1, 3)
    w_dt = jnp.pad(w_dt.reshape(d, groups, 2 * SSD_GROUP_HEADS), ((0, 0), (0, 0), (0, LANES - 2 * SSD_GROUP_HEADS)))
    tiles_pair = dict(tiles_in, tm=IN_TM_PAIR)
    u = _inproj(x, mods, 0, norm_g[0:1], ssd_w_in[0].astype(BF16), n_out=n_main, name="ssd_in", **tiles_pair)
    dt = _inproj(x, mods, 0, norm_g[0:1], w_dt.reshape(d, groups * LANES).astype(BF16), name="ssd_dt_in", out_dtype=F32,
                 **tiles_pair)
    ssd_args = (ssd_conv_w[0], ssd_conv_b[0:1], _group_lanes(ssd_dt_bias[0], groups),
                _group_lanes(ssd_a_log[0], groups), jnp.repeat(ssd_d[0], SSD_HEAD_DIM)[None])
    y_ctx, new_ssd = _ssd_call(u, dt, *ssd_args, None, n_seq=n_ctx, seq_len=l_ctx, seqs_per_step=l_lat // l_ctx,
                               row_block0=0, want_final=True, name="ssd_ctx")
    y_lat, = _ssd_call(u, dt, *ssd_args, state_ssd[:, 0], n_seq=n_lat, seq_len=l_lat, seqs_per_step=n_lat,
                       row_block0=t_ctx // t_lat, want_final=False, name="ssd_lat")
    x = _outproj((y_ctx, y_lat), u, 0, x, mods, 0, w_out_b[0], norm_g=ssd_norm_g[0:1], name="ssd_out", **tiles)

    u = _inproj(x, mods, 1, norm_g[1:2], mlp_w_in[0].astype(BF16), name="mlp_in", **tiles_in)
    b_lanes = jnp.repeat(mlp_b_s[0].T, e // mlp_b_s.shape[1], axis=1)
    y = _gmlp(u, mlp_ln_g[0:1], mlp_ln_b[0:1], mlp_w_s[0], b_lanes)
    x = _outproj(y, u, 2, x, mods, 1, w_out_b[1], name="mlp_out", **tiles)

    u = _inproj(x, mods, 2, norm_g[2:3], s5_w_in[0].astype(BF16), name="s5_in", **tiles_in)
    s5_groups = e // S5_GROUP
    bg = LANES // S5_GROUP
    blocks = s5_groups // bg

    def block_vec(v):
        return v.reshape(2, blocks, 1, S5_BLOCK_ST).transpose(1, 0, 2, 3)

    def block_mat_b(v):
        v = v.reshape(2, blocks, bg, S5_STATE, S5_GROUP).transpose(1, 0, 4, 2, 3)
        return v.reshape(blocks, 2, S5_GROUP, S5_BLOCK_ST)

    def block_mat_c(v):
        v = v.reshape(2, blocks, bg, S5_GROUP, S5_STATE).transpose(1, 0, 3, 2, 4)
        return v.reshape(blocks, 2, S5_GROUP, S5_BLOCK_ST)

    log_step = jnp.repeat(s5_log_step[0][:, :, None], S5_STATE, axis=2)
    prm = (block_vec(s5_lam_re[0]), block_vec(s5_lam_im[0]), block_vec(log_step), block_mat_b(s5_b_re[0]),
           block_mat_b(s5_b_im[0]), block_mat_c(s5_c_re[0]), block_mat_c(s5_c_im[0]))
    h0 = state_s5[:, 0].reshape(n_lat, 2, 2, blocks, S5_BLOCK_ST).transpose(3, 1, 2, 0, 4)
    ys, fin = _s5_core(prm, h0, u, n_ctx=n_ctx, l_ctx=l_ctx, n_lat=n_lat, l_lat=l_lat, e=e)
    new_s5 = fin[:, None]
    y = _s5_glu(ys, u, s5_d[0:1], s5_w_glu[0].astype(BF16), s5_b_glu[0:1])
    x = _outproj(y, u, 1, x, mods, 2, w_out_b[2], name="s5_out", **tiles)

    u = _inproj(x, mods, 3, norm_g[3:4], nat_w_in[0].astype(BF16), name="nat_in", **tiles_in)
    y_ctx, new_k, new_v = _ctx_attention(u, n_seq=n_ctx, seq_len=l_ctx, e=e)
    y_lat = _nat_attention(u, cache_k, cache_v, 0, _nat_bias(nat_rpb[0]), n_seq=n_lat, seq_len=l_lat,
                           row_block0=t_ctx // l_lat, e=e)
    out_ctx, out_lat = _outproj((y_ctx, y_lat), u, 3, x, mods, 3, w_out_b[3], final_g=final_g[None], split_rows=(t_ctx, t_lat),
                                name="nat_out", **tiles)

    return (out_ctx.reshape(n_ctx, l_ctx, d), out_lat.reshape(n_lat, l_lat, d),
            new_ssd[:, None], new_s5, new_k[:, None], new_v[:, None])
```

```python
import collections
import functools

import jax
import jax.numpy as jnp
from jax import lax
from jax.experimental import pallas as pl
from jax.experimental.pallas import tpu as pltpu

F32 = jnp.float32
BF16 = jnp.bfloat16
ACT = jnp.bfloat16
EPS = 1e-6

LANES = 128
CHUNK = 128
SSD_HEAD_DIM = 64
SSD_GROUP_HEADS = 4
SSD_GROUP_CH = SSD_HEAD_DIM * SSD_GROUP_HEADS
SSD_STATE = 128
CONV_W = 5
CONV_HALO = 16
SSD_INTERLEAVE = 4
CONV_WIN = 2 * CHUNK
S5_T = 8
S5_GROUP = 16
S5_STATE = 64
S5_BLOCK_ST = (LANES // S5_GROUP) * S5_STATE
HEAD_DIM = 64
GRID_W = 64
WIN_ROWS = 8
WIN_COLS = 16
NAT_ROWS_PER_STEP = 8
MASKED = -1e30
MOD_ROWS = 8

VMEM_LIMIT = 52 * 1024 * 1024
IN_TM, IN_TN = 2048, 1024
IN_TM_PAIR = 1024
OUT_TM = 512
GMLP_TM = 256
GLU_TM = 512
CTX_ATTN_WIDTH = 512


def _silu(x):
    return x * jax.nn.sigmoid(x)


def _gelu(x):
    return 0.5 * x * (1.0 + jnp.tanh(0.7978845608028654 * (x + 0.044715 * (x * x * x))))


def _softplus(x):
    return jnp.maximum(x, 0.0) + jnp.log1p(jnp.exp(-jnp.abs(x)))


def _dot(a, b):
    return jnp.dot(a, b, preferred_element_type=F32)


def _dot_nt(a, b):
    return lax.dot_general(a, b, (((1,), (1,)), ((), ())), preferred_element_type=F32)


def _cumsum_rows(lower_tri, x):
    hi = x.astype(BF16)
    r1 = x - hi.astype(F32)
    mid = r1.astype(BF16)
    lo = (r1 - mid.astype(F32)).astype(BF16)
    return _dot(lower_tri, hi) + _dot(lower_tri, mid) + _dot(lower_tri, lo)


def _params(*sem):
    return pltpu.CompilerParams(dimension_semantics=sem, vmem_limit_bytes=VMEM_LIMIT)


def _mod_row(i, *, tm, t_ctx, l_lat):
    start = i * tm
    return jnp.where(start < t_ctx, 0, 1 + (start - t_ctx) // l_lat)


def _mod_body(c_ref, w_ref, b_ref, o_ref):
    cond = _silu(c_ref[...]).astype(BF16)
    o_ref[...] = _dot(cond, w_ref[...].astype(BF16)) + b_ref[...]


def _modulation(cond, w_mod, b_mod, *, tn=IN_TN):
    depth, d, n = w_mod.shape
    rows = cond.shape[0]
    return pl.pallas_call(
        _mod_body,
        grid=(depth, n // tn),
        in_specs=[pl.BlockSpec((rows, d), lambda l, j: (0, 0)),
                  pl.BlockSpec((None, d, tn), lambda l, j: (l, 0, j)),
                  pl.BlockSpec((None, 1, tn), lambda l, j: (l, 0, j))],
        out_specs=pl.BlockSpec((None, rows, tn), lambda l, j: (l, 0, j)),
        out_shape=jax.ShapeDtypeStruct((depth, rows, n), F32),
        compiler_params=_params("parallel", "parallel"),
        name="modulation",
    )(cond, w_mod, b_mod.reshape(depth, 1, n))


def _stream_specs(x, tm, width, n_grid):
    ids = (lambda i, *_: i) if n_grid == 1 else (lambda i, j: i)
    if not isinstance(x, tuple):
        return [x], [pl.BlockSpec((tm, width), lambda *g: (ids(*g), 0))], None
    split = x[0].shape[0] // tm
    last = x[1].shape[0] // tm - 1
    return (list(x),
            [pl.BlockSpec((tm, width), lambda *g: (jnp.minimum(ids(*g), split - 1), 0)),
             pl.BlockSpec((tm, width), lambda *g: (jnp.clip(ids(*g) - split, 0, last), 0))],
            split)


def _inproj_body(*refs, split, extra_from):
    n_x = 1 if split is None else 2
    x_refs = refs[:n_x]
    if extra_from is None:
        shift_ref, scale_ref, g_ref, w_ref, o_ref, h_ref = refs[n_x:]
    else:
        shift_ref, scale_ref, g_ref, w_ref, w2_ref, o_ref, o2_ref, h_ref = refs[n_x:]

    def normalise(x_ref):
        x = x_ref[...]
        y = x * lax.rsqrt(jnp.mean(x * x, axis=-1, keepdims=True) + EPS) * g_ref[...]
        h_ref[...] = (y * (1.0 + scale_ref[...]) + shift_ref[...]).astype(BF16)

    first_col = pl.program_id(1) == 0
    if split is None:
        pl.when(first_col)(lambda: normalise(x_refs[0]))
    else:
        in_ctx = pl.program_id(0) < split
        pl.when(first_col & in_ctx)(lambda: normalise(x_refs[0]))
        pl.when(first_col & jnp.logical_not(in_ctx))(lambda: normalise(x_refs[1]))
    if extra_from is None:
        o_ref[...] = _dot(h_ref[...], w_ref[...]).astype(o_ref.dtype)
    else:
        main = pl.program_id(1) < extra_from

        @pl.when(main)
        def _():
            o_ref[...] = _dot(h_ref[...], w_ref[...]).astype(o_ref.dtype)

        @pl.when(jnp.logical_not(main))
        def _():
            o2_ref[...] = _dot(h_ref[...], w2_ref[...]).astype(o2_ref.dtype)


def _inproj(x, mods, layer, g, w, *, tm, tn, t_ctx, l_lat, name, out_dtype=ACT, n_out=None, w_extra=None):
    d = w.shape[0]
    n = w.shape[1] if n_out is None else n_out
    row = functools.partial(_mod_row, tm=tm, t_ctx=t_ctx, l_lat=l_lat)
    x_args, x_specs, split = _stream_specs(x, tm, d, 2)
    t = sum(a.shape[0] for a in x_args)
    nj = n // tn

    def mod_spec(part):
        return pl.BlockSpec((None, None, None, 1, d), lambda i, j: (layer, row(i), part, 0, 0))

    in_specs = x_specs + [mod_spec(0), mod_spec(1), pl.BlockSpec((1, d), lambda i, j: (0, 0))]
    if w_extra is None:
        in_specs.append(pl.BlockSpec((d, tn), lambda i, j: (0, j)))
        weights, steps = [w], nj
        out_specs = pl.BlockSpec((tm, tn), lambda i, j: (i, j))
        out_shape = jax.ShapeDtypeStruct((t, n), out_dtype)
    else:
        nj2 = w_extra.shape[1] // tn
        in_specs += [pl.BlockSpec((d, tn), lambda i, j: (0, jnp.minimum(j, nj - 1))),
                     pl.BlockSpec((d, tn), lambda i, j: (0, jnp.clip(j - nj, 0, nj2 - 1)))]
        weights, steps = [w, w_extra], nj + nj2
        out_specs = [pl.BlockSpec((tm, tn), lambda i, j: (i, jnp.minimum(j, nj - 1))),
                     pl.BlockSpec((tm, tn), lambda i, j: (i, jnp.clip(j - nj, 0, nj2 - 1)))]
        out_shape = [jax.ShapeDtypeStruct((t, n), out_dtype), jax.ShapeDtypeStruct((t, w_extra.shape[1]), F32)]
    return pl.pallas_call(
        functools.partial(_inproj_body, split=split, extra_from=None if w_extra is None else nj),
        grid=(t // tm, steps),
        in_specs=in_specs,
        out_specs=out_specs,
        out_shape=out_shape,
        scratch_shapes=[pltpu.VMEM((tm, d), BF16)],
        compiler_params=_params("parallel", "arbitrary"),
        name=name,
    )(*x_args, mods, mods, g, *weights)


def _outproj_body(*refs, n_y, n_x, split, gated_norm, final_norm):
    refs = list(refs)
    y_refs, z_ref, x_refs = refs[:n_y], refs[n_y], refs[n_y + 1:n_y + 1 + n_x]
    rest = refs[n_y + 1 + n_x:]
    gate_ref, w_ref = rest[:2]
    rest = rest[2:]
    ng_ref = rest.pop(0) if gated_norm else None
    fg_ref = rest.pop(0) if final_norm else None
    o_refs = rest

    def emit(y_ref, x_ref, o_ref):
        t = y_ref[...].astype(F32) * _silu(z_ref[...].astype(F32))
        if gated_norm:
            t = t * lax.rsqrt(jnp.mean(t * t, axis=-1, keepdims=True) + EPS) * ng_ref[...]
        xn = x_ref[...] + gate_ref[...] * _dot(t.astype(BF16), w_ref[...])
        if final_norm:
            xn = xn * lax.rsqrt(jnp.mean(xn * xn, axis=-1, keepdims=True) + EPS) * fg_ref[...]
        o_ref[...] = xn

    if split is None:
        emit(y_refs[0], x_refs[0], o_refs[0])
    else:
        in_ctx = pl.program_id(0) < split
        pl.when(in_ctx)(lambda: emit(y_refs[0], x_refs[0], o_refs[0]))
        pl.when(jnp.logical_not(in_ctx))(lambda: emit(y_refs[-1], x_refs[-1], o_refs[-1]))


def _outproj(y, u, z_block, x, mods, layer, w, *, norm_g=None, final_g=None, split_rows=None, tm, t_ctx, l_lat,
             name):
    t, e, d = u.shape[0], w.shape[1], w.shape[2]
    row = functools.partial(_mod_row, tm=tm, t_ctx=t_ctx, l_lat=l_lat)
    y_args, y_specs, split_y = _stream_specs(y, tm, e, 1)
    x_args, x_specs, split_in = _stream_specs(x, tm, d, 1)
    in_specs = (y_specs + [pl.BlockSpec((tm, e), lambda i: (i, z_block))] + x_specs
                + [pl.BlockSpec((None, None, None, 1, d), lambda i: (layer, row(i), 2, 0, 0)),
                   pl.BlockSpec((None, e, d), lambda i: (layer, 0, 0))])
    args = y_args + [u] + x_args + [mods, w]
    if norm_g is not None:
        in_specs.append(pl.BlockSpec((1, e), lambda i: (0, 0)))
        args.append(norm_g)
    if final_g is not None:
        in_specs.append(pl.BlockSpec((1, d), lambda i: (0, 0)))
        args.append(final_g)
    if split_rows is None:
        split_out = None
        out_specs = pl.BlockSpec((tm, d), lambda i: (i, 0))
        out_shape = jax.ShapeDtypeStruct((t, d), F32)
    else:
        split_out = split_rows[0] // tm
        last = split_rows[1] // tm - 1
        out_specs = [pl.BlockSpec((tm, d), lambda i: (jnp.minimum(i, split_out - 1), 0)),
                     pl.BlockSpec((tm, d), lambda i: (jnp.clip(i - split_out, 0, last), 0))]
        out_shape = [jax.ShapeDtypeStruct((rows, d), F32) for rows in split_rows]
    return pl.pallas_call(
        functools.partial(_outproj_body, n_y=len(y_args), n_x=len(x_args),
                          split=next((s for s in (split_y, split_in, split_out) if s is not None), None),
                          gated_norm=norm_g is not None, final_norm=final_g is not None),
        grid=(t // tm,),
        in_specs=in_specs,
        out_specs=out_specs,
        out_shape=out_shape,
        compiler_params=_params("arbitrary"),
        name=name,
    )(*args)


_SsdScratch = collections.namedtuple("_SsdScratch", "pad xbd xt bm bt cm e et ct dtt y sf sb")


def _ssd_scratch_shapes(seq_len):
    nc = seq_len // CHUNK
    width = SSD_GROUP_CH + 2 * SSD_STATE
    return [pltpu.VMEM((seq_len + CONV_WIN - CHUNK, width), ACT),
            pltpu.VMEM((nc, SSD_GROUP_HEADS * CHUNK, SSD_GROUP_CH), BF16),
            pltpu.VMEM((nc, SSD_GROUP_CH, CHUNK), F32),
            pltpu.VMEM((seq_len, SSD_STATE), BF16),
            pltpu.VMEM((nc, SSD_STATE, CHUNK), BF16),
            pltpu.VMEM((seq_len, SSD_STATE), BF16),
            pltpu.VMEM((seq_len, LANES), F32),
            pltpu.VMEM((nc, 2 * SSD_GROUP_HEADS, CHUNK), F32),
            pltpu.VMEM((nc, 2 * SSD_GROUP_HEADS, CHUNK), F32),
            pltpu.VMEM((nc, 2 * SSD_GROUP_HEADS, CHUNK), F32),
            pltpu.VMEM((seq_len, SSD_GROUP_CH), F32),
            pltpu.VMEM((SSD_GROUP_CH, SSD_STATE), F32),
            pltpu.VMEM((SSD_GROUP_CH, SSD_STATE), F32)]


def _ssd_body(*refs, seq_len, seqs_per_step, has_h0, want_final):
    refs = list(refs)
    x_ref, b_ref, c_ref, dt_ref, wx_ref, wb_ref, wc_ref, bx_ref, bb_ref, bc_ref, dtb_ref, alog_ref, dsk_ref = refs[:13]
    rest = refs[13:]
    h0_ref = rest.pop(0) if has_h0 else None
    y_ref = rest.pop(0)
    hf_ref = rest.pop(0) if want_final else None
    shift_s = rest.pop(0)
    il = min(SSD_INTERLEAVE, seqs_per_step)
    per_seq = len(rest) // il
    sc = [_SsdScratch(*rest[q * per_seq:(q + 1) * per_seq]) for q in range(il)]
    lanes_q = range(il)
    nc = seq_len // CHUNK
    width = SSD_GROUP_CH + 2 * SSD_STATE
    half = width // 2
    nh = SSD_GROUP_HEADS
    mid = CONV_W // 2

    tail = CONV_WIN - CHUNK - CONV_HALO
    for q in lanes_q:
        sc[q].pad[0:CONV_HALO, :] = jnp.zeros((CONV_HALO, width), ACT)
        sc[q].pad[CONV_HALO + seq_len:CONV_HALO + seq_len + tail, :] = jnp.zeros((tail, width), ACT)
    win_row = lax.broadcasted_iota(jnp.int32, (CHUNK, CONV_WIN), 1)
    tok_row = lax.broadcasted_iota(jnp.int32, (CHUNK, CONV_WIN), 0)
    taps = [j for j in range(CONV_W) if j != mid]
    for n, j in enumerate(taps):
        shift_s[n] = (win_row == tok_row + (CONV_HALO + j - mid)).astype(BF16)

    conv_w = jnp.concatenate([wx_ref[...], wb_ref[...], wc_ref[...]], axis=1)
    conv_b = jnp.concatenate([bx_ref[...], bb_ref[...], bc_ref[...]], axis=1)
    a_row = -jnp.exp(alog_ref[...])
    row_i = lax.broadcasted_iota(jnp.int32, (CHUNK, CHUNK), 0)
    col_i = lax.broadcasted_iota(jnp.int32, (CHUNK, CHUNK), 1)
    lower = row_i >= col_i
    upper = row_i <= col_i
    lower_b = lower.astype(BF16)
    fwd_lane = lax.broadcasted_iota(jnp.int32, (1, LANES), 1) < nh
    head_of_lane = lax.broadcasted_iota(jnp.int32, (CHUNK, SSD_GROUP_CH), 1) // SSD_HEAD_DIM

    def chunk_rows(ci):
        return pl.ds(pl.multiple_of(ci * CHUNK, CHUNK), CHUNK)

    def copy_chunk(offs, ci, carry):
        base = pl.multiple_of(ci * CHUNK, CHUNK)
        dst = pl.ds(base + CONV_HALO, CHUNK)
        for q in lanes_q:
            src = pl.ds(offs[q] + base, CHUNK)
            sc[q].pad[dst, 0:SSD_GROUP_CH] = x_ref[src, :]
            sc[q].pad[dst, SSD_GROUP_CH:SSD_GROUP_CH + SSD_STATE] = b_ref[src, :]
            sc[q].pad[dst, SSD_GROUP_CH + SSD_STATE:width] = c_ref[src, :]
        return carry

    def prep_chunk(offs, ci, carry):
        r = chunk_rows(ci)
        units = [(q, h) for h in range(2) for q in lanes_q]

        def window(unit):
            q, h = unit
            return sc[q].pad[pl.ds(pl.multiple_of(ci * CHUNK, CHUNK), CONV_WIN), h * half:(h + 1) * half]

        def shift_dots(unit):
            win = window(unit)
            return [_dot(shift_s[t], win) for t in range(len(taps))]

        def finish(unit, shifted):
            q, h = unit
            cols = slice(h * half, (h + 1) * half)
            acc = conv_b[:, cols] + window(unit)[CONV_HALO:CONV_HALO + CHUNK].astype(F32) * conv_w[mid:mid + 1, cols]
            for t, j in enumerate(taps):
                acc = acc + shifted[t] * conv_w[j:j + 1, cols]
            v = _silu(acc)
            if h == 0:
                sc[q].y[r, :] = v * dsk_ref[...]
                vb = v.astype(BF16)
                for hd in range(nh):
                    sc[q].xbd[ci, hd * CHUNK:(hd + 1) * CHUNK, :] = jnp.where(head_of_lane == hd, vb,
                                                                              jnp.zeros_like(vb))
                sc[q].xt[ci] = v.T
            else:
                bm = v[:, 0:SSD_STATE]
                sc[q].bm[r, :] = bm.astype(BF16)
                sc[q].bt[ci] = bm.T.astype(BF16)
                sc[q].cm[r, :] = v[:, SSD_STATE:half].astype(BF16)

        dts = [_softplus(dt_ref[pl.ds(offs[q] + pl.multiple_of(ci * CHUNK, CHUNK), CHUNK), :] + dtb_ref[...])
               for q in lanes_q]
        das = [dt * a_row for dt in dts]
        pending = shift_dots(units[0])
        cums = []
        for i, unit in enumerate(units):
            following = shift_dots(units[i + 1]) if i + 1 < len(units) else None
            if unit[1] == 0:
                cums.append(_cumsum_rows(lower_b, das[unit[0]]))
            finish(unit, pending)
            pending = following
        for q in lanes_q:
            e = jnp.where(fwd_lane, cums[q], cums[q] - das[q])
            sc[q].e[r, :] = e
            sc[q].et[ci] = e.T[0:2 * nh]
            sc[q].ct[ci] = cums[q].T[0:2 * nh]
            sc[q].dtt[ci] = dts[q].T[0:2 * nh]
        return carry

    def scan_chunk(i, carry):
        chunks = (i, nc - 1 - i)
        rows = [chunk_rows(ci) for ci in chunks]
        streams = [(q, d) for q in lanes_q for d in range(2)]
        st = {(q, d): (sc[q].sf if d == 0 else sc[q].sb) for q, d in streams}
        s_prev = {k: st[k][...] for k in streams}
        g, cs = {}, {}
        for q, d in streams:
            cm = sc[q].cm[rows[d], :]
            g[q, d] = _dot(cm, sc[q].bt[chunks[d]])
            cs[q, d] = _dot_nt(cm, s_prev[q, d].astype(BF16))
        w_rows, keeps, ys = {}, {}, {}
        for q, d in streams:
            e = sc[q].e[rows[d], :]
            e_t, c_t, dt_t = sc[q].et[chunks[d]], sc[q].ct[chunks[d]], sc[q].dtt[chunks[d]]
            w_d, keep_d, mixes, offs = [], [], [], []
            for h in range(nh):
                col = h + nh * d
                ec = jnp.broadcast_to(e[:, col:col + 1], (CHUNK, CHUNK))
                er = e_t[col:col + 1, :]
                dt_h = dt_t[col:col + 1, :]
                log_dt = jnp.log(dt_h)
                tot = c_t[col:col + 1, CHUNK - 1:CHUNK]
                if d == 0:
                    decay_dt = jnp.exp(jnp.where(lower, ec - (er - log_dt), -jnp.inf))
                    offs.append(jnp.exp(ec))
                    w_d.append(jnp.broadcast_to(dt_h * jnp.exp(tot - er), (SSD_HEAD_DIM, CHUNK)))
                else:
                    decay_dt = jnp.exp(jnp.where(upper, (er + log_dt) - ec, -jnp.inf))
                    offs.append(jnp.exp(tot - ec))
                    w_d.append(jnp.broadcast_to(dt_h * jnp.exp(er), (SSD_HEAD_DIM, CHUNK)))
                keep_d.append(jnp.broadcast_to(jnp.exp(tot), (SSD_HEAD_DIM, SSD_STATE)))
                mixes.append((g[q, d] * decay_dt).astype(BF16))
            y_diag = _dot(jnp.concatenate(mixes, axis=1), sc[q].xbd[chunks[d]])
            off = jnp.concatenate([offs[nh - 1]] * (SSD_GROUP_CH // CHUNK), axis=1)
            for h in range(nh - 2, -1, -1):
                off = jnp.where(head_of_lane == h, jnp.concatenate([offs[h]] * (SSD_GROUP_CH // CHUNK), axis=1), off)
            ys[q, d] = y_diag + cs[q, d] * off
            w_rows[q, d] = jnp.concatenate(w_d, axis=0)
            keeps[q, d] = jnp.concatenate(keep_d, axis=0)
        for q, d in streams:
            w_t = (sc[q].xt[chunks[d]] * w_rows[q, d]).astype(BF16)
            st[q, d][...] = keeps[q, d] * s_prev[q, d] + _dot(w_t, sc[q].bm[rows[d], :])
            sc[q].y[rows[d], :] += ys[q, d]
        return carry

    def emit_chunk(offs, ci, carry):
        base = pl.multiple_of(ci * CHUNK, CHUNK)
        for q in lanes_q:
            y_ref[pl.ds(offs[q] + base, CHUNK), :] = sc[q].y[pl.ds(base, CHUNK), :].astype(y_ref.dtype)
        return carry

    def one_group(s, carry):
        seqs = [s * il + q for q in lanes_q]
        offs = [pl.multiple_of(sq * seq_len, CHUNK) for sq in seqs]
        lax.fori_loop(0, nc, functools.partial(copy_chunk, offs), 0)
        lax.fori_loop(0, nc, functools.partial(prep_chunk, offs), 0)
        for q in lanes_q:
            if has_h0:
                sc[q].sf[...] = h0_ref[seqs[q], 0].reshape(SSD_GROUP_CH, SSD_STATE)
                sc[q].sb[...] = h0_ref[seqs[q], 1].reshape(SSD_GROUP_CH, SSD_STATE)
            else:
                sc[q].sf[...] = jnp.zeros((SSD_GROUP_CH, SSD_STATE), F32)
                sc[q].sb[...] = jnp.zeros((SSD_GROUP_CH, SSD_STATE), F32)
        lax.fori_loop(0, nc, scan_chunk, 0)
        lax.fori_loop(0, nc, functools.partial(emit_chunk, offs), 0)
        if want_final:
            for q in lanes_q:
                hf_ref[seqs[q], 0] = sc[q].sf[...].reshape(SSD_GROUP_HEADS, SSD_HEAD_DIM, SSD_STATE)
                hf_ref[seqs[q], 1] = sc[q].sb[...].reshape(SSD_GROUP_HEADS, SSD_HEAD_DIM, SSD_STATE)
        return carry

    lax.fori_loop(0, seqs_per_step // il, one_group, 0)


def _ssd_call(u, dt, conv_w, conv_b, dtb, alog, dsk, h0, *, n_seq, seq_len, seqs_per_step, row_block0,
              want_final, name):
    t, n_u = u.shape
    e = dsk.shape[1]
    groups = e // SSD_GROUP_CH
    xb0 = e // SSD_GROUP_CH
    bb0 = 2 * e // SSD_STATE
    cb0 = bb0 + groups
    has_h0 = h0 is not None
    seq = lambda b: row_block0 + b
    rows = seqs_per_step * seq_len
    in_specs = [pl.BlockSpec((rows, SSD_GROUP_CH), lambda b, g: (seq(b), xb0 + g)),
                pl.BlockSpec((rows, SSD_STATE), lambda b, g: (seq(b), bb0 + g)),
                pl.BlockSpec((rows, SSD_STATE), lambda b, g: (seq(b), cb0 + g)),
                pl.BlockSpec((rows, LANES), lambda b, g: (seq(b), g)),
                pl.BlockSpec((CONV_W, SSD_GROUP_CH), lambda b, g: (0, g)),
                pl.BlockSpec((CONV_W, SSD_STATE), lambda b, g: (0, bb0 - xb0 * 2 + g)),
                pl.BlockSpec((CONV_W, SSD_STATE), lambda b, g: (0, cb0 - xb0 * 2 + g)),
                pl.BlockSpec((1, SSD_GROUP_CH), lambda b, g: (0, g)),
                pl.BlockSpec((1, SSD_STATE), lambda b, g: (0, bb0 - xb0 * 2 + g)),
                pl.BlockSpec((1, SSD_STATE), lambda b, g: (0, cb0 - xb0 * 2 + g)),
                pl.BlockSpec((1, LANES), lambda b, g: (0, g)),
                pl.BlockSpec((1, LANES), lambda b, g: (0, g)),
                pl.BlockSpec((1, SSD_GROUP_CH), lambda b, g: (0, g))]
    args = [u, u, u, dt, conv_w, conv_w, conv_w, conv_b, conv_b, conv_b, dtb, alog, dsk]
    state_block = (seqs_per_step, 2, SSD_GROUP_HEADS, SSD_HEAD_DIM, SSD_STATE)
    if has_h0:
        in_specs.append(pl.BlockSpec(state_block, lambda b, g: (b, 0, g, 0, 0)))
        args.append(h0)
    out_specs = [pl.BlockSpec((rows, SSD_GROUP_CH), lambda b, g: (b, g))]
    out_shape = [jax.ShapeDtypeStruct((n_seq * seq_len, e), ACT)]
    if want_final:
        out_specs.append(pl.BlockSpec(state_block, lambda b, g: (b, 0, g, 0, 0)))
        out_shape.append(jax.ShapeDtypeStruct((n_seq, 2, e // SSD_HEAD_DIM, SSD_HEAD_DIM, SSD_STATE), F32))
    return pl.pallas_call(
        functools.partial(_ssd_body, seq_len=seq_len, seqs_per_step=seqs_per_step, has_h0=has_h0,
                          want_final=want_final),
        grid=(n_seq // seqs_per_step, groups),
        in_specs=in_specs,
        out_specs=out_specs,
        out_shape=out_shape,
        scratch_shapes=([pltpu.VMEM((CONV_W - 1, CHUNK, CONV_WIN), BF16)]
                        + _ssd_scratch_shapes(seq_len) * min(SSD_INTERLEAVE, seqs_per_step)),
        compiler_params=_params("parallel", "parallel"),
        name=name,
    )(*args)


def _group_lanes(v, groups):
    per_group = v.reshape(2, groups, SSD_GROUP_HEADS).transpose(1, 0, 2).reshape(groups, 2 * SSD_GROUP_HEADS)
    return jnp.pad(per_group, ((0, 0), (0, LANES - 2 * SSD_GROUP_HEADS))).reshape(1, groups * LANES)


def _gmlp_body(u_ref, v_ref, lng_ref, lnb_ref, ws_ref, bs_ref, o_ref, *, tm):
    groups = ws_ref.shape[0]
    gch = u_ref.shape[1] // groups
    for c in range(tm // CHUNK):
        rows = slice(c * CHUNK, (c + 1) * CHUNK)
        v = _gelu(v_ref[rows, :].astype(F32))
        vc = v - jnp.mean(v, axis=-1, keepdims=True)
        vn = vc * lax.rsqrt(jnp.mean(vc * vc, axis=-1, keepdims=True) + EPS) * lng_ref[...] + lnb_ref[...]
        vb = vn.astype(BF16)
        for g in range(groups):
            cols = slice(g * gch, (g + 1) * gch)
            s = _dot(ws_ref[g].astype(BF16), vb[:, cols]) + bs_ref[:, cols]
            o_ref[rows, cols] = (_gelu(u_ref[rows, cols].astype(F32)) * s).astype(o_ref.dtype)


def _gmlp(u, ln_g, ln_b, w_s, b_lanes, *, tm=GMLP_TM):
    t = u.shape[0]
    e = ln_g.shape[1]
    return pl.pallas_call(
        functools.partial(_gmlp_body, tm=tm),
        grid=(t // tm,),
        in_specs=[pl.BlockSpec((tm, e), lambda i: (i, 0)),
                  pl.BlockSpec((tm, e), lambda i: (i, 1)),
                  pl.BlockSpec((1, e), lambda i: (0, 0)),
                  pl.BlockSpec((1, e), lambda i: (0, 0)),
                  pl.BlockSpec(w_s.shape, lambda i: (0, 0, 0)),
                  pl.BlockSpec((CHUNK, e), lambda i: (0, 0))],
        out_specs=pl.BlockSpec((tm, e), lambda i: (i, 0)),
        out_shape=jax.ShapeDtypeStruct((t, e), ACT),
        compiler_params=_params("parallel"),
        name="gmlp",
    )(u, u, ln_g, ln_b, w_s, b_lanes)


def _s5_body(lr_ref, li_ref, ls_ref, brt_ref, bit_ref, cre_ref, cim_ref, h0_ref, u_ref, y_ref, fin_ref,
             win_s, t_s, ef_s, eb_s, z_s, a_s, zin_s, d_s, spf_s, spb_s, yo_s, uf_s,
             *, seqs, l_ctx, n_lat, l_lat, ctx_parts):
    bst = S5_BLOCK_ST
    ng = LANES // S5_GROUP
    kw = S5_T * LANES
    part = pl.program_id(1)

    @pl.when(part == 0)
    def _build():
        own = (lax.broadcasted_iota(jnp.int32, (ng, S5_GROUP, bst), 2) // S5_STATE
               == lax.broadcasted_iota(jnp.int32, (ng, S5_GROUP, bst), 0))

        def spread(v):
            return jnp.where(own, v[None], 0.0).reshape(LANES, bst).astype(BF16)

        tau = lax.broadcasted_iota(jnp.int32, (S5_T + 8, 1), 0).astype(F32)
        e_refs = (ef_s, eb_s)
        for d in range(2):
            lr, li = lr_ref[d], li_ref[d]
            step = jnp.exp(ls_ref[d])
            mag = jnp.exp(tau * (lr * step))
            p_re = mag * jnp.cos(tau * (li * step))
            p_im = mag * jnp.sin(tau * (li * step))
            ab_re, ab_im = p_re[1:2], p_im[1:2]
            den = lr * lr + li * li
            nr = ab_re - 1.0
            cr = (nr * lr + ab_im * li) / den
            ci = (ab_im * lr - nr * li) / den
            brt, bit = brt_ref[d], bit_ref[d]
            bb_re = cr * brt - ci * bit
            bb_im = cr * bit + ci * brt
            cre, cim = cre_ref[d], cim_ref[d]
            for k in range(S5_T):
                rows = slice(k * LANES, (k + 1) * LANES)
                tq = S5_T - 1 - k if d == 0 else k
                te = k + 1 if d == 0 else S5_T - k
                pr, pi = p_re[tq:tq + 1], p_im[tq:tq + 1]
                win_s[rows, 2 * d * bst:(2 * d + 1) * bst] = spread(pr * bb_re - pi * bb_im)
                win_s[rows, (2 * d + 1) * bst:(2 * d + 2) * bst] = spread(pr * bb_im + pi * bb_re)
                pr, pi = p_re[te:te + 1], p_im[te:te + 1]
                e_refs[d][rows, 0:bst] = spread(cre * pr - cim * pi)
                e_refs[d][rows, bst:2 * bst] = spread(-(cre * pi + cim * pr))
            c_own = jnp.concatenate([spread(cre), spread(-cim)], axis=1)
            z = _dot_nt(win_s[:, 2 * d * bst:(2 * d + 2) * bst], c_own)
            if d == 0:
                z_s[0:kw, :] = z
            else:
                z_s[kw - LANES:kw, :] += z[0:LANES]
                z_s[kw:2 * kw - LANES, :] = z[LANES:kw]
            a_s[2 * d:2 * d + 1, :] = p_re[S5_T:S5_T + 1]
            a_s[2 * d + 1:2 * d + 2, :] = p_im[S5_T:S5_T + 1]
        for k in range(S5_T):
            r0 = (S5_T - 1 - k) * LANES
            t_s[:, k * LANES:(k + 1) * LANES] = z_s[r0:r0 + kw, :].astype(BF16)

    def outputs():
        yo_s[...] = (_dot(zin_s[...], t_s[...]) + _dot_nt(spf_s[...].astype(BF16), ef_s[...])
                     + _dot_nt(spb_s[...].astype(BF16), eb_s[...]))

    @pl.when(part < ctx_parts)
    def _context():
        nj = l_ctx // S5_T
        uf_s[...] = u_ref[...].astype(F32)
        for j in range(nj):
            for k in range(S5_T):
                zin_s[j * seqs:(j + 1) * seqs, k * LANES:(k + 1) * LANES] = (
                    uf_s[pl.ds(j * S5_T + k, seqs, stride=l_ctx), :].astype(BF16))
        d_s[...] = _dot(zin_s[...], win_s[...])
        yo_s[...] = _dot(zin_s[...], t_s[...])
        coef = [jnp.broadcast_to(a_s[i:i + 1, :], (seqs, bst)) for i in range(4)]
        fr = fi = br = bi = jnp.zeros((seqs, bst), F32)
        for j in range(nj):
            rf = slice(j * seqs, (j + 1) * seqs)
            rb = slice((nj - 1 - j) * seqs, (nj - j) * seqs)
            spf_s[rf, 0:bst] = fr
            spf_s[rf, bst:2 * bst] = fi
            spb_s[rb, 0:bst] = br
            spb_s[rb, bst:2 * bst] = bi
            fr, fi, br, bi = (coef[0] * fr - coef[1] * fi + d_s[rf, 0:bst],
                              coef[0] * fi + coef[1] * fr + d_s[rf, bst:2 * bst],
                              coef[2] * br - coef[3] * bi + d_s[rb, 2 * bst:3 * bst],
                              coef[2] * bi + coef[3] * br + d_s[rb, 3 * bst:4 * bst])
        for d, ri, val in ((0, 0, fr), (0, 1, fi), (1, 0, br), (1, 1, bi)):
            for gg in range(ng):
                fin_ref[:, d, ri, gg, :] = val[:, gg * S5_STATE:(gg + 1) * S5_STATE]
        yo_s[...] += (_dot_nt(spf_s[...].astype(BF16), ef_s[...]) + _dot_nt(spb_s[...].astype(BF16), eb_s[...]))
        for j in range(nj):
            for k in range(S5_T):
                y_ref[pl.ds(j * S5_T + k, seqs, stride=l_ctx), :] = (
                    yo_s[j * seqs:(j + 1) * seqs, k * LANES:(k + 1) * LANES])

    @pl.when(part == ctx_parts)
    def _latent():
        nj = l_lat // S5_T
        uf_s[...] = u_ref[...].astype(F32)
        for b in range(n_lat):
            for k in range(S5_T):
                zin_s[b * nj:(b + 1) * nj, k * LANES:(k + 1) * LANES] = (
                    uf_s[pl.ds(b * l_lat + k, nj, stride=S5_T), :].astype(BF16))
        d_s[...] = _dot(zin_s[...], win_s[...])
        coef_re = jnp.concatenate([jnp.broadcast_to(a_s[0:1, :], (n_lat, bst)),
                                   jnp.broadcast_to(a_s[2:3, :], (n_lat, bst))], axis=0)
        coef_im = jnp.concatenate([jnp.broadcast_to(a_s[1:2, :], (n_lat, bst)),
                                   jnp.broadcast_to(a_s[3:4, :], (n_lat, bst))], axis=0)
        s_re0 = jnp.concatenate([h0_ref[0, 0], h0_ref[1, 0]], axis=0)
        s_im0 = jnp.concatenate([h0_ref[0, 1], h0_ref[1, 1]], axis=0)

        def step(j, carry):
            s_re, s_im = carry
            d_re, d_im = [], []
            for b in range(n_lat):
                row = pl.ds(b * nj + j, 1)
                spf_s[row, 0:bst] = s_re[b:b + 1]
                spf_s[row, bst:2 * bst] = s_im[b:b + 1]
                d_re.append(d_s[row, 0:bst])
                d_im.append(d_s[row, bst:2 * bst])
            for b in range(n_lat):
                row = pl.ds(b * nj + nj - 1 - j, 1)
                spb_s[row, 0:bst] = s_re[n_lat + b:n_lat + b + 1]
                spb_s[row, bst:2 * bst] = s_im[n_lat + b:n_lat + b + 1]
                d_re.append(d_s[row, 2 * bst:3 * bst])
                d_im.append(d_s[row, 3 * bst:4 * bst])
            d_re = jnp.concatenate(d_re, axis=0)
            d_im = jnp.concatenate(d_im, axis=0)
            return coef_re * s_re - coef_im * s_im + d_re, coef_re * s_im + coef_im * s_re + d_im

        lax.fori_loop(0, nj, step, (s_re0, s_im0))
        outputs()
        for b in range(n_lat):
            for k in range(S5_T):
                y_ref[pl.ds(b * l_lat + k, nj, stride=S5_T), :] = yo_s[b * nj:(b + 1) * nj, k * LANES:(k + 1) * LANES]


def _s5_core(prm, h0, u, *, n_ctx, l_ctx, n_lat, l_lat, e):
    blocks = prm[0].shape[0]
    part_tokens = n_lat * l_lat
    t = u.shape[0]
    n_parts = t // part_tokens
    seqs = part_tokens // l_ctx
    r = part_tokens // S5_T
    vec = pl.BlockSpec((None, 2, 1, S5_BLOCK_ST), lambda g, p: (g, 0, 0, 0))
    mat = pl.BlockSpec((None, 2, S5_GROUP, S5_BLOCK_ST), lambda g, p: (g, 0, 0, 0))
    k_in = S5_T * LANES
    return pl.pallas_call(
        functools.partial(_s5_body, seqs=seqs, l_ctx=l_ctx, n_lat=n_lat, l_lat=l_lat, ctx_parts=n_parts - 1),
        grid=(blocks, n_parts),
        in_specs=[vec, vec, vec, mat, mat, mat, mat,
                  pl.BlockSpec((None, 2, 2, n_lat, S5_BLOCK_ST), lambda g, p: (g, 0, 0, 0, 0)),
                  pl.BlockSpec((part_tokens, LANES), lambda g, p: (p, g))],
        out_specs=[pl.BlockSpec((part_tokens, LANES), lambda g, p: (p, g)),
                   pl.BlockSpec((seqs, 2, 2, LANES // S5_GROUP, S5_STATE),
                                lambda g, p: (jnp.minimum(p, n_parts - 2), 0, 0, g, 0))],
        out_shape=[jax.ShapeDtypeStruct((t, e), F32),
                   jax.ShapeDtypeStruct((n_ctx, 2, 2, e // S5_GROUP, S5_STATE), F32)],
        scratch_shapes=[pltpu.VMEM((k_in, 4 * S5_BLOCK_ST), BF16),
                        pltpu.VMEM((k_in, k_in), BF16),
                        pltpu.VMEM((k_in, 2 * S5_BLOCK_ST), BF16),
                        pltpu.VMEM((k_in, 2 * S5_BLOCK_ST), BF16),
                        pltpu.VMEM(((2 * S5_T - 1) * LANES, LANES), F32),
                        pltpu.VMEM((8, S5_BLOCK_ST), F32),
                        pltpu.VMEM((r, k_in), BF16),
                        pltpu.VMEM((r, 4 * S5_BLOCK_ST), F32),
                        pltpu.VMEM((r, 2 * S5_BLOCK_ST), F32),
                        pltpu.VMEM((r, 2 * S5_BLOCK_ST), F32),
                        pltpu.VMEM((r, k_in), F32),
                        pltpu.VMEM((part_tokens, LANES), F32)],
        compiler_params=_params("parallel", "arbitrary"),
        name="s5_core",
    )(*prm, h0, u)


def _s5_glu_body(ys_ref, u_ref, dsk_ref, w_ref, b_ref, o_ref):
    y = _gelu(ys_ref[...] + dsk_ref[...] * u_ref[...].astype(F32))
    o_ref[...] = (y * jax.nn.sigmoid(_dot(y.astype(BF16), w_ref[...]) + b_ref[...])).astype(o_ref.dtype)


def _s5_glu(ys, u, dsk, w, b, *, tm=GLU_TM):
    t, e = ys.shape
    return pl.pallas_call(
        _s5_glu_body,
        grid=(t // tm,),
        in_specs=[pl.BlockSpec((tm, e), lambda i: (i, 0)),
                  pl.BlockSpec((tm, e), lambda i: (i, 0)),
                  pl.BlockSpec((1, e), lambda i: (0, 0)),
                  pl.BlockSpec((e, e), lambda i: (0, 0)),
                  pl.BlockSpec((1, e), lambda i: (0, 0))],
        out_specs=pl.BlockSpec((tm, e), lambda i: (i, 0)),
        out_shape=jax.ShapeDtypeStruct((t, e), ACT),
        compiler_params=_params("parallel"),
        name="s5_glu",
    )(ys, u, dsk, w, b)


def _ctx_attn_body(q_ref, k_ref, v_ref, o_ref, ko_ref, vo_ref):
    scale = HEAD_DIM ** -0.5
    seq_len = q_ref.shape[0]
    first = lax.broadcasted_iota(jnp.int32, (seq_len, LANES), 1) < HEAD_DIM
    masks = (first, jnp.logical_not(first))
    n_pairs = q_ref.shape[1] // LANES
    logits, values = [], []
    for pair in range(n_pairs):
        cols = slice(pair * LANES, (pair + 1) * LANES)
        q2, k2, v2 = q_ref[:, cols].astype(F32) * scale, k_ref[:, cols], v_ref[:, cols]
        kb, vb = k2.astype(BF16), v2.astype(BF16)
        for h in range(LANES // HEAD_DIM):
            ko_ref[2 * pair + h] = k2[:, h * HEAD_DIM:(h + 1) * HEAD_DIM].astype(F32)
            vo_ref[2 * pair + h] = v2[:, h * HEAD_DIM:(h + 1) * HEAD_DIM].astype(F32)
            logits.append(_dot_nt(jnp.where(masks[h], q2, 0.0).astype(BF16), kb))
            values.append(jnp.where(masks[h], vb, jnp.ones_like(vb)))
    probs = [jnp.exp(s - jnp.max(s, axis=-1, keepdims=True)).astype(BF16) for s in logits]
    for pair in range(n_pairs):
        o0, o1 = _dot(probs[2 * pair], values[2 * pair]), _dot(probs[2 * pair + 1], values[2 * pair + 1])
        num = jnp.where(first, o0, o1)
        den = jnp.where(first, pltpu.roll(o0, HEAD_DIM, 1), pltpu.roll(o1, HEAD_DIM, 1))
        o_ref[:, pair * LANES:(pair + 1) * LANES] = (num / den).astype(o_ref.dtype)


def _ctx_attention(u, *, n_seq, seq_len, e, width=CTX_ATTN_WIDTH):
    t = n_seq * seq_len
    hp = e // width
    heads = e // HEAD_DIM
    kv_block = (None, width // HEAD_DIM, seq_len, HEAD_DIM)
    in_specs = [pl.BlockSpec((seq_len, width), lambda b, h: (b, h)),
                pl.BlockSpec((seq_len, width), lambda b, h: (b, hp + h)),
                pl.BlockSpec((seq_len, width), lambda b, h: (b, 2 * hp + h))]
    return pl.pallas_call(
        _ctx_attn_body,
        grid=(n_seq, hp),
        in_specs=in_specs,
        out_specs=[pl.BlockSpec((seq_len, width), lambda b, h: (b, h)),
                   pl.BlockSpec(kv_block, lambda b, h: (b, h, 0, 0)),
                   pl.BlockSpec(kv_block, lambda b, h: (b, h, 0, 0))],
        out_shape=[jax.ShapeDtypeStruct((t, e), ACT),
                   jax.ShapeDtypeStruct((n_seq, heads, seq_len, HEAD_DIM), F32),
                   jax.ShapeDtypeStruct((n_seq, heads, seq_len, HEAD_DIM), F32)],
        compiler_params=_params("parallel", "parallel"),
        name="ctx_attention",
    )(u, u, u)


def _nat_bias_body(rp_ref, o_ref):
    n_rel_rows = rp_ref.shape[1]
    q = lax.broadcasted_iota(jnp.int32, (GRID_W, LANES), 0)
    lane = lax.broadcasted_iota(jnp.int32, (GRID_W, LANES), 1)
    kc = lane % GRID_W
    c_start = jnp.clip(q - WIN_COLS // 2, 0, GRID_W - WIN_COLS)
    ok = (kc >= c_start) & (kc < c_start + WIN_COLS)
    left = lane < GRID_W
    for h in range(o_ref.shape[0]):
        halves = []
        for i in range(n_rel_rows):
            row = jnp.broadcast_to(rp_ref[h, i:i + 1, :], (GRID_W, LANES))
            halves.append((pltpu.roll(row, 0, 1, stride=1, stride_axis=0),
                           pltpu.roll(row, GRID_W, 1, stride=1, stride_axis=0)))
        for i in range(n_rel_rows - 1):
            o_ref[h, i] = jnp.where(ok, jnp.where(left, halves[i][0], halves[i + 1][1]), MASKED)


def _nat_bias(rpb, *, heads_per_step=4):
    heads, n_rel_rows, n_rel_cols = rpb.shape
    rows = jnp.roll(jnp.pad(rpb, ((0, 0), (0, 0), (0, LANES - n_rel_cols))), -(WIN_COLS - 1), axis=-1)
    return pl.pallas_call(
        _nat_bias_body,
        grid=(heads // heads_per_step,),
        in_specs=[pl.BlockSpec((heads_per_step, n_rel_rows, LANES), lambda h: (h, 0, 0))],
        out_specs=pl.BlockSpec((heads_per_step, n_rel_rows - 1, GRID_W, 2 * GRID_W), lambda h: (h, 0, 0, 0)),
        out_shape=jax.ShapeDtypeStruct((heads, n_rel_rows - 1, GRID_W, 2 * GRID_W), F32),
        compiler_params=_params("parallel"),
        name="nat_bias",
    )(rows)


def _nat_body(q_ref, k_ref, v_ref, ck_ref, cv_ref, bias_ref, o_ref, ckb_s, cvb_s, *, rows):
    scale = HEAD_DIM ** -0.5
    wr = min(WIN_ROWS, rows)
    nw = wr * GRID_W
    heads = tuple(range(LANES // HEAD_DIM))
    first = lax.broadcasted_iota(jnp.int32, (GRID_W, LANES), 1) < HEAD_DIM
    masks = (first, jnp.logical_not(first))
    ckb_s[...] = jnp.concatenate([ck_ref[0], ck_ref[1]], axis=1).astype(BF16)
    cvb_s[...] = jnp.concatenate([cv_ref[0], cv_ref[1]], axis=1).astype(BF16)

    def row_group(g, carry):
        rws = [g * NAT_ROWS_PER_STEP + i for i in range(NAT_ROWS_PER_STEP)]
        starts = [jnp.clip(r - wr // 2, 0, rows - wr) for r in rws]
        q_rows = [pl.ds(pl.multiple_of(r * GRID_W, GRID_W), GRID_W) for r in rws]
        k_rows = [pl.ds(pl.multiple_of(rs * GRID_W, GRID_W), nw) for rs in starts]
        logits = []
        for r, rs, qr, kr in zip(rws, starts, q_rows, k_rows):
            q2 = q_ref[qr, :].astype(F32) * scale
            kb = k_ref[kr, :].astype(BF16)
            for h in heads:
                q = jnp.where(masks[h], q2, 0.0).astype(BF16)
                logits.append((_dot_nt(q, kb), _dot_nt(q, ckb_s[...])))
        probs = []
        for idx, (s_win, s_ctx) in enumerate(logits):
            r, rs, h = rws[idx // 2], starts[idx // 2], heads[idx % 2]
            i0 = (WIN_ROWS - 1) - (r - rs)
            s_win = s_win + jnp.concatenate([bias_ref[h, i0 + 2 * jj] for jj in range(wr // 2)], axis=1)
            m = jnp.maximum(jnp.max(s_win, axis=-1, keepdims=True), jnp.max(s_ctx, axis=-1, keepdims=True))
            p_win = jnp.exp(s_win - m)
            p_ctx = jnp.exp(s_ctx - m)
            inv = 1.0 / (jnp.sum(p_win, axis=-1, keepdims=True) + jnp.sum(p_ctx, axis=-1, keepdims=True))
            probs.append(((p_win * inv).astype(BF16), (p_ctx * inv).astype(BF16)))
        for i, (qr, kr) in enumerate(zip(q_rows, k_rows)):
            vb = v_ref[kr, :].astype(BF16)
            outs = [_dot(probs[2 * i + h][0], vb) + _dot(probs[2 * i + h][1], cvb_s[...]) for h in heads]
            o_ref[qr, :] = jnp.where(first, outs[0], outs[1]).astype(o_ref.dtype)
        return carry

    lax.fori_loop(0, rows // NAT_ROWS_PER_STEP, row_group, 0)


def _nat_attention(u, cache_k, cache_v, cache_layer, bias, *, n_seq, seq_len, row_block0, e):
    hp = e // LANES
    hpb = LANES // HEAD_DIM
    past = cache_k.shape[3]
    seq = lambda b: row_block0 + b
    cache_block = (None, None, hpb, past, HEAD_DIM)
    return pl.pallas_call(
        functools.partial(_nat_body, rows=seq_len // GRID_W),
        grid=(n_seq, hp),
        in_specs=[pl.BlockSpec((seq_len, LANES), lambda b, h: (seq(b), h)),
                  pl.BlockSpec((seq_len, LANES), lambda b, h: (seq(b), hp + h)),
                  pl.BlockSpec((seq_len, LANES), lambda b, h: (seq(b), 2 * hp + h)),
                  pl.BlockSpec(cache_block, lambda b, h: (b, cache_layer, h, 0, 0)),
                  pl.BlockSpec(cache_block, lambda b, h: (b, cache_layer, h, 0, 0)),
                  pl.BlockSpec((hpb,) + bias.shape[1:], lambda b, h: (h, 0, 0, 0))],
        out_specs=pl.BlockSpec((seq_len, LANES), lambda b, h: (b, h)),
        out_shape=jax.ShapeDtypeStruct((n_seq * seq_len, e), ACT),
        scratch_shapes=[pltpu.VMEM((past, LANES), BF16), pltpu.VMEM((past, LANES), BF16)],
        compiler_params=_params("parallel", "parallel"),
        name="nat_attention",
    )(u, u, u, cache_k, cache_v, bias)


def kernel(x_prompt, x_sample, state_ssd, state_s5, cache_k, cache_v, c, c_ctx, norm_g, w_mod, b_mod, w_out, final_g, ssd_w_in, ssd_conv_w, ssd_conv_b, ssd_dt_bias, ssd_a_log, ssd_d, ssd_norm_g, mlp_w_in, mlp_ln_g, mlp_ln_b, mlp_w_s, mlp_b_s, s5_w_in, s5_lam_re, s5_lam_im, s5_log_step, s5_b_re, s5_b_im, s5_c_re, s5_c_im, s5_d, s5_w_glu, s5_b_glu, nat_w_in, nat_rpb):
    n_ctx, l_ctx, d = x_prompt.shape
    n_lat, l_lat, _ = x_sample.shape
    t_ctx = n_ctx * l_ctx
    depth = norm_g.shape[0]
    e = w_out.shape[1]
    assert depth == 4 and l_lat % l_ctx == 0 and t_ctx % (n_lat * l_lat) == 0
    tiles = dict(tm=OUT_TM, t_ctx=t_ctx, l_lat=l_lat)
    tiles_in = dict(tm=IN_TM, tn=IN_TN, t_ctx=t_ctx, l_lat=l_lat)

    t_lat = n_lat * l_lat
    x = (x_prompt.reshape(t_ctx, d), x_sample.reshape(t_lat, d))
    cond = jnp.concatenate([c_ctx[None], c, jnp.zeros((MOD_ROWS - 1 - n_lat, d), F32)], axis=0)
    mods = _modulation(cond, w_mod, b_mod).reshape(depth, MOD_ROWS, 3, 1, d)
    w_out_b = w_out.astype(BF16)

    groups = e // SSD_GROUP_CH
    n_main = 3 * e
    w_dt = ssd_w_in[0][:, n_main:].reshape(d, 2, groups, SSD_GROUP_HEADS).transpose(0, 2, 1, 3)
    w_dt = jnp.pad(w_dt.reshape(d, groups, 2 * SSD_GROUP_HEADS), ((0, 0), (0, 0), (0, LANES - 2 * SSD_GROUP_HEADS)))
    u, dt = _inproj(x, mods, 0, norm_g[0:1], ssd_w_in[0].astype(BF16), n_out=n_main,
                    w_extra=w_dt.reshape(d, groups * LANES).astype(BF16), name="ssd_in", **dict(tiles_in, tm=IN_TM_PAIR))
    ssd_args = (ssd_conv_w[0], ssd_conv_b[0:1], _group_lanes(ssd_dt_bias[0], groups),
                _group_lanes(ssd_a_log[0], groups), jnp.repeat(ssd_d[0], SSD_HEAD_DIM)[None])
    y_ctx, new_ssd = _ssd_call(u, dt, *ssd_args, None, n_seq=n_ctx, seq_len=l_ctx, seqs_per_step=l_lat // l_ctx,
                               row_block0=0, want_final=True, name="ssd_ctx")
    y_lat, = _ssd_call(u, dt, *ssd_args, state_ssd[:, 0], n_seq=n_lat, seq_len=l_lat, seqs_per_step=n_lat,
                       row_block0=t_ctx // t_lat, want_final=False, name="ssd_lat")
    x = _outproj((y_ctx, y_lat), u, 0, x, mods, 0, w_out_b, norm_g=ssd_norm_g[0:1], name="ssd_out", **tiles)

    u = _inproj(x, mods, 1, norm_g[1:2], mlp_w_in[0].astype(BF16), name="mlp_in", **tiles_in)
    b_lanes = jnp.repeat(mlp_b_s[0].T, e // mlp_b_s.shape[1], axis=1)
    y = _gmlp(u, mlp_ln_g[0:1], mlp_ln_b[0:1], mlp_w_s[0], b_lanes)
    x = _outproj(y, u, 2, x, mods, 1, w_out_b, name="mlp_out", **tiles)

    u = _inproj(x, mods, 2, norm_g[2:3], s5_w_in[0].astype(BF16), name="s5_in", **tiles_in)
    s5_groups = e // S5_GROUP
    bg = LANES // S5_GROUP
    blocks = s5_groups // bg

    def block_vec(v):
        return v.reshape(2, blocks, 1, S5_BLOCK_ST).transpose(1, 0, 2, 3)

    def block_mat_b(v):
        v = v.reshape(2, blocks, bg, S5_STATE, S5_GROUP).transpose(1, 0, 4, 2, 3)
        return v.reshape(blocks, 2, S5_GROUP, S5_BLOCK_ST)

    def block_mat_c(v):
        v = v.reshape(2, blocks, bg, S5_GROUP, S5_STATE).transpose(1, 0, 3, 2, 4)
        return v.reshape(blocks, 2, S5_GROUP, S5_BLOCK_ST)

    log_step = jnp.repeat(s5_log_step[0][:, :, None], S5_STATE, axis=2)
    prm = (block_vec(s5_lam_re[0]), block_vec(s5_lam_im[0]), block_vec(log_step), block_mat_b(s5_b_re[0]),
           block_mat_b(s5_b_im[0]), block_mat_c(s5_c_re[0]), block_mat_c(s5_c_im[0]))
    h0 = state_s5[:, 0].reshape(n_lat, 2, 2, blocks, S5_BLOCK_ST).transpose(3, 1, 2, 0, 4)
    ys, fin = _s5_core(prm, h0, u, n_ctx=n_ctx, l_ctx=l_ctx, n_lat=n_lat, l_lat=l_lat, e=e)
    new_s5 = fin[:, None]
    y = _s5_glu(ys, u, s5_d[0:1], s5_w_glu[0].astype(BF16), s5_b_glu[0:1])
    x = _outproj(y, u, 1, x, mods, 2, w_out_b, name="s5_out", **tiles)

    u = _inproj(x, mods, 3, norm_g[3:4], nat_w_in[0].astype(BF16), name="nat_in", **tiles_in)
    y_ctx, new_k, new_v = _ctx_attention(u, n_seq=n_ctx, seq_len=l_ctx, e=e)
    y_lat = _nat_attention(u, cache_k, cache_v, 0, _nat_bias(nat_rpb[0]), n_seq=n_lat, seq_len=l_lat,
                           row_block0=t_ctx // l_lat, e=e)
    out_ctx, out_lat = _outproj((y_ctx, y_lat), u, 3, x, mods, 3, w_out_b, final_g=final_g[None], split_rows=(t_ctx, t_lat),
                                name="nat_out", **tiles)

    return (out_ctx.reshape(n_ctx, l_ctx, d), out_lat.reshape(n_lat, l_lat, d),
            new_ssd[:, None], new_s5, new_k[:, None], new_v[:, None])
```

```python
import collections
import functools

import jax
import jax.numpy as jnp
from jax import lax
from jax.experimental import pallas as pl
from jax.experimental.pallas import tpu as pltpu

F32 = jnp.float32
BF16 = jnp.bfloat16
ACT = jnp.bfloat16
EPS = 1e-6

LANES = 128
CHUNK = 128
SSD_HEAD_DIM = 64
SSD_GROUP_HEADS = 4
SSD_GROUP_CH = SSD_HEAD_DIM * SSD_GROUP_HEADS
SSD_STATE = 128
CONV_W = 5
CONV_HALO = 16
SSD_INTERLEAVE = 4
CONV_WIN = 2 * CHUNK
S5_T = 8
S5_GROUP = 16
S5_STATE = 64
S5_BLOCK_ST = (LANES // S5_GROUP) * S5_STATE
HEAD_DIM = 64
GRID_W = 64
WIN_ROWS = 8
WIN_COLS = 16
NAT_ROWS_PER_STEP = 16
MASKED = -1e30
MOD_ROWS = 8

VMEM_LIMIT = 52 * 1024 * 1024
IN_TM, IN_TN = 2048, 1024
IN_TM_PAIR = 1024
OUT_TM = 512
GMLP_TM = 512
GLU_TM = 512
CTX_ATTN_WIDTH = 512


def _silu(x):
    return x * jax.nn.sigmoid(x)


def _gelu(x):
    return 0.5 * x * (1.0 + jnp.tanh(0.7978845608028654 * (x + 0.044715 * (x * x * x))))


def _softplus(x):
    return jnp.maximum(x, 0.0) + jnp.log1p(jnp.exp(-jnp.abs(x)))


def _dot(a, b):
    return jnp.dot(a, b, preferred_element_type=F32)


def _dot_nt(a, b):
    return lax.dot_general(a, b, (((1,), (1,)), ((), ())), preferred_element_type=F32)


def _cumsum_rows(lower_tri, x):
    hi = x.astype(BF16)
    r1 = x - hi.astype(F32)
    mid = r1.astype(BF16)
    lo = (r1 - mid.astype(F32)).astype(BF16)
    return _dot(lower_tri, hi) + _dot(lower_tri, mid) + _dot(lower_tri, lo)


def _params(*sem):
    return pltpu.CompilerParams(dimension_semantics=sem, vmem_limit_bytes=VMEM_LIMIT)


def _mod_row(i, *, tm, t_ctx, l_lat):
    start = i * tm
    return jnp.where(start < t_ctx, 0, 1 + (start - t_ctx) // l_lat)


def _mod_body(c_ref, w_ref, b_ref, o_ref):
    cond = _silu(c_ref[...]).astype(BF16)
    o_ref[...] = _dot(cond, w_ref[...].astype(BF16)) + b_ref[...]


def _modulation(cond, w_mod, b_mod, *, tn=IN_TN):
    depth, d, n = w_mod.shape
    rows = cond.shape[0]
    return pl.pallas_call(
        _mod_body,
        grid=(depth, n // tn),
        in_specs=[pl.BlockSpec((rows, d), lambda l, j: (0, 0)),
                  pl.BlockSpec((None, d, tn), lambda l, j: (l, 0, j)),
                  pl.BlockSpec((None, 1, tn), lambda l, j: (l, 0, j))],
        out_specs=pl.BlockSpec((None, rows, tn), lambda l, j: (l, 0, j)),
        out_shape=jax.ShapeDtypeStruct((depth, rows, n), F32),
        compiler_params=_params("parallel", "parallel"),
        name="modulation",
    )(cond, w_mod, b_mod.reshape(depth, 1, n))


def _stream_specs(x, tm, width, n_grid):
    ids = (lambda i, *_: i) if n_grid == 1 else (lambda i, j: i)
    if not isinstance(x, tuple):
        return [x], [pl.BlockSpec((tm, width), lambda *g: (ids(*g), 0))], None
    split = x[0].shape[0] // tm
    last = x[1].shape[0] // tm - 1
    return (list(x),
            [pl.BlockSpec((tm, width), lambda *g: (jnp.minimum(ids(*g), split - 1), 0)),
             pl.BlockSpec((tm, width), lambda *g: (jnp.clip(ids(*g) - split, 0, last), 0))],
            split)


def _inproj_body(*refs, split, extra_from):
    n_x = 1 if split is None else 2
    x_refs = refs[:n_x]
    if extra_from is None:
        shift_ref, scale_ref, g_ref, w_ref, o_ref, h_ref = refs[n_x:]
    else:
        shift_ref, scale_ref, g_ref, w_ref, w2_ref, o_ref, o2_ref, h_ref = refs[n_x:]

    def normalise(x_ref):
        x = x_ref[...]
        y = x * lax.rsqrt(jnp.mean(x * x, axis=-1, keepdims=True) + EPS) * g_ref[...]
        h_ref[...] = (y * (1.0 + scale_ref[...]) + shift_ref[...]).astype(BF16)

    first_col = pl.program_id(1) == 0
    if split is None:
        pl.when(first_col)(lambda: normalise(x_refs[0]))
    else:
        in_ctx = pl.program_id(0) < split
        pl.when(first_col & in_ctx)(lambda: normalise(x_refs[0]))
        pl.when(first_col & jnp.logical_not(in_ctx))(lambda: normalise(x_refs[1]))
    if extra_from is None:
        o_ref[...] = _dot(h_ref[...], w_ref[...]).astype(o_ref.dtype)
    else:
        main = pl.program_id(1) < extra_from

        @pl.when(main)
        def _():
            o_ref[...] = _dot(h_ref[...], w_ref[...]).astype(o_ref.dtype)

        @pl.when(jnp.logical_not(main))
        def _():
            o2_ref[...] = _dot(h_ref[...], w2_ref[...]).astype(o2_ref.dtype)


def _inproj(x, mods, layer, g, w, *, tm, tn, t_ctx, l_lat, name, out_dtype=ACT, n_out=None, w_extra=None):
    d = w.shape[0]
    n = w.shape[1] if n_out is None else n_out
    row = functools.partial(_mod_row, tm=tm, t_ctx=t_ctx, l_lat=l_lat)
    x_args, x_specs, split = _stream_specs(x, tm, d, 2)
    t = sum(a.shape[0] for a in x_args)
    nj = n // tn

    def mod_spec(part):
        return pl.BlockSpec((None, None, None, 1, d), lambda i, j: (layer, row(i), part, 0, 0))

    in_specs = x_specs + [mod_spec(0), mod_spec(1), pl.BlockSpec((1, d), lambda i, j: (0, 0))]
    if w_extra is None:
        in_specs.append(pl.BlockSpec((d, tn), lambda i, j: (0, j)))
        weights, steps = [w], nj
        out_specs = pl.BlockSpec((tm, tn), lambda i, j: (i, j))
        out_shape = jax.ShapeDtypeStruct((t, n), out_dtype)
    else:
        nj2 = w_extra.shape[1] // tn
        in_specs += [pl.BlockSpec((d, tn), lambda i, j: (0, jnp.minimum(j, nj - 1))),
                     pl.BlockSpec((d, tn), lambda i, j: (0, jnp.clip(j - nj, 0, nj2 - 1)))]
        weights, steps = [w, w_extra], nj + nj2
        out_specs = [pl.BlockSpec((tm, tn), lambda i, j: (i, jnp.minimum(j, nj - 1))),
                     pl.BlockSpec((tm, tn), lambda i, j: (i, jnp.clip(j - nj, 0, nj2 - 1)))]
        out_shape = [jax.ShapeDtypeStruct((t, n), out_dtype), jax.ShapeDtypeStruct((t, w_extra.shape[1]), F32)]
    return pl.pallas_call(
        functools.partial(_inproj_body, split=split, extra_from=None if w_extra is None else nj),
        grid=(t // tm, steps),
        in_specs=in_specs,
        out_specs=out_specs,
        out_shape=out_shape,
        scratch_shapes=[pltpu.VMEM((tm, d), BF16)],
        compiler_params=_params("parallel", "arbitrary"),
        name=name,
    )(*x_args, mods, mods, g, *weights)


def _outproj_body(*refs, n_y, n_x, split, gated_norm, final_norm):
    refs = list(refs)
    y_refs, z_ref, x_refs = refs[:n_y], refs[n_y], refs[n_y + 1:n_y + 1 + n_x]
    rest = refs[n_y + 1 + n_x:]
    gate_ref, w_ref = rest[:2]
    rest = rest[2:]
    ng_ref = rest.pop(0) if gated_norm else None
    fg_ref = rest.pop(0) if final_norm else None
    o_refs = rest

    def emit(y_ref, x_ref, o_ref):
        t = y_ref[...].astype(F32) * _silu(z_ref[...].astype(F32))
        if gated_norm:
            t = t * lax.rsqrt(jnp.mean(t * t, axis=-1, keepdims=True) + EPS) * ng_ref[...]
        xn = x_ref[...] + gate_ref[...] * _dot(t.astype(BF16), w_ref[...])
        if final_norm:
            xn = xn * lax.rsqrt(jnp.mean(xn * xn, axis=-1, keepdims=True) + EPS) * fg_ref[...]
        o_ref[...] = xn

    if split is None:
        emit(y_refs[0], x_refs[0], o_refs[0])
    else:
        in_ctx = pl.program_id(0) < split
        pl.when(in_ctx)(lambda: emit(y_refs[0], x_refs[0], o_refs[0]))
        pl.when(jnp.logical_not(in_ctx))(lambda: emit(y_refs[-1], x_refs[-1], o_refs[-1]))


def _outproj(y, u, z_block, x, mods, layer, w, *, norm_g=None, final_g=None, split_rows=None, tm, t_ctx, l_lat,
             name):
    t, e, d = u.shape[0], w.shape[1], w.shape[2]
    row = functools.partial(_mod_row, tm=tm, t_ctx=t_ctx, l_lat=l_lat)
    y_args, y_specs, split_y = _stream_specs(y, tm, e, 1)
    x_args, x_specs, split_in = _stream_specs(x, tm, d, 1)
    in_specs = (y_specs + [pl.BlockSpec((tm, e), lambda i: (i, z_block))] + x_specs
                + [pl.BlockSpec((None, None, None, 1, d), lambda i: (layer, row(i), 2, 0, 0)),
                   pl.BlockSpec((None, e, d), lambda i: (layer, 0, 0))])
    args = y_args + [u] + x_args + [mods, w]
    if norm_g is not None:
        in_specs.append(pl.BlockSpec((1, e), lambda i: (0, 0)))
        args.append(norm_g)
    if final_g is not None:
        in_specs.append(pl.BlockSpec((1, d), lambda i: (0, 0)))
        args.append(final_g)
    if split_rows is None:
        split_out = None
        out_specs = pl.BlockSpec((tm, d), lambda i: (i, 0))
        out_shape = jax.ShapeDtypeStruct((t, d), F32)
    else:
        split_out = split_rows[0] // tm
        last = split_rows[1] // tm - 1
        out_specs = [pl.BlockSpec((tm, d), lambda i: (jnp.minimum(i, split_out - 1), 0)),
                     pl.BlockSpec((tm, d), lambda i: (jnp.clip(i - split_out, 0, last), 0))]
        out_shape = [jax.ShapeDtypeStruct((rows, d), F32) for rows in split_rows]
    return pl.pallas_call(
        functools.partial(_outproj_body, n_y=len(y_args), n_x=len(x_args),
                          split=next((s for s in (split_y, split_in, split_out) if s is not None), None),
                          gated_norm=norm_g is not None, final_norm=final_g is not None),
        grid=(t // tm,),
        in_specs=in_specs,
        out_specs=out_specs,
        out_shape=out_shape,
        compiler_params=_params("arbitrary"),
        name=name,
    )(*args)


_SsdScratch = collections.namedtuple("_SsdScratch", "pad xbd xt bm bt cm e et ct dtt y sf sb")


def _ssd_scratch_shapes(seq_len):
    nc = seq_len // CHUNK
    width = SSD_GROUP_CH + 2 * SSD_STATE
    return [pltpu.VMEM((seq_len + CONV_WIN - CHUNK, width), ACT),
            pltpu.VMEM((nc, SSD_GROUP_HEADS * CHUNK, SSD_GROUP_CH), BF16),
            pltpu.VMEM((nc, SSD_GROUP_CH, CHUNK), F32),
            pltpu.VMEM((seq_len, SSD_STATE), BF16),
            pltpu.VMEM((nc, SSD_STATE, CHUNK), BF16),
            pltpu.VMEM((seq_len, SSD_STATE), BF16),
            pltpu.VMEM((seq_len, LANES), F32),
            pltpu.VMEM((nc, 2 * SSD_GROUP_HEADS, CHUNK), F32),
            pltpu.VMEM((nc, 2 * SSD_GROUP_HEADS, CHUNK), F32),
            pltpu.VMEM((nc, 2 * SSD_GROUP_HEADS, CHUNK), F32),
            pltpu.VMEM((seq_len, SSD_GROUP_CH), F32),
            pltpu.VMEM((SSD_GROUP_CH, SSD_STATE), F32),
            pltpu.VMEM((SSD_GROUP_CH, SSD_STATE), F32)]


def _ssd_body(*refs, seq_len, seqs_per_step, has_h0, want_final):
    refs = list(refs)
    x_ref, b_ref, c_ref, dt_ref, wx_ref, wb_ref, wc_ref, bx_ref, bb_ref, bc_ref, dtb_ref, alog_ref, dsk_ref = refs[:13]
    rest = refs[13:]
    h0_ref = rest.pop(0) if has_h0 else None
    y_ref = rest.pop(0)
    hf_ref = rest.pop(0) if want_final else None
    shift_s = rest.pop(0)
    il = min(SSD_INTERLEAVE, seqs_per_step)
    per_seq = len(rest) // il
    sc = [_SsdScratch(*rest[q * per_seq:(q + 1) * per_seq]) for q in range(il)]
    lanes_q = range(il)
    nc = seq_len // CHUNK
    width = SSD_GROUP_CH + 2 * SSD_STATE
    half = width // 2
    nh = SSD_GROUP_HEADS
    mid = CONV_W // 2

    tail = CONV_WIN - CHUNK - CONV_HALO
    for q in lanes_q:
        sc[q].pad[0:CONV_HALO, :] = jnp.zeros((CONV_HALO, width), ACT)
        sc[q].pad[CONV_HALO + seq_len:CONV_HALO + seq_len + tail, :] = jnp.zeros((tail, width), ACT)
    win_row = lax.broadcasted_iota(jnp.int32, (CHUNK, CONV_WIN), 1)
    tok_row = lax.broadcasted_iota(jnp.int32, (CHUNK, CONV_WIN), 0)
    taps = [j for j in range(CONV_W) if j != mid]
    for n, j in enumerate(taps):
        shift_s[n] = (win_row == tok_row + (CONV_HALO + j - mid)).astype(BF16)

    conv_w = jnp.concatenate([wx_ref[...], wb_ref[...], wc_ref[...]], axis=1)
    conv_b = jnp.concatenate([bx_ref[...], bb_ref[...], bc_ref[...]], axis=1)
    a_row = -jnp.exp(alog_ref[...])
    row_i = lax.broadcasted_iota(jnp.int32, (CHUNK, CHUNK), 0)
    col_i = lax.broadcasted_iota(jnp.int32, (CHUNK, CHUNK), 1)
    lower = row_i >= col_i
    upper = row_i <= col_i
    lower_b = lower.astype(BF16)
    fwd_lane = lax.broadcasted_iota(jnp.int32, (1, LANES), 1) < nh
    head_of_lane = lax.broadcasted_iota(jnp.int32, (CHUNK, SSD_GROUP_CH), 1) // SSD_HEAD_DIM

    def chunk_rows(ci):
        return pl.ds(pl.multiple_of(ci * CHUNK, CHUNK), CHUNK)

    def copy_chunk(offs, ci, carry):
        base = pl.multiple_of(ci * CHUNK, CHUNK)
        dst = pl.ds(base + CONV_HALO, CHUNK)
        for q in lanes_q:
            src = pl.ds(offs[q] + base, CHUNK)
            sc[q].pad[dst, 0:SSD_GROUP_CH] = x_ref[src, :]
            sc[q].pad[dst, SSD_GROUP_CH:SSD_GROUP_CH + SSD_STATE] = b_ref[src, :]
            sc[q].pad[dst, SSD_GROUP_CH + SSD_STATE:width] = c_ref[src, :]
        return carry

    def prep_chunk(offs, ci, carry):
        r = chunk_rows(ci)
        units = [(q, h) for h in range(2) for q in lanes_q]

        def window(unit):
            q, h = unit
            return sc[q].pad[pl.ds(pl.multiple_of(ci * CHUNK, CHUNK), CONV_WIN), h * half:(h + 1) * half]

        def shift_dots(unit):
            win = window(unit)
            return [_dot(shift_s[t], win) for t in range(len(taps))]

        def finish(unit, shifted):
            q, h = unit
            cols = slice(h * half, (h + 1) * half)
            acc = conv_b[:, cols] + window(unit)[CONV_HALO:CONV_HALO + CHUNK].astype(F32) * conv_w[mid:mid + 1, cols]
            for t, j in enumerate(taps):
                acc = acc + shifted[t] * conv_w[j:j + 1, cols]
            v = _silu(acc)
            if h == 0:
                sc[q].y[r, :] = v * dsk_ref[...]
                vb = v.astype(BF16)
                for hd in range(nh):
                    sc[q].xbd[ci, hd * CHUNK:(hd + 1) * CHUNK, :] = jnp.where(head_of_lane == hd, vb,
                                                                              jnp.zeros_like(vb))
                sc[q].xt[ci] = v.T
            else:
                bm = v[:, 0:SSD_STATE]
                sc[q].bm[r, :] = bm.astype(BF16)
                sc[q].bt[ci] = bm.T.astype(BF16)
                sc[q].cm[r, :] = v[:, SSD_STATE:half].astype(BF16)

        dts = [_softplus(dt_ref[pl.ds(offs[q] + pl.multiple_of(ci * CHUNK, CHUNK), CHUNK), :] + dtb_ref[...])
               for q in lanes_q]
        das = [dt * a_row for dt in dts]
        pending = shift_dots(units[0])
        cums = []
        for i, unit in enumerate(units):
            following = shift_dots(units[i + 1]) if i + 1 < len(units) else None
            if unit[1] == 0:
                cums.append(_cumsum_rows(lower_b, das[unit[0]]))
            finish(unit, pending)
            pending = following
        for q in lanes_q:
            e = jnp.where(fwd_lane, cums[q], cums[q] - das[q])
            sc[q].e[r, :] = e
            sc[q].et[ci] = e.T[0:2 * nh]
            sc[q].ct[ci] = cums[q].T[0:2 * nh]
            sc[q].dtt[ci] = dts[q].T[0:2 * nh]
        return carry

    def scan_chunk(i, carry):
        chunks = (i, nc - 1 - i)
        rows = [chunk_rows(ci) for ci in chunks]
        streams = [(q, d) for q in lanes_q for d in range(2)]
        st = {(q, d): (sc[q].sf if d == 0 else sc[q].sb) for q, d in streams}
        s_prev = {k: st[k][...] for k in streams}
        g, cs = {}, {}
        for q, d in streams:
            cm = sc[q].cm[rows[d], :]
            g[q, d] = _dot(cm, sc[q].bt[chunks[d]])
            cs[q, d] = _dot_nt(cm, s_prev[q, d].astype(BF16))
        w_rows, keeps, ys = {}, {}, {}
        for q, d in streams:
            e = sc[q].e[rows[d], :]
            e_t, c_t, dt_t = sc[q].et[chunks[d]], sc[q].ct[chunks[d]], sc[q].dtt[chunks[d]]
            w_d, keep_d, mixes, offs = [], [], [], []
            for h in range(nh):
                col = h + nh * d
                ec = jnp.broadcast_to(e[:, col:col + 1], (CHUNK, CHUNK))
                er = e_t[col:col + 1, :]
                dt_h = dt_t[col:col + 1, :]
                log_dt = jnp.log(dt_h)
                tot = c_t[col:col + 1, CHUNK - 1:CHUNK]
                if d == 0:
                    decay_dt = jnp.exp(jnp.where(lower, ec - (er - log_dt), -jnp.inf))
                    offs.append(jnp.exp(ec))
                    w_d.append(jnp.broadcast_to(dt_h * jnp.exp(tot - er), (SSD_HEAD_DIM, CHUNK)))
                else:
                    decay_dt = jnp.exp(jnp.where(upper, (er + log_dt) - ec, -jnp.inf))
                    offs.append(jnp.exp(tot - ec))
                    w_d.append(jnp.broadcast_to(dt_h * jnp.exp(er), (SSD_HEAD_DIM, CHUNK)))
                keep_d.append(jnp.broadcast_to(jnp.exp(tot), (SSD_HEAD_DIM, SSD_STATE)))
                mixes.append((g[q, d] * decay_dt).astype(BF16))
            y_diag = _dot(jnp.concatenate(mixes, axis=1), sc[q].xbd[chunks[d]])
            off = jnp.concatenate([offs[nh - 1]] * (SSD_GROUP_CH // CHUNK), axis=1)
            for h in range(nh - 2, -1, -1):
                off = jnp.where(head_of_lane == h, jnp.concatenate([offs[h]] * (SSD_GROUP_CH // CHUNK), axis=1), off)
            ys[q, d] = y_diag + cs[q, d] * off
            w_rows[q, d] = jnp.concatenate(w_d, axis=0)
            keeps[q, d] = jnp.concatenate(keep_d, axis=0)
        for q, d in streams:
            w_t = (sc[q].xt[chunks[d]] * w_rows[q, d]).astype(BF16)
            st[q, d][...] = keeps[q, d] * s_prev[q, d] + _dot(w_t, sc[q].bm[rows[d], :])
            sc[q].y[rows[d], :] += ys[q, d]
        return carry

    def emit_chunk(offs, ci, carry):
        base = pl.multiple_of(ci * CHUNK, CHUNK)
        for q in lanes_q:
            y_ref[pl.ds(offs[q] + base, CHUNK), :] = sc[q].y[pl.ds(base, CHUNK), :].astype(y_ref.dtype)
        return carry

    def one_group(s, carry):
        seqs = [s * il + q for q in lanes_q]
        offs = [pl.multiple_of(sq * seq_len, CHUNK) for sq in seqs]
        lax.fori_loop(0, nc, functools.partial(copy_chunk, offs), 0)
        lax.fori_loop(0, nc, functools.partial(prep_chunk, offs), 0)
        for q in lanes_q:
            if has_h0:
                sc[q].sf[...] = h0_ref[seqs[q], 0].reshape(SSD_GROUP_CH, SSD_STATE)
                sc[q].sb[...] = h0_ref[seqs[q], 1].reshape(SSD_GROUP_CH, SSD_STATE)
            else:
                sc[q].sf[...] = jnp.zeros((SSD_GROUP_CH, SSD_STATE), F32)
                sc[q].sb[...] = jnp.zeros((SSD_GROUP_CH, SSD_STATE), F32)
        lax.fori_loop(0, nc, scan_chunk, 0)
        lax.fori_loop(0, nc, functools.partial(emit_chunk, offs), 0)
        if want_final:
            for q in lanes_q:
                hf_ref[seqs[q], 0] = sc[q].sf[...].reshape(SSD_GROUP_HEADS, SSD_HEAD_DIM, SSD_STATE)
                hf_ref[seqs[q], 1] = sc[q].sb[...].reshape(SSD_GROUP_HEADS, SSD_HEAD_DIM, SSD_STATE)
        return carry

    lax.fori_loop(0, seqs_per_step // il, one_group, 0)


def _ssd_call(u, dt, conv_w, conv_b, dtb, alog, dsk, h0, *, n_seq, seq_len, seqs_per_step, row_block0,
              want_final, name):
    t, n_u = u.shape
    e = dsk.shape[1]
    groups = e // SSD_GROUP_CH
    xb0 = e // SSD_GROUP_CH
    bb0 = 2 * e // SSD_STATE
    cb0 = bb0 + groups
    has_h0 = h0 is not None
    seq = lambda b: row_block0 + b
    rows = seqs_per_step * seq_len
    in_specs = [pl.BlockSpec((rows, SSD_GROUP_CH), lambda b, g: (seq(b), xb0 + g)),
                pl.BlockSpec((rows, SSD_STATE), lambda b, g: (seq(b), bb0 + g)),
                pl.BlockSpec((rows, SSD_STATE), lambda b, g: (seq(b), cb0 + g)),
                pl.BlockSpec((rows, LANES), lambda b, g: (seq(b), g)),
                pl.BlockSpec((CONV_W, SSD_GROUP_CH), lambda b, g: (0, g)),
                pl.BlockSpec((CONV_W, SSD_STATE), lambda b, g: (0, bb0 - xb0 * 2 + g)),
                pl.BlockSpec((CONV_W, SSD_STATE), lambda b, g: (0, cb0 - xb0 * 2 + g)),
                pl.BlockSpec((1, SSD_GROUP_CH), lambda b, g: (0, g)),
                pl.BlockSpec((1, SSD_STATE), lambda b, g: (0, bb0 - xb0 * 2 + g)),
                pl.BlockSpec((1, SSD_STATE), lambda b, g: (0, cb0 - xb0 * 2 + g)),
                pl.BlockSpec((1, LANES), lambda b, g: (0, g)),
                pl.BlockSpec((1, LANES), lambda b, g: (0, g)),
                pl.BlockSpec((1, SSD_GROUP_CH), lambda b, g: (0, g))]
    args = [u, u, u, dt, conv_w, conv_w, conv_w, conv_b, conv_b, conv_b, dtb, alog, dsk]
    state_block = (seqs_per_step, 2, SSD_GROUP_HEADS, SSD_HEAD_DIM, SSD_STATE)
    if has_h0:
        in_specs.append(pl.BlockSpec(state_block, lambda b, g: (b, 0, g, 0, 0)))
        args.append(h0)
    out_specs = [pl.BlockSpec((rows, SSD_GROUP_CH), lambda b, g: (b, g))]
    out_shape = [jax.ShapeDtypeStruct((n_seq * seq_len, e), ACT)]
    if want_final:
        out_specs.append(pl.BlockSpec(state_block, lambda b, g: (b, 0, g, 0, 0)))
        out_shape.append(jax.ShapeDtypeStruct((n_seq, 2, e // SSD_HEAD_DIM, SSD_HEAD_DIM, SSD_STATE), F32))
    return pl.pallas_call(
        functools.partial(_ssd_body, seq_len=seq_len, seqs_per_step=seqs_per_step, has_h0=has_h0,
                          want_final=want_final),
        grid=(n_seq // seqs_per_step, groups),
        in_specs=in_specs,
        out_specs=out_specs,
        out_shape=out_shape,
        scratch_shapes=([pltpu.VMEM((CONV_W - 1, CHUNK, CONV_WIN), BF16)]
                        + _ssd_scratch_shapes(seq_len) * min(SSD_INTERLEAVE, seqs_per_step)),
        compiler_params=_params("parallel", "parallel"),
        name=name,
    )(*args)


def _group_lanes(v, groups):
    per_group = v.reshape(2, groups, SSD_GROUP_HEADS).transpose(1, 0, 2).reshape(groups, 2 * SSD_GROUP_HEADS)
    return jnp.pad(per_group, ((0, 0), (0, LANES - 2 * SSD_GROUP_HEADS))).reshape(1, groups * LANES)


def _gmlp_body(u_ref, v_ref, lng_ref, lnb_ref, ws_ref, bs_ref, o_ref, *, tm):
    groups = ws_ref.shape[0]
    gch = u_ref.shape[1] // groups
    for c in range(tm // CHUNK):
        rows = slice(c * CHUNK, (c + 1) * CHUNK)
        v = _gelu(v_ref[rows, :].astype(F32))
        vc = v - jnp.mean(v, axis=-1, keepdims=True)
        vn = vc * lax.rsqrt(jnp.mean(vc * vc, axis=-1, keepdims=True) + EPS) * lng_ref[...] + lnb_ref[...]
        vb = vn.astype(BF16)
        for g in range(groups):
            cols = slice(g * gch, (g + 1) * gch)
            s = _dot(ws_ref[g].astype(BF16), vb[:, cols]) + bs_ref[:, cols]
            o_ref[rows, cols] = (_gelu(u_ref[rows, cols].astype(F32)) * s).astype(o_ref.dtype)


def _gmlp(u, ln_g, ln_b, w_s, b_lanes, *, tm=GMLP_TM):
    t = u.shape[0]
    e = ln_g.shape[1]
    return pl.pallas_call(
        functools.partial(_gmlp_body, tm=tm),
        grid=(t // tm,),
        in_specs=[pl.BlockSpec((tm, e), lambda i: (i, 0)),
                  pl.BlockSpec((tm, e), lambda i: (i, 1)),
                  pl.BlockSpec((1, e), lambda i: (0, 0)),
                  pl.BlockSpec((1, e), lambda i: (0, 0)),
                  pl.BlockSpec(w_s.shape, lambda i: (0, 0, 0)),
                  pl.BlockSpec((CHUNK, e), lambda i: (0, 0))],
        out_specs=pl.BlockSpec((tm, e), lambda i: (i, 0)),
        out_shape=jax.ShapeDtypeStruct((t, e), ACT),
        compiler_params=_params("parallel"),
        name="gmlp",
    )(u, u, ln_g, ln_b, w_s, b_lanes)


def _s5_body(lr_ref, li_ref, ls_ref, brt_ref, bit_ref, cre_ref, cim_ref, h0_ref, u_ref, y_ref, fin_ref,
             win_s, t_s, ef_s, eb_s, z_s, a_s, zin_s, d_s, spf_s, spb_s, yo_s, uf_s,
             *, seqs, l_ctx, n_lat, l_lat, ctx_parts):
    bst = S5_BLOCK_ST
    ng = LANES // S5_GROUP
    kw = S5_T * LANES
    part = pl.program_id(1)

    @pl.when(part == 0)
    def _build():
        own = (lax.broadcasted_iota(jnp.int32, (ng, S5_GROUP, bst), 2) // S5_STATE
               == lax.broadcasted_iota(jnp.int32, (ng, S5_GROUP, bst), 0))

        def spread(v):
            return jnp.where(own, v[None], 0.0).reshape(LANES, bst).astype(BF16)

        tau = lax.broadcasted_iota(jnp.int32, (S5_T + 8, 1), 0).astype(F32)
        e_refs = (ef_s, eb_s)
        for d in range(2):
            lr, li = lr_ref[d], li_ref[d]
            step = jnp.exp(ls_ref[d])
            mag = jnp.exp(tau * (lr * step))
            p_re = mag * jnp.cos(tau * (li * step))
            p_im = mag * jnp.sin(tau * (li * step))
            ab_re, ab_im = p_re[1:2], p_im[1:2]
            den = lr * lr + li * li
            nr = ab_re - 1.0
            cr = (nr * lr + ab_im * li) / den
            ci = (ab_im * lr - nr * li) / den
            brt, bit = brt_ref[d], bit_ref[d]
            bb_re = cr * brt - ci * bit
            bb_im = cr * bit + ci * brt
            cre, cim = cre_ref[d], cim_ref[d]
            for k in range(S5_T):
                rows = slice(k * LANES, (k + 1) * LANES)
                tq = S5_T - 1 - k if d == 0 else k
                te = k + 1 if d == 0 else S5_T - k
                pr, pi = p_re[tq:tq + 1], p_im[tq:tq + 1]
                win_s[rows, 2 * d * bst:(2 * d + 1) * bst] = spread(pr * bb_re - pi * bb_im)
                win_s[rows, (2 * d + 1) * bst:(2 * d + 2) * bst] = spread(pr * bb_im + pi * bb_re)
                pr, pi = p_re[te:te + 1], p_im[te:te + 1]
                e_refs[d][rows, 0:bst] = spread(cre * pr - cim * pi)
                e_refs[d][rows, bst:2 * bst] = spread(-(cre * pi + cim * pr))
            c_own = jnp.concatenate([spread(cre), spread(-cim)], axis=1)
            z = _dot_nt(win_s[:, 2 * d * bst:(2 * d + 2) * bst], c_own)
            if d == 0:
                z_s[0:kw, :] = z
            else:
                z_s[kw - LANES:kw, :] += z[0:LANES]
                z_s[kw:2 * kw - LANES, :] = z[LANES:kw]
            a_s[2 * d:2 * d + 1, :] = p_re[S5_T:S5_T + 1]
            a_s[2 * d + 1:2 * d + 2, :] = p_im[S5_T:S5_T + 1]
        for k in range(S5_T):
            r0 = (S5_T - 1 - k) * LANES
            t_s[:, k * LANES:(k + 1) * LANES] = z_s[r0:r0 + kw, :].astype(BF16)

    def outputs():
        yo_s[...] = (_dot(zin_s[...], t_s[...]) + _dot_nt(spf_s[...].astype(BF16), ef_s[...])
                     + _dot_nt(spb_s[...].astype(BF16), eb_s[...]))

    @pl.when(part < ctx_parts)
    def _context():
        nj = l_ctx // S5_T
        uf_s[...] = u_ref[...].astype(F32)
        for j in range(nj):
            for k in range(S5_T):
                zin_s[j * seqs:(j + 1) * seqs, k * LANES:(k + 1) * LANES] = (
                    uf_s[pl.ds(j * S5_T + k, seqs, stride=l_ctx), :].astype(BF16))
        d_s[...] = _dot(zin_s[...], win_s[...])
        yo_s[...] = _dot(zin_s[...], t_s[...])
        coef = [jnp.broadcast_to(a_s[i:i + 1, :], (seqs, bst)) for i in range(4)]
        fr = fi = br = bi = jnp.zeros((seqs, bst), F32)
        for j in range(nj):
            rf = slice(j * seqs, (j + 1) * seqs)
            rb = slice((nj - 1 - j) * seqs, (nj - j) * seqs)
            spf_s[rf, 0:bst] = fr
            spf_s[rf, bst:2 * bst] = fi
            spb_s[rb, 0:bst] = br
            spb_s[rb, bst:2 * bst] = bi
            fr, fi, br, bi = (coef[0] * fr - coef[1] * fi + d_s[rf, 0:bst],
                              coef[0] * fi + coef[1] * fr + d_s[rf, bst:2 * bst],
                              coef[2] * br - coef[3] * bi + d_s[rb, 2 * bst:3 * bst],
                              coef[2] * bi + coef[3] * br + d_s[rb, 3 * bst:4 * bst])
        for d, ri, val in ((0, 0, fr), (0, 1, fi), (1, 0, br), (1, 1, bi)):
            for gg in range(ng):
                fin_ref[:, d, ri, gg, :] = val[:, gg * S5_STATE:(gg + 1) * S5_STATE]
        yo_s[...] += (_dot_nt(spf_s[...].astype(BF16), ef_s[...]) + _dot_nt(spb_s[...].astype(BF16), eb_s[...]))
        for j in range(nj):
            for k in range(S5_T):
                y_ref[pl.ds(j * S5_T + k, seqs, stride=l_ctx), :] = (
                    yo_s[j * seqs:(j + 1) * seqs, k * LANES:(k + 1) * LANES])

    @pl.when(part == ctx_parts)
    def _latent():
        nj = l_lat // S5_T
        uf_s[...] = u_ref[...].astype(F32)
        for b in range(n_lat):
            for k in range(S5_T):
                zin_s[b * nj:(b + 1) * nj, k * LANES:(k + 1) * LANES] = (
                    uf_s[pl.ds(b * l_lat + k, nj, stride=S5_T), :].astype(BF16))
        d_s[...] = _dot(zin_s[...], win_s[...])
        coef_re = jnp.concatenate([jnp.broadcast_to(a_s[0:1, :], (n_lat, bst)),
                                   jnp.broadcast_to(a_s[2:3, :], (n_lat, bst))], axis=0)
        coef_im = jnp.concatenate([jnp.broadcast_to(a_s[1:2, :], (n_lat, bst)),
                                   jnp.broadcast_to(a_s[3:4, :], (n_lat, bst))], axis=0)
        s_re0 = jnp.concatenate([h0_ref[0, 0], h0_ref[1, 0]], axis=0)
        s_im0 = jnp.concatenate([h0_ref[0, 1], h0_ref[1, 1]], axis=0)

        def step(j, carry):
            s_re, s_im = carry
            d_re, d_im = [], []
            for b in range(n_lat):
                row = pl.ds(b * nj + j, 1)
                spf_s[row, 0:bst] = s_re[b:b + 1]
                spf_s[row, bst:2 * bst] = s_im[b:b + 1]
                d_re.append(d_s[row, 0:bst])
                d_im.append(d_s[row, bst:2 * bst])
            for b in range(n_lat):
                row = pl.ds(b * nj + nj - 1 - j, 1)
                spb_s[row, 0:bst] = s_re[n_lat + b:n_lat + b + 1]
                spb_s[row, bst:2 * bst] = s_im[n_lat + b:n_lat + b + 1]
                d_re.append(d_s[row, 2 * bst:3 * bst])
                d_im.append(d_s[row, 3 * bst:4 * bst])
            d_re = jnp.concatenate(d_re, axis=0)
            d_im = jnp.concatenate(d_im, axis=0)
            return coef_re * s_re - coef_im * s_im + d_re, coef_re * s_im + coef_im * s_re + d_im

        lax.fori_loop(0, nj, step, (s_re0, s_im0))
        outputs()
        for b in range(n_lat):
            for k in range(S5_T):
                y_ref[pl.ds(b * l_lat + k, nj, stride=S5_T), :] = yo_s[b * nj:(b + 1) * nj, k * LANES:(k + 1) * LANES]


def _s5_core(prm, h0, u, *, n_ctx, l_ctx, n_lat, l_lat, e):
    blocks = prm[0].shape[0]
    part_tokens = n_lat * l_lat
    t = u.shape[0]
    n_parts = t // part_tokens
    seqs = part_tokens // l_ctx
    r = part_tokens // S5_T
    vec = pl.BlockSpec((None, 2, 1, S5_BLOCK_ST), lambda g, p: (g, 0, 0, 0))
    mat = pl.BlockSpec((None, 2, S5_GROUP, S5_BLOCK_ST), lambda g, p: (g, 0, 0, 0))
    k_in = S5_T * LANES
    return pl.pallas_call(
        functools.partial(_s5_body, seqs=seqs, l_ctx=l_ctx, n_lat=n_lat, l_lat=l_lat, ctx_parts=n_parts - 1),
        grid=(blocks, n_parts),
        in_specs=[vec, vec, vec, mat, mat, mat, mat,
                  pl.BlockSpec((None, 2, 2, n_lat, S5_BLOCK_ST), lambda g, p: (g, 0, 0, 0, 0)),
                  pl.BlockSpec((part_tokens, LANES), lambda g, p: (p, g))],
        out_specs=[pl.BlockSpec((part_tokens, LANES), lambda g, p: (p, g)),
                   pl.BlockSpec((seqs, 2, 2, LANES // S5_GROUP, S5_STATE),
                                lambda g, p: (jnp.minimum(p, n_parts - 2), 0, 0, g, 0))],
        out_shape=[jax.ShapeDtypeStruct((t, e), F32),
                   jax.ShapeDtypeStruct((n_ctx, 2, 2, e // S5_GROUP, S5_STATE), F32)],
        scratch_shapes=[pltpu.VMEM((k_in, 4 * S5_BLOCK_ST), BF16),
                        pltpu.VMEM((k_in, k_in), BF16),
                        pltpu.VMEM((k_in, 2 * S5_BLOCK_ST), BF16),
                        pltpu.VMEM((k_in, 2 * S5_BLOCK_ST), BF16),
                        pltpu.VMEM(((2 * S5_T - 1) * LANES, LANES), F32),
                        pltpu.VMEM((8, S5_BLOCK_ST), F32),
                        pltpu.VMEM((r, k_in), BF16),
                        pltpu.VMEM((r, 4 * S5_BLOCK_ST), F32),
                        pltpu.VMEM((r, 2 * S5_BLOCK_ST), F32),
                        pltpu.VMEM((r, 2 * S5_BLOCK_ST), F32),
                        pltpu.VMEM((r, k_in), F32),
                        pltpu.VMEM((part_tokens, LANES), F32)],
        compiler_params=_params("parallel", "arbitrary"),
        name="s5_core",
    )(*prm, h0, u)


def _s5_glu_body(ys_ref, u_ref, dsk_ref, w_ref, b_ref, o_ref):
    y = _gelu(ys_ref[...] + dsk_ref[...] * u_ref[...].astype(F32))
    o_ref[...] = (y * jax.nn.sigmoid(_dot(y.astype(BF16), w_ref[...]) + b_ref[...])).astype(o_ref.dtype)


def _s5_glu(ys, u, dsk, w, b, *, tm=GLU_TM):
    t, e = ys.shape
    return pl.pallas_call(
        _s5_glu_body,
        grid=(t // tm,),
        in_specs=[pl.BlockSpec((tm, e), lambda i: (i, 0)),
                  pl.BlockSpec((tm, e), lambda i: (i, 0)),
                  pl.BlockSpec((1, e), lambda i: (0, 0)),
                  pl.BlockSpec((e, e), lambda i: (0, 0)),
                  pl.BlockSpec((1, e), lambda i: (0, 0))],
        out_specs=pl.BlockSpec((tm, e), lambda i: (i, 0)),
        out_shape=jax.ShapeDtypeStruct((t, e), ACT),
        compiler_params=_params("parallel"),
        name="s5_glu",
    )(ys, u, dsk, w, b)


def _ctx_attn_body(q_ref, k_ref, v_ref, o_ref, ko_ref, vo_ref):
    scale = HEAD_DIM ** -0.5
    seq_len = q_ref.shape[0]
    first = lax.broadcasted_iota(jnp.int32, (seq_len, LANES), 1) < HEAD_DIM
    masks = (first, jnp.logical_not(first))
    n_pairs = q_ref.shape[1] // LANES
    logits, values = [], []
    for pair in range(n_pairs):
        cols = slice(pair * LANES, (pair + 1) * LANES)
        q2, k2, v2 = q_ref[:, cols].astype(F32) * scale, k_ref[:, cols], v_ref[:, cols]
        kb, vb = k2.astype(BF16), v2.astype(BF16)
        for h in range(LANES // HEAD_DIM):
            ko_ref[2 * pair + h] = k2[:, h * HEAD_DIM:(h + 1) * HEAD_DIM].astype(F32)
            vo_ref[2 * pair + h] = v2[:, h * HEAD_DIM:(h + 1) * HEAD_DIM].astype(F32)
            logits.append(_dot_nt(jnp.where(masks[h], q2, 0.0).astype(BF16), kb))
            values.append(jnp.where(masks[h], vb, jnp.ones_like(vb)))
    probs = [jnp.exp(s - jnp.max(s, axis=-1, keepdims=True)).astype(BF16) for s in logits]
    for pair in range(n_pairs):
        o0, o1 = _dot(probs[2 * pair], values[2 * pair]), _dot(probs[2 * pair + 1], values[2 * pair + 1])
        num = jnp.where(first, o0, o1)
        den = jnp.where(first, pltpu.roll(o0, HEAD_DIM, 1), pltpu.roll(o1, HEAD_DIM, 1))
        o_ref[:, pair * LANES:(pair + 1) * LANES] = (num / den).astype(o_ref.dtype)


def _ctx_attention(u, *, n_seq, seq_len, e, width=CTX_ATTN_WIDTH):
    t = n_seq * seq_len
    hp = e // width
    heads = e // HEAD_DIM
    kv_block = (None, width // HEAD_DIM, seq_len, HEAD_DIM)
    in_specs = [pl.BlockSpec((seq_len, width), lambda b, h: (b, h)),
                pl.BlockSpec((seq_len, width), lambda b, h: (b, hp + h)),
                pl.BlockSpec((seq_len, width), lambda b, h: (b, 2 * hp + h))]
    return pl.pallas_call(
        _ctx_attn_body,
        grid=(n_seq, hp),
        in_specs=in_specs,
        out_specs=[pl.BlockSpec((seq_len, width), lambda b, h: (b, h)),
                   pl.BlockSpec(kv_block, lambda b, h: (b, h, 0, 0)),
                   pl.BlockSpec(kv_block, lambda b, h: (b, h, 0, 0))],
        out_shape=[jax.ShapeDtypeStruct((t, e), ACT),
                   jax.ShapeDtypeStruct((n_seq, heads, seq_len, HEAD_DIM), F32),
                   jax.ShapeDtypeStruct((n_seq, heads, seq_len, HEAD_DIM), F32)],
        compiler_params=_params("parallel", "parallel"),
        name="ctx_attention",
    )(u, u, u)


def _nat_bias_body(rp_ref, o_ref):
    n_rel_rows = rp_ref.shape[1]
    q = lax.broadcasted_iota(jnp.int32, (GRID_W, LANES), 0)
    lane = lax.broadcasted_iota(jnp.int32, (GRID_W, LANES), 1)
    kc = lane % GRID_W
    c_start = jnp.clip(q - WIN_COLS // 2, 0, GRID_W - WIN_COLS)
    ok = (kc >= c_start) & (kc < c_start + WIN_COLS)
    left = lane < GRID_W
    for h in range(o_ref.shape[0]):
        halves = []
        for i in range(n_rel_rows):
            row = jnp.broadcast_to(rp_ref[h, i:i + 1, :], (GRID_W, LANES))
            halves.append((pltpu.roll(row, 0, 1, stride=1, stride_axis=0),
                           pltpu.roll(row, GRID_W, 1, stride=1, stride_axis=0)))
        for i in range(n_rel_rows - 1):
            o_ref[h, i] = jnp.where(ok, jnp.where(left, halves[i][0], halves[i + 1][1]), MASKED)


def _nat_bias(rpb, *, heads_per_step=4):
    heads, n_rel_rows, n_rel_cols = rpb.shape
    rows = jnp.roll(jnp.pad(rpb, ((0, 0), (0, 0), (0, LANES - n_rel_cols))), -(WIN_COLS - 1), axis=-1)
    return pl.pallas_call(
        _nat_bias_body,
        grid=(heads // heads_per_step,),
        in_specs=[pl.BlockSpec((heads_per_step, n_rel_rows, LANES), lambda h: (h, 0, 0))],
        out_specs=pl.BlockSpec((heads_per_step, n_rel_rows - 1, GRID_W, 2 * GRID_W), lambda h: (h, 0, 0, 0)),
        out_shape=jax.ShapeDtypeStruct((heads, n_rel_rows - 1, GRID_W, 2 * GRID_W), F32),
        compiler_params=_params("parallel"),
        name="nat_bias",
    )(rows)


def _nat_body(q_ref, k_ref, v_ref, ck_ref, cv_ref, bias_ref, o_ref, ckb_s, cvb_s, *, rows):
    scale = HEAD_DIM ** -0.5
    wr = min(WIN_ROWS, rows)
    nw = wr * GRID_W
    heads = tuple(range(LANES // HEAD_DIM))
    first = lax.broadcasted_iota(jnp.int32, (GRID_W, LANES), 1) < HEAD_DIM
    masks = (first, jnp.logical_not(first))
    ckb_s[...] = jnp.concatenate([ck_ref[0], ck_ref[1]], axis=1).astype(BF16)
    cvb_s[...] = jnp.concatenate([cv_ref[0], cv_ref[1]], axis=1).astype(BF16)

    def row_group(g, carry):
        rws = [g * NAT_ROWS_PER_STEP + i for i in range(NAT_ROWS_PER_STEP)]
        starts = [jnp.clip(r - wr // 2, 0, rows - wr) for r in rws]
        q_rows = [pl.ds(pl.multiple_of(r * GRID_W, GRID_W), GRID_W) for r in rws]
        k_rows = [pl.ds(pl.multiple_of(rs * GRID_W, GRID_W), nw) for rs in starts]
        logits = []
        for r, rs, qr, kr in zip(rws, starts, q_rows, k_rows):
            q2 = q_ref[qr, :].astype(F32) * scale
            kb = k_ref[kr, :].astype(BF16)
            for h in heads:
                q = jnp.where(masks[h], q2, 0.0).astype(BF16)
                logits.append((_dot_nt(q, kb), _dot_nt(q, ckb_s[...])))
        probs = []
        for idx, (s_win, s_ctx) in enumerate(logits):
            r, rs, h = rws[idx // 2], starts[idx // 2], heads[idx % 2]
            i0 = (WIN_ROWS - 1) - (r - rs)
            s_win = s_win + jnp.concatenate([bias_ref[h, i0 + 2 * jj] for jj in range(wr // 2)], axis=1)
            m = jnp.maximum(jnp.max(s_win, axis=-1, keepdims=True), jnp.max(s_ctx, axis=-1, keepdims=True))
            p_win = jnp.exp(s_win - m)
            p_ctx = jnp.exp(s_ctx - m)
            inv = 1.0 / (jnp.sum(p_win, axis=-1, keepdims=True) + jnp.sum(p_ctx, axis=-1, keepdims=True))
            probs.append(((p_win * inv).astype(BF16), (p_ctx * inv).astype(BF16)))
        for i, (qr, kr) in enumerate(zip(q_rows, k_rows)):
            vb = v_ref[kr, :].astype(BF16)
            outs = [_dot(probs[2 * i + h][0], vb) + _dot(probs[2 * i + h][1], cvb_s[...]) for h in heads]
            o_ref[qr, :] = jnp.where(first, outs[0], outs[1]).astype(o_ref.dtype)
        return carry

    lax.fori_loop(0, rows // NAT_ROWS_PER_STEP, row_group, 0)


def _nat_attention(u, cache_k, cache_v, cache_layer, bias, *, n_seq, seq_len, row_block0, e):
    hp = e // LANES
    hpb = LANES // HEAD_DIM
    past = cache_k.shape[3]
    seq = lambda b: row_block0 + b
    cache_block = (None, None, hpb, past, HEAD_DIM)
    return pl.pallas_call(
        functools.partial(_nat_body, rows=seq_len // GRID_W),
        grid=(n_seq, hp),
        in_specs=[pl.BlockSpec((seq_len, LANES), lambda b, h: (seq(b), h)),
                  pl.BlockSpec((seq_len, LANES), lambda b, h: (seq(b), hp + h)),
                  pl.BlockSpec((seq_len, LANES), lambda b, h: (seq(b), 2 * hp + h)),
                  pl.BlockSpec(cache_block, lambda b, h: (b, cache_layer, h, 0, 0)),
                  pl.BlockSpec(cache_block, lambda b, h: (b, cache_layer, h, 0, 0)),
                  pl.BlockSpec((hpb,) + bias.shape[1:], lambda b, h: (h, 0, 0, 0))],
        out_specs=pl.BlockSpec((seq_len, LANES), lambda b, h: (b, h)),
        out_shape=jax.ShapeDtypeStruct((n_seq * seq_len, e), ACT),
        scratch_shapes=[pltpu.VMEM((past, LANES), BF16), pltpu.VMEM((past, LANES), BF16)],
        compiler_params=_params("parallel", "parallel"),
        name="nat_attention",
    )(u, u, u, cache_k, cache_v, bias)


def kernel(x_prompt, x_sample, state_ssd, state_s5, cache_k, cache_v, c, c_ctx, norm_g, w_mod, b_mod, w_out, final_g, ssd_w_in, ssd_conv_w, ssd_conv_b, ssd_dt_bias, ssd_a_log, ssd_d, ssd_norm_g, mlp_w_in, mlp_ln_g, mlp_ln_b, mlp_w_s, mlp_b_s, s5_w_in, s5_lam_re, s5_lam_im, s5_log_step, s5_b_re, s5_b_im, s5_c_re, s5_c_im, s5_d, s5_w_glu, s5_b_glu, nat_w_in, nat_rpb):
    n_ctx, l_ctx, d = x_prompt.shape
    n_lat, l_lat, _ = x_sample.shape
    t_ctx = n_ctx * l_ctx
    depth = norm_g.shape[0]
    e = w_out.shape[1]
    assert depth == 4 and l_lat % l_ctx == 0 and t_ctx % (n_lat * l_lat) == 0
    tiles = dict(tm=OUT_TM, t_ctx=t_ctx, l_lat=l_lat)
    tiles_in = dict(tm=IN_TM, tn=IN_TN, t_ctx=t_ctx, l_lat=l_lat)

    t_lat = n_lat * l_lat
    x = (x_prompt.reshape(t_ctx, d), x_sample.reshape(t_lat, d))
    cond = jnp.concatenate([c_ctx[None], c, jnp.zeros((MOD_ROWS - 1 - n_lat, d), F32)], axis=0)
    mods = _modulation(cond, w_mod, b_mod).reshape(depth, MOD_ROWS, 3, 1, d)
    w_out_b = w_out.astype(BF16)

    groups = e // SSD_GROUP_CH
    n_main = 3 * e
    w_dt = ssd_w_in[0][:, n_main:].reshape(d, 2, groups, SSD_GROUP_HEADS).transpose(0, 2, 1, 3)
    w_dt = jnp.pad(w_dt.reshape(d, groups, 2 * SSD_GROUP_HEADS), ((0, 0), (0, 0), (0, LANES - 2 * SSD_GROUP_HEADS)))
    u, dt = _inproj(x, mods, 0, norm_g[0:1], ssd_w_in[0].astype(BF16), n_out=n_main,
                    w_extra=w_dt.reshape(d, groups * LANES).astype(BF16), name="ssd_in", **dict(tiles_in, tm=IN_TM_PAIR))
    ssd_args = (ssd_conv_w[0], ssd_conv_b[0:1], _group_lanes(ssd_dt_bias[0], groups),
                _group_lanes(ssd_a_log[0], groups), jnp.repeat(ssd_d[0], SSD_HEAD_DIM)[None])
    y_ctx, new_ssd = _ssd_call(u, dt, *ssd_args, None, n_seq=n_ctx, seq_len=l_ctx, seqs_per_step=l_lat // l_ctx,
                               row_block0=0, want_final=True, name="ssd_ctx")
    y_lat, = _ssd_call(u, dt, *ssd_args, state_ssd[:, 0], n_seq=n_lat, seq_len=l_lat, seqs_per_step=n_lat,
                       row_block0=t_ctx // t_lat, want_final=False, name="ssd_lat")
    x = _outproj((y_ctx, y_lat), u, 0, x, mods, 0, w_out_b, norm_g=ssd_norm_g[0:1], name="ssd_out", **tiles)

    u = _inproj(x, mods, 1, norm_g[1:2], mlp_w_in[0].astype(BF16), name="mlp_in", **tiles_in)
    b_lanes = jnp.repeat(mlp_b_s[0].T, e // mlp_b_s.shape[1], axis=1)
    y = _gmlp(u, mlp_ln_g[0:1], mlp_ln_b[0:1], mlp_w_s[0], b_lanes)
    x = _outproj(y, u, 2, x, mods, 1, w_out_b, name="mlp_out", **tiles)

    u = _inproj(x, mods, 2, norm_g[2:3], s5_w_in[0].astype(BF16), name="s5_in", **tiles_in)
    s5_groups = e // S5_GROUP
    bg = LANES // S5_GROUP
    blocks = s5_groups // bg

    def block_vec(v):
        return v.reshape(2, blocks, 1, S5_BLOCK_ST).transpose(1, 0, 2, 3)

    def block_mat_b(v):
        v = v.reshape(2, blocks, bg, S5_STATE, S5_GROUP).transpose(1, 0, 4, 2, 3)
        return v.reshape(blocks, 2, S5_GROUP, S5_BLOCK_ST)

    def block_mat_c(v):
        v = v.reshape(2, blocks, bg, S5_GROUP, S5_STATE).transpose(1, 0, 3, 2, 4)
        return v.reshape(blocks, 2, S5_GROUP, S5_BLOCK_ST)

    log_step = jnp.repeat(s5_log_step[0][:, :, None], S5_STATE, axis=2)
    prm = (block_vec(s5_lam_re[0]), block_vec(s5_lam_im[0]), block_vec(log_step), block_mat_b(s5_b_re[0]),
           block_mat_b(s5_b_im[0]), block_mat_c(s5_c_re[0]), block_mat_c(s5_c_im[0]))
    h0 = state_s5[:, 0].reshape(n_lat, 2, 2, blocks, S5_BLOCK_ST).transpose(3, 1, 2, 0, 4)
    ys, fin = _s5_core(prm, h0, u, n_ctx=n_ctx, l_ctx=l_ctx, n_lat=n_lat, l_lat=l_lat, e=e)
    new_s5 = fin[:, None]
    y = _s5_glu(ys, u, s5_d[0:1], s5_w_glu[0].astype(BF16), s5_b_glu[0:1])
    x = _outproj(y, u, 1, x, mods, 2, w_out_b, name="s5_out", **tiles)

    u = _inproj(x, mods, 3, norm_g[3:4], nat_w_in[0].astype(BF16), name="nat_in", **tiles_in)
    y_ctx, new_k, new_v = _ctx_attention(u, n_seq=n_ctx, seq_len=l_ctx, e=e)
    y_lat = _nat_attention(u, cache_k, cache_v, 0, _nat_bias(nat_rpb[0]), n_seq=n_lat, seq_len=l_lat,
                           row_block0=t_ctx // l_lat, e=e)
    out_ctx, out_lat = _outproj((y_ctx, y_lat), u, 3, x, mods, 3, w_out_b, final_g=final_g[None], split_rows=(t_ctx, t_lat),
                                name="nat_out", **tiles)

    return (out_ctx.reshape(n_ctx, l_ctx, d), out_lat.reshape(n_lat, l_lat, d),
            new_ssd[:, None], new_s5, new_k[:, None], new_v[:, None])
```

```python
import collections
import functools

import jax
import jax.numpy as jnp
from jax import lax
from jax.experimental import pallas as pl
from jax.experimental.pallas import tpu as pltpu

F32 = jnp.float32
BF16 = jnp.bfloat16
ACT = jnp.bfloat16
EPS = 1e-6

LANES = 128
CHUNK = 128
SSD_HEAD_DIM = 64
SSD_GROUP_HEADS = 4
SSD_GROUP_CH = SSD_HEAD_DIM * SSD_GROUP_HEADS
SSD_STATE = 128
CONV_W = 5
CONV_HALO = 16
SSD_INTERLEAVE = 4
CONV_WIN = 2 * CHUNK
S5_T = 8
S5_GROUP = 16
S5_STATE = 64
S5_BLOCK_ST = (LANES // S5_GROUP) * S5_STATE
HEAD_DIM = 64
GRID_W = 64
WIN_ROWS = 8
WIN_COLS = 16
NAT_ROWS_PER_STEP = 8
MASKED = -1e30
MOD_ROWS = 8

VMEM_LIMIT = 52 * 1024 * 1024
IN_TM, IN_TN = 2048, 1024
IN_TM_PAIR = 1024
OUT_TM = 512
GMLP_TM = 256
GLU_TM = 512
CTX_ATTN_WIDTH = 1024


def _silu(x):
    return x * jax.nn.sigmoid(x)


def _gelu(x):
    return 0.5 * x * (1.0 + jnp.tanh(0.7978845608028654 * (x + 0.044715 * (x * x * x))))


def _softplus(x):
    return jnp.maximum(x, 0.0) + jnp.log1p(jnp.exp(-jnp.abs(x)))


def _dot(a, b):
    return jnp.dot(a, b, preferred_element_type=F32)


def _dot_nt(a, b):
    return lax.dot_general(a, b, (((1,), (1,)), ((), ())), preferred_element_type=F32)


def _cumsum_rows(lower_tri, x):
    hi = x.astype(BF16)
    r1 = x - hi.astype(F32)
    mid = r1.astype(BF16)
    lo = (r1 - mid.astype(F32)).astype(BF16)
    return _dot(lower_tri, hi) + _dot(lower_tri, mid) + _dot(lower_tri, lo)


def _params(*sem):
    return pltpu.CompilerParams(dimension_semantics=sem, vmem_limit_bytes=VMEM_LIMIT)


def _mod_row(i, *, tm, t_ctx, l_lat):
    start = i * tm
    return jnp.where(start < t_ctx, 0, 1 + (start - t_ctx) // l_lat)


def _mod_body(c_ref, w_ref, b_ref, o_ref):
    cond = _silu(c_ref[...]).astype(BF16)
    o_ref[...] = _dot(cond, w_ref[...].astype(BF16)) + b_ref[...]


def _modulation(cond, w_mod, b_mod, *, tn=IN_TN):
    depth, d, n = w_mod.shape
    rows = cond.shape[0]
    return pl.pallas_call(
        _mod_body,
        grid=(depth, n // tn),
        in_specs=[pl.BlockSpec((rows, d), lambda l, j: (0, 0)),
                  pl.BlockSpec((None, d, tn), lambda l, j: (l, 0, j)),
                  pl.BlockSpec((None, 1, tn), lambda l, j: (l, 0, j))],
        out_specs=pl.BlockSpec((None, rows, tn), lambda l, j: (l, 0, j)),
        out_shape=jax.ShapeDtypeStruct((depth, rows, n), F32),
        compiler_params=_params("parallel", "parallel"),
        name="modulation",
    )(cond, w_mod, b_mod.reshape(depth, 1, n))


def _stream_specs(x, tm, width, n_grid):
    ids = (lambda i, *_: i) if n_grid == 1 else (lambda i, j: i)
    if not isinstance(x, tuple):
        return [x], [pl.BlockSpec((tm, width), lambda *g: (ids(*g), 0))], None
    split = x[0].shape[0] // tm
    last = x[1].shape[0] // tm - 1
    return (list(x),
            [pl.BlockSpec((tm, width), lambda *g: (jnp.minimum(ids(*g), split - 1), 0)),
             pl.BlockSpec((tm, width), lambda *g: (jnp.clip(ids(*g) - split, 0, last), 0))],
            split)


def _inproj_body(*refs, split, extra_from):
    n_x = 1 if split is None else 2
    x_refs = refs[:n_x]
    if extra_from is None:
        shift_ref, scale_ref, g_ref, w_ref, o_ref, h_ref = refs[n_x:]
    else:
        shift_ref, scale_ref, g_ref, w_ref, w2_ref, o_ref, o2_ref, h_ref = refs[n_x:]

    def normalise(x_ref):
        x = x_ref[...]
        y = x * lax.rsqrt(jnp.mean(x * x, axis=-1, keepdims=True) + EPS) * g_ref[...]
        h_ref[...] = (y * (1.0 + scale_ref[...]) + shift_ref[...]).astype(BF16)

    first_col = pl.program_id(1) == 0
    if split is None:
        pl.when(first_col)(lambda: normalise(x_refs[0]))
    else:
        in_ctx = pl.program_id(0) < split
        pl.when(first_col & in_ctx)(lambda: normalise(x_refs[0]))
        pl.when(first_col & jnp.logical_not(in_ctx))(lambda: normalise(x_refs[1]))
    if extra_from is None:
        o_ref[...] = _dot(h_ref[...], w_ref[...]).astype(o_ref.dtype)
    else:
        main = pl.program_id(1) < extra_from

        @pl.when(main)
        def _():
            o_ref[...] = _dot(h_ref[...], w_ref[...]).astype(o_ref.dtype)

        @pl.when(jnp.logical_not(main))
        def _():
            o2_ref[...] = _dot(h_ref[...], w2_ref[...]).astype(o2_ref.dtype)


def _inproj(x, mods, layer, g, w, *, tm, tn, t_ctx, l_lat, name, out_dtype=ACT, n_out=None, w_extra=None):
    d = w.shape[0]
    n = w.shape[1] if n_out is None else n_out
    row = functools.partial(_mod_row, tm=tm, t_ctx=t_ctx, l_lat=l_lat)
    x_args, x_specs, split = _stream_specs(x, tm, d, 2)
    t = sum(a.shape[0] for a in x_args)
    nj = n // tn

    def mod_spec(part):
        return pl.BlockSpec((None, None, None, 1, d), lambda i, j: (layer, row(i), part, 0, 0))

    in_specs = x_specs + [mod_spec(0), mod_spec(1), pl.BlockSpec((1, d), lambda i, j: (0, 0))]
    if w_extra is None:
        in_specs.append(pl.BlockSpec((d, tn), lambda i, j: (0, j)))
        weights, steps = [w], nj
        out_specs = pl.BlockSpec((tm, tn), lambda i, j: (i, j))
        out_shape = jax.ShapeDtypeStruct((t, n), out_dtype)
    else:
        nj2 = w_extra.shape[1] // tn
        in_specs += [pl.BlockSpec((d, tn), lambda i, j: (0, jnp.minimum(j, nj - 1))),
                     pl.BlockSpec((d, tn), lambda i, j: (0, jnp.clip(j - nj, 0, nj2 - 1)))]
        weights, steps = [w, w_extra], nj + nj2
        out_specs = [pl.BlockSpec((tm, tn), lambda i, j: (i, jnp.minimum(j, nj - 1))),
                     pl.BlockSpec((tm, tn), lambda i, j: (i, jnp.clip(j - nj, 0, nj2 - 1)))]
        out_shape = [jax.ShapeDtypeStruct((t, n), out_dtype), jax.ShapeDtypeStruct((t, w_extra.shape[1]), F32)]
    return pl.pallas_call(
        functools.partial(_inproj_body, split=split, extra_from=None if w_extra is None else nj),
        grid=(t // tm, steps),
        in_specs=in_specs,
        out_specs=out_specs,
        out_shape=out_shape,
        scratch_shapes=[pltpu.VMEM((tm, d), BF16)],
        compiler_params=_params("parallel", "arbitrary"),
        name=name,
    )(*x_args, mods, mods, g, *weights)


def _outproj_body(*refs, n_y, n_x, split, gated_norm, final_norm):
    refs = list(refs)
    y_refs, z_ref, x_refs = refs[:n_y], refs[n_y], refs[n_y + 1:n_y + 1 + n_x]
    rest = refs[n_y + 1 + n_x:]
    gate_ref, w_ref = rest[:2]
    rest = rest[2:]
    ng_ref = rest.pop(0) if gated_norm else None
    fg_ref = rest.pop(0) if final_norm else None
    o_refs = rest

    def emit(y_ref, x_ref, o_ref):
        t = y_ref[...].astype(F32) * _silu(z_ref[...].astype(F32))
        if gated_norm:
            t = t * lax.rsqrt(jnp.mean(t * t, axis=-1, keepdims=True) + EPS) * ng_ref[...]
        xn = x_ref[...] + gate_ref[...] * _dot(t.astype(BF16), w_ref[...])
        if final_norm:
            xn = xn * lax.rsqrt(jnp.mean(xn * xn, axis=-1, keepdims=True) + EPS) * fg_ref[...]
        o_ref[...] = xn

    if split is None:
        emit(y_refs[0], x_refs[0], o_refs[0])
    else:
        in_ctx = pl.program_id(0) < split
        pl.when(in_ctx)(lambda: emit(y_refs[0], x_refs[0], o_refs[0]))
        pl.when(jnp.logical_not(in_ctx))(lambda: emit(y_refs[-1], x_refs[-1], o_refs[-1]))


def _outproj(y, u, z_block, x, mods, layer, w, *, norm_g=None, final_g=None, split_rows=None, tm, t_ctx, l_lat,
             name):
    t, e, d = u.shape[0], w.shape[1], w.shape[2]
    row = functools.partial(_mod_row, tm=tm, t_ctx=t_ctx, l_lat=l_lat)
    y_args, y_specs, split_y = _stream_specs(y, tm, e, 1)
    x_args, x_specs, split_in = _stream_specs(x, tm, d, 1)
    in_specs = (y_specs + [pl.BlockSpec((tm, e), lambda i: (i, z_block))] + x_specs
                + [pl.BlockSpec((None, None, None, 1, d), lambda i: (layer, row(i), 2, 0, 0)),
                   pl.BlockSpec((None, e, d), lambda i: (layer, 0, 0))])
    args = y_args + [u] + x_args + [mods, w]
    if norm_g is not None:
        in_specs.append(pl.BlockSpec((1, e), lambda i: (0, 0)))
        args.append(norm_g)
    if final_g is not None:
        in_specs.append(pl.BlockSpec((1, d), lambda i: (0, 0)))
        args.append(final_g)
    if split_rows is None:
        split_out = None
        out_specs = pl.BlockSpec((tm, d), lambda i: (i, 0))
        out_shape = jax.ShapeDtypeStruct((t, d), F32)
    else:
        split_out = split_rows[0] // tm
        last = split_rows[1] // tm - 1
        out_specs = [pl.BlockSpec((tm, d), lambda i: (jnp.minimum(i, split_out - 1), 0)),
                     pl.BlockSpec((tm, d), lambda i: (jnp.clip(i - split_out, 0, last), 0))]
        out_shape = [jax.ShapeDtypeStruct((rows, d), F32) for rows in split_rows]
    return pl.pallas_call(
        functools.partial(_outproj_body, n_y=len(y_args), n_x=len(x_args),
                          split=next((s for s in (split_y, split_in, split_out) if s is not None), None),
                          gated_norm=norm_g is not None, final_norm=final_g is not None),
        grid=(t // tm,),
        in_specs=in_specs,
        out_specs=out_specs,
        out_shape=out_shape,
        compiler_params=_params("arbitrary"),
        name=name,
    )(*args)


_SsdScratch = collections.namedtuple("_SsdScratch", "pad xbd xt bm bt cm e et ct dtt y sf sb")


def _ssd_scratch_shapes(seq_len):
    nc = seq_len // CHUNK
    width = SSD_GROUP_CH + 2 * SSD_STATE
    return [pltpu.VMEM((seq_len + CONV_WIN - CHUNK, width), ACT),
            pltpu.VMEM((nc, SSD_GROUP_HEADS * CHUNK, SSD_GROUP_CH), BF16),
            pltpu.VMEM((nc, SSD_GROUP_CH, CHUNK), F32),
            pltpu.VMEM((seq_len, SSD_STATE), BF16),
            pltpu.VMEM((nc, SSD_STATE, CHUNK), BF16),
            pltpu.VMEM((seq_len, SSD_STATE), BF16),
            pltpu.VMEM((seq_len, LANES), F32),
            pltpu.VMEM((nc, 2 * SSD_GROUP_HEADS, CHUNK), F32),
            pltpu.VMEM((nc, 2 * SSD_GROUP_HEADS, CHUNK), F32),
            pltpu.VMEM((nc, 2 * SSD_GROUP_HEADS, CHUNK), F32),
            pltpu.VMEM((seq_len, SSD_GROUP_CH), F32),
            pltpu.VMEM((SSD_GROUP_CH, SSD_STATE), F32),
            pltpu.VMEM((SSD_GROUP_CH, SSD_STATE), F32)]


def _ssd_body(*refs, seq_len, seqs_per_step, has_h0, want_final):
    refs = list(refs)
    x_ref, b_ref, c_ref, dt_ref, wx_ref, wb_ref, wc_ref, bx_ref, bb_ref, bc_ref, dtb_ref, alog_ref, dsk_ref = refs[:13]
    rest = refs[13:]
    h0_ref = rest.pop(0) if has_h0 else None
    y_ref = rest.pop(0)
    hf_ref = rest.pop(0) if want_final else None
    shift_s = rest.pop(0)
    il = min(SSD_INTERLEAVE, seqs_per_step)
    per_seq = len(rest) // il
    sc = [_SsdScratch(*rest[q * per_seq:(q + 1) * per_seq]) for q in range(il)]
    lanes_q = range(il)
    nc = seq_len // CHUNK
    width = SSD_GROUP_CH + 2 * SSD_STATE
    half = width // 2
    nh = SSD_GROUP_HEADS
    mid = CONV_W // 2

    tail = CONV_WIN - CHUNK - CONV_HALO
    for q in lanes_q:
        sc[q].pad[0:CONV_HALO, :] = jnp.zeros((CONV_HALO, width), ACT)
        sc[q].pad[CONV_HALO + seq_len:CONV_HALO + seq_len + tail, :] = jnp.zeros((tail, width), ACT)
    win_row = lax.broadcasted_iota(jnp.int32, (CHUNK, CONV_WIN), 1)
    tok_row = lax.broadcasted_iota(jnp.int32, (CHUNK, CONV_WIN), 0)
    taps = [j for j in range(CONV_W) if j != mid]
    for n, j in enumerate(taps):
        shift_s[n] = (win_row == tok_row + (CONV_HALO + j - mid)).astype(BF16)

    conv_w = jnp.concatenate([wx_ref[...], wb_ref[...], wc_ref[...]], axis=1)
    conv_b = jnp.concatenate([bx_ref[...], bb_ref[...], bc_ref[...]], axis=1)
    a_row = -jnp.exp(alog_ref[...])
    row_i = lax.broadcasted_iota(jnp.int32, (CHUNK, CHUNK), 0)
    col_i = lax.broadcasted_iota(jnp.int32, (CHUNK, CHUNK), 1)
    lower = row_i >= col_i
    upper = row_i <= col_i
    lower_b = lower.astype(BF16)
    fwd_lane = lax.broadcasted_iota(jnp.int32, (1, LANES), 1) < nh
    head_of_lane = lax.broadcasted_iota(jnp.int32, (CHUNK, SSD_GROUP_CH), 1) // SSD_HEAD_DIM

    def chunk_rows(ci):
        return pl.ds(pl.multiple_of(ci * CHUNK, CHUNK), CHUNK)

    def copy_chunk(offs, ci, carry):
        base = pl.multiple_of(ci * CHUNK, CHUNK)
        dst = pl.ds(base + CONV_HALO, CHUNK)
        for q in lanes_q:
            src = pl.ds(offs[q] + base, CHUNK)
            sc[q].pad[dst, 0:SSD_GROUP_CH] = x_ref[src, :]
            sc[q].pad[dst, SSD_GROUP_CH:SSD_GROUP_CH + SSD_STATE] = b_ref[src, :]
            sc[q].pad[dst, SSD_GROUP_CH + SSD_STATE:width] = c_ref[src, :]
        return carry

    def prep_chunk(offs, ci, carry):
        r = chunk_rows(ci)
        units = [(q, h) for h in range(2) for q in lanes_q]

        def window(unit):
            q, h = unit
            return sc[q].pad[pl.ds(pl.multiple_of(ci * CHUNK, CHUNK), CONV_WIN), h * half:(h + 1) * half]

        def shift_dots(unit):
            win = window(unit)
            return [_dot(shift_s[t], win) for t in range(len(taps))]

        def finish(unit, shifted):
            q, h = unit
            cols = slice(h * half, (h + 1) * half)
            acc = conv_b[:, cols] + window(unit)[CONV_HALO:CONV_HALO + CHUNK].astype(F32) * conv_w[mid:mid + 1, cols]
            for t, j in enumerate(taps):
                acc = acc + shifted[t] * conv_w[j:j + 1, cols]
            v = _silu(acc)
            if h == 0:
                sc[q].y[r, :] = v * dsk_ref[...]
                vb = v.astype(BF16)
                for hd in range(nh):
                    sc[q].xbd[ci, hd * CHUNK:(hd + 1) * CHUNK, :] = jnp.where(head_of_lane == hd, vb,
                                                                              jnp.zeros_like(vb))
                sc[q].xt[ci] = v.T
            else:
                bm = v[:, 0:SSD_STATE]
                sc[q].bm[r, :] = bm.astype(BF16)
                sc[q].bt[ci] = bm.T.astype(BF16)
                sc[q].cm[r, :] = v[:, SSD_STATE:half].astype(BF16)

        dts = [_softplus(dt_ref[pl.ds(offs[q] + pl.multiple_of(ci * CHUNK, CHUNK), CHUNK), :] + dtb_ref[...])
               for q in lanes_q]
        das = [dt * a_row for dt in dts]
        pending = shift_dots(units[0])
        cums = []
        for i, unit in enumerate(units):
            following = shift_dots(units[i + 1]) if i + 1 < len(units) else None
            if unit[1] == 0:
                cums.append(_cumsum_rows(lower_b, das[unit[0]]))
            finish(unit, pending)
            pending = following
        for q in lanes_q:
            e = jnp.where(fwd_lane, cums[q], cums[q] - das[q])
            sc[q].e[r, :] = e
            sc[q].et[ci] = e.T[0:2 * nh]
            sc[q].ct[ci] = cums[q].T[0:2 * nh]
            sc[q].dtt[ci] = dts[q].T[0:2 * nh]
        return carry

    def scan_chunk(i, carry):
        chunks = (i, nc - 1 - i)
        rows = [chunk_rows(ci) for ci in chunks]
        streams = [(q, d) for q in lanes_q for d in range(2)]
        st = {(q, d): (sc[q].sf if d == 0 else sc[q].sb) for q, d in streams}
        s_prev = {k: st[k][...] for k in streams}
        g, cs = {}, {}
        for q, d in streams:
            cm = sc[q].cm[rows[d], :]
            g[q, d] = _dot(cm, sc[q].bt[chunks[d]])
            cs[q, d] = _dot_nt(cm, s_prev[q, d].astype(BF16))
        w_rows, keeps, ys = {}, {}, {}
        for q, d in streams:
            e = sc[q].e[rows[d], :]
            e_t, c_t, dt_t = sc[q].et[chunks[d]], sc[q].ct[chunks[d]], sc[q].dtt[chunks[d]]
            w_d, keep_d, mixes, offs = [], [], [], []
            for h in range(nh):
                col = h + nh * d
                ec = jnp.broadcast_to(e[:, col:col + 1], (CHUNK, CHUNK))
                er = e_t[col:col + 1, :]
                dt_h = dt_t[col:col + 1, :]
                log_dt = jnp.log(dt_h)
                tot = c_t[col:col + 1, CHUNK - 1:CHUNK]
                if d == 0:
                    decay_dt = jnp.exp(jnp.where(lower, ec - (er - log_dt), -jnp.inf))
                    offs.append(jnp.exp(ec))
                    w_d.append(jnp.broadcast_to(dt_h * jnp.exp(tot - er), (SSD_HEAD_DIM, CHUNK)))
                else:
                    decay_dt = jnp.exp(jnp.where(upper, (er + log_dt) - ec, -jnp.inf))
                    offs.append(jnp.exp(tot - ec))
                    w_d.append(jnp.broadcast_to(dt_h * jnp.exp(er), (SSD_HEAD_DIM, CHUNK)))
                keep_d.append(jnp.broadcast_to(jnp.exp(tot), (SSD_HEAD_DIM, SSD_STATE)))
                mixes.append((g[q, d] * decay_dt).astype(BF16))
            y_diag = _dot(jnp.concatenate(mixes, axis=1), sc[q].xbd[chunks[d]])
            off = jnp.concatenate([offs[nh - 1]] * (SSD_GROUP_CH // CHUNK), axis=1)
            for h in range(nh - 2, -1, -1):
                off = jnp.where(head_of_lane == h, jnp.concatenate([offs[h]] * (SSD_GROUP_CH // CHUNK), axis=1), off)
            ys[q, d] = y_diag + cs[q, d] * off
            w_rows[q, d] = jnp.concatenate(w_d, axis=0)
            keeps[q, d] = jnp.concatenate(keep_d, axis=0)
        for q, d in streams:
            w_t = (sc[q].xt[chunks[d]] * w_rows[q, d]).astype(BF16)
            st[q, d][...] = keeps[q, d] * s_prev[q, d] + _dot(w_t, sc[q].bm[rows[d], :])
            sc[q].y[rows[d], :] += ys[q, d]
        return carry

    def emit_chunk(offs, ci, carry):
        base = pl.multiple_of(ci * CHUNK, CHUNK)
        for q in lanes_q:
            y_ref[pl.ds(offs[q] + base, CHUNK), :] = sc[q].y[pl.ds(base, CHUNK), :].astype(y_ref.dtype)
        return carry

    def one_group(s, carry):
        seqs = [s * il + q for q in lanes_q]
        offs = [pl.multiple_of(sq * seq_len, CHUNK) for sq in seqs]
        lax.fori_loop(0, nc, functools.partial(copy_chunk, offs), 0)
        lax.fori_loop(0, nc, functools.partial(prep_chunk, offs), 0)
        for q in lanes_q:
            if has_h0:
                sc[q].sf[...] = h0_ref[seqs[q], 0].reshape(SSD_GROUP_CH, SSD_STATE)
                sc[q].sb[...] = h0_ref[seqs[q], 1].reshape(SSD_GROUP_CH, SSD_STATE)
            else:
                sc[q].sf[...] = jnp.zeros((SSD_GROUP_CH, SSD_STATE), F32)
                sc[q].sb[...] = jnp.zeros((SSD_GROUP_CH, SSD_STATE), F32)
        lax.fori_loop(0, nc, scan_chunk, 0)
        lax.fori_loop(0, nc, functools.partial(emit_chunk, offs), 0)
        if want_final:
            for q in lanes_q:
                hf_ref[seqs[q], 0] = sc[q].sf[...].reshape(SSD_GROUP_HEADS, SSD_HEAD_DIM, SSD_STATE)
                hf_ref[seqs[q], 1] = sc[q].sb[...].reshape(SSD_GROUP_HEADS, SSD_HEAD_DIM, SSD_STATE)
        return carry

    lax.fori_loop(0, seqs_per_step // il, one_group, 0)


def _ssd_call(u, dt, conv_w, conv_b, dtb, alog, dsk, h0, *, n_seq, seq_len, seqs_per_step, row_block0,
              want_final, name):
    t, n_u = u.shape
    e = dsk.shape[1]
    groups = e // SSD_GROUP_CH
    xb0 = e // SSD_GROUP_CH
    bb0 = 2 * e // SSD_STATE
    cb0 = bb0 + groups
    has_h0 = h0 is not None
    seq = lambda b: row_block0 + b
    rows = seqs_per_step * seq_len
    in_specs = [pl.BlockSpec((rows, SSD_GROUP_CH), lambda b, g: (seq(b), xb0 + g)),
                pl.BlockSpec((rows, SSD_STATE), lambda b, g: (seq(b), bb0 + g)),
                pl.BlockSpec((rows, SSD_STATE), lambda b, g: (seq(b), cb0 + g)),
                pl.BlockSpec((rows, LANES), lambda b, g: (seq(b), g)),
                pl.BlockSpec((CONV_W, SSD_GROUP_CH), lambda b, g: (0, g)),
                pl.BlockSpec((CONV_W, SSD_STATE), lambda b, g: (0, bb0 - xb0 * 2 + g)),
                pl.BlockSpec((CONV_W, SSD_STATE), lambda b, g: (0, cb0 - xb0 * 2 + g)),
                pl.BlockSpec((1, SSD_GROUP_CH), lambda b, g: (0, g)),
                pl.BlockSpec((1, SSD_STATE), lambda b, g: (0, bb0 - xb0 * 2 + g)),
                pl.BlockSpec((1, SSD_STATE), lambda b, g: (0, cb0 - xb0 * 2 + g)),
                pl.BlockSpec((1, LANES), lambda b, g: (0, g)),
                pl.BlockSpec((1, LANES), lambda b, g: (0, g)),
                pl.BlockSpec((1, SSD_GROUP_CH), lambda b, g: (0, g))]
    args = [u, u, u, dt, conv_w, conv_w, conv_w, conv_b, conv_b, conv_b, dtb, alog, dsk]
    state_block = (seqs_per_step, 2, SSD_GROUP_HEADS, SSD_HEAD_DIM, SSD_STATE)
    if has_h0:
        in_specs.append(pl.BlockSpec(state_block, lambda b, g: (b, 0, g, 0, 0)))
        args.append(h0)
    out_specs = [pl.BlockSpec((rows, SSD_GROUP_CH), lambda b, g: (b, g))]
    out_shape = [jax.ShapeDtypeStruct((n_seq * seq_len, e), ACT)]
    if want_final:
        out_specs.append(pl.BlockSpec(state_block, lambda b, g: (b, 0, g, 0, 0)))
        out_shape.append(jax.ShapeDtypeStruct((n_seq, 2, e // SSD_HEAD_DIM, SSD_HEAD_DIM, SSD_STATE), F32))
    return pl.pallas_call(
        functools.partial(_ssd_body, seq_len=seq_len, seqs_per_step=seqs_per_step, has_h0=has_h0,
                          want_final=want_final),
        grid=(n_seq // seqs_per_step, groups),
        in_specs=in_specs,
        out_specs=out_specs,
        out_shape=out_shape,
        scratch_shapes=([pltpu.VMEM((CONV_W - 1, CHUNK, CONV_WIN), BF16)]
                        + _ssd_scratch_shapes(seq_len) * min(SSD_INTERLEAVE, seqs_per_step)),
        compiler_params=_params("parallel", "parallel"),
        name=name,
    )(*args)


def _group_lanes(v, groups):
    per_group = v.reshape(2, groups, SSD_GROUP_HEADS).transpose(1, 0, 2).reshape(groups, 2 * SSD_GROUP_HEADS)
    return jnp.pad(per_group, ((0, 0), (0, LANES - 2 * SSD_GROUP_HEADS))).reshape(1, groups * LANES)


def _gmlp_body(u_ref, v_ref, lng_ref, lnb_ref, ws_ref, bs_ref, o_ref, *, tm):
    groups = ws_ref.shape[0]
    gch = u_ref.shape[1] // groups
    for c in range(tm // CHUNK):
        rows = slice(c * CHUNK, (c + 1) * CHUNK)
        v = _gelu(v_ref[rows, :].astype(F32))
        vc = v - jnp.mean(v, axis=-1, keepdims=True)
        vn = vc * lax.rsqrt(jnp.mean(vc * vc, axis=-1, keepdims=True) + EPS) * lng_ref[...] + lnb_ref[...]
        vb = vn.astype(BF16)
        for g in range(groups):
            cols = slice(g * gch, (g + 1) * gch)
            s = _dot(ws_ref[g].astype(BF16), vb[:, cols]) + bs_ref[:, cols]
            o_ref[rows, cols] = (_gelu(u_ref[rows, cols].astype(F32)) * s).astype(o_ref.dtype)


def _gmlp(u, ln_g, ln_b, w_s, b_lanes, *, tm=GMLP_TM):
    t = u.shape[0]
    e = ln_g.shape[1]
    return pl.pallas_call(
        functools.partial(_gmlp_body, tm=tm),
        grid=(t // tm,),
        in_specs=[pl.BlockSpec((tm, e), lambda i: (i, 0)),
                  pl.BlockSpec((tm, e), lambda i: (i, 1)),
                  pl.BlockSpec((1, e), lambda i: (0, 0)),
                  pl.BlockSpec((1, e), lambda i: (0, 0)),
                  pl.BlockSpec(w_s.shape, lambda i: (0, 0, 0)),
                  pl.BlockSpec((CHUNK, e), lambda i: (0, 0))],
        out_specs=pl.BlockSpec((tm, e), lambda i: (i, 0)),
        out_shape=jax.ShapeDtypeStruct((t, e), ACT),
        compiler_params=_params("parallel"),
        name="gmlp",
    )(u, u, ln_g, ln_b, w_s, b_lanes)


def _s5_body(lr_ref, li_ref, ls_ref, brt_ref, bit_ref, cre_ref, cim_ref, h0_ref, u_ref, y_ref, fin_ref,
             win_s, t_s, ef_s, eb_s, z_s, a_s, zin_s, d_s, spf_s, spb_s, yo_s, uf_s,
             *, seqs, l_ctx, n_lat, l_lat, ctx_parts):
    bst = S5_BLOCK_ST
    ng = LANES // S5_GROUP
    kw = S5_T * LANES
    part = pl.program_id(1)

    @pl.when(part == 0)
    def _build():
        own = (lax.broadcasted_iota(jnp.int32, (ng, S5_GROUP, bst), 2) // S5_STATE
               == lax.broadcasted_iota(jnp.int32, (ng, S5_GROUP, bst), 0))

        def spread(v):
            return jnp.where(own, v[None], 0.0).reshape(LANES, bst).astype(BF16)

        tau = lax.broadcasted_iota(jnp.int32, (S5_T + 8, 1), 0).astype(F32)
        e_refs = (ef_s, eb_s)
        for d in range(2):
            lr, li = lr_ref[d], li_ref[d]
            step = jnp.exp(ls_ref[d])
            mag = jnp.exp(tau * (lr * step))
            p_re = mag * jnp.cos(tau * (li * step))
            p_im = mag * jnp.sin(tau * (li * step))
            ab_re, ab_im = p_re[1:2], p_im[1:2]
            den = lr * lr + li * li
            nr = ab_re - 1.0
            cr = (nr * lr + ab_im * li) / den
            ci = (ab_im * lr - nr * li) / den
            brt, bit = brt_ref[d], bit_ref[d]
            bb_re = cr * brt - ci * bit
            bb_im = cr * bit + ci * brt
            cre, cim = cre_ref[d], cim_ref[d]
            for k in range(S5_T):
                rows = slice(k * LANES, (k + 1) * LANES)
                tq = S5_T - 1 - k if d == 0 else k
                te = k + 1 if d == 0 else S5_T - k
                pr, pi = p_re[tq:tq + 1], p_im[tq:tq + 1]
                win_s[rows, 2 * d * bst:(2 * d + 1) * bst] = spread(pr * bb_re - pi * bb_im)
                win_s[rows, (2 * d + 1) * bst:(2 * d + 2) * bst] = spread(pr * bb_im + pi * bb_re)
                pr, pi = p_re[te:te + 1], p_im[te:te + 1]
                e_refs[d][rows, 0:bst] = spread(cre * pr - cim * pi)
                e_refs[d][rows, bst:2 * bst] = spread(-(cre * pi + cim * pr))
            c_own = jnp.concatenate([spread(cre), spread(-cim)], axis=1)
            z = _dot_nt(win_s[:, 2 * d * bst:(2 * d + 2) * bst], c_own)
            if d == 0:
                z_s[0:kw, :] = z
            else:
                z_s[kw - LANES:kw, :] += z[0:LANES]
                z_s[kw:2 * kw - LANES, :] = z[LANES:kw]
            a_s[2 * d:2 * d + 1, :] = p_re[S5_T:S5_T + 1]
            a_s[2 * d + 1:2 * d + 2, :] = p_im[S5_T:S5_T + 1]
        for k in range(S5_T):
            r0 = (S5_T - 1 - k) * LANES
            t_s[:, k * LANES:(k + 1) * LANES] = z_s[r0:r0 + kw, :].astype(BF16)

    def outputs():
        yo_s[...] = (_dot(zin_s[...], t_s[...]) + _dot_nt(spf_s[...].astype(BF16), ef_s[...])
                     + _dot_nt(spb_s[...].astype(BF16), eb_s[...]))

    @pl.when(part < ctx_parts)
    def _context():
        nj = l_ctx // S5_T
        uf_s[...] = u_ref[...].astype(F32)
        for j in range(nj):
            for k in range(S5_T):
                zin_s[j * seqs:(j + 1) * seqs, k * LANES:(k + 1) * LANES] = (
                    uf_s[pl.ds(j * S5_T + k, seqs, stride=l_ctx), :].astype(BF16))
        d_s[...] = _dot(zin_s[...], win_s[...])
        yo_s[...] = _dot(zin_s[...], t_s[...])
        coef = [jnp.broadcast_to(a_s[i:i + 1, :], (seqs, bst)) for i in range(4)]
        fr = fi = br = bi = jnp.zeros((seqs, bst), F32)
        for j in range(nj):
            rf = slice(j * seqs, (j + 1) * seqs)
            rb = slice((nj - 1 - j) * seqs, (nj - j) * seqs)
            spf_s[rf, 0:bst] = fr
            spf_s[rf, bst:2 * bst] = fi
            spb_s[rb, 0:bst] = br
            spb_s[rb, bst:2 * bst] = bi
            fr, fi, br, bi = (coef[0] * fr - coef[1] * fi + d_s[rf, 0:bst],
                              coef[0] * fi + coef[1] * fr + d_s[rf, bst:2 * bst],
                              coef[2] * br - coef[3] * bi + d_s[rb, 2 * bst:3 * bst],
                              coef[2] * bi + coef[3] * br + d_s[rb, 3 * bst:4 * bst])
        for d, ri, val in ((0, 0, fr), (0, 1, fi), (1, 0, br), (1, 1, bi)):
            for gg in range(ng):
                fin_ref[:, d, ri, gg, :] = val[:, gg * S5_STATE:(gg + 1) * S5_STATE]
        yo_s[...] += (_dot_nt(spf_s[...].astype(BF16), ef_s[...]) + _dot_nt(spb_s[...].astype(BF16), eb_s[...]))
        for j in range(nj):
            for k in range(S5_T):
                y_ref[pl.ds(j * S5_T + k, seqs, stride=l_ctx), :] = (
                    yo_s[j * seqs:(j + 1) * seqs, k * LANES:(k + 1) * LANES])

    @pl.when(part == ctx_parts)
    def _latent():
        nj = l_lat // S5_T
        uf_s[...] = u_ref[...].astype(F32)
        for b in range(n_lat):
            for k in range(S5_T):
                zin_s[b * nj:(b + 1) * nj, k * LANES:(k + 1) * LANES] = (
                    uf_s[pl.ds(b * l_lat + k, nj, stride=S5_T), :].astype(BF16))
        d_s[...] = _dot(zin_s[...], win_s[...])
        coef_re = jnp.concatenate([jnp.broadcast_to(a_s[0:1, :], (n_lat, bst)),
                                   jnp.broadcast_to(a_s[2:3, :], (n_lat, bst))], axis=0)
        coef_im = jnp.concatenate([jnp.broadcast_to(a_s[1:2, :], (n_lat, bst)),
                                   jnp.broadcast_to(a_s[3:4, :], (n_lat, bst))], axis=0)
        s_re0 = jnp.concatenate([h0_ref[0, 0], h0_ref[1, 0]], axis=0)
        s_im0 = jnp.concatenate([h0_ref[0, 1], h0_ref[1, 1]], axis=0)

        def step(j, carry):
            s_re, s_im = carry
            d_re, d_im = [], []
            for b in range(n_lat):
                row = pl.ds(b * nj + j, 1)
                spf_s[row, 0:bst] = s_re[b:b + 1]
                spf_s[row, bst:2 * bst] = s_im[b:b + 1]
                d_re.append(d_s[row, 0:bst])
                d_im.append(d_s[row, bst:2 * bst])
            for b in range(n_lat):
                row = pl.ds(b * nj + nj - 1 - j, 1)
                spb_s[row, 0:bst] = s_re[n_lat + b:n_lat + b + 1]
                spb_s[row, bst:2 * bst] = s_im[n_lat + b:n_lat + b + 1]
                d_re.append(d_s[row, 2 * bst:3 * bst])
                d_im.append(d_s[row, 3 * bst:4 * bst])
            d_re = jnp.concatenate(d_re, axis=0)
            d_im = jnp.concatenate(d_im, axis=0)
            return coef_re * s_re - coef_im * s_im + d_re, coef_re * s_im + coef_im * s_re + d_im

        lax.fori_loop(0, nj, step, (s_re0, s_im0))
        outputs()
        for b in range(n_lat):
            for k in range(S5_T):
                y_ref[pl.ds(b * l_lat + k, nj, stride=S5_T), :] = yo_s[b * nj:(b + 1) * nj, k * LANES:(k + 1) * LANES]


def _s5_core(prm, h0, u, *, n_ctx, l_ctx, n_lat, l_lat, e):
    blocks = prm[0].shape[0]
    part_tokens = n_lat * l_lat
    t = u.shape[0]
    n_parts = t // part_tokens
    seqs = part_tokens // l_ctx
    r = part_tokens // S5_T
    vec = pl.BlockSpec((None, 2, 1, S5_BLOCK_ST), lambda g, p: (g, 0, 0, 0))
    mat = pl.BlockSpec((None, 2, S5_GROUP, S5_BLOCK_ST), lambda g, p: (g, 0, 0, 0))
    k_in = S5_T * LANES
    return pl.pallas_call(
        functools.partial(_s5_body, seqs=seqs, l_ctx=l_ctx, n_lat=n_lat, l_lat=l_lat, ctx_parts=n_parts - 1),
        grid=(blocks, n_parts),
        in_specs=[vec, vec, vec, mat, mat, mat, mat,
                  pl.BlockSpec((None, 2, 2, n_lat, S5_BLOCK_ST), lambda g, p: (g, 0, 0, 0, 0)),
                  pl.BlockSpec((part_tokens, LANES), lambda g, p: (p, g))],
        out_specs=[pl.BlockSpec((part_tokens, LANES), lambda g, p: (p, g)),
                   pl.BlockSpec((seqs, 2, 2, LANES // S5_GROUP, S5_STATE),
                                lambda g, p: (jnp.minimum(p, n_parts - 2), 0, 0, g, 0))],
        out_shape=[jax.ShapeDtypeStruct((t, e), F32),
                   jax.ShapeDtypeStruct((n_ctx, 2, 2, e // S5_GROUP, S5_STATE), F32)],
        scratch_shapes=[pltpu.VMEM((k_in, 4 * S5_BLOCK_ST), BF16),
                        pltpu.VMEM((k_in, k_in), BF16),
                        pltpu.VMEM((k_in, 2 * S5_BLOCK_ST), BF16),
                        pltpu.VMEM((k_in, 2 * S5_BLOCK_ST), BF16),
                        pltpu.VMEM(((2 * S5_T - 1) * LANES, LANES), F32),
                        pltpu.VMEM((8, S5_BLOCK_ST), F32),
                        pltpu.VMEM((r, k_in), BF16),
                        pltpu.VMEM((r, 4 * S5_BLOCK_ST), F32),
                        pltpu.VMEM((r, 2 * S5_BLOCK_ST), F32),
                        pltpu.VMEM((r, 2 * S5_BLOCK_ST), F32),
                        pltpu.VMEM((r, k_in), F32),
                        pltpu.VMEM((part_tokens, LANES), F32)],
        compiler_params=_params("parallel", "arbitrary"),
        name="s5_core",
    )(*prm, h0, u)


def _s5_glu_body(ys_ref, u_ref, dsk_ref, w_ref, b_ref, o_ref):
    y = _gelu(ys_ref[...] + dsk_ref[...] * u_ref[...].astype(F32))
    o_ref[...] = (y * jax.nn.sigmoid(_dot(y.astype(BF16), w_ref[...]) + b_ref[...])).astype(o_ref.dtype)


def _s5_glu(ys, u, dsk, w, b, *, tm=GLU_TM):
    t, e = ys.shape
    return pl.pallas_call(
        _s5_glu_body,
        grid=(t // tm,),
        in_specs=[pl.BlockSpec((tm, e), lambda i: (i, 0)),
                  pl.BlockSpec((tm, e), lambda i: (i, 0)),
                  pl.BlockSpec((1, e), lambda i: (0, 0)),
                  pl.BlockSpec((e, e), lambda i: (0, 0)),
                  pl.BlockSpec((1, e), lambda i: (0, 0))],
        out_specs=pl.BlockSpec((tm, e), lambda i: (i, 0)),
        out_shape=jax.ShapeDtypeStruct((t, e), ACT),
        compiler_params=_params("parallel"),
        name="s5_glu",
    )(ys, u, dsk, w, b)


def _ctx_attn_body(q_ref, k_ref, v_ref, o_ref, ko_ref, vo_ref):
    scale = HEAD_DIM ** -0.5
    seq_len = q_ref.shape[0]
    first = lax.broadcasted_iota(jnp.int32, (seq_len, LANES), 1) < HEAD_DIM
    masks = (first, jnp.logical_not(first))
    n_pairs = q_ref.shape[1] // LANES
    logits, values = [], []
    for pair in range(n_pairs):
        cols = slice(pair * LANES, (pair + 1) * LANES)
        q2, k2, v2 = q_ref[:, cols].astype(F32) * scale, k_ref[:, cols], v_ref[:, cols]
        kb, vb = k2.astype(BF16), v2.astype(BF16)
        for h in range(LANES // HEAD_DIM):
            ko_ref[2 * pair + h] = k2[:, h * HEAD_DIM:(h + 1) * HEAD_DIM].astype(F32)
            vo_ref[2 * pair + h] = v2[:, h * HEAD_DIM:(h + 1) * HEAD_DIM].astype(F32)
            logits.append(_dot_nt(jnp.where(masks[h], q2, 0.0).astype(BF16), kb))
            values.append(jnp.where(masks[h], vb, jnp.ones_like(vb)))
    probs = [jnp.exp(s - jnp.max(s, axis=-1, keepdims=True)).astype(BF16) for s in logits]
    for pair in range(n_pairs):
        o0, o1 = _dot(probs[2 * pair], values[2 * pair]), _dot(probs[2 * pair + 1], values[2 * pair + 1])
        num = jnp.where(first, o0, o1)
        den = jnp.where(first, pltpu.roll(o0, HEAD_DIM, 1), pltpu.roll(o1, HEAD_DIM, 1))
        o_ref[:, pair * LANES:(pair + 1) * LANES] = (num / den).astype(o_ref.dtype)


def _ctx_attention(u, *, n_seq, seq_len, e, width=CTX_ATTN_WIDTH):
    t = n_seq * seq_len
    hp = e // width
    heads = e // HEAD_DIM
    kv_block = (None, width // HEAD_DIM, seq_len, HEAD_DIM)
    in_specs = [pl.BlockSpec((seq_len, width), lambda b, h: (b, h)),
                pl.BlockSpec((seq_len, width), lambda b, h: (b, hp + h)),
                pl.BlockSpec((seq_len, width), lambda b, h: (b, 2 * hp + h))]
    return pl.pallas_call(
        _ctx_attn_body,
        grid=(n_seq, hp),
        in_specs=in_specs,
        out_specs=[pl.BlockSpec((seq_len, width), lambda b, h: (b, h)),
                   pl.BlockSpec(kv_block, lambda b, h: (b, h, 0, 0)),
                   pl.BlockSpec(kv_block, lambda b, h: (b, h, 0, 0))],
        out_shape=[jax.ShapeDtypeStruct((t, e), ACT),
                   jax.ShapeDtypeStruct((n_seq, heads, seq_len, HEAD_DIM), F32),
                   jax.ShapeDtypeStruct((n_seq, heads, seq_len, HEAD_DIM), F32)],
        compiler_params=_params("parallel", "parallel"),
        name="ctx_attention",
    )(u, u, u)


def _nat_bias_body(rp_ref, o_ref):
    n_rel_rows = rp_ref.shape[1]
    q = lax.broadcasted_iota(jnp.int32, (GRID_W, LANES), 0)
    lane = lax.broadcasted_iota(jnp.int32, (GRID_W, LANES), 1)
    kc = lane % GRID_W
    c_start = jnp.clip(q - WIN_COLS // 2, 0, GRID_W - WIN_COLS)
    ok = (kc >= c_start) & (kc < c_start + WIN_COLS)
    left = lane < GRID_W
    for h in range(o_ref.shape[0]):
        halves = []
        for i in range(n_rel_rows):
            row = jnp.broadcast_to(rp_ref[h, i:i + 1, :], (GRID_W, LANES))
            halves.append((pltpu.roll(row, 0, 1, stride=1, stride_axis=0),
                           pltpu.roll(row, GRID_W, 1, stride=1, stride_axis=0)))
        for i in range(n_rel_rows - 1):
            o_ref[h, i] = jnp.where(ok, jnp.where(left, halves[i][0], halves[i + 1][1]), MASKED)


def _nat_bias(rpb, *, heads_per_step=4):
    heads, n_rel_rows, n_rel_cols = rpb.shape
    rows = jnp.roll(jnp.pad(rpb, ((0, 0), (0, 0), (0, LANES - n_rel_cols))), -(WIN_COLS - 1), axis=-1)
    return pl.pallas_call(
        _nat_bias_body,
        grid=(heads // heads_per_step,),
        in_specs=[pl.BlockSpec((heads_per_step, n_rel_rows, LANES), lambda h: (h, 0, 0))],
        out_specs=pl.BlockSpec((heads_per_step, n_rel_rows - 1, GRID_W, 2 * GRID_W), lambda h: (h, 0, 0, 0)),
        out_shape=jax.ShapeDtypeStruct((heads, n_rel_rows - 1, GRID_W, 2 * GRID_W), F32),
        compiler_params=_params("parallel"),
        name="nat_bias",
    )(rows)


def _nat_body(q_ref, k_ref, v_ref, ck_ref, cv_ref, bias_ref, o_ref, ckb_s, cvb_s, *, rows):
    scale = HEAD_DIM ** -0.5
    wr = min(WIN_ROWS, rows)
    nw = wr * GRID_W
    heads = tuple(range(LANES // HEAD_DIM))
    first = lax.broadcasted_iota(jnp.int32, (GRID_W, LANES), 1) < HEAD_DIM
    masks = (first, jnp.logical_not(first))
    ckb_s[...] = jnp.concatenate([ck_ref[0], ck_ref[1]], axis=1).astype(BF16)
    cvb_s[...] = jnp.concatenate([cv_ref[0], cv_ref[1]], axis=1).astype(BF16)

    def row_group(g, carry):
        rws = [g * NAT_ROWS_PER_STEP + i for i in range(NAT_ROWS_PER_STEP)]
        starts = [jnp.clip(r - wr // 2, 0, rows - wr) for r in rws]
        q_rows = [pl.ds(pl.multiple_of(r * GRID_W, GRID_W), GRID_W) for r in rws]
        k_rows = [pl.ds(pl.multiple_of(rs * GRID_W, GRID_W), nw) for rs in starts]
        logits = []
        for r, rs, qr, kr in zip(rws, starts, q_rows, k_rows):
            q2 = q_ref[qr, :].astype(F32) * scale
            kb = k_ref[kr, :].astype(BF16)
            for h in heads:
                q = jnp.where(masks[h], q2, 0.0).astype(BF16)
                logits.append((_dot_nt(q, kb), _dot_nt(q, ckb_s[...])))
        probs = []
        for idx, (s_win, s_ctx) in enumerate(logits):
            r, rs, h = rws[idx // 2], starts[idx // 2], heads[idx % 2]
            i0 = (WIN_ROWS - 1) - (r - rs)
            s_win = s_win + jnp.concatenate([bias_ref[h, i0 + 2 * jj] for jj in range(wr // 2)], axis=1)
            m = jnp.maximum(jnp.max(s_win, axis=-1, keepdims=True), jnp.max(s_ctx, axis=-1, keepdims=True))
            p_win = jnp.exp(s_win - m)
            p_ctx = jnp.exp(s_ctx - m)
            inv = 1.0 / (jnp.sum(p_win, axis=-1, keepdims=True) + jnp.sum(p_ctx, axis=-1, keepdims=True))
            probs.append(((p_win * inv).astype(BF16), (p_ctx * inv).astype(BF16)))
        for i, (qr, kr) in enumerate(zip(q_rows, k_rows)):
            vb = v_ref[kr, :].astype(BF16)
            outs = [_dot(probs[2 * i + h][0], vb) + _dot(probs[2 * i + h][1], cvb_s[...]) for h in heads]
            o_ref[qr, :] = jnp.where(first, outs[0], outs[1]).astype(o_ref.dtype)
        return carry

    lax.fori_loop(0, rows // NAT_ROWS_PER_STEP, row_group, 0)


def _nat_attention(u, cache_k, cache_v, cache_layer, bias, *, n_seq, seq_len, row_block0, e):
    hp = e // LANES
    hpb = LANES // HEAD_DIM
    past = cache_k.shape[3]
    seq = lambda b: row_block0 + b
    cache_block = (None, None, hpb, past, HEAD_DIM)
    return pl.pallas_call(
        functools.partial(_nat_body, rows=seq_len // GRID_W),
        grid=(n_seq, hp),
        in_specs=[pl.BlockSpec((seq_len, LANES), lambda b, h: (seq(b), h)),
                  pl.BlockSpec((seq_len, LANES), lambda b, h: (seq(b), hp + h)),
                  pl.BlockSpec((seq_len, LANES), lambda b, h: (seq(b), 2 * hp + h)),
                  pl.BlockSpec(cache_block, lambda b, h: (b, cache_layer, h, 0, 0)),
                  pl.BlockSpec(cache_block, lambda b, h: (b, cache_layer, h, 0, 0)),
                  pl.BlockSpec((hpb,) + bias.shape[1:], lambda b, h: (h, 0, 0, 0))],
        out_specs=pl.BlockSpec((seq_len, LANES), lambda b, h: (b, h)),
        out_shape=jax.ShapeDtypeStruct((n_seq * seq_len, e), ACT),
        scratch_shapes=[pltpu.VMEM((past, LANES), BF16), pltpu.VMEM((past, LANES), BF16)],
        compiler_params=_params("parallel", "parallel"),
        name="nat_attention",
    )(u, u, u, cache_k, cache_v, bias)


def kernel(x_prompt, x_sample, state_ssd, state_s5, cache_k, cache_v, c, c_ctx, norm_g, w_mod, b_mod, w_out, final_g, ssd_w_in, ssd_conv_w, ssd_conv_b, ssd_dt_bias, ssd_a_log, ssd_d, ssd_norm_g, mlp_w_in, mlp_ln_g, mlp_ln_b, mlp_w_s, mlp_b_s, s5_w_in, s5_lam_re, s5_lam_im, s5_log_step, s5_b_re, s5_b_im, s5_c_re, s5_c_im, s5_d, s5_w_glu, s5_b_glu, nat_w_in, nat_rpb):
    n_ctx, l_ctx, d = x_prompt.shape
    n_lat, l_lat, _ = x_sample.shape
    t_ctx = n_ctx * l_ctx
    depth = norm_g.shape[0]
    e = w_out.shape[1]
    assert depth == 4 and l_lat % l_ctx == 0 and t_ctx % (n_lat * l_lat) == 0
    tiles = dict(tm=OUT_TM, t_ctx=t_ctx, l_lat=l_lat)
    tiles_in = dict(tm=IN_TM, tn=IN_TN, t_ctx=t_ctx, l_lat=l_lat)

    t_lat = n_lat * l_lat
    x = (x_prompt.reshape(t_ctx, d), x_sample.reshape(t_lat, d))
    cond = jnp.concatenate([c_ctx[None], c, jnp.zeros((MOD_ROWS - 1 - n_lat, d), F32)], axis=0)
    mods = _modulation(cond, w_mod, b_mod).reshape(depth, MOD_ROWS, 3, 1, d)
    w_out_b = w_out.astype(BF16)

    groups = e // SSD_GROUP_CH
    n_main = 3 * e
    w_dt = ssd_w_in[0][:, n_main:].reshape(d, 2, groups, SSD_GROUP_HEADS).transpose(0, 2, 1, 3)
    w_dt = jnp.pad(w_dt.reshape(d, groups, 2 * SSD_GROUP_HEADS), ((0, 0), (0, 0), (0, LANES - 2 * SSD_GROUP_HEADS)))
    u, dt = _inproj(x, mods, 0, norm_g[0:1], ssd_w_in[0].astype(BF16), n_out=n_main,
                    w_extra=w_dt.reshape(d, groups * LANES).astype(BF16), name="ssd_in", **dict(tiles_in, tm=IN_TM_PAIR))
    ssd_args = (ssd_conv_w[0], ssd_conv_b[0:1], _group_lanes(ssd_dt_bias[0], groups),
                _group_lanes(ssd_a_log[0], groups), jnp.repeat(ssd_d[0], SSD_HEAD_DIM)[None])
    y_ctx, new_ssd = _ssd_call(u, dt, *ssd_args, None, n_seq=n_ctx, seq_len=l_ctx, seqs_per_step=l_lat // l_ctx,
                               row_block0=0, want_final=True, name="ssd_ctx")
    y_lat, = _ssd_call(u, dt, *ssd_args, state_ssd[:, 0], n_seq=n_lat, seq_len=l_lat, seqs_per_step=n_lat,
                       row_block0=t_ctx // t_lat, want_final=False, name="ssd_lat")
    x = _outproj((y_ctx, y_lat), u, 0, x, mods, 0, w_out_b, norm_g=ssd_norm_g[0:1], name="ssd_out", **tiles)

    u = _inproj(x, mods, 1, norm_g[1:2], mlp_w_in[0].astype(BF16), name="mlp_in", **tiles_in)
    b_lanes = jnp.repeat(mlp_b_s[0].T, e // mlp_b_s.shape[1], axis=1)
    y = _gmlp(u, mlp_ln_g[0:1], mlp_ln_b[0:1], mlp_w_s[0], b_lanes)
    x = _outproj(y, u, 2, x, mods, 1, w_out_b, name="mlp_out", **tiles)

    u = _inproj(x, mods, 2, norm_g[2:3], s5_w_in[0].astype(BF16), name="s5_in", **tiles_in)
    s5_groups = e // S5_GROUP
    bg = LANES // S5_GROUP
    blocks = s5_groups // bg

    def block_vec(v):
        return v.reshape(2, blocks, 1, S5_BLOCK_ST).transpose(1, 0, 2, 3)

    def block_mat_b(v):
        v = v.reshape(2, blocks, bg, S5_STATE, S5_GROUP).transpose(1, 0, 4, 2, 3)
        return v.reshape(blocks, 2, S5_GROUP, S5_BLOCK_ST)

    def block_mat_c(v):
        v = v.reshape(2, blocks, bg, S5_GROUP, S5_STATE).transpose(1, 0, 3, 2, 4)
        return v.reshape(blocks, 2, S5_GROUP, S5_BLOCK_ST)

    log_step = jnp.repeat(s5_log_step[0][:, :, None], S5_STATE, axis=2)
    prm = (block_vec(s5_lam_re[0]), block_vec(s5_lam_im[0]), block_vec(log_step), block_mat_b(s5_b_re[0]),
           block_mat_b(s5_b_im[0]), block_mat_c(s5_c_re[0]), block_mat_c(s5_c_im[0]))
    h0 = state_s5[:, 0].reshape(n_lat, 2, 2, blocks, S5_BLOCK_ST).transpose(3, 1, 2, 0, 4)
    ys, fin = _s5_core(prm, h0, u, n_ctx=n_ctx, l_ctx=l_ctx, n_lat=n_lat, l_lat=l_lat, e=e)
    new_s5 = fin[:, None]
    y = _s5_glu(ys, u, s5_d[0:1], s5_w_glu[0].astype(BF16), s5_b_glu[0:1])
    x = _outproj(y, u, 1, x, mods, 2, w_out_b, name="s5_out", **tiles)

    u = _inproj(x, mods, 3, norm_g[3:4], nat_w_in[0].astype(BF16), name="nat_in", **tiles_in)
    y_ctx, new_k, new_v = _ctx_attention(u, n_seq=n_ctx, seq_len=l_ctx, e=e)
    y_lat = _nat_attention(u, cache_k, cache_v, 0, _nat_bias(nat_rpb[0]), n_seq=n_lat, seq_len=l_lat,
                           row_block0=t_ctx // l_lat, e=e)
    out_ctx, out_lat = _outproj((y_ctx, y_lat), u, 3, x, mods, 3, w_out_b, final_g=final_g[None], split_rows=(t_ctx, t_lat),
                                name="nat_out", **tiles)

    return (out_ctx.reshape(n_ctx, l_ctx, d), out_lat.reshape(n_lat, l_lat, d),
            new_ssd[:, None], new_s5, new_k[:, None], new_v[:, None])
```

```python
import collections
import functools

import jax
import jax.numpy as jnp
from jax import lax
from jax.experimental import pallas as pl
from jax.experimental.pallas import tpu as pltpu

F32 = jnp.float32
BF16 = jnp.bfloat16
ACT = jnp.bfloat16
EPS = 1e-6

LANES = 128
CHUNK = 128
SSD_HEAD_DIM = 64
SSD_GROUP_HEADS = 4
SSD_GROUP_CH = SSD_HEAD_DIM * SSD_GROUP_HEADS
SSD_STATE = 128
CONV_W = 5
CONV_HALO = 16
SSD_INTERLEAVE = 4
CONV_WIN = 2 * CHUNK
S5_T = 8
S5_GROUP = 16
S5_STATE = 64
S5_BLOCK_ST = (LANES // S5_GROUP) * S5_STATE
HEAD_DIM = 64
GRID_W = 64
WIN_ROWS = 8
WIN_COLS = 16
NAT_ROWS_PER_STEP = 8
MASKED = -1e30
MOD_ROWS = 8

VMEM_LIMIT = 52 * 1024 * 1024
IN_TM, IN_TN = 2048, 1024
IN_TM_PAIR = 1024
OUT_TM = 512
GMLP_TM = 256
GLU_TM = 512
CTX_ATTN_WIDTH = 1024


def _silu(x):
    return x * jax.nn.sigmoid(x)


def _gelu(x):
    return 0.5 * x * (1.0 + jnp.tanh(0.7978845608028654 * (x + 0.044715 * (x * x * x))))


def _softplus(x):
    return jnp.maximum(x, 0.0) + jnp.log1p(jnp.exp(-jnp.abs(x)))


def _dot(a, b):
    return jnp.dot(a, b, preferred_element_type=F32)


def _dot_nt(a, b):
    return lax.dot_general(a, b, (((1,), (1,)), ((), ())), preferred_element_type=F32)


def _cumsum_rows(lower_tri, x):
    hi = x.astype(BF16)
    r1 = x - hi.astype(F32)
    mid = r1.astype(BF16)
    lo = (r1 - mid.astype(F32)).astype(BF16)
    return _dot(lower_tri, hi) + _dot(lower_tri, mid) + _dot(lower_tri, lo)


def _params(*sem):
    return pltpu.CompilerParams(dimension_semantics=sem, vmem_limit_bytes=VMEM_LIMIT)


def _mod_row(i, *, tm, t_ctx, l_lat):
    start = i * tm
    return jnp.where(start < t_ctx, 0, 1 + (start - t_ctx) // l_lat)


def _mod_body(c_ref, w_ref, b_ref, o_ref):
    cond = _silu(c_ref[...]).astype(BF16)
    o_ref[...] = _dot(cond, w_ref[...].astype(BF16)) + b_ref[...]


def _modulation(cond, w_mod, b_mod, *, tn=IN_TN):
    depth, d, n = w_mod.shape
    rows = cond.shape[0]
    return pl.pallas_call(
        _mod_body,
        grid=(depth, n // tn),
        in_specs=[pl.BlockSpec((rows, d), lambda l, j: (0, 0)),
                  pl.BlockSpec((None, d, tn), lambda l, j: (l, 0, j)),
                  pl.BlockSpec((None, 1, tn), lambda l, j: (l, 0, j))],
        out_specs=pl.BlockSpec((None, rows, tn), lambda l, j: (l, 0, j)),
        out_shape=jax.ShapeDtypeStruct((depth, rows, n), F32),
        compiler_params=_params("parallel", "parallel"),
        name="modulation",
    )(cond, w_mod, b_mod.reshape(depth, 1, n))


def _stream_specs(x, tm, width, n_grid):
    ids = (lambda i, *_: i) if n_grid == 1 else (lambda i, j: i)
    if not isinstance(x, tuple):
        return [x], [pl.BlockSpec((tm, width), lambda *g: (ids(*g), 0))], None
    split = x[0].shape[0] // tm
    last = x[1].shape[0] // tm - 1
    return (list(x),
            [pl.BlockSpec((tm, width), lambda *g: (jnp.minimum(ids(*g), split - 1), 0)),
             pl.BlockSpec((tm, width), lambda *g: (jnp.clip(ids(*g) - split, 0, last), 0))],
            split)


def _inproj_body(*refs, split, extra_from):
    n_x = 1 if split is None else 2
    x_refs = refs[:n_x]
    if extra_from is None:
        shift_ref, scale_ref, g_ref, w_ref, o_ref, h_ref = refs[n_x:]
    else:
        shift_ref, scale_ref, g_ref, w_ref, w2_ref, o_ref, o2_ref, h_ref = refs[n_x:]

    def normalise(x_ref):
        x = x_ref[...]
        y = x * lax.rsqrt(jnp.mean(x * x, axis=-1, keepdims=True) + EPS) * g_ref[...]
        h_ref[...] = (y * (1.0 + scale_ref[...]) + shift_ref[...]).astype(BF16)

    first_col = pl.program_id(1) == 0
    if split is None:
        pl.when(first_col)(lambda: normalise(x_refs[0]))
    else:
        in_ctx = pl.program_id(0) < split
        pl.when(first_col & in_ctx)(lambda: normalise(x_refs[0]))
        pl.when(first_col & jnp.logical_not(in_ctx))(lambda: normalise(x_refs[1]))
    if extra_from is None:
        o_ref[...] = _dot(h_ref[...], w_ref[...]).astype(o_ref.dtype)
    else:
        main = pl.program_id(1) < extra_from

        @pl.when(main)
        def _():
            o_ref[...] = _dot(h_ref[...], w_ref[...]).astype(o_ref.dtype)

        @pl.when(jnp.logical_not(main))
        def _():
            o2_ref[...] = _dot(h_ref[...], w2_ref[...]).astype(o2_ref.dtype)


def _inproj(x, mods, layer, g, w, *, tm, tn, t_ctx, l_lat, name, out_dtype=ACT, n_out=None, w_extra=None):
    d = w.shape[0]
    n = w.shape[1] if n_out is None else n_out
    row = functools.partial(_mod_row, tm=tm, t_ctx=t_ctx, l_lat=l_lat)
    x_args, x_specs, split = _stream_specs(x, tm, d, 2)
    t = sum(a.shape[0] for a in x_args)
    nj = n // tn

    def mod_spec(part):
        return pl.BlockSpec((None, None, None, 1, d), lambda i, j: (layer, row(i), part, 0, 0))

    in_specs = x_specs + [mod_spec(0), mod_spec(1), pl.BlockSpec((1, d), lambda i, j: (0, 0))]
    if w_extra is None:
        in_specs.append(pl.BlockSpec((d, tn), lambda i, j: (0, j)))
        weights, steps = [w], nj
        out_specs = pl.BlockSpec((tm, tn), lambda i, j: (i, j))
        out_shape = jax.ShapeDtypeStruct((t, n), out_dtype)
    else:
        nj2 = w_extra.shape[1] // tn
        in_specs += [pl.BlockSpec((d, tn), lambda i, j: (0, jnp.minimum(j, nj - 1))),
                     pl.BlockSpec((d, tn), lambda i, j: (0, jnp.clip(j - nj, 0, nj2 - 1)))]
        weights, steps = [w, w_extra], nj + nj2
        out_specs = [pl.BlockSpec((tm, tn), lambda i, j: (i, jnp.minimum(j, nj - 1))),
                     pl.BlockSpec((tm, tn), lambda i, j: (i, jnp.clip(j - nj, 0, nj2 - 1)))]
        out_shape = [jax.ShapeDtypeStruct((t, n), out_dtype), jax.ShapeDtypeStruct((t, w_extra.shape[1]), F32)]
    return pl.pallas_call(
        functools.partial(_inproj_body, split=split, extra_from=None if w_extra is None else nj),
        grid=(t // tm, steps),
        in_specs=in_specs,
        out_specs=out_specs,
        out_shape=out_shape,
        scratch_shapes=[pltpu.VMEM((tm, d), BF16)],
        compiler_params=_params("parallel", "arbitrary"),
        name=name,
    )(*x_args, mods, mods, g, *weights)


def _outproj_body(*refs, n_y, n_x, split, gated_norm, final_norm):
    refs = list(refs)
    y_refs, z_ref, x_refs = refs[:n_y], refs[n_y], refs[n_y + 1:n_y + 1 + n_x]
    rest = refs[n_y + 1 + n_x:]
    gate_ref, w_ref = rest[:2]
    rest = rest[2:]
    ng_ref = rest.pop(0) if gated_norm else None
    fg_ref = rest.pop(0) if final_norm else None
    o_refs, wb_s = rest[:-1], rest[-1]

    @pl.when(pl.program_id(0) == 0)
    def _():
        wb_s[...] = w_ref[...].astype(BF16)

    def emit(y_ref, x_ref, o_ref):
        t = y_ref[...].astype(F32) * _silu(z_ref[...].astype(F32))
        if gated_norm:
            t = t * lax.rsqrt(jnp.mean(t * t, axis=-1, keepdims=True) + EPS) * ng_ref[...]
        xn = x_ref[...] + gate_ref[...] * _dot(t.astype(BF16), wb_s[...])
        if final_norm:
            xn = xn * lax.rsqrt(jnp.mean(xn * xn, axis=-1, keepdims=True) + EPS) * fg_ref[...]
        o_ref[...] = xn

    if split is None:
        emit(y_refs[0], x_refs[0], o_refs[0])
    else:
        in_ctx = pl.program_id(0) < split
        pl.when(in_ctx)(lambda: emit(y_refs[0], x_refs[0], o_refs[0]))
        pl.when(jnp.logical_not(in_ctx))(lambda: emit(y_refs[-1], x_refs[-1], o_refs[-1]))


def _outproj(y, u, z_block, x, mods, layer, w, *, norm_g=None, final_g=None, split_rows=None, tm, t_ctx, l_lat,
             name):
    t, e, d = u.shape[0], w.shape[1], w.shape[2]
    row = functools.partial(_mod_row, tm=tm, t_ctx=t_ctx, l_lat=l_lat)
    y_args, y_specs, split_y = _stream_specs(y, tm, e, 1)
    x_args, x_specs, split_in = _stream_specs(x, tm, d, 1)
    in_specs = (y_specs + [pl.BlockSpec((tm, e), lambda i: (i, z_block))] + x_specs
                + [pl.BlockSpec((None, None, None, 1, d), lambda i: (layer, row(i), 2, 0, 0)),
                   pl.BlockSpec((None, e, d), lambda i: (layer, 0, 0), pipeline_mode=pl.Buffered(1))])
    args = y_args + [u] + x_args + [mods, w]
    if norm_g is not None:
        in_specs.append(pl.BlockSpec((1, e), lambda i: (0, 0)))
        args.append(norm_g)
    if final_g is not None:
        in_specs.append(pl.BlockSpec((1, d), lambda i: (0, 0)))
        args.append(final_g)
    if split_rows is None:
        split_out = None
        out_specs = pl.BlockSpec((tm, d), lambda i: (i, 0))
        out_shape = jax.ShapeDtypeStruct((t, d), F32)
    else:
        split_out = split_rows[0] // tm
        last = split_rows[1] // tm - 1
        out_specs = [pl.BlockSpec((tm, d), lambda i: (jnp.minimum(i, split_out - 1), 0)),
                     pl.BlockSpec((tm, d), lambda i: (jnp.clip(i - split_out, 0, last), 0))]
        out_shape = [jax.ShapeDtypeStruct((rows, d), F32) for rows in split_rows]
    return pl.pallas_call(
        functools.partial(_outproj_body, n_y=len(y_args), n_x=len(x_args),
                          split=next((s for s in (split_y, split_in, split_out) if s is not None), None),
                          gated_norm=norm_g is not None, final_norm=final_g is not None),
        grid=(t // tm,),
        in_specs=in_specs,
        out_specs=out_specs,
        out_shape=out_shape,
        scratch_shapes=[pltpu.VMEM((e, d), BF16)],
        compiler_params=_params("arbitrary"),
        name=name,
    )(*args)


_SsdScratch = collections.namedtuple("_SsdScratch", "pad xbd xt bm bt cm e et ct dtt y sf sb")


def _ssd_scratch_shapes(seq_len):
    nc = seq_len // CHUNK
    width = SSD_GROUP_CH + 2 * SSD_STATE
    return [pltpu.VMEM((seq_len + CONV_WIN - CHUNK, width), ACT),
            pltpu.VMEM((nc, SSD_GROUP_HEADS * CHUNK, SSD_GROUP_CH), BF16),
            pltpu.VMEM((nc, SSD_GROUP_CH, CHUNK), F32),
            pltpu.VMEM((seq_len, SSD_STATE), BF16),
            pltpu.VMEM((nc, SSD_STATE, CHUNK), BF16),
            pltpu.VMEM((seq_len, SSD_STATE), BF16),
            pltpu.VMEM((seq_len, LANES), F32),
            pltpu.VMEM((nc, 2 * SSD_GROUP_HEADS, CHUNK), F32),
            pltpu.VMEM((nc, 2 * SSD_GROUP_HEADS, CHUNK), F32),
            pltpu.VMEM((nc, 2 * SSD_GROUP_HEADS, CHUNK), F32),
            pltpu.VMEM((seq_len, SSD_GROUP_CH), F32),
            pltpu.VMEM((SSD_GROUP_CH, SSD_STATE), F32),
            pltpu.VMEM((SSD_GROUP_CH, SSD_STATE), F32)]


def _ssd_body(*refs, seq_len, seqs_per_step, has_h0, want_final):
    refs = list(refs)
    x_ref, b_ref, c_ref, dt_ref, wx_ref, wb_ref, wc_ref, bx_ref, bb_ref, bc_ref, dtb_ref, alog_ref, dsk_ref = refs[:13]
    rest = refs[13:]
    h0_ref = rest.pop(0) if has_h0 else None
    y_ref = rest.pop(0)
    hf_ref = rest.pop(0) if want_final else None
    shift_s = rest.pop(0)
    il = min(SSD_INTERLEAVE, seqs_per_step)
    per_seq = len(rest) // il
    sc = [_SsdScratch(*rest[q * per_seq:(q + 1) * per_seq]) for q in range(il)]
    lanes_q = range(il)
    nc = seq_len // CHUNK
    width = SSD_GROUP_CH + 2 * SSD_STATE
    half = width // 2
    nh = SSD_GROUP_HEADS
    mid = CONV_W // 2

    tail = CONV_WIN - CHUNK - CONV_HALO
    for q in lanes_q:
        sc[q].pad[0:CONV_HALO, :] = jnp.zeros((CONV_HALO, width), ACT)
        sc[q].pad[CONV_HALO + seq_len:CONV_HALO + seq_len + tail, :] = jnp.zeros((tail, width), ACT)
    win_row = lax.broadcasted_iota(jnp.int32, (CHUNK, CONV_WIN), 1)
    tok_row = lax.broadcasted_iota(jnp.int32, (CHUNK, CONV_WIN), 0)
    taps = [j for j in range(CONV_W) if j != mid]
    for n, j in enumerate(taps):
        shift_s[n] = (win_row == tok_row + (CONV_HALO + j - mid)).astype(BF16)

    conv_w = jnp.concatenate([wx_ref[...], wb_ref[...], wc_ref[...]], axis=1)
    conv_b = jnp.concatenate([bx_ref[...], bb_ref[...], bc_ref[...]], axis=1)
    a_row = -jnp.exp(alog_ref[...])
    row_i = lax.broadcasted_iota(jnp.int32, (CHUNK, CHUNK), 0)
    col_i = lax.broadcasted_iota(jnp.int32, (CHUNK, CHUNK), 1)
    lower = row_i >= col_i
    upper = row_i <= col_i
    lower_b = lower.astype(BF16)
    fwd_lane = lax.broadcasted_iota(jnp.int32, (1, LANES), 1) < nh
    head_of_lane = lax.broadcasted_iota(jnp.int32, (CHUNK, SSD_GROUP_CH), 1) // SSD_HEAD_DIM

    def chunk_rows(ci):
        return pl.ds(pl.multiple_of(ci * CHUNK, CHUNK), CHUNK)

    def copy_chunk(offs, ci, carry):
        base = pl.multiple_of(ci * CHUNK, CHUNK)
        dst = pl.ds(base + CONV_HALO, CHUNK)
        for q in lanes_q:
            src = pl.ds(offs[q] + base, CHUNK)
            sc[q].pad[dst, 0:SSD_GROUP_CH] = x_ref[src, :]
            sc[q].pad[dst, SSD_GROUP_CH:SSD_GROUP_CH + SSD_STATE] = b_ref[src, :]
            sc[q].pad[dst, SSD_GROUP_CH + SSD_STATE:width] = c_ref[src, :]
        return carry

    def prep_chunk(offs, ci, carry):
        r = chunk_rows(ci)
        units = [(q, h) for h in range(2) for q in lanes_q]

        def window(unit):
            q, h = unit
            return sc[q].pad[pl.ds(pl.multiple_of(ci * CHUNK, CHUNK), CONV_WIN), h * half:(h + 1) * half]

        def shift_dots(unit):
            win = window(unit)
            return [_dot(shift_s[t], win) for t in range(len(taps))]

        def finish(unit, shifted):
            q, h = unit
            cols = slice(h * half, (h + 1) * half)
            acc = conv_b[:, cols] + window(unit)[CONV_HALO:CONV_HALO + CHUNK].astype(F32) * conv_w[mid:mid + 1, cols]
            for t, j in enumerate(taps):
                acc = acc + shifted[t] * conv_w[j:j + 1, cols]
            v = _silu(acc)
            if h == 0:
                sc[q].y[r, :] = v * dsk_ref[...]
                vb = v.astype(BF16)
                for hd in range(nh):
                    sc[q].xbd[ci, hd * CHUNK:(hd + 1) * CHUNK, :] = jnp.where(head_of_lane == hd, vb,
                                                                              jnp.zeros_like(vb))
                sc[q].xt[ci] = v.T
            else:
                bm = v[:, 0:SSD_STATE]
                sc[q].bm[r, :] = bm.astype(BF16)
                sc[q].bt[ci] = bm.T.astype(BF16)
                sc[q].cm[r, :] = v[:, SSD_STATE:half].astype(BF16)

        dts = [_softplus(dt_ref[pl.ds(offs[q] + pl.multiple_of(ci * CHUNK, CHUNK), CHUNK), :] + dtb_ref[...])
               for q in lanes_q]
        das = [dt * a_row for dt in dts]
        pending = shift_dots(units[0])
        cums = []
        for i, unit in enumerate(units):
            following = shift_dots(units[i + 1]) if i + 1 < len(units) else None
            if unit[1] == 0:
                cums.append(_cumsum_rows(lower_b, das[unit[0]]))
            finish(unit, pending)
            pending = following
        for q in lanes_q:
            e = jnp.where(fwd_lane, cums[q], cums[q] - das[q])
            sc[q].e[r, :] = e
            sc[q].et[ci] = e.T[0:2 * nh]
            sc[q].ct[ci] = cums[q].T[0:2 * nh]
            sc[q].dtt[ci] = dts[q].T[0:2 * nh]
        return carry

    def scan_chunk(i, carry):
        chunks = (i, nc - 1 - i)
        rows = [chunk_rows(ci) for ci in chunks]
        streams = [(q, d) for q in lanes_q for d in range(2)]
        st = {(q, d): (sc[q].sf if d == 0 else sc[q].sb) for q, d in streams}
        s_prev = {k: st[k][...] for k in streams}
        g, cs = {}, {}
        for q, d in streams:
            cm = sc[q].cm[rows[d], :]
            g[q, d] = _dot(cm, sc[q].bt[chunks[d]])
            cs[q, d] = _dot_nt(cm, s_prev[q, d].astype(BF16))
        w_rows, keeps, ys = {}, {}, {}
        for q, d in streams:
            e = sc[q].e[rows[d], :]
            e_t, c_t, dt_t = sc[q].et[chunks[d]], sc[q].ct[chunks[d]], sc[q].dtt[chunks[d]]
            w_d, keep_d, mixes, offs = [], [], [], []
            for h in range(nh):
                col = h + nh * d
                ec = jnp.broadcast_to(e[:, col:col + 1], (CHUNK, CHUNK))
                er = e_t[col:col + 1, :]
                dt_h = dt_t[col:col + 1, :]
                log_dt = jnp.log(dt_h)
                tot = c_t[col:col + 1, CHUNK - 1:CHUNK]
                if d == 0:
                    decay_dt = jnp.exp(jnp.where(lower, ec - (er - log_dt), -jnp.inf))
                    offs.append(jnp.exp(ec))
                    w_d.append(jnp.broadcast_to(dt_h * jnp.exp(tot - er), (SSD_HEAD_DIM, CHUNK)))
                else:
                    decay_dt = jnp.exp(jnp.where(upper, (er + log_dt) - ec, -jnp.inf))
                    offs.append(jnp.exp(tot - ec))
                    w_d.append(jnp.broadcast_to(dt_h * jnp.exp(er), (SSD_HEAD_DIM, CHUNK)))
                keep_d.append(jnp.broadcast_to(jnp.exp(tot), (SSD_HEAD_DIM, SSD_STATE)))
                mixes.append((g[q, d] * decay_dt).astype(BF16))
            y_diag = _dot(jnp.concatenate(mixes, axis=1), sc[q].xbd[chunks[d]])
            off = jnp.concatenate([offs[nh - 1]] * (SSD_GROUP_CH // CHUNK), axis=1)
            for h in range(nh - 2, -1, -1):
                off = jnp.where(head_of_lane == h, jnp.concatenate([offs[h]] * (SSD_GROUP_CH // CHUNK), axis=1), off)
            ys[q, d] = y_diag + cs[q, d] * off
            w_rows[q, d] = jnp.concatenate(w_d, axis=0)
            keeps[q, d] = jnp.concatenate(keep_d, axis=0)
        for q, d in streams:
            w_t = (sc[q].xt[chunks[d]] * w_rows[q, d]).astype(BF16)
            st[q, d][...] = keeps[q, d] * s_prev[q, d] + _dot(w_t, sc[q].bm[rows[d], :])
            sc[q].y[rows[d], :] += ys[q, d]
        return carry

    def emit_chunk(offs, ci, carry):
        base = pl.multiple_of(ci * CHUNK, CHUNK)
        for q in lanes_q:
            y_ref[pl.ds(offs[q] + base, CHUNK), :] = sc[q].y[pl.ds(base, CHUNK), :].astype(y_ref.dtype)
        return carry

    def one_group(s, carry):
        seqs = [s * il + q for q in lanes_q]
        offs = [pl.multiple_of(sq * seq_len, CHUNK) for sq in seqs]
        lax.fori_loop(0, nc, functools.partial(copy_chunk, offs), 0)
        lax.fori_loop(0, nc, functools.partial(prep_chunk, offs), 0)
        for q in lanes_q:
            if has_h0:
                sc[q].sf[...] = h0_ref[seqs[q], 0].reshape(SSD_GROUP_CH, SSD_STATE)
                sc[q].sb[...] = h0_ref[seqs[q], 1].reshape(SSD_GROUP_CH, SSD_STATE)
            else:
                sc[q].sf[...] = jnp.zeros((SSD_GROUP_CH, SSD_STATE), F32)
                sc[q].sb[...] = jnp.zeros((SSD_GROUP_CH, SSD_STATE), F32)
        lax.fori_loop(0, nc, scan_chunk, 0)
        lax.fori_loop(0, nc, functools.partial(emit_chunk, offs), 0)
        if want_final:
            for q in lanes_q:
                hf_ref[seqs[q], 0] = sc[q].sf[...].reshape(SSD_GROUP_HEADS, SSD_HEAD_DIM, SSD_STATE)
                hf_ref[seqs[q], 1] = sc[q].sb[...].reshape(SSD_GROUP_HEADS, SSD_HEAD_DIM, SSD_STATE)
        return carry

    lax.fori_loop(0, seqs_per_step // il, one_group, 0)


def _ssd_call(u, dt, conv_w, conv_b, dtb, alog, dsk, h0, *, n_seq, seq_len, seqs_per_step, row_block0,
              want_final, name):
    t, n_u = u.shape
    e = dsk.shape[1]
    groups = e // SSD_GROUP_CH
    xb0 = e // SSD_GROUP_CH
    bb0 = 2 * e // SSD_STATE
    cb0 = bb0 + groups
    has_h0 = h0 is not None
    seq = lambda b: row_block0 + b
    rows = seqs_per_step * seq_len
    in_specs = [pl.BlockSpec((rows, SSD_GROUP_CH), lambda b, g: (seq(b), xb0 + g)),
                pl.BlockSpec((rows, SSD_STATE), lambda b, g: (seq(b), bb0 + g)),
                pl.BlockSpec((rows, SSD_STATE), lambda b, g: (seq(b), cb0 + g)),
                pl.BlockSpec((rows, LANES), lambda b, g: (seq(b), g)),
                pl.BlockSpec((CONV_W, SSD_GROUP_CH), lambda b, g: (0, g)),
                pl.BlockSpec((CONV_W, SSD_STATE), lambda b, g: (0, bb0 - xb0 * 2 + g)),
                pl.BlockSpec((CONV_W, SSD_STATE), lambda b, g: (0, cb0 - xb0 * 2 + g)),
                pl.BlockSpec((1, SSD_GROUP_CH), lambda b, g: (0, g)),
                pl.BlockSpec((1, SSD_STATE), lambda b, g: (0, bb0 - xb0 * 2 + g)),
                pl.BlockSpec((1, SSD_STATE), lambda b, g: (0, cb0 - xb0 * 2 + g)),
                pl.BlockSpec((1, LANES), lambda b, g: (0, g)),
                pl.BlockSpec((1, LANES), lambda b, g: (0, g)),
                pl.BlockSpec((1, SSD_GROUP_CH), lambda b, g: (0, g))]
    args = [u, u, u, dt, conv_w, conv_w, conv_w, conv_b, conv_b, conv_b, dtb, alog, dsk]
    state_block = (seqs_per_step, 2, SSD_GROUP_HEADS, SSD_HEAD_DIM, SSD_STATE)
    if has_h0:
        in_specs.append(pl.BlockSpec(state_block, lambda b, g: (b, 0, g, 0, 0)))
        args.append(h0)
    out_specs = [pl.BlockSpec((rows, SSD_GROUP_CH), lambda b, g: (b, g))]
    out_shape = [jax.ShapeDtypeStruct((n_seq * seq_len, e), ACT)]
    if want_final:
        out_specs.append(pl.BlockSpec(state_block, lambda b, g: (b, 0, g, 0, 0)))
        out_shape.append(jax.ShapeDtypeStruct((n_seq, 2, e // SSD_HEAD_DIM, SSD_HEAD_DIM, SSD_STATE), F32))
    return pl.pallas_call(
        functools.partial(_ssd_body, seq_len=seq_len, seqs_per_step=seqs_per_step, has_h0=has_h0,
                          want_final=want_final),
        grid=(n_seq // seqs_per_step, groups),
        in_specs=in_specs,
        out_specs=out_specs,
        out_shape=out_shape,
        scratch_shapes=([pltpu.VMEM((CONV_W - 1, CHUNK, CONV_WIN), BF16)]
                        + _ssd_scratch_shapes(seq_len) * min(SSD_INTERLEAVE, seqs_per_step)),
        compiler_params=_params("parallel", "parallel"),
        name=name,
    )(*args)


def _group_lanes(v, groups):
    per_group = v.reshape(2, groups, SSD_GROUP_HEADS).transpose(1, 0, 2).reshape(groups, 2 * SSD_GROUP_HEADS)
    return jnp.pad(per_group, ((0, 0), (0, LANES - 2 * SSD_GROUP_HEADS))).reshape(1, groups * LANES)


def _gmlp_body(u_ref, v_ref, lng_ref, lnb_ref, ws_ref, bs_ref, o_ref, *, tm):
    groups = ws_ref.shape[0]
    gch = u_ref.shape[1] // groups
    for c in range(tm // CHUNK):
        rows = slice(c * CHUNK, (c + 1) * CHUNK)
        v = _gelu(v_ref[rows, :].astype(F32))
        vc = v - jnp.mean(v, axis=-1, keepdims=True)
        vn = vc * lax.rsqrt(jnp.mean(vc * vc, axis=-1, keepdims=True) + EPS) * lng_ref[...] + lnb_ref[...]
        vb = vn.astype(BF16)
        for g in range(groups):
            cols = slice(g * gch, (g + 1) * gch)
            s = _dot(ws_ref[g].astype(BF16), vb[:, cols]) + bs_ref[:, cols]
            o_ref[rows, cols] = (_gelu(u_ref[rows, cols].astype(F32)) * s).astype(o_ref.dtype)


def _gmlp(u, ln_g, ln_b, w_s, b_lanes, *, tm=GMLP_TM):
    t = u.shape[0]
    e = ln_g.shape[1]
    return pl.pallas_call(
        functools.partial(_gmlp_body, tm=tm),
        grid=(t // tm,),
        in_specs=[pl.BlockSpec((tm, e), lambda i: (i, 0)),
                  pl.BlockSpec((tm, e), lambda i: (i, 1)),
                  pl.BlockSpec((1, e), lambda i: (0, 0)),
                  pl.BlockSpec((1, e), lambda i: (0, 0)),
                  pl.BlockSpec(w_s.shape, lambda i: (0, 0, 0)),
                  pl.BlockSpec((CHUNK, e), lambda i: (0, 0))],
        out_specs=pl.BlockSpec((tm, e), lambda i: (i, 0)),
        out_shape=jax.ShapeDtypeStruct((t, e), ACT),
        compiler_params=_params("parallel"),
        name="gmlp",
    )(u, u, ln_g, ln_b, w_s, b_lanes)


def _s5_body(lr_ref, li_ref, ls_ref, brt_ref, bit_ref, cre_ref, cim_ref, h0_ref, u_ref, y_ref, fin_ref,
             win_s, t_s, ef_s, eb_s, z_s, a_s, zin_s, d_s, spf_s, spb_s, yo_s, uf_s,
             *, seqs, l_ctx, n_lat, l_lat, ctx_parts):
    bst = S5_BLOCK_ST
    ng = LANES // S5_GROUP
    kw = S5_T * LANES
    part = pl.program_id(1)

    @pl.when(part == 0)
    def _build():
        own = (lax.broadcasted_iota(jnp.int32, (ng, S5_GROUP, bst), 2) // S5_STATE
               == lax.broadcasted_iota(jnp.int32, (ng, S5_GROUP, bst), 0))

        def spread(v):
            return jnp.where(own, v[None], 0.0).reshape(LANES, bst).astype(BF16)

        tau = lax.broadcasted_iota(jnp.int32, (S5_T + 8, 1), 0).astype(F32)
        e_refs = (ef_s, eb_s)
        for d in range(2):
            lr, li = lr_ref[d], li_ref[d]
            step = jnp.exp(ls_ref[d])
            mag = jnp.exp(tau * (lr * step))
            p_re = mag * jnp.cos(tau * (li * step))
            p_im = mag * jnp.sin(tau * (li * step))
            ab_re, ab_im = p_re[1:2], p_im[1:2]
            den = lr * lr + li * li
            nr = ab_re - 1.0
            cr = (nr * lr + ab_im * li) / den
            ci = (ab_im * lr - nr * li) / den
            brt, bit = brt_ref[d], bit_ref[d]
            bb_re = cr * brt - ci * bit
            bb_im = cr * bit + ci * brt
            cre, cim = cre_ref[d], cim_ref[d]
            for k in range(S5_T):
                rows = slice(k * LANES, (k + 1) * LANES)
                tq = S5_T - 1 - k if d == 0 else k
                te = k + 1 if d == 0 else S5_T - k
                pr, pi = p_re[tq:tq + 1], p_im[tq:tq + 1]
                win_s[rows, 2 * d * bst:(2 * d + 1) * bst] = spread(pr * bb_re - pi * bb_im)
                win_s[rows, (2 * d + 1) * bst:(2 * d + 2) * bst] = spread(pr * bb_im + pi * bb_re)
                pr, pi = p_re[te:te + 1], p_im[te:te + 1]
                e_refs[d][rows, 0:bst] = spread(cre * pr - cim * pi)
                e_refs[d][rows, bst:2 * bst] = spread(-(cre * pi + cim * pr))
            c_own = jnp.concatenate([spread(cre), spread(-cim)], axis=1)
            z = _dot_nt(win_s[:, 2 * d * bst:(2 * d + 2) * bst], c_own)
            if d == 0:
                z_s[0:kw, :] = z
            else:
                z_s[kw - LANES:kw, :] += z[0:LANES]
                z_s[kw:2 * kw - LANES, :] = z[LANES:kw]
            a_s[2 * d:2 * d + 1, :] = p_re[S5_T:S5_T + 1]
            a_s[2 * d + 1:2 * d + 2, :] = p_im[S5_T:S5_T + 1]
        for k in range(S5_T):
            r0 = (S5_T - 1 - k) * LANES
            t_s[:, k * LANES:(k + 1) * LANES] = z_s[r0:r0 + kw, :].astype(BF16)

    def outputs():
        yo_s[...] = (_dot(zin_s[...], t_s[...]) + _dot_nt(spf_s[...].astype(BF16), ef_s[...])
                     + _dot_nt(spb_s[...].astype(BF16), eb_s[...]))

    @pl.when(part < ctx_parts)
    def _context():
        nj = l_ctx // S5_T
        uf_s[...] = u_ref[...].astype(F32)
        for j in range(nj):
            for k in range(S5_T):
                zin_s[j * seqs:(j + 1) * seqs, k * LANES:(k + 1) * LANES] = (
                    uf_s[pl.ds(j * S5_T + k, seqs, stride=l_ctx), :].astype(BF16))
        d_s[...] = _dot(zin_s[...], win_s[...])
        yo_s[...] = _dot(zin_s[...], t_s[...])
        coef = [jnp.broadcast_to(a_s[i:i + 1, :], (seqs, bst)) for i in range(4)]
        fr = fi = br = bi = jnp.zeros((seqs, bst), F32)
        for j in range(nj):
            rf = slice(j * seqs, (j + 1) * seqs)
            rb = slice((nj - 1 - j) * seqs, (nj - j) * seqs)
            spf_s[rf, 0:bst] = fr
            spf_s[rf, bst:2 * bst] = fi
            spb_s[rb, 0:bst] = br
            spb_s[rb, bst:2 * bst] = bi
            fr, fi, br, bi = (coef[0] * fr - coef[1] * fi + d_s[rf, 0:bst],
                              coef[0] * fi + coef[1] * fr + d_s[rf, bst:2 * bst],
                              coef[2] * br - coef[3] * bi + d_s[rb, 2 * bst:3 * bst],
                              coef[2] * bi + coef[3] * br + d_s[rb, 3 * bst:4 * bst])
        for d, ri, val in ((0, 0, fr), (0, 1, fi), (1, 0, br), (1, 1, bi)):
            for gg in range(ng):
                fin_ref[:, d, ri, gg, :] = val[:, gg * S5_STATE:(gg + 1) * S5_STATE]
        yo_s[...] += (_dot_nt(spf_s[...].astype(BF16), ef_s[...]) + _dot_nt(spb_s[...].astype(BF16), eb_s[...]))
        for j in range(nj):
            for k in range(S5_T):
                y_ref[pl.ds(j * S5_T + k, seqs, stride=l_ctx), :] = (
                    yo_s[j * seqs:(j + 1) * seqs, k * LANES:(k + 1) * LANES])

    @pl.when(part == ctx_parts)
    def _latent():
        nj = l_lat // S5_T
        uf_s[...] = u_ref[...].astype(F32)
        for b in range(n_lat):
            for k in range(S5_T):
                zin_s[b * nj:(b + 1) * nj, k * LANES:(k + 1) * LANES] = (
                    uf_s[pl.ds(b * l_lat + k, nj, stride=S5_T), :].astype(BF16))
        d_s[...] = _dot(zin_s[...], win_s[...])
        coef_re = jnp.concatenate([jnp.broadcast_to(a_s[0:1, :], (n_lat, bst)),
                                   jnp.broadcast_to(a_s[2:3, :], (n_lat, bst))], axis=0)
        coef_im = jnp.concatenate([jnp.broadcast_to(a_s[1:2, :], (n_lat, bst)),
                                   jnp.broadcast_to(a_s[3:4, :], (n_lat, bst))], axis=0)
        s_re0 = jnp.concatenate([h0_ref[0, 0], h0_ref[1, 0]], axis=0)
        s_im0 = jnp.concatenate([h0_ref[0, 1], h0_ref[1, 1]], axis=0)

        def step(j, carry):
            s_re, s_im = carry
            d_re, d_im = [], []
            for b in range(n_lat):
                row = pl.ds(b * nj + j, 1)
                spf_s[row, 0:bst] = s_re[b:b + 1]
                spf_s[row, bst:2 * bst] = s_im[b:b + 1]
                d_re.append(d_s[row, 0:bst])
                d_im.append(d_s[row, bst:2 * bst])
            for b in range(n_lat):
                row = pl.ds(b * nj + nj - 1 - j, 1)
                spb_s[row, 0:bst] = s_re[n_lat + b:n_lat + b + 1]
                spb_s[row, bst:2 * bst] = s_im[n_lat + b:n_lat + b + 1]
                d_re.append(d_s[row, 2 * bst:3 * bst])
                d_im.append(d_s[row, 3 * bst:4 * bst])
            d_re = jnp.concatenate(d_re, axis=0)
            d_im = jnp.concatenate(d_im, axis=0)
            return coef_re * s_re - coef_im * s_im + d_re, coef_re * s_im + coef_im * s_re + d_im

        lax.fori_loop(0, nj, step, (s_re0, s_im0))
        outputs()
        for b in range(n_lat):
            for k in range(S5_T):
                y_ref[pl.ds(b * l_lat + k, nj, stride=S5_T), :] = yo_s[b * nj:(b + 1) * nj, k * LANES:(k + 1) * LANES]


def _s5_core(prm, h0, u, *, n_ctx, l_ctx, n_lat, l_lat, e):
    blocks = prm[0].shape[0]
    part_tokens = n_lat * l_lat
    t = u.shape[0]
    n_parts = t // part_tokens
    seqs = part_tokens // l_ctx
    r = part_tokens // S5_T
    vec = pl.BlockSpec((None, 2, 1, S5_BLOCK_ST), lambda g, p: (g, 0, 0, 0))
    mat = pl.BlockSpec((None, 2, S5_GROUP, S5_BLOCK_ST), lambda g, p: (g, 0, 0, 0))
    k_in = S5_T * LANES
    return pl.pallas_call(
        functools.partial(_s5_body, seqs=seqs, l_ctx=l_ctx, n_lat=n_lat, l_lat=l_lat, ctx_parts=n_parts - 1),
        grid=(blocks, n_parts),
        in_specs=[vec, vec, vec, mat, mat, mat, mat,
                  pl.BlockSpec((None, 2, 2, n_lat, S5_BLOCK_ST), lambda g, p: (g, 0, 0, 0, 0)),
                  pl.BlockSpec((part_tokens, LANES), lambda g, p: (p, g))],
        out_specs=[pl.BlockSpec((part_tokens, LANES), lambda g, p: (p, g)),
                   pl.BlockSpec((seqs, 2, 2, LANES // S5_GROUP, S5_STATE),
                                lambda g, p: (jnp.minimum(p, n_parts - 2), 0, 0, g, 0))],
        out_shape=[jax.ShapeDtypeStruct((t, e), F32),
                   jax.ShapeDtypeStruct((n_ctx, 2, 2, e // S5_GROUP, S5_STATE), F32)],
        scratch_shapes=[pltpu.VMEM((k_in, 4 * S5_BLOCK_ST), BF16),
                        pltpu.VMEM((k_in, k_in), BF16),
                        pltpu.VMEM((k_in, 2 * S5_BLOCK_ST), BF16),
                        pltpu.VMEM((k_in, 2 * S5_BLOCK_ST), BF16),
                        pltpu.VMEM(((2 * S5_T - 1) * LANES, LANES), F32),
                        pltpu.VMEM((8, S5_BLOCK_ST), F32),
                        pltpu.VMEM((r, k_in), BF16),
                        pltpu.VMEM((r, 4 * S5_BLOCK_ST), F32),
                        pltpu.VMEM((r, 2 * S5_BLOCK_ST), F32),
                        pltpu.VMEM((r, 2 * S5_BLOCK_ST), F32),
                        pltpu.VMEM((r, k_in), F32),
                        pltpu.VMEM((part_tokens, LANES), F32)],
        compiler_params=_params("parallel", "arbitrary"),
        name="s5_core",
    )(*prm, h0, u)


def _s5_glu_body(ys_ref, u_ref, dsk_ref, w_ref, b_ref, o_ref):
    y = _gelu(ys_ref[...] + dsk_ref[...] * u_ref[...].astype(F32))
    o_ref[...] = (y * jax.nn.sigmoid(_dot(y.astype(BF16), w_ref[...]) + b_ref[...])).astype(o_ref.dtype)


def _s5_glu(ys, u, dsk, w, b, *, tm=GLU_TM):
    t, e = ys.shape
    return pl.pallas_call(
        _s5_glu_body,
        grid=(t // tm,),
        in_specs=[pl.BlockSpec((tm, e), lambda i: (i, 0)),
                  pl.BlockSpec((tm, e), lambda i: (i, 0)),
                  pl.BlockSpec((1, e), lambda i: (0, 0)),
                  pl.BlockSpec((e, e), lambda i: (0, 0)),
                  pl.BlockSpec((1, e), lambda i: (0, 0))],
        out_specs=pl.BlockSpec((tm, e), lambda i: (i, 0)),
        out_shape=jax.ShapeDtypeStruct((t, e), ACT),
        compiler_params=_params("parallel"),
        name="s5_glu",
    )(ys, u, dsk, w, b)


def _ctx_attn_body(q_ref, k_ref, v_ref, o_ref, ko_ref, vo_ref):
    scale = HEAD_DIM ** -0.5
    seq_len = q_ref.shape[0]
    first = lax.broadcasted_iota(jnp.int32, (seq_len, LANES), 1) < HEAD_DIM
    masks = (first, jnp.logical_not(first))
    n_pairs = q_ref.shape[1] // LANES
    logits, values = [], []
    for pair in range(n_pairs):
        cols = slice(pair * LANES, (pair + 1) * LANES)
        q2, k2, v2 = q_ref[:, cols].astype(F32) * scale, k_ref[:, cols], v_ref[:, cols]
        kb, vb = k2.astype(BF16), v2.astype(BF16)
        for h in range(LANES // HEAD_DIM):
            ko_ref[2 * pair + h] = k2[:, h * HEAD_DIM:(h + 1) * HEAD_DIM].astype(F32)
            vo_ref[2 * pair + h] = v2[:, h * HEAD_DIM:(h + 1) * HEAD_DIM].astype(F32)
            logits.append(_dot_nt(jnp.where(masks[h], q2, 0.0).astype(BF16), kb))
            values.append(jnp.where(masks[h], vb, jnp.ones_like(vb)))
    probs = [jnp.exp(s - jnp.max(s, axis=-1, keepdims=True)).astype(BF16) for s in logits]
    for pair in range(n_pairs):
        o0, o1 = _dot(probs[2 * pair], values[2 * pair]), _dot(probs[2 * pair + 1], values[2 * pair + 1])
        num = jnp.where(first, o0, o1)
        den = jnp.where(first, pltpu.roll(o0, HEAD_DIM, 1), pltpu.roll(o1, HEAD_DIM, 1))
        o_ref[:, pair * LANES:(pair + 1) * LANES] = (num / den).astype(o_ref.dtype)


def _ctx_attention(u, *, n_seq, seq_len, e, width=CTX_ATTN_WIDTH):
    t = n_seq * seq_len
    hp = e // width
    heads = e // HEAD_DIM
    kv_block = (None, width // HEAD_DIM, seq_len, HEAD_DIM)
    in_specs = [pl.BlockSpec((seq_len, width), lambda b, h: (b, h)),
                pl.BlockSpec((seq_len, width), lambda b, h: (b, hp + h)),
                pl.BlockSpec((seq_len, width), lambda b, h: (b, 2 * hp + h))]
    return pl.pallas_call(
        _ctx_attn_body,
        grid=(n_seq, hp),
        in_specs=in_specs,
        out_specs=[pl.BlockSpec((seq_len, width), lambda b, h: (b, h)),
                   pl.BlockSpec(kv_block, lambda b, h: (b, h, 0, 0)),
                   pl.BlockSpec(kv_block, lambda b, h: (b, h, 0, 0))],
        out_shape=[jax.ShapeDtypeStruct((t, e), ACT),
                   jax.ShapeDtypeStruct((n_seq, heads, seq_len, HEAD_DIM), F32),
                   jax.ShapeDtypeStruct((n_seq, heads, seq_len, HEAD_DIM), F32)],
        compiler_params=_params("parallel", "parallel"),
        name="ctx_attention",
    )(u, u, u)


def _nat_bias_body(rp_ref, o_ref):
    n_rel_rows = rp_ref.shape[1]
    q = lax.broadcasted_iota(jnp.int32, (GRID_W, LANES), 0)
    lane = lax.broadcasted_iota(jnp.int32, (GRID_W, LANES), 1)
    kc = lane % GRID_W
    c_start = jnp.clip(q - WIN_COLS // 2, 0, GRID_W - WIN_COLS)
    ok = (kc >= c_start) & (kc < c_start + WIN_COLS)
    left = lane < GRID_W
    for h in range(o_ref.shape[0]):
        halves = []
        for i in range(n_rel_rows):
            row = jnp.broadcast_to(rp_ref[h, i:i + 1, :], (GRID_W, LANES))
            halves.append((pltpu.roll(row, 0, 1, stride=1, stride_axis=0),
                           pltpu.roll(row, GRID_W, 1, stride=1, stride_axis=0)))
        for i in range(n_rel_rows - 1):
            o_ref[h, i] = jnp.where(ok, jnp.where(left, halves[i][0], halves[i + 1][1]), MASKED)


def _nat_bias(rpb, *, heads_per_step=4):
    heads, n_rel_rows, n_rel_cols = rpb.shape
    rows = jnp.roll(jnp.pad(rpb, ((0, 0), (0, 0), (0, LANES - n_rel_cols))), -(WIN_COLS - 1), axis=-1)
    return pl.pallas_call(
        _nat_bias_body,
        grid=(heads // heads_per_step,),
        in_specs=[pl.BlockSpec((heads_per_step, n_rel_rows, LANES), lambda h: (h, 0, 0))],
        out_specs=pl.BlockSpec((heads_per_step, n_rel_rows - 1, GRID_W, 2 * GRID_W), lambda h: (h, 0, 0, 0)),
        out_shape=jax.ShapeDtypeStruct((heads, n_rel_rows - 1, GRID_W, 2 * GRID_W), F32),
        compiler_params=_params("parallel"),
        name="nat_bias",
    )(rows)


def _nat_body(q_ref, k_ref, v_ref, ck_ref, cv_ref, bias_ref, o_ref, ckb_s, cvb_s, *, rows):
    scale = HEAD_DIM ** -0.5
    wr = min(WIN_ROWS, rows)
    nw = wr * GRID_W
    heads = tuple(range(LANES // HEAD_DIM))
    first = lax.broadcasted_iota(jnp.int32, (GRID_W, LANES), 1) < HEAD_DIM
    masks = (first, jnp.logical_not(first))
    ckb_s[...] = jnp.concatenate([ck_ref[0], ck_ref[1]], axis=1).astype(BF16)
    cvb_s[...] = jnp.concatenate([cv_ref[0], cv_ref[1]], axis=1).astype(BF16)

    def row_group(g, carry):
        rws = [g * NAT_ROWS_PER_STEP + i for i in range(NAT_ROWS_PER_STEP)]
        starts = [jnp.clip(r - wr // 2, 0, rows - wr) for r in rws]
        q_rows = [pl.ds(pl.multiple_of(r * GRID_W, GRID_W), GRID_W) for r in rws]
        k_rows = [pl.ds(pl.multiple_of(rs * GRID_W, GRID_W), nw) for rs in starts]
        logits = []
        for r, rs, qr, kr in zip(rws, starts, q_rows, k_rows):
            q2 = q_ref[qr, :].astype(F32) * scale
            kb = k_ref[kr, :].astype(BF16)
            for h in heads:
                q = jnp.where(masks[h], q2, 0.0).astype(BF16)
                logits.append((_dot_nt(q, kb), _dot_nt(q, ckb_s[...])))
        probs = []
        for idx, (s_win, s_ctx) in enumerate(logits):
            r, rs, h = rws[idx // 2], starts[idx // 2], heads[idx % 2]
            i0 = (WIN_ROWS - 1) - (r - rs)
            s_win = s_win + jnp.concatenate([bias_ref[h, i0 + 2 * jj] for jj in range(wr // 2)], axis=1)
            m = jnp.maximum(jnp.max(s_win, axis=-1, keepdims=True), jnp.max(s_ctx, axis=-1, keepdims=True))
            p_win = jnp.exp(s_win - m)
            p_ctx = jnp.exp(s_ctx - m)
            inv = 1.0 / (jnp.sum(p_win, axis=-1, keepdims=True) + jnp.sum(p_ctx, axis=-1, keepdims=True))
            probs.append(((p_win * inv).astype(BF16), (p_ctx * inv).astype(BF16)))
        for i, (qr, kr) in enumerate(zip(q_rows, k_rows)):
            vb = v_ref[kr, :].astype(BF16)
            outs = [_dot(probs[2 * i + h][0], vb) + _dot(probs[2 * i + h][1], cvb_s[...]) for h in heads]
            o_ref[qr, :] = jnp.where(first, outs[0], outs[1]).astype(o_ref.dtype)
        return carry

    lax.fori_loop(0, rows // NAT_ROWS_PER_STEP, row_group, 0)


def _nat_attention(u, cache_k, cache_v, cache_layer, bias, *, n_seq, seq_len, row_block0, e):
    hp = e // LANES
    hpb = LANES // HEAD_DIM
    past = cache_k.shape[3]
    seq = lambda b: row_block0 + b
    cache_block = (None, None, hpb, past, HEAD_DIM)
    return pl.pallas_call(
        functools.partial(_nat_body, rows=seq_len // GRID_W),
        grid=(n_seq, hp),
        in_specs=[pl.BlockSpec((seq_len, LANES), lambda b, h: (seq(b), h)),
                  pl.BlockSpec((seq_len, LANES), lambda b, h: (seq(b), hp + h)),
                  pl.BlockSpec((seq_len, LANES), lambda b, h: (seq(b), 2 * hp + h)),
                  pl.BlockSpec(cache_block, lambda b, h: (b, cache_layer, h, 0, 0)),
                  pl.BlockSpec(cache_block, lambda b, h: (b, cache_layer, h, 0, 0)),
                  pl.BlockSpec((hpb,) + bias.shape[1:], lambda b, h: (h, 0, 0, 0))],
        out_specs=pl.BlockSpec((seq_len, LANES), lambda b, h: (b, h)),
        out_shape=jax.ShapeDtypeStruct((n_seq * seq_len, e), ACT),
        scratch_shapes=[pltpu.VMEM((past, LANES), BF16), pltpu.VMEM((past, LANES), BF16)],
        compiler_params=_params("parallel", "parallel"),
        name="nat_attention",
    )(u, u, u, cache_k, cache_v, bias)


def kernel(x_prompt, x_sample, state_ssd, state_s5, cache_k, cache_v, c, c_ctx, norm_g, w_mod, b_mod, w_out, final_g, ssd_w_in, ssd_conv_w, ssd_conv_b, ssd_dt_bias, ssd_a_log, ssd_d, ssd_norm_g, mlp_w_in, mlp_ln_g, mlp_ln_b, mlp_w_s, mlp_b_s, s5_w_in, s5_lam_re, s5_lam_im, s5_log_step, s5_b_re, s5_b_im, s5_c_re, s5_c_im, s5_d, s5_w_glu, s5_b_glu, nat_w_in, nat_rpb):
    n_ctx, l_ctx, d = x_prompt.shape
    n_lat, l_lat, _ = x_sample.shape
    t_ctx = n_ctx * l_ctx
    depth = norm_g.shape[0]
    e = w_out.shape[1]
    assert depth == 4 and l_lat % l_ctx == 0 and t_ctx % (n_lat * l_lat) == 0
    tiles = dict(tm=OUT_TM, t_ctx=t_ctx, l_lat=l_lat)
    tiles_in = dict(tm=IN_TM, tn=IN_TN, t_ctx=t_ctx, l_lat=l_lat)

    t_lat = n_lat * l_lat
    x = (x_prompt.reshape(t_ctx, d), x_sample.reshape(t_lat, d))
    cond = jnp.concatenate([c_ctx[None], c, jnp.zeros((MOD_ROWS - 1 - n_lat, d), F32)], axis=0)
    mods = _modulation(cond, w_mod, b_mod).reshape(depth, MOD_ROWS, 3, 1, d)
    w_out_b = w_out

    groups = e // SSD_GROUP_CH
    n_main = 3 * e
    w_dt = ssd_w_in[0][:, n_main:].reshape(d, 2, groups, SSD_GROUP_HEADS).transpose(0, 2, 1, 3)
    w_dt = jnp.pad(w_dt.reshape(d, groups, 2 * SSD_GROUP_HEADS), ((0, 0), (0, 0), (0, LANES - 2 * SSD_GROUP_HEADS)))
    u, dt = _inproj(x, mods, 0, norm_g[0:1], ssd_w_in[0].astype(BF16), n_out=n_main,
                    w_extra=w_dt.reshape(d, groups * LANES).astype(BF16), name="ssd_in", **dict(tiles_in, tm=IN_TM_PAIR))
    ssd_args = (ssd_conv_w[0], ssd_conv_b[0:1], _group_lanes(ssd_dt_bias[0], groups),
                _group_lanes(ssd_a_log[0], groups), jnp.repeat(ssd_d[0], SSD_HEAD_DIM)[None])
    y_ctx, new_ssd = _ssd_call(u, dt, *ssd_args, None, n_seq=n_ctx, seq_len=l_ctx, seqs_per_step=l_lat // l_ctx,
                               row_block0=0, want_final=True, name="ssd_ctx")
    y_lat, = _ssd_call(u, dt, *ssd_args, state_ssd[:, 0], n_seq=n_lat, seq_len=l_lat, seqs_per_step=n_lat,
                       row_block0=t_ctx // t_lat, want_final=False, name="ssd_lat")
    x = _outproj((y_ctx, y_lat), u, 0, x, mods, 0, w_out_b, norm_g=ssd_norm_g[0:1], name="ssd_out", **tiles)

    u = _inproj(x, mods, 1, norm_g[1:2], mlp_w_in[0].astype(BF16), name="mlp_in", **tiles_in)
    b_lanes = jnp.repeat(mlp_b_s[0].T, e // mlp_b_s.shape[1], axis=1)
    y = _gmlp(u, mlp_ln_g[0:1], mlp_ln_b[0:1], mlp_w_s[0], b_lanes)
    x = _outproj(y, u, 2, x, mods, 1, w_out_b, name="mlp_out", **tiles)

    u = _inproj(x, mods, 2, norm_g[2:3], s5_w_in[0].astype(BF16), name="s5_in", **tiles_in)
    s5_groups = e // S5_GROUP
    bg = LANES // S5_GROUP
    blocks = s5_groups // bg

    def block_vec(v):
        return v.reshape(2, blocks, 1, S5_BLOCK_ST).transpose(1, 0, 2, 3)

    def block_mat_b(v):
        v = v.reshape(2, blocks, bg, S5_STATE, S5_GROUP).transpose(1, 0, 4, 2, 3)
        return v.reshape(blocks, 2, S5_GROUP, S5_BLOCK_ST)

    def block_mat_c(v):
        v = v.reshape(2, blocks, bg, S5_GROUP, S5_STATE).transpose(1, 0, 3, 2, 4)
        return v.reshape(blocks, 2, S5_GROUP, S5_BLOCK_ST)

    log_step = jnp.repeat(s5_log_step[0][:, :, None], S5_STATE, axis=2)
    prm = (block_vec(s5_lam_re[0]), block_vec(s5_lam_im[0]), block_vec(log_step), block_mat_b(s5_b_re[0]),
           block_mat_b(s5_b_im[0]), block_mat_c(s5_c_re[0]), block_mat_c(s5_c_im[0]))
    h0 = state_s5[:, 0].reshape(n_lat, 2, 2, blocks, S5_BLOCK_ST).transpose(3, 1, 2, 0, 4)
    ys, fin = _s5_core(prm, h0, u, n_ctx=n_ctx, l_ctx=l_ctx, n_lat=n_lat, l_lat=l_lat, e=e)
    new_s5 = fin[:, None]
    y = _s5_glu(ys, u, s5_d[0:1], s5_w_glu[0].astype(BF16), s5_b_glu[0:1])
    x = _outproj(y, u, 1, x, mods, 2, w_out_b, name="s5_out", **tiles)

    u = _inproj(x, mods, 3, norm_g[3:4], nat_w_in[0].astype(BF16), name="nat_in", **tiles_in)
    y_ctx, new_k, new_v = _ctx_attention(u, n_seq=n_ctx, seq_len=l_ctx, e=e)
    y_lat = _nat_attention(u, cache_k, cache_v, 0, _nat_bias(nat_rpb[0]), n_seq=n_lat, seq_len=l_lat,
                           row_block0=t_ctx // l_lat, e=e)
    out_ctx, out_lat = _outproj((y_ctx, y_lat), u, 3, x, mods, 3, w_out_b, final_g=final_g[None], split_rows=(t_ctx, t_lat),
                                name="nat_out", **tiles)

    return (out_ctx.reshape(n_ctx, l_ctx, d), out_lat.reshape(n_lat, l_lat, d),
            new_ssd[:, None], new_s5, new_k[:, None], new_v[:, None])
```
